```python
import jax, jax.numpy as jnp
from jax import lax
import numpy as np

D_MODEL = 1024
BATCH = 16
SEQ = 256
DEPTH = 2
DEC_BATCH = 2
DEC_SEQ = 1024
PAST_LEN = 256

GRID_W = 64
N_EVEN = (DEPTH + 1) // 2
N_ODD = DEPTH // 2

MLA_HEADS = 8
QK_NOPE = 64
QK_ROPE = 32
QK_HEAD = QK_NOPE + QK_ROPE
V_HEAD = 64
Q_LORA = 256
KV_LORA = 128
ROPE_THETA = 10000.0
MLA_WIDTH = MLA_HEADS * V_HEAD

SC_GROUPS = 8
SC_WIDTH = D_MODEL // 2
SC_KERNEL = 3

IN0_WIDTH = Q_LORA + KV_LORA + QK_ROPE + 3 * SC_WIDTH
IN0_SPLITS = (Q_LORA, Q_LORA + KV_LORA, Q_LORA + KV_LORA + QK_ROPE,
              Q_LORA + KV_LORA + QK_ROPE + SC_WIDTH, Q_LORA + KV_LORA + QK_ROPE + 2 * SC_WIDTH)

CONF_KERNEL = 31

D_FF = 2816
N_EXPERTS = 8
TOP_K = 2
D_FF_EXPERT = 3584

Q_BLOCK = 128
EPS = 1e-6

kernel_name = "hybrid_mla_shortconv_conformer_moe_dit_step"


def _rms(x, g):
    xf = x.astype(jnp.float32)
    y = xf * lax.rsqrt(jnp.mean(xf * xf, axis=-1, keepdims=True) + EPS)
    return (y * g.astype(jnp.float32)).astype(x.dtype)


def _layernorm(x, g, b):
    xf = x.astype(jnp.float32)
    mu = jnp.mean(xf, axis=-1, keepdims=True)
    var = jnp.mean(jnp.square(xf - mu), axis=-1, keepdims=True)
    y = (xf - mu) * lax.rsqrt(var + EPS)
    return (y * g.astype(jnp.float32) + b.astype(jnp.float32)).astype(x.dtype)


def _adaln(cond, w, b):
    m = jax.nn.silu(cond) @ w + b
    return [t[:, None, :] for t in jnp.split(m, 6, axis=-1)]


def _modulate(x, shift, scale):
    return x * (1.0 + scale) + shift


def _axial_rope(x):
    n = x.shape[1]
    rows = n // GRID_W
    row = jnp.repeat(jnp.arange(rows, dtype=jnp.float32), GRID_W)
    col = jnp.tile(jnp.arange(GRID_W, dtype=jnp.float32), rows)
    half = QK_ROPE // 2
    nf = half // 2
    inv = ROPE_THETA ** (-jnp.arange(nf, dtype=jnp.float32) / nf)

    def rot(xa, pos):
        ang = pos[:, None] * inv[None, :]
        ang = jnp.concatenate([ang, ang], axis=-1)[:, None, :]
        xf = xa.astype(jnp.float32)
        x1, x2 = xf[..., :nf], xf[..., nf:]
        return xf * jnp.cos(ang) + jnp.concatenate([-x2, x1], axis=-1) * jnp.sin(ang)

    out = jnp.concatenate([rot(x[..., :half], row), rot(x[..., half:], col)], axis=-1)
    return out.astype(x.dtype)


def _attend(q, k, v):
    b, h, nq, dh = q.shape
    nb = nq // Q_BLOCK
    qb = jnp.moveaxis(q.reshape(b, h, nb, Q_BLOCK, dh), 2, 0)

    def block(qi):
        s = jnp.einsum('bhqd,bhkd->bhqk', qi, k).astype(jnp.float32) * (QK_HEAD ** -0.5)
        p = jax.nn.softmax(s, axis=-1).astype(v.dtype)
        return jnp.einsum('bhqk,bhkv->bhqv', p, v)

    o = lax.map(block, qb)
    return jnp.moveaxis(o, 0, 2).reshape(b, h, nq, v.shape[-1])


def _short_conv(gb, gc, u, w):
    v = gc * u
    n = v.shape[1]
    pad = SC_KERNEL // 2
    vp = jnp.pad(v, ((0, 0), (pad, pad), (0, 0)))
    y = w[0] * vp[:, 0:n]
    for j in range(1, SC_KERNEL):
        y = y + w[j] * vp[:, j:j + n]
    return gb * y


def _even_project(h, p):
    z = h @ p["w_in"]
    q_lat, ckv, kpe, gb, gc, u = jnp.split(z, IN0_SPLITS, axis=-1)
    ckv = _rms(ckv, p["kv_a_norm"])
    return q_lat, ckv, kpe, gb, gc, u


def _mla_queries(q_lat, p, rope):
    b, n, _ = q_lat.shape
    q = (_rms(q_lat, p["q_a_norm"]) @ p["w_qb"]).reshape(b, n, MLA_HEADS, QK_HEAD)
    q = _rms(q, p["q_norm"])
    if rope:
        q = jnp.concatenate([q[..., :QK_NOPE], _axial_rope(q[..., QK_NOPE:])], axis=-1)
    return q.transpose(0, 2, 1, 3)


def _mla_keys_values(ckv, kpe, p, rope):
    b, n, _ = ckv.shape
    kv = (ckv @ p["w_kvb"]).reshape(b, n, MLA_HEADS, QK_NOPE + V_HEAD)
    k_nope, v = kv[..., :QK_NOPE], kv[..., QK_NOPE:]
    k_pe = jnp.broadcast_to(kpe[:, :, None, :], (b, n, MLA_HEADS, QK_ROPE))
    k = _rms(jnp.concatenate([k_nope, k_pe], axis=-1), p["k_norm"])
    if rope:
        k = jnp.concatenate([k[..., :QK_NOPE], _axial_rope(k[..., QK_NOPE:])], axis=-1)
    return k.transpose(0, 2, 1, 3), v.transpose(0, 2, 1, 3)


def _even_out(attn, sc, w_o):
    b, h, n, dv = attn.shape
    a = attn.transpose(0, 2, 1, 3).reshape(b, n, h * dv)
    return jnp.concatenate([a, sc], axis=-1) @ w_o


def _even_mixer_context(h, p):
    q_lat, ckv, kpe, gb, gc, u = _even_project(h, p)
    q = _mla_queries(q_lat, p, rope=False)
    k, v = _mla_keys_values(ckv, kpe, p, rope=False)
    out = _even_out(_attend(q, k, v), _short_conv(gb, gc, u, p["w_sc"]), p["w_o"])
    return out, ckv, kpe


def _even_mixer_latent(h, ckv_ctx, kpe_ctx, p):
    q_lat, ckv, kpe, gb, gc, u = _even_project(h, p)
    q = _mla_queries(q_lat, p, rope=True)
    k_lat, v_lat = _mla_keys_values(ckv, kpe, p, rope=True)
    k_ctx, v_ctx = _mla_keys_values(ckv_ctx, kpe_ctx, p, rope=False)
    k = jnp.concatenate([k_ctx, k_lat], axis=2)
    v = jnp.concatenate([v_ctx, v_lat], axis=2)
    return _even_out(_attend(q, k, v), _short_conv(gb, gc, u, p["w_sc"]), p["w_o"])


def _conformer_conv(h, p):
    z = h @ p["w_pw1"] + p["b_pw1"]
    a, g = jnp.split(z, 2, axis=-1)
    u = a * jax.nn.sigmoid(g)
    u = lax.conv_general_dilated(
        u, p["w_dw"][:, None, :], window_strides=(1,),
        padding=((CONF_KERNEL // 2, CONF_KERNEL // 2),),
        dimension_numbers=('NWC', 'WIO', 'NWC'),
        feature_group_count=u.shape[-1]) + p["b_dw"]
    u = jax.nn.silu(_layernorm(u, p["ln_g"], p["ln_b"]))
    return u @ p["w_pw2"] + p["b_pw2"]


def _swiglu(h, w_gu, w_down):
    g, u = jnp.split(h @ w_gu, 2, axis=-1)
    return (jax.nn.silu(g) * u) @ w_down


def _moe(h, router, w_gu, w_down):
    b, n, d = h.shape
    t = h.reshape(b * n, d)
    logits = (t @ router).astype(jnp.float32)
    top_v, top_i = lax.top_k(logits, TOP_K)
    w = jax.nn.softmax(top_v, axis=-1)
    gates = jnp.sum(jax.nn.one_hot(top_i, N_EXPERTS, dtype=jnp.float32) * w[..., None], axis=-2)
    out = jnp.zeros_like(t)
    for e in range(N_EXPERTS):
        out = out + gates[:, e:e + 1].astype(t.dtype) * _swiglu(t, w_gu[e], w_down[e])
    return out.reshape(b, n, d)


def setup_inputs(seed: int = 0) -> dict:
    key = jax.random.key(seed)
    kiter = iter(list(jax.random.split(key, 64)))
    D = D_MODEL

    def nrm(shape, scale):
        return scale * jax.random.normal(next(kiter), shape, jnp.float32)

    def gain(shape):
        return 1.0 + nrm(shape, 0.1)

    return {
        "x_prompt": nrm((BATCH, SEQ, D), 1.0),
        "x_sample": nrm((DEC_BATCH, DEC_SEQ, D), 1.0),
        "cache_ckv": nrm((DEC_BATCH, N_EVEN, PAST_LEN, KV_LORA), 1.0),
        "cache_kpe": nrm((DEC_BATCH, N_EVEN, PAST_LEN, QK_ROPE), 1.0),
        "c": nrm((DEC_BATCH, D), 1.0),
        "c_ctx": nrm((D,), 1.0),
        "ada_w": nrm((DEPTH, D, 6 * D), 0.5 * D ** -0.5),
        "ada_b": nrm((DEPTH, 6 * D), 0.02),
        "norm_mix": gain((DEPTH, D)),
        "norm_ffn": gain((DEPTH, D)),
        "w_in": nrm((N_EVEN, D, IN0_WIDTH), D ** -0.5),
        "q_a_norm": gain((N_EVEN, Q_LORA)),
        "w_qb": nrm((N_EVEN, Q_LORA, MLA_HEADS * QK_HEAD), Q_LORA ** -0.5),
        "kv_a_norm": gain((N_EVEN, KV_LORA)),
        "w_kvb": nrm((N_EVEN, KV_LORA, MLA_HEADS * (QK_NOPE + V_HEAD)), KV_LORA ** -0.5),
        "q_norm": gain((N_EVEN, QK_HEAD)),
        "k_norm": gain((N_EVEN, QK_HEAD)),
        "w_sc": nrm((N_EVEN, SC_KERNEL, SC_WIDTH), SC_KERNEL ** -0.5),
        "w_o": nrm((N_EVEN, MLA_WIDTH + SC_WIDTH, D), (MLA_WIDTH + SC_WIDTH) ** -0.5),
        "ffn_gu": nrm((N_EVEN, D, 2 * D_FF), D ** -0.5),
        "ffn_down": nrm((N_EVEN, D_FF, D), D_FF ** -0.5),
        "conv_pw1": nrm((N_ODD, D, 2 * D), D ** -0.5),
        "conv_pw1_b": nrm((N_ODD, 2 * D), 0.02),
        "conv_dw": nrm((N_ODD, CONF_KERNEL, D), CONF_KERNEL ** -0.5),
        "conv_dw_b": nrm((N_ODD, D), 0.02),
        "conv_ln_g": gain((N_ODD, D)),
        "conv_ln_b": nrm((N_ODD, D), 0.02),
        "conv_pw2": nrm((N_ODD, D, D), D ** -0.5),
        "conv_pw2_b": nrm((N_ODD, D), 0.02),
        "router": nrm((N_ODD, D, N_EXPERTS), D ** -0.5),
        "moe_gu": nrm((N_ODD, N_EXPERTS, D, 2 * D_FF_EXPERT), D ** -0.5),
        "moe_down": nrm((N_ODD, N_EXPERTS, D_FF_EXPERT, D), D_FF_EXPERT ** -0.5),
    }


def reference(x_prompt, x_sample, cache_ckv, cache_kpe, c, c_ctx,
              ada_w, ada_b, norm_mix, norm_ffn,
              w_in, q_a_norm, w_qb, kv_a_norm, w_kvb, q_norm, k_norm, w_sc, w_o,
              ffn_gu, ffn_down,
              conv_pw1, conv_pw1_b, conv_dw, conv_dw_b, conv_ln_g, conv_ln_b, conv_pw2, conv_pw2_b,
              router, moe_gu, moe_down):
    xp = x_prompt
    xs = x_sample
    new_ckv = []
    new_kpe = []
    for l in range(DEPTH):
        i = l // 2
        m_ctx = _adaln(c_ctx[None, :], ada_w[l], ada_b[l])
        m_lat = _adaln(c, ada_w[l], ada_b[l])
        hp = _modulate(_rms(xp, norm_mix[l]), m_ctx[0], m_ctx[1])
        hs = _modulate(_rms(xs, norm_mix[l]), m_lat[0], m_lat[1])
        if l % 2 == 0:
            p = dict(w_in=w_in[i], q_a_norm=q_a_norm[i], w_qb=w_qb[i], kv_a_norm=kv_a_norm[i],
                     w_kvb=w_kvb[i], q_norm=q_norm[i], k_norm=k_norm[i], w_sc=w_sc[i], w_o=w_o[i])
            op, ckv_p, kpe_p = _even_mixer_context(hp, p)
            os_ = _even_mixer_latent(hs, cache_ckv[:, i], cache_kpe[:, i], p)
            new_ckv.append(ckv_p)
            new_kpe.append(kpe_p)
        else:
            p = dict(w_pw1=conv_pw1[i], b_pw1=conv_pw1_b[i], w_dw=conv_dw[i], b_dw=conv_dw_b[i],
                     ln_g=conv_ln_g[i], ln_b=conv_ln_b[i], w_pw2=conv_pw2[i], b_pw2=conv_pw2_b[i])
            op = _conformer_conv(hp, p)
            os_ = _conformer_conv(hs, p)
        xp = xp + m_ctx[2] * op
        xs = xs + m_lat[2] * os_
        hp = _modulate(_rms(xp, norm_ffn[l]), m_ctx[3], m_ctx[4])
        hs = _modulate(_rms(xs, norm_ffn[l]), m_lat[3], m_lat[4])
        if l % 2 == 0:
            fp = _swiglu(hp, ffn_gu[i], ffn_down[i])
            fs = _swiglu(hs, ffn_gu[i], ffn_down[i])
        else:
            fp = _moe(hp, router[i], moe_gu[i], moe_down[i])
            fs = _moe(hs, router[i], moe_gu[i], moe_down[i])
        xp = xp + m_ctx[5] * fp
        xs = xs + m_lat[5] * fs
    state_ckv = jnp.stack(new_ckv, axis=1)
    state_kpe = jnp.stack(new_kpe, axis=1)
    return (xp, xs, state_ckv, state_kpe)
```

```python
import functools

import jax
import jax.numpy as jnp
from jax import lax
from jax.experimental import pallas as pl
from jax.experimental.pallas import tpu as pltpu

F32 = jnp.float32
BF16 = jnp.bfloat16

D = 1024
N_CTX_SEQ, CTX_LEN = 16, 256
N_LAT_SEQ, LAT_LEN = 2, 1024
T_CTX = N_CTX_SEQ * CTX_LEN
T_LAT = N_LAT_SEQ * LAT_LEN
T = T_CTX + T_LAT
PAST = 256
GRID_W = 64
HEADS = 8
QK_NOPE, QK_ROPE, V_HEAD = 64, 32, 64
QK_HEAD = QK_NOPE + QK_ROPE
HEAD_PAD = 128
Q_LORA, KV_LORA = 256, 128
SC_W = 512
CONF_K = 31
D_FF = 2816
N_EXP = 8
D_FFE = 3584
EPS = 1e-6
ROPE_THETA = 10000.0

TM = 1024
N_TILES = T // TM
CTX_TILES = T_CTX // TM
TKV = 512
TQ = 256
FFN_FC = 256
MOE_FC = 512
VMEM_LIMIT = 56 * 1024 * 1024


def _dot(a, b):
    return jnp.dot(a, b, preferred_element_type=F32)


def _dot_nt(a, b):
    return lax.dot_general(a, b, (((1,), (1,)), ((), ())), preferred_element_type=F32)


def _rms(x, g):
    return x * lax.rsqrt(jnp.mean(x * x, axis=-1, keepdims=True) + EPS) * g


def _silu(x):
    return x * jax.nn.sigmoid(x)


def _params(*sem):
    return pltpu.CompilerParams(dimension_semantics=sem, vmem_limit_bytes=VMEM_LIMIT)


def _cond_tile(i):
    return jnp.maximum(i - (CTX_TILES - 1), 0)


def _adaln_kernel(c_ref, w_ref, b_ref, o_ref):
    s = _silu(c_ref[...]).astype(BF16)
    o_ref[0] = _dot(s, w_ref[0].astype(BF16)) + b_ref[0]


def _adaln(conds, ada_w, ada_b):
    depth = ada_w.shape[0]
    tn = 1024
    return pl.pallas_call(
        _adaln_kernel,
        grid=(depth, 6 * D // tn),
        in_specs=[
            pl.BlockSpec((8, D), lambda l, j: (0, 0)),
            pl.BlockSpec((1, D, tn), lambda l, j: (l, 0, j)),
            pl.BlockSpec((1, 1, tn), lambda l, j: (l, 0, j)),
        ],
        out_specs=pl.BlockSpec((1, 8, tn), lambda l, j: (l, 0, j)),
        out_shape=jax.ShapeDtypeStruct((depth, 8, 6 * D), F32),
        compiler_params=_params("arbitrary", "arbitrary"),
        name="adaln",
    )(conds, ada_w, ada_b.reshape(depth, 1, 6 * D))


def _rope_tables():
    half = QK_ROPE // 2
    nf = half // 2
    pos = jnp.arange(LAT_LEN)
    row = (pos // GRID_W).astype(F32)
    col = (pos % GRID_W).astype(F32)
    inv = ROPE_THETA ** (-jnp.arange(nf, dtype=F32) / nf)
    k = jnp.arange(QK_ROPE)
    part, idx = k // half, k % half
    p = jnp.where(part[None, :] == 0, row[:, None], col[:, None])
    ang = p * inv[idx % nf][None, :]
    cos, sin = jnp.cos(ang), jnp.sin(ang)
    first = (idx < nf)[None, :]
    s1 = jnp.where(first, -sin, 0.0)
    s2 = jnp.where(first, 0.0, sin)

    def place(t, fill):
        lat = jnp.full((LAT_LEN, HEAD_PAD), fill, F32).at[:, QK_NOPE:QK_HEAD].set(t)
        return jnp.concatenate([lat, jnp.full((LAT_LEN, HEAD_PAD), fill, F32)], axis=0)

    return place(cos, 1.0), place(s1, 0.0), place(s2, 0.0)


def _rope(blk, cos, s1, s2):
    return blk * cos + pltpu.roll(blk, 8, 1) * s2 + pltpu.roll(blk, HEAD_PAD - 8, 1) * s1


def _head_norm(blk, g):
    ms = jnp.sum(blk * blk, axis=-1, keepdims=True) * (1.0 / QK_HEAD)
    return blk * lax.rsqrt(ms + EPS) * g


def _even_proj_kernel(xp_ref, xs_ref, mod_ref, nm_ref, wa_ref, wb_ref, qan_ref, wqb_ref, kvan_ref,
                      qn_ref, wsc_ref, cos_ref, s1_ref, s2_ref, q_ref, ckv_ref, kpe_ref, sc_ref):
    i = pl.program_id(0)
    x = jnp.where(i < CTX_TILES, xp_ref[...], xs_ref[...])
    m = mod_ref[0, 0]
    h = _rms(x, nm_ref[...]) * (1.0 + m[:, D:2 * D]) + m[:, 0:D]
    hb = h.astype(BF16)

    za = _dot(hb, wa_ref[...])
    ckv_ref[...] = _rms(za[:, Q_LORA:Q_LORA + KV_LORA], kvan_ref[...])
    kpe_ref[...] = za[:, Q_LORA + KV_LORA:]
    qa = _rms(za[:, :Q_LORA], qan_ref[...]).astype(BF16)
    cos, s1, s2 = cos_ref[...], s1_ref[...], s2_ref[...]
    qn = qn_ref[...]
    scale = QK_HEAD ** -0.5
    for hp in range(HEADS // 2):
        qq = _dot(qa, wqb_ref[:, hp * 256:(hp + 1) * 256])
        for j in range(2):
            blk = _head_norm(qq[:, j * HEAD_PAD:(j + 1) * HEAD_PAD], qn)
            blk = _rope(blk, cos, s1, s2) * scale
            h0 = (2 * hp + j) * HEAD_PAD
            q_ref[:, h0:h0 + HEAD_PAD] = blk.astype(BF16)

    gb = _dot(hb, wb_ref[:, 0:SC_W])
    v = _dot(hb, wb_ref[:, SC_W:2 * SC_W]) * _dot(hb, wb_ref[:, 2 * SC_W:3 * SC_W])
    seq = jnp.where(i < CTX_TILES, CTX_LEN, LAT_LEN)
    r = lax.broadcasted_iota(jnp.int32, (TM, 1), 0) & (seq - 1)
    vp = jnp.where(r == 0, 0.0, pltpu.roll(v, 1, 0))
    vn = jnp.where(r == seq - 1, 0.0, pltpu.roll(v, TM - 1, 0))
    w = wsc_ref[...]
    y = w[0:1] * vp + w[1:2] * v + w[2:3] * vn
    sc_ref[...] = (gb * y).astype(BF16)


def _even_proj(xp, xs, mods, norm_mix0, wa, wb, q_a_norm, wqb, kv_a_norm, qn, w_sc, tabs):
    full = lambda shape: pl.BlockSpec(shape, lambda i: (0,) * len(shape))
    tab = pl.BlockSpec((TM, HEAD_PAD), lambda i: (jnp.where(i < CTX_TILES, 1, 0), 0))
    row = lambda n: pl.BlockSpec((TM, n), lambda i: (i, 0))
    return pl.pallas_call(
        _even_proj_kernel,
        grid=(N_TILES,),
        in_specs=[
            pl.BlockSpec((TM, D), lambda i: (jnp.minimum(i, CTX_TILES - 1), 0)),
            pl.BlockSpec((TM, D), lambda i: (jnp.maximum(i - CTX_TILES, 0), 0)),
            pl.BlockSpec((1, 1, 1, 6 * D), lambda i: (0, _cond_tile(i), 0, 0)),
            full((1, D)), full((D, 512)), full((D, 3 * SC_W)), full((1, Q_LORA)),
            full((Q_LORA, HEADS * HEAD_PAD)), full((1, KV_LORA)), full((1, HEAD_PAD)),
            full((3, SC_W)), tab, tab, tab,
        ],
        out_specs=[row(HEADS * HEAD_PAD), row(KV_LORA), row(HEAD_PAD), row(SC_W)],
        out_shape=[
            jax.ShapeDtypeStruct((T, HEADS * HEAD_PAD), BF16),
            jax.ShapeDtypeStruct((T, KV_LORA), F32),
            jax.ShapeDtypeStruct((T, HEAD_PAD), F32),
            jax.ShapeDtypeStruct((T, SC_W), BF16),
        ],
        compiler_params=_params("arbitrary"),
        name="even_proj",
    )(xp, xs, mods, norm_mix0, wa, wb, q_a_norm, wqb, kv_a_norm, qn, w_sc, *tabs)


def _kv_proj_kernel(ckv_ref, kpe_ref, wkvb_ref, kn_ref, cos_ref, s1_ref, s2_ref, k_ref, kv_ref):
    kv = _dot(ckv_ref[...].astype(BF16), wkvb_ref[...])
    kv_ref[...] = kv.astype(BF16)
    kpe = pltpu.roll(kpe_ref[...], QK_NOPE, 1)
    lane = lax.broadcasted_iota(jnp.int32, (1, HEAD_PAD), 1)
    cos, s1, s2 = cos_ref[...], s1_ref[...], s2_ref[...]
    kn = kn_ref[...]
    for h in range(HEADS):
        blk = jnp.where(lane < QK_NOPE, kv[:, h * HEAD_PAD:(h + 1) * HEAD_PAD], kpe)
        blk = _rope(_head_norm(blk, kn), cos, s1, s2)
        k_ref[:, h * HEAD_PAD:(h + 1) * HEAD_PAD] = blk.astype(BF16)


def _kv_proj(ckv, kpe, wkvb, kn, tabs, tab_index, name):
    n = ckv.shape[0]
    full = lambda shape: pl.BlockSpec(shape, lambda i: (0,) * len(shape))
    tab = pl.BlockSpec((TKV, HEAD_PAD), lambda i: (tab_index(i), 0))
    row = lambda w: pl.BlockSpec((TKV, w), lambda i: (i, 0))
    return pl.pallas_call(
        _kv_proj_kernel,
        grid=(n // TKV,),
        in_specs=[row(KV_LORA), row(HEAD_PAD), full((KV_LORA, HEADS * HEAD_PAD)), full((1, HEAD_PAD)),
                  tab, tab, tab],
        out_specs=[row(HEADS * HEAD_PAD), row(HEADS * HEAD_PAD)],
        out_shape=[jax.ShapeDtypeStruct((n, HEADS * HEAD_PAD), BF16)] * 2,
        compiler_params=_params("arbitrary"),
        name=name,
    )(ckv, kpe, wkvb, kn, *tabs)


def _pair_out(o0, o1):
    lane = lax.broadcasted_iota(jnp.int32, (1, HEAD_PAD), 1)
    return jnp.where(lane < V_HEAD, pltpu.roll(o0, V_HEAD, 1), o1).astype(BF16)


def _attn_ctx_kernel(q_ref, k_ref, kv_ref, o_ref):
    for hp in range(HEADS // 2):
        outs = []
        for j in range(2):
            h0 = (2 * hp + j) * HEAD_PAD
            s = _dot_nt(q_ref[:, h0:h0 + HEAD_PAD], k_ref[:, h0:h0 + HEAD_PAD])
            p = jnp.exp(s - jnp.max(s, axis=-1, keepdims=True))
            l = jnp.sum(p, axis=-1, keepdims=True)
            outs.append(_dot(p.astype(BF16), kv_ref[:, h0:h0 + HEAD_PAD]) / l)
        o_ref[:, hp * HEAD_PAD:(hp + 1) * HEAD_PAD] = _pair_out(*outs)


def _attn_ctx(q, k, kv):
    blk = pl.BlockSpec((CTX_LEN, HEADS * HEAD_PAD), lambda b: (b, 0))
    return pl.pallas_call(
        _attn_ctx_kernel,
        grid=(N_CTX_SEQ,),
        in_specs=[blk, blk, blk],
        out_specs=pl.BlockSpec((CTX_LEN, HEADS * V_HEAD), lambda b: (b, 0)),
        out_shape=jax.ShapeDtypeStruct((T_CTX, HEADS * V_HEAD), BF16),
        compiler_params=_params("arbitrary"),
        name="attn_ctx",
    )(q, k, kv)


def _attn_lat_kernel(q_ref, kc_ref, kvc_ref, kl_ref, kvl_ref, o_ref):
    outs = []
    for j in range(2):
        lanes = slice(j * HEAD_PAD, (j + 1) * HEAD_PAD)
        q = q_ref[:, lanes]
        sc = _dot_nt(q, kc_ref[:, lanes])
        sl = _dot_nt(q, kl_ref[:, lanes])
        m = jnp.maximum(jnp.max(sc, axis=-1, keepdims=True), jnp.max(sl, axis=-1, keepdims=True))
        pc, pl_ = jnp.exp(sc - m), jnp.exp(sl - m)
        l = jnp.sum(pc, axis=-1, keepdims=True) + jnp.sum(pl_, axis=-1, keepdims=True)
        o = _dot(pc.astype(BF16), kvc_ref[:, lanes]) + _dot(pl_.astype(BF16), kvl_ref[:, lanes])
        outs.append(o / l)
    o_ref[...] = _pair_out(*outs)


def _attn_lat(q, kc, kvc, k, kv):
    nq = LAT_LEN // TQ
    q0 = T_CTX // TQ
    kl0 = T_CTX // LAT_LEN
    pair = 2 * HEAD_PAD
    lat = pl.BlockSpec((LAT_LEN, pair), lambda b, hp, t: (kl0 + b, hp))
    ctx = pl.BlockSpec((PAST, pair), lambda b, hp, t: (b, hp))
    return pl.pallas_call(
        _attn_lat_kernel,
        grid=(N_LAT_SEQ, HEADS // 2, nq),
        in_specs=[pl.BlockSpec((TQ, pair), lambda b, hp, t: (q0 + b * nq + t, hp)), ctx, ctx, lat, lat],
        out_specs=pl.BlockSpec((TQ, HEAD_PAD), lambda b, hp, t: (b * nq + t, hp)),
        out_shape=jax.ShapeDtypeStruct((T_LAT, HEADS * V_HEAD), BF16),
        compiler_params=_params("arbitrary", "arbitrary", "arbitrary"),
        name="attn_lat",
    )(q, kc, kvc, k, kv)


def _even_out_kernel(oc_ref, ol_ref, sc_ref, xp_ref, xs_ref, mod_ref, wo_ref, nf_ref, x1_ref, h_ref):
    i = pl.program_id(0)
    ctx = i < CTX_TILES
    attn = jnp.where(ctx, oc_ref[...], ol_ref[...])
    x = jnp.where(ctx, xp_ref[...], xs_ref[...])
    m = mod_ref[0, 0]
    out = _dot(attn, wo_ref[0:HEADS * V_HEAD, :]) + _dot(sc_ref[...], wo_ref[HEADS * V_HEAD:, :])
    x1 = x + m[:, 2 * D:3 * D] * out
    x1_ref[...] = x1
    h_ref[...] = (_rms(x1, nf_ref[...]) * (1.0 + m[:, 4 * D:5 * D]) + m[:, 3 * D:4 * D]).astype(BF16)


def _even_out(oc, ol, sc, xp, xs, mods, wo, norm_ffn0):
    full = lambda shape: pl.BlockSpec(shape, lambda i: (0,) * len(shape))
    first = lambda n: pl.BlockSpec((TM, n), lambda i: (jnp.minimum(i, CTX_TILES - 1), 0))
    second = lambda n: pl.BlockSpec((TM, n), lambda i: (jnp.maximum(i - CTX_TILES, 0), 0))
    row = lambda n: pl.BlockSpec((TM, n), lambda i: (i, 0))
    return pl.pallas_call(
        _even_out_kernel,
        grid=(N_TILES,),
        in_specs=[first(HEADS * V_HEAD), second(HEADS * V_HEAD), row(SC_W), first(D), second(D),
                  pl.BlockSpec((1, 1, 1, 6 * D), lambda i: (0, _cond_tile(i), 0, 0)),
                  full((HEADS * V_HEAD + SC_W, D)), full((1, D))],
        out_specs=[row(D), row(D)],
        out_shape=[jax.ShapeDtypeStruct((T, D), F32), jax.ShapeDtypeStruct((T, D), BF16)],
        compiler_params=_params("arbitrary"),
        name="even_out",
    )(oc, ol, sc, xp, xs, mods, wo, norm_ffn0)


def _ffn_kernel(h_ref, wg_ref, wu_ref, wd_ref, x1_ref, mod0_ref, mod1_ref, nm_ref, x2_ref, h3_ref, acc_ref):
    c = pl.program_id(1)

    @pl.when(c == 0)
    def _():
        acc_ref[...] = jnp.zeros_like(acc_ref)

    h = h_ref[...]
    g = _dot(h, wg_ref[...].astype(BF16))
    u = _dot(h, wu_ref[...].astype(BF16))
    acc_ref[...] += _dot((_silu(g) * u).astype(BF16), wd_ref[...].astype(BF16))

    @pl.when(c == pl.num_programs(1) - 1)
    def _():
        m0, m1 = mod0_ref[0, 0], mod1_ref[0, 0]
        x2 = x1_ref[...] + m0[:, 5 * D:6 * D] * acc_ref[...]
        x2_ref[...] = x2
        h3_ref[...] = (_rms(x2, nm_ref[...]) * (1.0 + m1[:, D:2 * D]) + m1[:, 0:D]).astype(BF16)


def _ffn(h, w_gu, w_down, x1, mods, norm_mix1):
    nc = D_FF // FFN_FC
    row = lambda n: pl.BlockSpec((TM, n), lambda i, c: (i, 0))
    mod = lambda l: pl.BlockSpec((1, 1, 1, 6 * D), lambda i, c: (l, _cond_tile(i), 0, 0))
    return pl.pallas_call(
        _ffn_kernel,
        grid=(N_TILES, nc),
        in_specs=[row(D),
                  pl.BlockSpec((D, FFN_FC), lambda i, c: (0, c)),
                  pl.BlockSpec((D, FFN_FC), lambda i, c: (0, nc + c)),
                  pl.BlockSpec((FFN_FC, D), lambda i, c: (c, 0)),
                  row(D), mod(0), mod(1), pl.BlockSpec((1, D), lambda i, c: (0, 0))],
        out_specs=[row(D), row(D)],
        out_shape=[jax.ShapeDtypeStruct((T, D), F32), jax.ShapeDtypeStruct((T, D), BF16)],
        scratch_shapes=[pltpu.VMEM((TM, D), F32)],
        compiler_params=_params("arbitrary", "arbitrary"),
        name="ffn_dense",
    )(h, w_gu, w_gu, w_down, x1, mods, mods, norm_mix1)


CONF_CB = 256
CONF_SEG = 256
CONF_HALO = 16
CONF_SEGP = CONF_SEG + 2 * CONF_HALO
CONF_PIECE = 64


def _conf_kernel(h_ref, x2_ref, w1_ref, b1_ref, wdw_ref, bdw_ref, lng_ref, lnb_ref, w2_ref, b2_ref,
                 mod_ref, nf_ref, rt_ref, x3_ref, h4_ref, lg_ref, pad_ref, conv_ref):
    i = pl.program_id(0)
    nseg = TM // CONF_SEG
    h = h_ref[...]
    joined = jnp.where(i < CTX_TILES, 0.0, 1.0)
    zeros_halo = jnp.zeros((CONF_HALO, CONF_CB), F32)
    for cb in range(D // CONF_CB):
        c0 = cb * CONF_CB
        a = _dot(h, w1_ref[:, c0:c0 + CONF_CB]) + b1_ref[:, c0:c0 + CONF_CB]
        g = _dot(h, w1_ref[:, D + c0:D + c0 + CONF_CB]) + b1_ref[:, D + c0:D + c0 + CONF_CB]
        u = a * jax.nn.sigmoid(g)
        for s in range(nseg):
            base = s * CONF_SEGP
            top = u[s * CONF_SEG - CONF_HALO:s * CONF_SEG] * joined if s > 0 else zeros_halo
            bot = (u[(s + 1) * CONF_SEG:(s + 1) * CONF_SEG + CONF_HALO] * joined
                   if s < nseg - 1 else zeros_halo)
            pad_ref[0, base:base + CONF_HALO, :] = top
            pad_ref[0, base + CONF_HALO:base + CONF_HALO + CONF_SEG, :] = u[s * CONF_SEG:(s + 1) * CONF_SEG]
            pad_ref[0, base + CONF_HALO + CONF_SEG:base + CONF_SEGP, :] = bot
        p0 = pad_ref[0]
        rows = nseg * CONF_SEGP
        for b in range(1, 8):
            pad_ref[b] = pltpu.roll(p0, rows - b, 0)

        def piece(t, carry):
            s = t // (CONF_SEG // CONF_PIECE)
            q0 = (t % (CONF_SEG // CONF_PIECE)) * CONF_PIECE
            src = pl.multiple_of(s * CONF_SEGP + q0, 8)
            acc = jnp.zeros((CONF_PIECE, CONF_CB), F32)
            for j in range(CONF_K):
                hi, lo = (j + 1) // 8, (j + 1) % 8
                acc = acc + wdw_ref[j:j + 1, c0:c0 + CONF_CB] * pad_ref[lo, pl.ds(src + 8 * hi, CONF_PIECE), :]
            dst = pl.multiple_of(s * CONF_SEG + q0, 8)
            conv_ref[pl.ds(dst, CONF_PIECE), c0:c0 + CONF_CB] = acc + bdw_ref[:, c0:c0 + CONF_CB]
            return carry

        lax.fori_loop(0, TM // CONF_PIECE, piece, 0)

    y = conv_ref[...]
    mu = jnp.mean(y, axis=-1, keepdims=True)
    yc = y - mu
    var = jnp.mean(yc * yc, axis=-1, keepdims=True)
    y = _silu(yc * lax.rsqrt(var + EPS) * lng_ref[...] + lnb_ref[...])
    out = _dot(y.astype(BF16), w2_ref[...]) + b2_ref[...]
    m = mod_ref[0, 0]
    x3 = x2_ref[...] + m[:, 2 * D:3 * D] * out
    x3_ref[...] = x3
    h4 = _rms(x3, nf_ref[...]) * (1.0 + m[:, 4 * D:5 * D]) + m[:, 3 * D:4 * D]
    h4_ref[...] = h4.astype(BF16)
    lg_ref[...] = lax.dot_general(rt_ref[...], h4, (((1,), (1,)), ((), ())),
                                  precision=lax.Precision.HIGHEST, preferred_element_type=F32)


def _conf(h3, x2, w1, b1, wdw, bdw, lng, lnb, w2, b2, mods, norm_ffn1, router_t):
    full = lambda shape: pl.BlockSpec(shape, lambda i: (0,) * len(shape))
    row = lambda n: pl.BlockSpec((TM, n), lambda i: (i, 0))
    return pl.pallas_call(
        _conf_kernel,
        grid=(N_TILES,),
        in_specs=[row(D), row(D), full((D, 2 * D)), full((1, 2 * D)), full((CONF_K, D)), full((1, D)),
                  full((1, D)), full((1, D)), full((D, D)), full((1, D)),
                  pl.BlockSpec((1, 1, 1, 6 * D), lambda i: (1, _cond_tile(i), 0, 0)),
                  full((1, D)), full((N_EXP, D))],
        out_specs=[row(D), row(D), pl.BlockSpec((N_EXP, TM), lambda i: (0, i))],
        out_shape=[jax.ShapeDtypeStruct((T, D), F32), jax.ShapeDtypeStruct((T, D), BF16),
                   jax.ShapeDtypeStruct((N_EXP, T), F32)],
        scratch_shapes=[pltpu.VMEM((8, (TM // CONF_SEG) * CONF_SEGP, CONF_CB), F32),
                        pltpu.VMEM((TM, D), F32)],
        compiler_params=_params("arbitrary"),
        name="conformer_conv",
    )(h3, x2, w1, b1, wdw, bdw, lng, lnb, w2, b2, mods, norm_ffn1, router_t)


def _route_kernel(lg_ref, g_ref):
    lg = lg_ref[...]
    idx = lax.broadcasted_iota(jnp.int32, lg.shape, 0).astype(F32)
    none = float(N_EXP)
    m1 = jnp.max(lg, axis=0, keepdims=True)
    i1 = jnp.min(jnp.where(lg == m1, idx, none), axis=0, keepdims=True)
    rest = jnp.where(idx == i1, -jnp.inf, lg)
    m2 = jnp.max(rest, axis=0, keepdims=True)
    i2 = jnp.min(jnp.where(rest == m2, idx, none), axis=0, keepdims=True)
    e = jnp.exp(m2 - m1)
    w1 = 1.0 / (1.0 + e)
    w2 = e / (1.0 + e)
    g_ref[...] = jnp.where(idx == i1, w1, 0.0) + jnp.where(idx == i2, w2, 0.0)


def _route(logits_t):
    return pl.pallas_call(
        _route_kernel,
        out_shape=jax.ShapeDtypeStruct((N_EXP, T), F32),
        compiler_params=pltpu.CompilerParams(vmem_limit_bytes=VMEM_LIMIT),
        name="route",
    )(logits_t)


def _moe_kernel(h_ref, wg_ref, wu_ref, wd_ref, gate_ref, x3_ref, mod_ref, yp_ref, ys_ref, acc_ref):
    i, e, c = pl.program_id(0), pl.program_id(1), pl.program_id(2)

    @pl.when((e == 0) & (c == 0))
    def _():
        acc_ref[...] = jnp.zeros_like(acc_ref)

    h = h_ref[...]
    g = _dot(h, wg_ref[0].astype(BF16))
    u = _dot(h, wu_ref[0].astype(BF16))
    y = _dot((_silu(g) * u).astype(BF16), wd_ref[0].astype(BF16))
    lane = lax.broadcasted_iota(jnp.int32, (1, 128), 1)
    gate = jnp.sum(jnp.where(lane == e, gate_ref[...], 0.0), axis=-1, keepdims=True)
    acc_ref[...] += gate * y

    last = (e == pl.num_programs(1) - 1) & (c == pl.num_programs(2) - 1)

    @pl.when(last & (i < CTX_TILES))
    def _():
        yp_ref[...] = x3_ref[...] + mod_ref[0, 0][:, 5 * D:6 * D] * acc_ref[...]

    @pl.when(last & (i >= CTX_TILES))
    def _():
        ys_ref[...] = x3_ref[...] + mod_ref[0, 0][:, 5 * D:6 * D] * acc_ref[...]


def _moe(h4, w_gu, w_down, gates, x3, mods):
    nc = D_FFE // MOE_FC
    row = lambda n: pl.BlockSpec((TM, n), lambda i, e, c: (i, 0))
    return pl.pallas_call(
        _moe_kernel,
        grid=(N_TILES, N_EXP, nc),
        in_specs=[row(D),
                  pl.BlockSpec((1, D, MOE_FC), lambda i, e, c: (e, 0, c)),
                  pl.BlockSpec((1, D, MOE_FC), lambda i, e, c: (e, 0, nc + c)),
                  pl.BlockSpec((1, MOE_FC, D), lambda i, e, c: (e, c, 0)),
                  row(128), row(D),
                  pl.BlockSpec((1, 1, 1, 6 * D), lambda i, e, c: (1, _cond_tile(i), 0, 0))],
        out_specs=[pl.BlockSpec((TM, D), lambda i, e, c: (jnp.minimum(i, CTX_TILES - 1), 0)),
                   pl.BlockSpec((TM, D), lambda i, e, c: (jnp.maximum(i - CTX_TILES, 0), 0))],
        out_shape=[jax.ShapeDtypeStruct((T_CTX, D), F32), jax.ShapeDtypeStruct((T_LAT, D), F32)],
        scratch_shapes=[pltpu.VMEM((TM, D), F32)],
        compiler_params=_params("arbitrary", "arbitrary", "arbitrary"),
        name="moe_dense",
    )(h4, w_gu, w_gu, w_down, gates, x3, mods)


def _pad_heads(w, width):
    lead = w.shape[:-1]
    w = w.reshape(*lead, HEADS, width)
    w = jnp.pad(w, [(0, 0)] * len(lead) + [(0, 0), (0, HEAD_PAD - width)])
    return w.reshape(*lead, HEADS * HEAD_PAD)


def kernel(x_prompt, x_sample, cache_ckv, cache_kpe, c, c_ctx, ada_w, ada_b, norm_mix, norm_ffn, w_in, q_a_norm,
           w_qb, kv_a_norm, w_kvb, q_norm, k_norm, w_sc, w_o, ffn_gu, ffn_down, conv_pw1, conv_pw1_b, conv_dw,
           conv_dw_b, conv_ln_g, conv_ln_b, conv_pw2, conv_pw2_b, router, moe_gu, moe_down):
    xp = x_prompt.reshape(T_CTX, D)
    xs = x_sample.reshape(T_LAT, D)

    conds = jnp.zeros((8, D), F32).at[0].set(c_ctx).at[1:1 + N_LAT_SEQ].set(c)
    mods = _adaln(conds, ada_w, ada_b).reshape(2, 8, 1, 6 * D)

    n_a = Q_LORA + KV_LORA + QK_ROPE
    wa = jnp.pad(w_in[0, :, :n_a], ((0, 0), (0, 512 - n_a))).astype(BF16)
    wb = w_in[0, :, n_a:].astype(BF16)
    wqb = _pad_heads(w_qb[0], QK_HEAD).astype(BF16)
    wkvb = w_kvb[0].astype(BF16)
    qn = jnp.pad(q_norm[0], (0, HEAD_PAD - QK_HEAD)).reshape(1, HEAD_PAD)
    kn = jnp.pad(k_norm[0], (0, HEAD_PAD - QK_HEAD)).reshape(1, HEAD_PAD)
    tabs = _rope_tables()

    q, ckv, kpe, sc = _even_proj(xp, xs, mods, norm_mix[0:1], wa, wb, q_a_norm, wqb, kv_a_norm, qn, w_sc[0], tabs)

    lat_tile0 = T_CTX // TKV
    ident = LAT_LEN // TKV
    k, kv = _kv_proj(ckv, kpe, wkvb, kn, tabs,
                     lambda i: jnp.where(i < lat_tile0, ident, (i - lat_tile0) % ident), "kv_proj")
    cache_kpe_p = jnp.pad(cache_kpe[:, 0].reshape(N_LAT_SEQ * PAST, QK_ROPE), ((0, 0), (0, HEAD_PAD - QK_ROPE)))
    kc, kvc = _kv_proj(cache_ckv[:, 0].reshape(N_LAT_SEQ * PAST, KV_LORA), cache_kpe_p, wkvb, kn, tabs,
                       lambda i: ident, "kv_proj_cache")

    oc = _attn_ctx(q, k, kv)
    ol = _attn_lat(q, kc, kvc, k, kv)
    x1, h2 = _even_out(oc, ol, sc, xp, xs, mods, w_o[0].astype(BF16), norm_ffn[0:1])
    x2, h3 = _ffn(h2, ffn_gu[0], ffn_down[0], x1, mods, norm_mix[1:2])

    x3, h4, logits_t = _conf(h3, x2, conv_pw1[0].astype(BF16), conv_pw1_b, conv_dw[0], conv_dw_b, conv_ln_g,
                             conv_ln_b, conv_pw2[0].astype(BF16), conv_pw2_b, mods, norm_ffn[1:2], router[0].T)
    gates = jnp.pad(_route(logits_t).T, ((0, 0), (0, 128 - N_EXP)))
    yp, ys = _moe(h4, moe_gu[0], moe_down[0], gates, x3, mods)

    state_ckv = ckv[:T_CTX].reshape(N_CTX_SEQ, 1, CTX_LEN, KV_LORA)
    state_kpe = kpe[:T_CTX, :QK_ROPE].reshape(N_CTX_SEQ, 1, CTX_LEN, QK_ROPE)
    return (yp.reshape(N_CTX_SEQ, CTX_LEN, D), ys.reshape(N_LAT_SEQ, LAT_LEN, D), state_ckv, state_kpe)
```

```python
import functools

import jax
import jax.numpy as jnp
from jax import lax
from jax.experimental import pallas as pl
from jax.experimental.pallas import tpu as pltpu

F32 = jnp.float32
BF16 = jnp.bfloat16

D = 1024
N_CTX_SEQ, CTX_LEN = 16, 256
N_LAT_SEQ, LAT_LEN = 2, 1024
T_CTX = N_CTX_SEQ * CTX_LEN
T_LAT = N_LAT_SEQ * LAT_LEN
T = T_CTX + T_LAT
PAST = 256
GRID_W = 64
HEADS = 8
QK_NOPE, QK_ROPE, V_HEAD = 64, 32, 64
QK_HEAD = QK_NOPE + QK_ROPE
HEAD_PAD = 128
Q_LORA, KV_LORA = 256, 128
SC_W = 512
CONF_K = 31
D_FF = 2816
N_EXP = 8
D_FFE = 3584
EPS = 1e-6
ROPE_THETA = 10000.0

TM = 1024
N_TILES = T // TM
CTX_TILES = T_CTX // TM
TKV = 512
TQ = 256
FFN_FC = 256
MOE_FC = 512
VMEM_LIMIT = 56 * 1024 * 1024


def _dot(a, b):
    return jnp.dot(a, b, preferred_element_type=F32)


def _dot_nt(a, b):
    return lax.dot_general(a, b, (((1,), (1,)), ((), ())), preferred_element_type=F32)


def _rms(x, g):
    return x * lax.rsqrt(jnp.mean(x * x, axis=-1, keepdims=True) + EPS) * g


def _silu(x):
    return x * jax.nn.sigmoid(x)


def _params(*sem):
    return pltpu.CompilerParams(dimension_semantics=sem, vmem_limit_bytes=VMEM_LIMIT)


def _cond_tile(i):
    return jnp.maximum(i - (CTX_TILES - 1), 0)


def _adaln_kernel(c_ref, w_ref, b_ref, o_ref):
    s = _silu(c_ref[...]).astype(BF16)
    o_ref[0] = _dot(s, w_ref[0].astype(BF16)) + b_ref[0]


def _adaln(conds, ada_w, ada_b):
    depth = ada_w.shape[0]
    tn = 1024
    return pl.pallas_call(
        _adaln_kernel,
        grid=(depth, 6 * D // tn),
        in_specs=[
            pl.BlockSpec((8, D), lambda l, j: (0, 0)),
            pl.BlockSpec((1, D, tn), lambda l, j: (l, 0, j)),
            pl.BlockSpec((1, 1, tn), lambda l, j: (l, 0, j)),
        ],
        out_specs=pl.BlockSpec((1, 8, tn), lambda l, j: (l, 0, j)),
        out_shape=jax.ShapeDtypeStruct((depth, 8, 6 * D), F32),
        compiler_params=_params("arbitrary", "arbitrary"),
        name="adaln",
    )(conds, ada_w, ada_b.reshape(depth, 1, 6 * D))


def _rope_tables():
    half = QK_ROPE // 2
    nf = half // 2
    pos = jnp.arange(LAT_LEN)
    row = (pos // GRID_W).astype(F32)
    col = (pos % GRID_W).astype(F32)
    inv = ROPE_THETA ** (-jnp.arange(nf, dtype=F32) / nf)
    k = jnp.arange(QK_ROPE)
    part, idx = k // half, k % half
    p = jnp.where(part[None, :] == 0, row[:, None], col[:, None])
    ang = p * inv[idx % nf][None, :]
    cos, sin = jnp.cos(ang), jnp.sin(ang)
    first = (idx < nf)[None, :]
    s1 = jnp.where(first, -sin, 0.0)
    s2 = jnp.where(first, 0.0, sin)

    def place(t, fill):
        lat = jnp.full((LAT_LEN, HEAD_PAD), fill, F32).at[:, QK_NOPE:QK_HEAD].set(t)
        return jnp.concatenate([lat, jnp.full((LAT_LEN, HEAD_PAD), fill, F32)], axis=0)

    return place(cos, 1.0), place(s1, 0.0), place(s2, 0.0)


def _rope(blk, cos, s1, s2):
    return blk * cos + pltpu.roll(blk, 8, 1) * s2 + pltpu.roll(blk, HEAD_PAD - 8, 1) * s1


def _head_norm(blk, g):
    ms = jnp.sum(blk * blk, axis=-1, keepdims=True) * (1.0 / QK_HEAD)
    return blk * lax.rsqrt(ms + EPS) * g


def _even_proj_kernel(xp_ref, xs_ref, mod_ref, nm_ref, wa_ref, wb_ref, qan_ref, wqb_ref, kvan_ref,
                      qn_ref, wsc_ref, cos_ref, s1_ref, s2_ref, q_ref, ckv_ref, kpe_ref, sc_ref):
    i = pl.program_id(0)
    x = jnp.where(i < CTX_TILES, xp_ref[...], xs_ref[...])
    m = mod_ref[0, 0]
    h = _rms(x, nm_ref[...]) * (1.0 + m[:, D:2 * D]) + m[:, 0:D]
    hb = h.astype(BF16)

    za = _dot(hb, wa_ref[...])
    ckv_ref[...] = _rms(za[:, Q_LORA:Q_LORA + KV_LORA], kvan_ref[...])
    kpe_ref[...] = za[:, Q_LORA + KV_LORA:]
    qa = _rms(za[:, :Q_LORA], qan_ref[...]).astype(BF16)
    cos, s1, s2 = cos_ref[...], s1_ref[...], s2_ref[...]
    qn = qn_ref[...]
    scale = QK_HEAD ** -0.5
    for hp in range(HEADS // 2):
        qq = _dot(qa, wqb_ref[:, hp * 256:(hp + 1) * 256])
        for j in range(2):
            blk = _head_norm(qq[:, j * HEAD_PAD:(j + 1) * HEAD_PAD], qn)
            blk = _rope(blk, cos, s1, s2) * scale
            h0 = (2 * hp + j) * HEAD_PAD
            q_ref[:, h0:h0 + HEAD_PAD] = blk.astype(BF16)

    gb = _dot(hb, wb_ref[:, 0:SC_W])
    v = _dot(hb, wb_ref[:, SC_W:2 * SC_W]) * _dot(hb, wb_ref[:, 2 * SC_W:3 * SC_W])
    seq = jnp.where(i < CTX_TILES, CTX_LEN, LAT_LEN)
    r = lax.broadcasted_iota(jnp.int32, (TM, 1), 0) & (seq - 1)
    vp = jnp.where(r == 0, 0.0, pltpu.roll(v, 1, 0))
    vn = jnp.where(r == seq - 1, 0.0, pltpu.roll(v, TM - 1, 0))
    w = wsc_ref[...]
    y = w[0:1] * vp + w[1:2] * v + w[2:3] * vn
    sc_ref[...] = (gb * y).astype(BF16)


def _even_proj(xp, xs, mods, norm_mix0, wa, wb, q_a_norm, wqb, kv_a_norm, qn, w_sc, tabs):
    full = lambda shape: pl.BlockSpec(shape, lambda i: (0,) * len(shape))
    tab = pl.BlockSpec((TM, HEAD_PAD), lambda i: (jnp.where(i < CTX_TILES, 1, 0), 0))
    row = lambda n: pl.BlockSpec((TM, n), lambda i: (i, 0))
    return pl.pallas_call(
        _even_proj_kernel,
        grid=(N_TILES,),
        in_specs=[
            pl.BlockSpec((TM, D), lambda i: (jnp.minimum(i, CTX_TILES - 1), 0)),
            pl.BlockSpec((TM, D), lambda i: (jnp.maximum(i - CTX_TILES, 0), 0)),
            pl.BlockSpec((1, 1, 1, 6 * D), lambda i: (0, _cond_tile(i), 0, 0)),
            full((1, D)), full((D, 512)), full((D, 3 * SC_W)), full((1, Q_LORA)),
            full((Q_LORA, HEADS * HEAD_PAD)), full((1, KV_LORA)), full((1, HEAD_PAD)),
            full((3, SC_W)), tab, tab, tab,
        ],
        out_specs=[row(HEADS * HEAD_PAD), row(KV_LORA), row(HEAD_PAD), row(SC_W)],
        out_shape=[
            jax.ShapeDtypeStruct((T, HEADS * HEAD_PAD), BF16),
            jax.ShapeDtypeStruct((T, KV_LORA), F32),
            jax.ShapeDtypeStruct((T, HEAD_PAD), F32),
            jax.ShapeDtypeStruct((T, SC_W), BF16),
        ],
        compiler_params=_params("arbitrary"),
        name="even_proj",
    )(xp, xs, mods, norm_mix0, wa, wb, q_a_norm, wqb, kv_a_norm, qn, w_sc, *tabs)


def _kv_proj_kernel(ckv_ref, kpe_ref, wkvb_ref, kn_ref, cos_ref, s1_ref, s2_ref, k_ref, kv_ref):
    kv = _dot(ckv_ref[...].astype(BF16), wkvb_ref[...])
    kv_ref[...] = kv.astype(BF16)
    kpe = pltpu.roll(kpe_ref[...], QK_NOPE, 1)
    lane = lax.broadcasted_iota(jnp.int32, (1, HEAD_PAD), 1)
    cos, s1, s2 = cos_ref[...], s1_ref[...], s2_ref[...]
    kn = kn_ref[...]
    for h in range(HEADS):
        blk = jnp.where(lane < QK_NOPE, kv[:, h * HEAD_PAD:(h + 1) * HEAD_PAD], kpe)
        blk = _rope(_head_norm(blk, kn), cos, s1, s2)
        k_ref[:, h * HEAD_PAD:(h + 1) * HEAD_PAD] = blk.astype(BF16)


def _kv_proj(ckv, kpe, wkvb, kn, tabs, tab_index, name):
    n = ckv.shape[0]
    full = lambda shape: pl.BlockSpec(shape, lambda i: (0,) * len(shape))
    tab = pl.BlockSpec((TKV, HEAD_PAD), lambda i: (tab_index(i), 0))
    row = lambda w: pl.BlockSpec((TKV, w), lambda i: (i, 0))
    return pl.pallas_call(
        _kv_proj_kernel,
        grid=(n // TKV,),
        in_specs=[row(KV_LORA), row(HEAD_PAD), full((KV_LORA, HEADS * HEAD_PAD)), full((1, HEAD_PAD)),
                  tab, tab, tab],
        out_specs=[row(HEADS * HEAD_PAD), row(HEADS * HEAD_PAD)],
        out_shape=[jax.ShapeDtypeStruct((n, HEADS * HEAD_PAD), BF16)] * 2,
        compiler_params=_params("arbitrary"),
        name=name,
    )(ckv, kpe, wkvb, kn, *tabs)


def _pair_out(o0, o1):
    lane = lax.broadcasted_iota(jnp.int32, (1, HEAD_PAD), 1)
    return jnp.where(lane < V_HEAD, pltpu.roll(o0, V_HEAD, 1), o1).astype(BF16)


def _attn_ctx_kernel(q_ref, k_ref, kv_ref, o_ref):
    for hp in range(HEADS // 2):
        outs = []
        for j in range(2):
            h0 = (2 * hp + j) * HEAD_PAD
            s = _dot_nt(q_ref[:, h0:h0 + HEAD_PAD], k_ref[:, h0:h0 + HEAD_PAD])
            p = jnp.exp(s - jnp.max(s, axis=-1, keepdims=True))
            l = jnp.sum(p, axis=-1, keepdims=True)
            outs.append(_dot(p.astype(BF16), kv_ref[:, h0:h0 + HEAD_PAD]) / l)
        o_ref[:, hp * HEAD_PAD:(hp + 1) * HEAD_PAD] = _pair_out(*outs)


def _attn_ctx(q, k, kv):
    blk = pl.BlockSpec((CTX_LEN, HEADS * HEAD_PAD), lambda b: (b, 0))
    return pl.pallas_call(
        _attn_ctx_kernel,
        grid=(N_CTX_SEQ,),
        in_specs=[blk, blk, blk],
        out_specs=pl.BlockSpec((CTX_LEN, HEADS * V_HEAD), lambda b: (b, 0)),
        out_shape=jax.ShapeDtypeStruct((T_CTX, HEADS * V_HEAD), BF16),
        compiler_params=_params("arbitrary"),
        name="attn_ctx",
    )(q, k, kv)


def _attn_lat_kernel(q_ref, kc_ref, kvc_ref, kl_ref, kvl_ref, o_ref):
    outs = []
    for j in range(2):
        lanes = slice(j * HEAD_PAD, (j + 1) * HEAD_PAD)
        q = q_ref[:, lanes]
        sc = _dot_nt(q, kc_ref[:, lanes])
        sl = _dot_nt(q, kl_ref[:, lanes])
        m = jnp.maximum(jnp.max(sc, axis=-1, keepdims=True), jnp.max(sl, axis=-1, keepdims=True))
        pc, pl_ = jnp.exp(sc - m), jnp.exp(sl - m)
        l = jnp.sum(pc, axis=-1, keepdims=True) + jnp.sum(pl_, axis=-1, keepdims=True)
        o = _dot(pc.astype(BF16), kvc_ref[:, lanes]) + _dot(pl_.astype(BF16), kvl_ref[:, lanes])
        outs.append(o / l)
    o_ref[...] = _pair_out(*outs)


def _attn_lat(q, kc, kvc, k, kv):
    nq = LAT_LEN // TQ
    q0 = T_CTX // TQ
    kl0 = T_CTX // LAT_LEN
    pair = 2 * HEAD_PAD
    lat = pl.BlockSpec((LAT_LEN, pair), lambda b, hp, t: (kl0 + b, hp))
    ctx = pl.BlockSpec((PAST, pair), lambda b, hp, t: (b, hp))
    return pl.pallas_call(
        _attn_lat_kernel,
        grid=(N_LAT_SEQ, HEADS // 2, nq),
        in_specs=[pl.BlockSpec((TQ, pair), lambda b, hp, t: (q0 + b * nq + t, hp)), ctx, ctx, lat, lat],
        out_specs=pl.BlockSpec((TQ, HEAD_PAD), lambda b, hp, t: (b * nq + t, hp)),
        out_shape=jax.ShapeDtypeStruct((T_LAT, HEADS * V_HEAD), BF16),
        compiler_params=_params("arbitrary", "arbitrary", "arbitrary"),
        name="attn_lat",
    )(q, kc, kvc, k, kv)


def _even_out_kernel(oc_ref, ol_ref, sc_ref, xp_ref, xs_ref, mod_ref, wo_ref, nf_ref, x1_ref, h_ref):
    i = pl.program_id(0)
    ctx = i < CTX_TILES
    attn = jnp.where(ctx, oc_ref[...], ol_ref[...])
    x = jnp.where(ctx, xp_ref[...], xs_ref[...])
    m = mod_ref[0, 0]
    out = _dot(attn, wo_ref[0:HEADS * V_HEAD, :]) + _dot(sc_ref[...], wo_ref[HEADS * V_HEAD:, :])
    x1 = x + m[:, 2 * D:3 * D] * out
    x1_ref[...] = x1
    h_ref[...] = (_rms(x1, nf_ref[...]) * (1.0 + m[:, 4 * D:5 * D]) + m[:, 3 * D:4 * D]).astype(BF16)


def _even_out(oc, ol, sc, xp, xs, mods, wo, norm_ffn0):
    full = lambda shape: pl.BlockSpec(shape, lambda i: (0,) * len(shape))
    first = lambda n: pl.BlockSpec((TM, n), lambda i: (jnp.minimum(i, CTX_TILES - 1), 0))
    second = lambda n: pl.BlockSpec((TM, n), lambda i: (jnp.maximum(i - CTX_TILES, 0), 0))
    row = lambda n: pl.BlockSpec((TM, n), lambda i: (i, 0))
    return pl.pallas_call(
        _even_out_kernel,
        grid=(N_TILES,),
        in_specs=[first(HEADS * V_HEAD), second(HEADS * V_HEAD), row(SC_W), first(D), second(D),
                  pl.BlockSpec((1, 1, 1, 6 * D), lambda i: (0, _cond_tile(i), 0, 0)),
                  full((HEADS * V_HEAD + SC_W, D)), full((1, D))],
        out_specs=[row(D), row(D)],
        out_shape=[jax.ShapeDtypeStruct((T, D), F32), jax.ShapeDtypeStruct((T, D), BF16)],
        compiler_params=_params("arbitrary"),
        name="even_out",
    )(oc, ol, sc, xp, xs, mods, wo, norm_ffn0)


def _ffn_kernel(h_ref, wg_ref, wu_ref, wd_ref, x1_ref, mod0_ref, mod1_ref, nm_ref, x2_ref, h3_ref, acc_ref):
    c = pl.program_id(1)

    @pl.when(c == 0)
    def _():
        acc_ref[...] = jnp.zeros_like(acc_ref)

    h = h_ref[...]
    g = _dot(h, wg_ref[...].astype(BF16))
    u = _dot(h, wu_ref[...].astype(BF16))
    acc_ref[...] += _dot((_silu(g) * u).astype(BF16), wd_ref[...].astype(BF16))

    @pl.when(c == pl.num_programs(1) - 1)
    def _():
        m0, m1 = mod0_ref[0, 0], mod1_ref[0, 0]
        x2 = x1_ref[...] + m0[:, 5 * D:6 * D] * acc_ref[...]
        x2_ref[...] = x2
        h3_ref[...] = (_rms(x2, nm_ref[...]) * (1.0 + m1[:, D:2 * D]) + m1[:, 0:D]).astype(BF16)


def _ffn(h, w_gu, w_down, x1, mods, norm_mix1):
    nc = D_FF // FFN_FC
    row = lambda n: pl.BlockSpec((TM, n), lambda i, c: (i, 0))
    mod = lambda l: pl.BlockSpec((1, 1, 1, 6 * D), lambda i, c: (l, _cond_tile(i), 0, 0))
    return pl.pallas_call(
        _ffn_kernel,
        grid=(N_TILES, nc),
        in_specs=[row(D),
                  pl.BlockSpec((D, FFN_FC), lambda i, c: (0, c)),
                  pl.BlockSpec((D, FFN_FC), lambda i, c: (0, nc + c)),
                  pl.BlockSpec((FFN_FC, D), lambda i, c: (c, 0)),
                  row(D), mod(0), mod(1), pl.BlockSpec((1, D), lambda i, c: (0, 0))],
        out_specs=[row(D), row(D)],
        out_shape=[jax.ShapeDtypeStruct((T, D), F32), jax.ShapeDtypeStruct((T, D), BF16)],
        scratch_shapes=[pltpu.VMEM((TM, D), F32)],
        compiler_params=_params("arbitrary", "arbitrary"),
        name="ffn_dense",
    )(h, w_gu, w_gu, w_down, x1, mods, mods, norm_mix1)


CONF_CB = 256
CONF_SEG = 256
CONF_HALO = 16
CONF_SEGP = CONF_SEG + 2 * CONF_HALO
CONF_PIECE = 64


def _conf_kernel(h_ref, x2_ref, w1_ref, b1_ref, wdw_ref, bdw_ref, lng_ref, lnb_ref, w2_ref, b2_ref,
                 mod_ref, nf_ref, rt_ref, x3_ref, h4_ref, lg_ref, pad_ref, conv_ref):
    i = pl.program_id(0)
    nseg = TM // CONF_SEG
    h = h_ref[...]
    joined = jnp.where(i < CTX_TILES, 0.0, 1.0)
    zeros_halo = jnp.zeros((CONF_HALO, CONF_CB), F32)
    for cb in range(D // CONF_CB):
        c0 = cb * CONF_CB
        a = _dot(h, w1_ref[:, c0:c0 + CONF_CB]) + b1_ref[:, c0:c0 + CONF_CB]
        g = _dot(h, w1_ref[:, D + c0:D + c0 + CONF_CB]) + b1_ref[:, D + c0:D + c0 + CONF_CB]
        u = a * jax.nn.sigmoid(g)
        for s in range(nseg):
            base = s * CONF_SEGP
            top = u[s * CONF_SEG - CONF_HALO:s * CONF_SEG] * joined if s > 0 else zeros_halo
            bot = (u[(s + 1) * CONF_SEG:(s + 1) * CONF_SEG + CONF_HALO] * joined
                   if s < nseg - 1 else zeros_halo)
            pad_ref[0, base:base + CONF_HALO, :] = top
            pad_ref[0, base + CONF_HALO:base + CONF_HALO + CONF_SEG, :] = u[s * CONF_SEG:(s + 1) * CONF_SEG]
            pad_ref[0, base + CONF_HALO + CONF_SEG:base + CONF_SEGP, :] = bot
        p0 = pad_ref[0]
        rows = nseg * CONF_SEGP
        for b in range(1, 8):
            pad_ref[b] = pltpu.roll(p0, rows - b, 0)

        def piece(t, carry):
            s = t // (CONF_SEG // CONF_PIECE)
            q0 = (t % (CONF_SEG // CONF_PIECE)) * CONF_PIECE
            src = pl.multiple_of(s * CONF_SEGP + q0, 8)
            acc = jnp.zeros((CONF_PIECE, CONF_CB), F32)
            for j in range(CONF_K):
                hi, lo = (j + 1) // 8, (j + 1) % 8
                acc = acc + wdw_ref[j:j + 1, c0:c0 + CONF_CB] * pad_ref[lo, pl.ds(src + 8 * hi, CONF_PIECE), :]
            dst = pl.multiple_of(s * CONF_SEG + q0, 8)
            conv_ref[pl.ds(dst, CONF_PIECE), c0:c0 + CONF_CB] = acc + bdw_ref[:, c0:c0 + CONF_CB]
            return carry

        lax.fori_loop(0, TM // CONF_PIECE, piece, 0)

    y = conv_ref[...]
    mu = jnp.mean(y, axis=-1, keepdims=True)
    yc = y - mu
    var = jnp.mean(yc * yc, axis=-1, keepdims=True)
    y = _silu(yc * lax.rsqrt(var + EPS) * lng_ref[...] + lnb_ref[...])
    out = _dot(y.astype(BF16), w2_ref[...]) + b2_ref[...]
    m = mod_ref[0, 0]
    x3 = x2_ref[...] + m[:, 2 * D:3 * D] * out
    x3_ref[...] = x3
    h4 = _rms(x3, nf_ref[...]) * (1.0 + m[:, 4 * D:5 * D]) + m[:, 3 * D:4 * D]
    h4_ref[...] = h4.astype(BF16)
    lg_ref[...] = lax.dot_general(rt_ref[...], h4, (((1,), (1,)), ((), ())),
                                  precision=lax.Precision.HIGHEST, preferred_element_type=F32)


def _conf(h3, x2, w1, b1, wdw, bdw, lng, lnb, w2, b2, mods, norm_ffn1, router_t):
    full = lambda shape: pl.BlockSpec(shape, lambda i: (0,) * len(shape))
    row = lambda n: pl.BlockSpec((TM, n), lambda i: (i, 0))
    return pl.pallas_call(
        _conf_kernel,
        grid=(N_TILES,),
        in_specs=[row(D), row(D), full((D, 2 * D)), full((1, 2 * D)), full((CONF_K, D)), full((1, D)),
                  full((1, D)), full((1, D)), full((D, D)), full((1, D)),
                  pl.BlockSpec((1, 1, 1, 6 * D), lambda i: (1, _cond_tile(i), 0, 0)),
                  full((1, D)), full((N_EXP, D))],
        out_specs=[row(D), row(D), pl.BlockSpec((N_EXP, TM), lambda i: (0, i))],
        out_shape=[jax.ShapeDtypeStruct((T, D), F32), jax.ShapeDtypeStruct((T, D), BF16),
                   jax.ShapeDtypeStruct((N_EXP, T), F32)],
        scratch_shapes=[pltpu.VMEM((8, (TM // CONF_SEG) * CONF_SEGP, CONF_CB), F32),
                        pltpu.VMEM((TM, D), F32)],
        compiler_params=_params("arbitrary"),
        name="conformer_conv",
    )(h3, x2, w1, b1, wdw, bdw, lng, lnb, w2, b2, mods, norm_ffn1, router_t)


TB = 256
N_TB = T // TB
SUBS = 8
SM = SUBS * TB
N_SUB_MAX = 2 * T // TB + N_EXP
N_SUP_MAX = N_SUB_MAX // SUBS + N_EXP - 1
YS_ROWS = (N_SUB_MAX + 2) * TB
WIN_ALIGN = 16
WIN = TB + WIN_ALIGN
FIRST_STRIDE = 32


def _route_kernel(lg_ref, g_ref, rank_ref, first_ref):
    lg = lg_ref[...]
    idx = lax.broadcasted_iota(jnp.int32, lg.shape, 0).astype(F32)
    none = float(N_EXP)
    m1 = jnp.max(lg, axis=0, keepdims=True)
    i1 = jnp.min(jnp.where(lg == m1, idx, none), axis=0, keepdims=True)
    rest = jnp.where(idx == i1, -jnp.inf, lg)
    m2 = jnp.max(rest, axis=0, keepdims=True)
    i2 = jnp.min(jnp.where(rest == m2, idx, none), axis=0, keepdims=True)
    e = jnp.exp(m2 - m1)
    w1 = 1.0 / (1.0 + e)
    w2 = e / (1.0 + e)
    g_ref[...] = jnp.where(idx == i1, w1, 0.0) + jnp.where(idx == i2, w2, 0.0)

    mask = jnp.where(idx == i1, 1.0, 0.0) + jnp.where(idx == i2, 1.0, 0.0)
    before = (lax.broadcasted_iota(jnp.int32, (TB, TB), 0) < lax.broadcasted_iota(jnp.int32, (TB, TB), 1))
    before = jnp.where(before, 1.0, 0.0).astype(BF16)
    lane = lax.broadcasted_iota(jnp.int32, (N_EXP, 128), 1)
    carry = jnp.zeros((N_EXP, 1), F32)
    first = jnp.zeros((N_EXP, 128), F32)
    for b in range(N_TB):
        mb = mask[:, b * TB:(b + 1) * TB]
        local = _dot(mb.astype(BF16), before)
        rank_ref[:, b * TB:(b + 1) * TB] = jnp.where(mb > 0.0, local + carry, -1.0)
        first = jnp.where(lane == b, carry, first)
        carry = carry + jnp.sum(mb, axis=1, keepdims=True)
    first_ref[...] = jnp.where(lane == N_TB, carry, first)


def _route(logits_t):
    return pl.pallas_call(
        _route_kernel,
        out_shape=[jax.ShapeDtypeStruct((N_EXP, T), F32), jax.ShapeDtypeStruct((N_EXP, T), F32),
                   jax.ShapeDtypeStruct((N_EXP, 128), F32)],
        compiler_params=pltpu.CompilerParams(vmem_limit_bytes=VMEM_LIMIT),
        name="route",
    )(logits_t)


def _moe_plan(first):
    first = first[:, :FIRST_STRIDE].astype(jnp.int32)
    cnt = first[:, N_TB]
    nt = (cnt + TB - 1) // TB
    off = jnp.cumsum(nt) - nt
    nsub = jnp.sum(nt)
    nsup = (nt + SUBS - 1) // SUBS
    sup_end = jnp.cumsum(nsup)
    s = jnp.arange(N_SUP_MAX)
    valid = s < sup_end[-1]
    last = jnp.maximum(sup_end[-1] - 1, 0)
    se = jnp.minimum(jnp.searchsorted(sup_end, jnp.where(valid, s, last), side="right"), N_EXP - 1)
    sk0 = (jnp.where(valid, s, last) - (sup_end - nsup)[se]) * SUBS
    sns = jnp.where(valid, jnp.clip(nt[se] - sk0, 0, SUBS), 0)
    srow = TB * (off[se] + sk0)
    start = TB * off[:, None] + first[:, :N_TB]
    win = start - start % WIN_ALIGN
    rel = first[:, :N_TB] - start % WIN_ALIGN
    i32 = lambda a: a.astype(jnp.int32)
    return dict(se=i32(se), sk0=i32(sk0), sns=i32(sns), srow=i32(srow), nsub=i32(nsub).reshape(1),
                first=i32(first.reshape(-1)), win=i32(win.T.reshape(-1)), rel=i32(rel.T.reshape(-1)))


def _moe_gmm_kernel(se_ref, sk0_ref, sns_ref, srow_ref, nsub_ref, first_ref,
                    x_ref, rank_ref, gate_ref, wg_ref, wu_ref, wd_ref, ys_ref,
                    xs_ref, gs_ref, yacc_ref, ybuf_ref, acc_ref, gacc_ref, wgb_ref, wub_ref, wdb_ref, sem):
    s, c = pl.program_id(0), pl.program_id(1)
    nc = pl.num_programs(1)
    e, k0, ns = se_ref[s], sk0_ref[s], sns_ref[s]

    def sub_rows(k):
        return pl.ds(pl.multiple_of(k * TB, TB), TB)

    def out_copy(k, row0):
        dst = ys_ref.at[pl.ds(pl.multiple_of(row0 + k * TB, TB), TB)]
        return pltpu.make_async_copy(ybuf_ref.at[sub_rows(k)], dst, sem.at[k])

    @pl.when((ns > 0) & (c == 0))
    def _gather():
        def sub(k, carry):
            base = (k0 + k) * TB
            slot = (lax.broadcasted_iota(jnp.int32, (TB, 1), 0) + base).astype(F32)
            acc_ref[...] = jnp.zeros_like(acc_ref)
            gacc_ref[...] = jnp.zeros_like(gacc_ref)

            def blk(b, carry2):
                f0 = first_ref[e * FIRST_STRIDE + b]
                f1 = first_ref[e * FIRST_STRIDE + b + 1]

                @pl.when((f1 > f0) & (f0 < base + TB) & (f1 > base))
                def _():
                    t0 = pl.multiple_of(b * TB, TB)
                    hit = rank_ref[pl.ds(e, 1), pl.ds(t0, TB)] == slot
                    onehot = jnp.where(hit, 1.0, 0.0).astype(BF16)
                    acc_ref[...] += _dot(onehot, x_ref[pl.ds(t0, TB), :])
                    gacc_ref[...] += jnp.sum(jnp.where(hit, gate_ref[pl.ds(e, 1), pl.ds(t0, TB)], 0.0),
                                             axis=-1, keepdims=True)
                return carry2

            lax.fori_loop(0, N_TB, blk, 0)
            xs_ref[sub_rows(k), :] = acc_ref[...].astype(BF16)
            gs_ref[sub_rows(k), :] = gacc_ref[...]
            yacc_ref[sub_rows(k), :] = jnp.zeros((TB, D), F32)
            return carry

        lax.fori_loop(0, ns, sub, 0)

    @pl.when(ns > 0)
    def _compute():
        wgb_ref[...] = wg_ref[0].astype(BF16)
        wub_ref[...] = wu_ref[0].astype(BF16)
        wdb_ref[...] = wd_ref[0].astype(BF16)

        def sub(k, carry):
            x = xs_ref[sub_rows(k), :]
            g = _dot(x, wgb_ref[...])
            u = _dot(x, wub_ref[...])
            yacc_ref[sub_rows(k), :] += _dot((_silu(g) * u).astype(BF16), wdb_ref[...])
            return carry

        lax.fori_loop(0, ns, sub, 0)

    @pl.when((ns > 0) & (c == nc - 1))
    def _store():
        row0 = srow_ref[s]

        def put(k, carry):
            ybuf_ref[sub_rows(k), :] = (yacc_ref[sub_rows(k), :] * gs_ref[sub_rows(k), :]).astype(BF16)
            out_copy(k, row0).start()
            return carry

        def done(k, carry):
            out_copy(k, row0).wait()
            return carry

        lax.fori_loop(0, ns, put, 0)
        lax.fori_loop(0, ns, done, 0)

    @pl.when((s == pl.num_programs(0) - 1) & (c == nc - 1))
    def _zero_tail():
        ybuf_ref[0:TB, :] = jnp.zeros((TB, D), BF16)
        nsub = nsub_ref[0]

        def fill(k, carry):
            cp = out_copy(0, (nsub + k) * TB)
            cp.start()
            cp.wait()
            return carry

        lax.fori_loop(0, YS_ROWS // TB - nsub, fill, 0)


def _moe_gmm(plan, h4, rank, gates, w_gu, w_down):
    nc = D_FFE // MOE_FC

    def chunk(s, c, sns):
        return jnp.where(sns[s] > 0, c, nc - 1)

    return pl.pallas_call(
        _moe_gmm_kernel,
        grid_spec=pltpu.PrefetchScalarGridSpec(
            num_scalar_prefetch=6,
            grid=(N_SUP_MAX, nc),
            in_specs=[
                pl.BlockSpec((T, D), lambda s, c, *_: (0, 0), pipeline_mode=pl.Buffered(1)),
                pl.BlockSpec((N_EXP, T), lambda s, c, *_: (0, 0)),
                pl.BlockSpec((N_EXP, T), lambda s, c, *_: (0, 0)),
                pl.BlockSpec((1, D, MOE_FC), lambda s, c, se, sk0, sns, *_: (se[s], 0, chunk(s, c, sns))),
                pl.BlockSpec((1, D, MOE_FC), lambda s, c, se, sk0, sns, *_: (se[s], 0, nc + chunk(s, c, sns))),
                pl.BlockSpec((1, MOE_FC, D), lambda s, c, se, sk0, sns, *_: (se[s], chunk(s, c, sns), 0)),
            ],
            out_specs=pl.BlockSpec(memory_space=pl.ANY),
            scratch_shapes=[
                pltpu.VMEM((SM, D), BF16), pltpu.VMEM((SM, 1), F32), pltpu.VMEM((SM, D), F32),
                pltpu.VMEM((SM, D), BF16), pltpu.VMEM((TB, D), F32), pltpu.VMEM((TB, 1), F32),
                pltpu.VMEM((D, MOE_FC), BF16), pltpu.VMEM((D, MOE_FC), BF16), pltpu.VMEM((MOE_FC, D), BF16),
                pltpu.SemaphoreType.DMA((SUBS,)),
            ],
        ),
        out_shape=jax.ShapeDtypeStruct((YS_ROWS, D), BF16),
        compiler_params=_params("arbitrary", "arbitrary"),
        name="moe_gmm",
    )(plan["se"], plan["sk0"], plan["sns"], plan["srow"], plan["nsub"], plan["first"],
      h4, rank, gates, w_gu, w_gu, w_down)


def _moe_combine_kernel(win_ref, rel_ref, *refs):
    y_refs = refs[:N_EXP]
    rank_ref, x3_ref, mod_ref, yp_ref, ys_ref = refs[N_EXP:]
    b = pl.program_id(0)
    acc = jnp.zeros((TB, D), F32)
    row = lax.broadcasted_iota(jnp.int32, (WIN, 1), 0)
    for e in range(N_EXP):
        slot = (row + rel_ref[b * N_EXP + e]).astype(F32)
        onehot = jnp.where(rank_ref[e:e + 1, :] == slot, 1.0, 0.0).astype(BF16)
        acc = acc + lax.dot_general(onehot, y_refs[e][...], (((0,), (0,)), ((), ())),
                                    preferred_element_type=F32)
    out = x3_ref[...] + mod_ref[0, 0][:, 5 * D:6 * D] * acc

    @pl.when(b < T_CTX // TB)
    def _():
        yp_ref[...] = out

    @pl.when(b >= T_CTX // TB)
    def _():
        ys_ref[...] = out


def _moe_combine(plan, ysorted, rank, x3, mods):
    ctx_blocks = T_CTX // TB

    def window(e):
        return pl.BlockSpec((pl.Element(WIN), pl.Element(D)),
                            lambda b, win, rel: (pl.multiple_of(win[b * N_EXP + e], WIN_ALIGN), 0))

    def cond(b):
        return jnp.maximum((b - (ctx_blocks - LAT_LEN // TB)) // (LAT_LEN // TB), 0)

    return pl.pallas_call(
        _moe_combine_kernel,
        grid_spec=pltpu.PrefetchScalarGridSpec(
            num_scalar_prefetch=2,
            grid=(N_TB,),
            in_specs=[window(e) for e in range(N_EXP)] + [
                pl.BlockSpec((N_EXP, TB), lambda b, win, rel: (0, b)),
                pl.BlockSpec((TB, D), lambda b, win, rel: (b, 0)),
                pl.BlockSpec((1, 1, 1, 6 * D), lambda b, win, rel: (1, cond(b), 0, 0)),
            ],
            out_specs=[pl.BlockSpec((TB, D), lambda b, win, rel: (jnp.minimum(b, ctx_blocks - 1), 0)),
                       pl.BlockSpec((TB, D), lambda b, win, rel: (jnp.maximum(b - ctx_blocks, 0), 0))],
        ),
        out_shape=[jax.ShapeDtypeStruct((T_CTX, D), F32), jax.ShapeDtypeStruct((T_LAT, D), F32)],
        compiler_params=_params("arbitrary"),
        name="moe_combine",
    )(plan["win"], plan["rel"], *([ysorted] * N_EXP), rank, x3, mods)


def _pad_heads(w, width):
    lead = w.shape[:-1]
    w = w.reshape(*lead, HEADS, width)
    w = jnp.pad(w, [(0, 0)] * len(lead) + [(0, 0), (0, HEAD_PAD - width)])
    return w.reshape(*lead, HEADS * HEAD_PAD)


def kernel(x_prompt, x_sample, cache_ckv, cache_kpe, c, c_ctx, ada_w, ada_b, norm_mix, norm_ffn, w_in, q_a_norm,
           w_qb, kv_a_norm, w_kvb, q_norm, k_norm, w_sc, w_o, ffn_gu, ffn_down, conv_pw1, conv_pw1_b, conv_dw,
           conv_dw_b, conv_ln_g, conv_ln_b, conv_pw2, conv_pw2_b, router, moe_gu, moe_down):
    xp = x_prompt.reshape(T_CTX, D)
    xs = x_sample.reshape(T_LAT, D)

    conds = jnp.zeros((8, D), F32).at[0].set(c_ctx).at[1:1 + N_LAT_SEQ].set(c)
    mods = _adaln(conds, ada_w, ada_b).reshape(2, 8, 1, 6 * D)

    n_a = Q_LORA + KV_LORA + QK_ROPE
    wa = jnp.pad(w_in[0, :, :n_a], ((0, 0), (0, 512 - n_a))).astype(BF16)
    wb = w_in[0, :, n_a:].astype(BF16)
    wqb = _pad_heads(w_qb[0], QK_HEAD).astype(BF16)
    wkvb = w_kvb[0].astype(BF16)
    qn = jnp.pad(q_norm[0], (0, HEAD_PAD - QK_HEAD)).reshape(1, HEAD_PAD)
    kn = jnp.pad(k_norm[0], (0, HEAD_PAD - QK_HEAD)).reshape(1, HEAD_PAD)
    tabs = _rope_tables()

    q, ckv, kpe, sc = _even_proj(xp, xs, mods, norm_mix[0:1], wa, wb, q_a_norm, wqb, kv_a_norm, qn, w_sc[0], tabs)

    lat_tile0 = T_CTX // TKV
    ident = LAT_LEN // TKV
    k, kv = _kv_proj(ckv, kpe, wkvb, kn, tabs,
                     lambda i: jnp.where(i < lat_tile0, ident, (i - lat_tile0) % ident), "kv_proj")
    cache_kpe_p = jnp.pad(cache_kpe[:, 0].reshape(N_LAT_SEQ * PAST, QK_ROPE), ((0, 0), (0, HEAD_PAD - QK_ROPE)))
    kc, kvc = _kv_proj(cache_ckv[:, 0].reshape(N_LAT_SEQ * PAST, KV_LORA), cache_kpe_p, wkvb, kn, tabs,
                       lambda i: ident, "kv_proj_cache")

    oc = _attn_ctx(q, k, kv)
    ol = _attn_lat(q, kc, kvc, k, kv)
    x1, h2 = _even_out(oc, ol, sc, xp, xs, mods, w_o[0].astype(BF16), norm_ffn[0:1])
    x2, h3 = _ffn(h2, ffn_gu[0], ffn_down[0], x1, mods, norm_mix[1:2])

    x3, h4, logits_t = _conf(h3, x2, conv_pw1[0].astype(BF16), conv_pw1_b, conv_dw[0], conv_dw_b, conv_ln_g,
                             conv_ln_b, conv_pw2[0].astype(BF16), conv_pw2_b, mods, norm_ffn[1:2], router[0].T)
    gates, rank, first = _route(logits_t)
    plan = _moe_plan(first)
    ysorted = _moe_gmm(plan, h4, rank, gates, moe_gu[0], moe_down[0])
    yp, ys = _moe_combine(plan, ysorted, rank, x3, mods)

    state_ckv = ckv[:T_CTX].reshape(N_CTX_SEQ, 1, CTX_LEN, KV_LORA)
    state_kpe = kpe[:T_CTX, :QK_ROPE].reshape(N_CTX_SEQ, 1, CTX_LEN, QK_ROPE)
    return (yp.reshape(N_CTX_SEQ, CTX_LEN, D), ys.reshape(N_LAT_SEQ, LAT_LEN, D), state_ckv, state_kpe)
```

```python
import functools

import jax
import jax.numpy as jnp
import numpy as np
from jax import lax
from jax.experimental import pallas as pl
from jax.experimental.pallas import tpu as pltpu

F32 = jnp.float32
BF16 = jnp.bfloat16

D = 1024
N_CTX_SEQ, CTX_LEN = 16, 256
N_LAT_SEQ, LAT_LEN = 2, 1024
T_CTX = N_CTX_SEQ * CTX_LEN
T_LAT = N_LAT_SEQ * LAT_LEN
T = T_CTX + T_LAT
PAST = 256
GRID_W = 64
HEADS = 8
QK_NOPE, QK_ROPE, V_HEAD = 64, 32, 64
QK_HEAD = QK_NOPE + QK_ROPE
HEAD_PAD = 128
Q_LORA, KV_LORA = 256, 128
SC_W = 512
CONF_K = 31
D_FF = 2816
N_EXP = 8
D_FFE = 3584
EPS = 1e-6
ROPE_THETA = 10000.0

TM = 1024
N_TILES = T // TM
CTX_TILES = T_CTX // TM
TKV = 512
TQ = 256
FFN_FC = 256
MOE_FC = 512
VMEM_LIMIT = 56 * 1024 * 1024


def _dot(a, b):
    return jnp.dot(a, b, preferred_element_type=F32)


def _dot_nt(a, b):
    return lax.dot_general(a, b, (((1,), (1,)), ((), ())), preferred_element_type=F32)


def _rms(x, g):
    return x * lax.rsqrt(jnp.mean(x * x, axis=-1, keepdims=True) + EPS) * g


def _silu(x):
    return x * jax.nn.sigmoid(x)


def _params(*sem):
    return pltpu.CompilerParams(dimension_semantics=sem, vmem_limit_bytes=VMEM_LIMIT)


def _cond_tile(i):
    return jnp.maximum(i - (CTX_TILES - 1), 0)


def _adaln_kernel(c_ref, w_ref, b_ref, o_ref):
    s = _silu(c_ref[...]).astype(BF16)
    o_ref[0] = _dot(s, w_ref[0].astype(BF16)) + b_ref[0]


def _adaln(conds, ada_w, ada_b):
    depth = ada_w.shape[0]
    tn = 1024
    return pl.pallas_call(
        _adaln_kernel,
        grid=(depth, 6 * D // tn),
        in_specs=[
            pl.BlockSpec((8, D), lambda l, j: (0, 0)),
            pl.BlockSpec((1, D, tn), lambda l, j: (l, 0, j)),
            pl.BlockSpec((1, 1, tn), lambda l, j: (l, 0, j)),
        ],
        out_specs=pl.BlockSpec((1, 8, tn), lambda l, j: (l, 0, j)),
        out_shape=jax.ShapeDtypeStruct((depth, 8, 6 * D), F32),
        compiler_params=_params("arbitrary", "arbitrary"),
        name="adaln",
    )(conds, ada_w, ada_b.reshape(depth, 1, 6 * D))


def _rope_tables():
    half = QK_ROPE // 2
    nf = half // 2
    pos = np.arange(LAT_LEN)
    inv = ROPE_THETA ** (-np.arange(nf, dtype=np.float64) / nf)
    k = np.arange(QK_ROPE)
    part, idx = k // half, k % half
    p = np.where(part[None, :] == 0, (pos // GRID_W)[:, None], (pos % GRID_W)[:, None])
    ang = p * inv[idx % nf][None, :]
    cos, sin = np.cos(ang), np.sin(ang)
    first = (idx < nf)[None, :]
    s1 = np.where(first, -sin, 0.0)
    s2 = np.where(first, 0.0, sin)

    def place(t, fill):
        tab = np.full((2 * LAT_LEN, HEAD_PAD), fill, np.float32)
        tab[:LAT_LEN, QK_NOPE:QK_HEAD] = t
        return jnp.asarray(tab)

    return place(cos, 1.0), place(s1, 0.0), place(s2, 0.0)


def _rope(blk, cos, s1, s2):
    return blk * cos + pltpu.roll(blk, 8, 1) * s2 + pltpu.roll(blk, HEAD_PAD - 8, 1) * s1


def _head_norm(blk, g):
    ms = jnp.sum(blk * blk, axis=-1, keepdims=True) * (1.0 / QK_HEAD)
    return blk * lax.rsqrt(ms + EPS) * g


def _even_proj_kernel(xp_ref, xs_ref, mod_ref, nm_ref, wa_ref, wb_ref, qan_ref, wqb_ref, kvan_ref,
                      qn_ref, wsc_ref, cos_ref, s1_ref, s2_ref, q_ref, ckv_ref, kpe_ref, sc_ref):
    i = pl.program_id(0)
    x = jnp.where(i < CTX_TILES, xp_ref[...], xs_ref[...])
    m = mod_ref[0, 0]
    h = _rms(x, nm_ref[...]) * (1.0 + m[:, D:2 * D]) + m[:, 0:D]
    hb = h.astype(BF16)

    za = _dot(hb, wa_ref[...])
    ckv_ref[...] = _rms(za[:, Q_LORA:Q_LORA + KV_LORA], kvan_ref[...])
    kpe_ref[...] = za[:, Q_LORA + KV_LORA:]
    qa = _rms(za[:, :Q_LORA], qan_ref[...]).astype(BF16)
    cos, s1, s2 = cos_ref[...], s1_ref[...], s2_ref[...]
    qn = qn_ref[...]
    scale = QK_HEAD ** -0.5
    for hp in range(HEADS // 2):
        qq = _dot(qa, wqb_ref[:, hp * 256:(hp + 1) * 256])
        for j in range(2):
            blk = _head_norm(qq[:, j * HEAD_PAD:(j + 1) * HEAD_PAD], qn)
            blk = _rope(blk, cos, s1, s2) * scale
            h0 = (2 * hp + j) * HEAD_PAD
            q_ref[:, h0:h0 + HEAD_PAD] = blk.astype(BF16)

    gb = _dot(hb, wb_ref[:, 0:SC_W])
    v = _dot(hb, wb_ref[:, SC_W:2 * SC_W]) * _dot(hb, wb_ref[:, 2 * SC_W:3 * SC_W])
    seq = jnp.where(i < CTX_TILES, CTX_LEN, LAT_LEN)
    r = lax.broadcasted_iota(jnp.int32, (TM, 1), 0) & (seq - 1)
    vp = jnp.where(r == 0, 0.0, pltpu.roll(v, 1, 0))
    vn = jnp.where(r == seq - 1, 0.0, pltpu.roll(v, TM - 1, 0))
    w = wsc_ref[0]
    y = w[0:1] * vp + w[1:2] * v + w[2:3] * vn
    sc_ref[...] = (gb * y).astype(BF16)


def _even_proj(xp, xs, mods, norm_mix0, wa, wb, q_a_norm, wqb, kv_a_norm, qn, w_sc, tabs):
    full = lambda shape: pl.BlockSpec(shape, lambda i: (0,) * len(shape))
    tab = pl.BlockSpec((TM, HEAD_PAD), lambda i: (jnp.where(i < CTX_TILES, 1, 0), 0))
    row = lambda n: pl.BlockSpec((TM, n), lambda i: (i, 0))
    return pl.pallas_call(
        _even_proj_kernel,
        grid=(N_TILES,),
        in_specs=[
            pl.BlockSpec((TM, D), lambda i: (jnp.minimum(i, CTX_TILES - 1), 0)),
            pl.BlockSpec((TM, D), lambda i: (jnp.maximum(i - CTX_TILES, 0), 0)),
            pl.BlockSpec((1, 1, 1, 6 * D), lambda i: (0, _cond_tile(i), 0, 0)),
            full((1, D)), full((D, 512)), full((D, 3 * SC_W)), full((1, Q_LORA)),
            full((Q_LORA, HEADS * HEAD_PAD)), full((1, KV_LORA)), full((1, HEAD_PAD)),
            full((1, 3, SC_W)), tab, tab, tab,
        ],
        out_specs=[row(HEADS * HEAD_PAD), row(KV_LORA), row(HEAD_PAD), row(SC_W)],
        out_shape=[
            jax.ShapeDtypeStruct((T, HEADS * HEAD_PAD), BF16),
            jax.ShapeDtypeStruct((T, KV_LORA), F32),
            jax.ShapeDtypeStruct((T, HEAD_PAD), F32),
            jax.ShapeDtypeStruct((T, SC_W), BF16),
        ],
        compiler_params=_params("arbitrary"),
        name="even_proj",
    )(xp, xs, mods, norm_mix0, wa, wb, q_a_norm, wqb, kv_a_norm, qn, w_sc, *tabs)


def _kv_proj_kernel(ckv_ref, kpe_ref, wkvb_ref, kn_ref, cos_ref, s1_ref, s2_ref, k_ref, kv_ref):
    kv = _dot(ckv_ref[...].astype(BF16), wkvb_ref[...])
    kv_ref[...] = kv.astype(BF16)
    kpe = pltpu.roll(kpe_ref[...], QK_NOPE, 1)
    lane = lax.broadcasted_iota(jnp.int32, (1, HEAD_PAD), 1)
    cos, s1, s2 = cos_ref[...], s1_ref[...], s2_ref[...]
    kn = kn_ref[...]
    for h in range(HEADS):
        blk = jnp.where(lane < QK_NOPE, kv[:, h * HEAD_PAD:(h + 1) * HEAD_PAD], kpe)
        blk = _rope(_head_norm(blk, kn), cos, s1, s2)
        k_ref[:, h * HEAD_PAD:(h + 1) * HEAD_PAD] = blk.astype(BF16)


def _kv_proj(ckv, kpe, wkvb, kn, tabs, tab_index, name):
    n = ckv.shape[0]
    full = lambda shape: pl.BlockSpec(shape, lambda i: (0,) * len(shape))
    tab = pl.BlockSpec((TKV, HEAD_PAD), lambda i: (tab_index(i), 0))
    row = lambda w: pl.BlockSpec((TKV, w), lambda i: (i, 0))
    return pl.pallas_call(
        _kv_proj_kernel,
        grid=(n // TKV,),
        in_specs=[row(KV_LORA), row(HEAD_PAD), full((KV_LORA, HEADS * HEAD_PAD)), full((1, HEAD_PAD)),
                  tab, tab, tab],
        out_specs=[row(HEADS * HEAD_PAD), row(HEADS * HEAD_PAD)],
        out_shape=[jax.ShapeDtypeStruct((n, HEADS * HEAD_PAD), BF16)] * 2,
        compiler_params=_params("arbitrary"),
        name=name,
    )(ckv, kpe, wkvb, kn, *tabs)


def _pair_out(o0, o1):
    lane = lax.broadcasted_iota(jnp.int32, (1, HEAD_PAD), 1)
    return jnp.where(lane < V_HEAD, pltpu.roll(o0, V_HEAD, 1), o1).astype(BF16)


def _attn_ctx_kernel(q_ref, k_ref, kv_ref, o_ref):
    for hp in range(HEADS // 2):
        outs = []
        for j in range(2):
            h0 = (2 * hp + j) * HEAD_PAD
            s = _dot_nt(q_ref[:, h0:h0 + HEAD_PAD], k_ref[:, h0:h0 + HEAD_PAD])
            p = jnp.exp(s - jnp.max(s, axis=-1, keepdims=True))
            l = jnp.sum(p, axis=-1, keepdims=True)
            outs.append(_dot(p.astype(BF16), kv_ref[:, h0:h0 + HEAD_PAD]) / l)
        o_ref[:, hp * HEAD_PAD:(hp + 1) * HEAD_PAD] = _pair_out(*outs)


def _attn_ctx(q, k, kv):
    blk = pl.BlockSpec((CTX_LEN, HEADS * HEAD_PAD), lambda b: (b, 0))
    return pl.pallas_call(
        _attn_ctx_kernel,
        grid=(N_CTX_SEQ,),
        in_specs=[blk, blk, blk],
        out_specs=pl.BlockSpec((CTX_LEN, HEADS * V_HEAD), lambda b: (b, 0)),
        out_shape=jax.ShapeDtypeStruct((T_CTX, HEADS * V_HEAD), BF16),
        compiler_params=_params("arbitrary"),
        name="attn_ctx",
    )(q, k, kv)


def _attn_lat_kernel(q_ref, kc_ref, kvc_ref, kl_ref, kvl_ref, o_ref):
    outs = []
    for j in range(2):
        lanes = slice(j * HEAD_PAD, (j + 1) * HEAD_PAD)
        q = q_ref[:, lanes]
        sc = _dot_nt(q, kc_ref[:, lanes])
        sl = _dot_nt(q, kl_ref[:, lanes])
        m = jnp.maximum(jnp.max(sc, axis=-1, keepdims=True), jnp.max(sl, axis=-1, keepdims=True))
        pc, pl_ = jnp.exp(sc - m), jnp.exp(sl - m)
        l = jnp.sum(pc, axis=-1, keepdims=True) + jnp.sum(pl_, axis=-1, keepdims=True)
        o = _dot(pc.astype(BF16), kvc_ref[:, lanes]) + _dot(pl_.astype(BF16), kvl_ref[:, lanes])
        outs.append(o / l)
    o_ref[...] = _pair_out(*outs)


def _attn_lat(q, kc, kvc, k, kv):
    nq = LAT_LEN // TQ
    q0 = T_CTX // TQ
    kl0 = T_CTX // LAT_LEN
    pair = 2 * HEAD_PAD
    lat = pl.BlockSpec((LAT_LEN, pair), lambda b, hp, t: (kl0 + b, hp))
    ctx = pl.BlockSpec((PAST, pair), lambda b, hp, t: (b, hp))
    return pl.pallas_call(
        _attn_lat_kernel,
        grid=(N_LAT_SEQ, HEADS // 2, nq),
        in_specs=[pl.BlockSpec((TQ, pair), lambda b, hp, t: (q0 + b * nq + t, hp)), ctx, ctx, lat, lat],
        out_specs=pl.BlockSpec((TQ, HEAD_PAD), lambda b, hp, t: (b * nq + t, hp)),
        out_shape=jax.ShapeDtypeStruct((T_LAT, HEADS * V_HEAD), BF16),
        compiler_params=_params("arbitrary", "arbitrary", "arbitrary"),
        name="attn_lat",
    )(q, kc, kvc, k, kv)


def _even_out_kernel(oc_ref, ol_ref, sc_ref, xp_ref, xs_ref, mod_ref, wo_ref, nf_ref, x1_ref, h_ref):
    i = pl.program_id(0)
    ctx = i < CTX_TILES
    attn = jnp.where(ctx, oc_ref[...], ol_ref[...])
    x = jnp.where(ctx, xp_ref[...], xs_ref[...])
    m = mod_ref[0, 0]
    out = _dot(attn, wo_ref[0:HEADS * V_HEAD, :]) + _dot(sc_ref[...], wo_ref[HEADS * V_HEAD:, :])
    x1 = x + m[:, 2 * D:3 * D] * out
    x1_ref[...] = x1
    h_ref[...] = (_rms(x1, nf_ref[...]) * (1.0 + m[:, 4 * D:5 * D]) + m[:, 3 * D:4 * D]).astype(BF16)


def _even_out(oc, ol, sc, xp, xs, mods, wo, norm_ffn0):
    full = lambda shape: pl.BlockSpec(shape, lambda i: (0,) * len(shape))
    first = lambda n: pl.BlockSpec((TM, n), lambda i: (jnp.minimum(i, CTX_TILES - 1), 0))
    second = lambda n: pl.BlockSpec((TM, n), lambda i: (jnp.maximum(i - CTX_TILES, 0), 0))
    row = lambda n: pl.BlockSpec((TM, n), lambda i: (i, 0))
    return pl.pallas_call(
        _even_out_kernel,
        grid=(N_TILES,),
        in_specs=[first(HEADS * V_HEAD), second(HEADS * V_HEAD), row(SC_W), first(D), second(D),
                  pl.BlockSpec((1, 1, 1, 6 * D), lambda i: (0, _cond_tile(i), 0, 0)),
                  full((HEADS * V_HEAD + SC_W, D)), full((1, D))],
        out_specs=[row(D), row(D)],
        out_shape=[jax.ShapeDtypeStruct((T, D), F32), jax.ShapeDtypeStruct((T, D), BF16)],
        compiler_params=_params("arbitrary"),
        name="even_out",
    )(oc, ol, sc, xp, xs, mods, wo, norm_ffn0)


def _ffn_kernel(h_ref, wg_ref, wu_ref, wd_ref, x1_ref, mod0_ref, mod1_ref, nm_ref, x2_ref, h3_ref, acc_ref):
    c = pl.program_id(1)

    @pl.when(c == 0)
    def _():
        acc_ref[...] = jnp.zeros_like(acc_ref)

    h = h_ref[...]
    g = _dot(h, wg_ref[...].astype(BF16))
    u = _dot(h, wu_ref[...].astype(BF16))
    acc_ref[...] += _dot((_silu(g) * u).astype(BF16), wd_ref[...].astype(BF16))

    @pl.when(c == pl.num_programs(1) - 1)
    def _():
        m0, m1 = mod0_ref[0, 0], mod1_ref[0, 0]
        x2 = x1_ref[...] + m0[:, 5 * D:6 * D] * acc_ref[...]
        x2_ref[...] = x2
        h3_ref[...] = (_rms(x2, nm_ref[...]) * (1.0 + m1[:, D:2 * D]) + m1[:, 0:D]).astype(BF16)


def _ffn(h, w_gu, w_down, x1, mods, norm_mix1):
    nc = D_FF // FFN_FC
    row = lambda n: pl.BlockSpec((TM, n), lambda i, c: (i, 0))
    mod = lambda l: pl.BlockSpec((1, 1, 1, 6 * D), lambda i, c: (l, _cond_tile(i), 0, 0))
    return pl.pallas_call(
        _ffn_kernel,
        grid=(N_TILES, nc),
        in_specs=[row(D),
                  pl.BlockSpec((D, FFN_FC), lambda i, c: (0, c)),
                  pl.BlockSpec((D, FFN_FC), lambda i, c: (0, nc + c)),
                  pl.BlockSpec((FFN_FC, D), lambda i, c: (c, 0)),
                  row(D), mod(0), mod(1), pl.BlockSpec((1, D), lambda i, c: (0, 0))],
        out_specs=[row(D), row(D)],
        out_shape=[jax.ShapeDtypeStruct((T, D), F32), jax.ShapeDtypeStruct((T, D), BF16)],
        scratch_shapes=[pltpu.VMEM((TM, D), F32)],
        compiler_params=_params("arbitrary", "arbitrary"),
        name="ffn_dense",
    )(h, w_gu, w_gu, w_down, x1, mods, mods, norm_mix1)


CONF_CB = 256
CONF_SEG = 256
CONF_HALO = 16
CONF_SEGP = CONF_SEG + 2 * CONF_HALO
CONF_PIECE = 64


def _conf_kernel(h_ref, x2_ref, w1_ref, b1_ref, wdw_ref, bdw_ref, lng_ref, lnb_ref, w2_ref, b2_ref,
                 mod_ref, nf_ref, rt_ref, x3_ref, h4_ref, lg_ref, pad_ref, conv_ref):
    i = pl.program_id(0)
    nseg = TM // CONF_SEG
    h = h_ref[...]
    joined = jnp.where(i < CTX_TILES, 0.0, 1.0)
    zeros_halo = jnp.zeros((CONF_HALO, CONF_CB), F32)
    for cb in range(D // CONF_CB):
        c0 = cb * CONF_CB
        a = _dot(h, w1_ref[:, c0:c0 + CONF_CB]) + b1_ref[:, c0:c0 + CONF_CB]
        g = _dot(h, w1_ref[:, D + c0:D + c0 + CONF_CB]) + b1_ref[:, D + c0:D + c0 + CONF_CB]
        u = a * jax.nn.sigmoid(g)
        for s in range(nseg):
            base = s * CONF_SEGP
            top = u[s * CONF_SEG - CONF_HALO:s * CONF_SEG] * joined if s > 0 else zeros_halo
            bot = (u[(s + 1) * CONF_SEG:(s + 1) * CONF_SEG + CONF_HALO] * joined
                   if s < nseg - 1 else zeros_halo)
            pad_ref[0, base:base + CONF_HALO, :] = top
            pad_ref[0, base + CONF_HALO:base + CONF_HALO + CONF_SEG, :] = u[s * CONF_SEG:(s + 1) * CONF_SEG]
            pad_ref[0, base + CONF_HALO + CONF_SEG:base + CONF_SEGP, :] = bot
        p0 = pad_ref[0]
        rows = nseg * CONF_SEGP
        for b in range(1, 8):
            pad_ref[b] = pltpu.roll(p0, rows - b, 0)

        def piece(t, carry):
            s = t // (CONF_SEG // CONF_PIECE)
            q0 = (t % (CONF_SEG // CONF_PIECE)) * CONF_PIECE
            src = pl.multiple_of(s * CONF_SEGP + q0, 8)
            acc = jnp.zeros((CONF_PIECE, CONF_CB), F32)
            for j in range(CONF_K):
                hi, lo = (j + 1) // 8, (j + 1) % 8
                acc = acc + wdw_ref[0, j:j + 1, c0:c0 + CONF_CB] * pad_ref[lo, pl.ds(src + 8 * hi, CONF_PIECE), :]
            dst = pl.multiple_of(s * CONF_SEG + q0, 8)
            conv_ref[pl.ds(dst, CONF_PIECE), c0:c0 + CONF_CB] = acc + bdw_ref[:, c0:c0 + CONF_CB]
            return carry

        lax.fori_loop(0, TM // CONF_PIECE, piece, 0)

    y = conv_ref[...]
    mu = jnp.mean(y, axis=-1, keepdims=True)
    yc = y - mu
    var = jnp.mean(yc * yc, axis=-1, keepdims=True)
    y = _silu(yc * lax.rsqrt(var + EPS) * lng_ref[...] + lnb_ref[...])
    out = _dot(y.astype(BF16), w2_ref[...]) + b2_ref[...]
    m = mod_ref[0, 0]
    x3 = x2_ref[...] + m[:, 2 * D:3 * D] * out
    x3_ref[...] = x3
    h4 = _rms(x3, nf_ref[...]) * (1.0 + m[:, 4 * D:5 * D]) + m[:, 3 * D:4 * D]
    h4_ref[...] = h4.astype(BF16)
    lg_ref[...] = lax.dot_general(rt_ref[...], h4, (((1,), (1,)), ((), ())),
                                  precision=lax.Precision.HIGHEST, preferred_element_type=F32)


def _conf(h3, x2, w1, b1, wdw, bdw, lng, lnb, w2, b2, mods, norm_ffn1, router_t):
    full = lambda shape: pl.BlockSpec(shape, lambda i: (0,) * len(shape))
    row = lambda n: pl.BlockSpec((TM, n), lambda i: (i, 0))
    return pl.pallas_call(
        _conf_kernel,
        grid=(N_TILES,),
        in_specs=[row(D), row(D), full((D, 2 * D)), full((1, 2 * D)), full((1, CONF_K, D)), full((1, D)),
                  full((1, D)), full((1, D)), full((D, D)), full((1, D)),
                  pl.BlockSpec((1, 1, 1, 6 * D), lambda i: (1, _cond_tile(i), 0, 0)),
                  full((1, D)), full((N_EXP, D))],
        out_specs=[row(D), row(D), pl.BlockSpec((N_EXP, TM), lambda i: (0, i))],
        out_shape=[jax.ShapeDtypeStruct((T, D), F32), jax.ShapeDtypeStruct((T, D), BF16),
                   jax.ShapeDtypeStruct((N_EXP, T), F32)],
        scratch_shapes=[pltpu.VMEM((8, (TM // CONF_SEG) * CONF_SEGP, CONF_CB), F32),
                        pltpu.VMEM((TM, D), F32)],
        compiler_params=_params("arbitrary"),
        name="conformer_conv",
    )(h3, x2, w1, b1, wdw, bdw, lng, lnb, w2, b2, mods, norm_ffn1, router_t)


TB = 256
N_TB = T // TB
SUBS = 8
SM = SUBS * TB
N_SUB_MAX = 2 * T // TB + N_EXP
N_SUP_MAX = N_SUB_MAX // SUBS + N_EXP - 1
YS_ROWS = (N_SUB_MAX + 2) * TB
WIN_ALIGN = 16
WIN = TB + WIN_ALIGN
FIRST_STRIDE = 32
GATHER_BLOCKS = 6


def _route_kernel(lg_ref, g_ref, rank_ref, first_ref):
    lg = lg_ref[...]
    idx = lax.broadcasted_iota(jnp.int32, lg.shape, 0).astype(F32)
    none = float(N_EXP)
    m1 = jnp.max(lg, axis=0, keepdims=True)
    i1 = jnp.min(jnp.where(lg == m1, idx, none), axis=0, keepdims=True)
    rest = jnp.where(idx == i1, -jnp.inf, lg)
    m2 = jnp.max(rest, axis=0, keepdims=True)
    i2 = jnp.min(jnp.where(rest == m2, idx, none), axis=0, keepdims=True)
    e = jnp.exp(m2 - m1)
    w1 = 1.0 / (1.0 + e)
    w2 = e / (1.0 + e)
    g_ref[...] = jnp.where(idx == i1, w1, 0.0) + jnp.where(idx == i2, w2, 0.0)

    mask = jnp.where(idx == i1, 1.0, 0.0) + jnp.where(idx == i2, 1.0, 0.0)
    before = (lax.broadcasted_iota(jnp.int32, (TB, TB), 0) < lax.broadcasted_iota(jnp.int32, (TB, TB), 1))
    before = jnp.where(before, 1.0, 0.0).astype(BF16)
    lane = lax.broadcasted_iota(jnp.int32, (N_EXP, 128), 1)
    carry = jnp.zeros((N_EXP, 1), F32)
    first = jnp.zeros((N_EXP, 128), F32)
    for b in range(N_TB):
        mb = mask[:, b * TB:(b + 1) * TB]
        local = _dot(mb.astype(BF16), before)
        rank_ref[:, b * TB:(b + 1) * TB] = jnp.where(mb > 0.0, local + carry, -1.0)
        first = jnp.where(lane == b, carry, first)
        carry = carry + jnp.sum(mb, axis=1, keepdims=True)
    first_ref[...] = jnp.where(lane == N_TB, carry, first)


def _route(logits_t):
    return pl.pallas_call(
        _route_kernel,
        out_shape=[jax.ShapeDtypeStruct((N_EXP, T), F32), jax.ShapeDtypeStruct((N_EXP, T), F32),
                   jax.ShapeDtypeStruct((N_EXP, 128), F32)],
        compiler_params=pltpu.CompilerParams(vmem_limit_bytes=VMEM_LIMIT),
        name="route",
    )(logits_t)


def _moe_plan(first):
    first = first[:, :FIRST_STRIDE].astype(jnp.int32)
    cnt = first[:, N_TB]
    nt = (cnt + (TB - 1)) >> 8
    off_end = jnp.cumsum(nt)
    off = off_end - nt
    nsub = off_end[-1]
    nsup = (nt + (SUBS - 1)) >> 3
    sup_end = jnp.cumsum(nsup)
    sup_off = sup_end - nsup
    s = jnp.minimum(jnp.arange(N_SUP_MAX), sup_end[-1] - 1)
    valid = jnp.arange(N_SUP_MAX) < sup_end[-1]
    se = jnp.sum(s[:, None] >= sup_end[None, :], axis=1)
    sk0 = (s - sup_off[se]) * SUBS
    sns = jnp.where(valid, jnp.clip(nt[se] - sk0, 0, SUBS), 0)
    sj0 = off[se] + sk0
    j = jnp.arange(N_SUB_MAX)
    je = jnp.minimum(jnp.sum(j[:, None] >= off_end[None, :], axis=1), N_EXP - 1)
    base = (j - off[je]) * TB
    fj = first[je]
    blo = jnp.minimum(jnp.sum(fj[:, 1:N_TB + 1] <= base[:, None], axis=1), N_TB - 1)
    end = jnp.minimum(base + TB, cnt[je])
    bhi = jnp.maximum(jnp.sum(fj[:, :N_TB] < end[:, None], axis=1) - 1, blo)
    ng = (bhi - blo) // GATHER_BLOCKS + 1
    start = TB * off[:, None] + first[:, :N_TB]
    lead = start & (WIN_ALIGN - 1)
    win = start - lead
    rel = first[:, :N_TB] - lead
    i32 = lambda a: a.astype(jnp.int32)
    return dict(se=i32(se), sk0=i32(sk0), sns=i32(sns), sj0=i32(sj0), nsub=i32(nsub).reshape(1),
                first=i32(first.reshape(-1)), blo=i32(blo), ng=i32(ng),
                win=i32(win.T.reshape(-1)), rel=i32(rel.T.reshape(-1)))


def _moe_gmm_kernel(se_ref, sk0_ref, sns_ref, sj0_ref, nsub_ref, first_ref, blo_ref, ng_ref,
                    x_ref, rank_ref, gate_ref, wg_ref, wu_ref, wd_ref, ys_ref,
                    xs_ref, gs_ref, yacc_ref, ybuf_ref, acc_ref, gacc_ref, wgb_ref, wub_ref, wdb_ref, sem):
    s, c = pl.program_id(0), pl.program_id(1)
    nc = pl.num_programs(1)
    e, k0, ns = se_ref[s], sk0_ref[s], sns_ref[s]

    def sub_rows(k):
        return pl.ds(pl.multiple_of(k * TB, TB), TB)

    def out_copy(k, row0):
        dst = ys_ref.at[pl.ds(pl.multiple_of(row0 + k * TB, TB), TB)]
        return pltpu.make_async_copy(ybuf_ref.at[sub_rows(k)], dst, sem.at[k])

    @pl.when((ns > 0) & (c == 0))
    def _gather():
        def sub(k, carry):
            base = (k0 + k) * TB
            slot = (lax.broadcasted_iota(jnp.int32, (TB, 1), 0) + base).astype(F32)
            acc_ref[...] = jnp.zeros_like(acc_ref)
            gacc_ref[...] = jnp.zeros_like(gacc_ref)
            blo = blo_ref[sj0_ref[s] + k]

            def group(g, carry2):
                b0 = blo + g * GATHER_BLOCKS
                t0 = pl.multiple_of(jnp.minimum(b0, N_TB - GATHER_BLOCKS) * TB, TB)
                lo = first_ref[e * FIRST_STRIDE + b0].astype(F32)
                cols = pl.ds(t0, GATHER_BLOCKS * TB)
                hit = rank_ref[pl.ds(e, 1), cols] == jnp.where(slot >= lo, slot, -2.0)
                onehot = jnp.where(hit, 1.0, 0.0).astype(BF16)
                acc_ref[...] += _dot(onehot, x_ref[cols, :])
                gacc_ref[...] += jnp.sum(jnp.where(hit, gate_ref[pl.ds(e, 1), cols], 0.0),
                                         axis=-1, keepdims=True)
                return carry2

            lax.fori_loop(0, ng_ref[sj0_ref[s] + k], group, 0)
            xs_ref[sub_rows(k), :] = acc_ref[...].astype(BF16)
            gs_ref[sub_rows(k), :] = gacc_ref[...]
            yacc_ref[sub_rows(k), :] = jnp.zeros((TB, D), F32)
            return carry

        lax.fori_loop(0, ns, sub, 0)

    @pl.when(ns > 0)
    def _compute():
        wgb_ref[...] = wg_ref[0].astype(BF16)
        wub_ref[...] = wu_ref[0].astype(BF16)
        wdb_ref[...] = wd_ref[0].astype(BF16)

        def swiglu(rows):
            x = xs_ref[rows, :]
            g = _dot(x, wgb_ref[...])
            u = _dot(x, wub_ref[...])
            yacc_ref[rows, :] += _dot((_silu(g) * u).astype(BF16), wdb_ref[...])

        def pair(k, carry):
            swiglu(pl.ds(pl.multiple_of(k * (2 * TB), 2 * TB), 2 * TB))
            return carry

        lax.fori_loop(0, ns >> 1, pair, 0)

        @pl.when((ns & 1) == 1)
        def _():
            swiglu(sub_rows(ns - 1))

    @pl.when((ns > 0) & (c == nc - 1))
    def _store():
        row0 = sj0_ref[s] * TB

        def put(k, carry):
            ybuf_ref[sub_rows(k), :] = (yacc_ref[sub_rows(k), :] * gs_ref[sub_rows(k), :]).astype(BF16)
            out_copy(k, row0).start()
            return carry

        def done(k, carry):
            out_copy(k, row0).wait()
            return carry

        lax.fori_loop(0, ns, put, 0)
        lax.fori_loop(0, ns, done, 0)

    @pl.when((s == pl.num_programs(0) - 1) & (c == nc - 1))
    def _zero_tail():
        ybuf_ref[0:TB, :] = jnp.zeros((TB, D), BF16)
        nsub = nsub_ref[0]

        def fill(k, carry):
            cp = out_copy(0, (nsub + k) * TB)
            cp.start()
            cp.wait()
            return carry

        lax.fori_loop(0, YS_ROWS // TB - nsub, fill, 0)


def _moe_gmm(plan, h4, rank, gates, w_gu, w_down):
    nc = D_FFE // MOE_FC

    def chunk(s, c, sns):
        return jnp.where(sns[s] > 0, c, nc - 1)

    return pl.pallas_call(
        _moe_gmm_kernel,
        grid_spec=pltpu.PrefetchScalarGridSpec(
            num_scalar_prefetch=8,
            grid=(N_SUP_MAX, nc),
            in_specs=[
                pl.BlockSpec((T, D), lambda s, c, *_: (0, 0), pipeline_mode=pl.Buffered(1)),
                pl.BlockSpec((N_EXP, T), lambda s, c, *_: (0, 0)),
                pl.BlockSpec((N_EXP, T), lambda s, c, *_: (0, 0)),
                pl.BlockSpec((1, D, MOE_FC), lambda s, c, se, sk0, sns, *_: (se[s], 0, chunk(s, c, sns))),
                pl.BlockSpec((1, D, MOE_FC), lambda s, c, se, sk0, sns, *_: (se[s], 0, nc + chunk(s, c, sns))),
                pl.BlockSpec((1, MOE_FC, D), lambda s, c, se, sk0, sns, *_: (se[s], chunk(s, c, sns), 0)),
            ],
            out_specs=pl.BlockSpec(memory_space=pl.ANY),
            scratch_shapes=[
                pltpu.VMEM((SM, D), BF16), pltpu.VMEM((SM, 1), F32), pltpu.VMEM((SM, D), F32),
                pltpu.VMEM((SM, D), BF16), pltpu.VMEM((TB, D), F32), pltpu.VMEM((TB, 1), F32),
                pltpu.VMEM((D, MOE_FC), BF16), pltpu.VMEM((D, MOE_FC), BF16), pltpu.VMEM((MOE_FC, D), BF16),
                pltpu.SemaphoreType.DMA((SUBS,)),
            ],
        ),
        out_shape=jax.ShapeDtypeStruct((YS_ROWS, D), BF16),
        compiler_params=_params("arbitrary", "arbitrary"),
        name="moe_gmm",
    )(plan["se"], plan["sk0"], plan["sns"], plan["sj0"], plan["nsub"], plan["first"], plan["blo"], plan["ng"],
      h4, rank, gates, w_gu, w_gu, w_down)


def _moe_combine_kernel(win_ref, rel_ref, *refs):
    y_refs = refs[:N_EXP]
    rank_ref, x3_ref, mod_ref, yp_ref, ys_ref, ycat_ref = refs[N_EXP:]
    b = pl.program_id(0)
    row = lax.broadcasted_iota(jnp.int32, (WIN, 1), 0)
    onehot = []
    for e in range(N_EXP):
        ycat_ref[e * WIN:(e + 1) * WIN, :] = y_refs[e][...]
        slot = (row + rel_ref[b * N_EXP + e]).astype(F32)
        onehot.append(jnp.where(rank_ref[e:e + 1, :] == slot, 1.0, 0.0).astype(BF16))
    acc = lax.dot_general(jnp.concatenate(onehot, axis=0), ycat_ref[...], (((0,), (0,)), ((), ())),
                          preferred_element_type=F32)
    out = x3_ref[...] + mod_ref[0, 0][:, 5 * D:6 * D] * acc

    @pl.when(b < T_CTX // TB)
    def _():
        yp_ref[...] = out

    @pl.when(b >= T_CTX // TB)
    def _():
        ys_ref[...] = out


def _moe_combine(plan, ysorted, rank, x3, mods):
    ctx_blocks = T_CTX // TB

    def window(e):
        return pl.BlockSpec((pl.Element(WIN), pl.Element(D)),
                            lambda b, win, rel: (pl.multiple_of(win[b * N_EXP + e], WIN_ALIGN), 0))

    def cond(b):
        return jnp.maximum((b - (ctx_blocks - LAT_LEN // TB)) // (LAT_LEN // TB), 0)

    return pl.pallas_call(
        _moe_combine_kernel,
        grid_spec=pltpu.PrefetchScalarGridSpec(
            num_scalar_prefetch=2,
            grid=(N_TB,),
            in_specs=[window(e) for e in range(N_EXP)] + [
                pl.BlockSpec((N_EXP, TB), lambda b, win, rel: (0, b)),
                pl.BlockSpec((TB, D), lambda b, win, rel: (b, 0)),
                pl.BlockSpec((1, 1, 1, 6 * D), lambda b, win, rel: (1, cond(b), 0, 0)),
            ],
            out_specs=[pl.BlockSpec((TB, D), lambda b, win, rel: (jnp.minimum(b, ctx_blocks - 1), 0)),
                       pl.BlockSpec((TB, D), lambda b, win, rel: (jnp.maximum(b - ctx_blocks, 0), 0))],
            scratch_shapes=[pltpu.VMEM((N_EXP * WIN, D), BF16)],
        ),
        out_shape=[jax.ShapeDtypeStruct((T_CTX, D), F32), jax.ShapeDtypeStruct((T_LAT, D), F32)],
        compiler_params=_params("arbitrary"),
        name="moe_combine",
    )(plan["win"], plan["rel"], *([ysorted] * N_EXP), rank, x3, mods)


def _pad_heads(w, width):
    lead = w.shape[:-1]
    w = w.reshape(*lead, HEADS, width)
    w = jnp.pad(w, [(0, 0)] * len(lead) + [(0, 0), (0, HEAD_PAD - width)])
    return w.reshape(*lead, HEADS * HEAD_PAD)


def kernel(x_prompt, x_sample, cache_ckv, cache_kpe, c, c_ctx, ada_w, ada_b, norm_mix, norm_ffn, w_in, q_a_norm,
           w_qb, kv_a_norm, w_kvb, q_norm, k_norm, w_sc, w_o, ffn_gu, ffn_down, conv_pw1, conv_pw1_b, conv_dw,
           conv_dw_b, conv_ln_g, conv_ln_b, conv_pw2, conv_pw2_b, router, moe_gu, moe_down):
    xp = x_prompt.reshape(T_CTX, D)
    xs = x_sample.reshape(T_LAT, D)

    conds = jnp.zeros((8, D), F32).at[0].set(c_ctx).at[1:1 + N_LAT_SEQ].set(c)
    mods = _adaln(conds, ada_w, ada_b).reshape(2, 8, 1, 6 * D)

    n_a = Q_LORA + KV_LORA + QK_ROPE
    wa = jnp.pad(w_in[0, :, :n_a], ((0, 0), (0, 512 - n_a))).astype(BF16)
    wb = w_in[0, :, n_a:].astype(BF16)
    wqb = _pad_heads(w_qb[0], QK_HEAD).astype(BF16)
    wkvb = w_kvb[0].astype(BF16)
    qn = jnp.pad(q_norm[0], (0, HEAD_PAD - QK_HEAD)).reshape(1, HEAD_PAD)
    kn = jnp.pad(k_norm[0], (0, HEAD_PAD - QK_HEAD)).reshape(1, HEAD_PAD)
    tabs = _rope_tables()

    q, ckv, kpe, sc = _even_proj(xp, xs, mods, norm_mix[0:1], wa, wb, q_a_norm, wqb, kv_a_norm, qn, w_sc, tabs)

    lat_tile0 = T_CTX // TKV
    ident = LAT_LEN // TKV
    k, kv = _kv_proj(ckv, kpe, wkvb, kn, tabs,
                     lambda i: jnp.where(i < lat_tile0, ident, (i - lat_tile0) % ident), "kv_proj")
    cache_kpe_p = jnp.pad(cache_kpe[:, 0].reshape(N_LAT_SEQ * PAST, QK_ROPE), ((0, 0), (0, HEAD_PAD - QK_ROPE)))
    kc, kvc = _kv_proj(cache_ckv[:, 0].reshape(N_LAT_SEQ * PAST, KV_LORA), cache_kpe_p, wkvb, kn, tabs,
                       lambda i: ident, "kv_proj_cache")

    oc = _attn_ctx(q, k, kv)
    ol = _attn_lat(q, kc, kvc, k, kv)
    x1, h2 = _even_out(oc, ol, sc, xp, xs, mods, w_o[0].astype(BF16), norm_ffn[0:1])
    x2, h3 = _ffn(h2, ffn_gu[0], ffn_down[0], x1, mods, norm_mix[1:2])

    x3, h4, logits_t = _conf(h3, x2, conv_pw1[0].astype(BF16), conv_pw1_b, conv_dw, conv_dw_b, conv_ln_g,
                             conv_ln_b, conv_pw2[0].astype(BF16), conv_pw2_b, mods, norm_ffn[1:2], router[0].T)
    gates, rank, first = _route(logits_t)
    plan = _moe_plan(first)
    ysorted = _moe_gmm(plan, h4, rank, gates, moe_gu[0], moe_down[0])
    yp, ys = _moe_combine(plan, ysorted, rank, x3, mods)

    state_ckv = ckv[:T_CTX].reshape(N_CTX_SEQ, 1, CTX_LEN, KV_LORA)
    state_kpe = kpe[:T_CTX, :QK_ROPE].reshape(N_CTX_SEQ, 1, CTX_LEN, QK_ROPE)
    return (yp.reshape(N_CTX_SEQ, CTX_LEN, D), ys.reshape(N_LAT_SEQ, LAT_LEN, D), state_ckv, state_kpe)
```

```python
import functools

import jax
import jax.numpy as jnp
import numpy as np
from jax import lax
from jax.experimental import pallas as pl
from jax.experimental.pallas import tpu as pltpu

F32 = jnp.float32
BF16 = jnp.bfloat16

D = 1024
N_CTX_SEQ, CTX_LEN = 16, 256
N_LAT_SEQ, LAT_LEN = 2, 1024
T_CTX = N_CTX_SEQ * CTX_LEN
T_LAT = N_LAT_SEQ * LAT_LEN
T = T_CTX + T_LAT
PAST = 256
GRID_W = 64
HEADS = 8
QK_NOPE, QK_ROPE, V_HEAD = 64, 32, 64
QK_HEAD = QK_NOPE + QK_ROPE
HEAD_PAD = 128
Q_LORA, KV_LORA = 256, 128
SC_W = 512
CONF_K = 31
D_FF = 2816
N_EXP = 8
D_FFE = 3584
EPS = 1e-6
ROPE_THETA = 10000.0

TM = 1024
N_TILES = T // TM
CTX_TILES = T_CTX // TM
TKV = 512
TQ = 256
FFN_FC = 256
MOE_FC = 512
VMEM_LIMIT = 56 * 1024 * 1024


def _dot(a, b):
    return jnp.dot(a, b, preferred_element_type=F32)


def _dot_nt(a, b):
    return lax.dot_general(a, b, (((1,), (1,)), ((), ())), preferred_element_type=F32)


def _rms(x, g):
    return x * lax.rsqrt(jnp.mean(x * x, axis=-1, keepdims=True) + EPS) * g


def _silu(x):
    return x * jax.nn.sigmoid(x)


def _params(*sem):
    return pltpu.CompilerParams(dimension_semantics=sem, vmem_limit_bytes=VMEM_LIMIT)


def _mod_row(mod_ref, row0):
    cond = jnp.maximum(row0 - (T_CTX - LAT_LEN), 0) >> 10
    return mod_ref[0, pl.ds(cond, 1), :]


def _mod_spec(layer, ngrid):
    return pl.BlockSpec((1, 8, 6 * D), lambda *_: (layer, 0, 0))


def _adaln_kernel(cc_ref, c_ref, w_ref, b_ref, o_ref):
    l = pl.program_id(0)
    row = lax.broadcasted_iota(jnp.int32, (8, 1), 0)
    cond = jnp.where(row == 0, cc_ref[...], 0.0)
    for b in range(N_LAT_SEQ):
        cond = jnp.where(row == 1 + b, c_ref[b:b + 1, :], cond)
    o_ref[0] = _dot(_silu(cond).astype(BF16), w_ref[0].astype(BF16)) + b_ref[pl.ds(l, 1), :]


def _adaln(c_ctx, c, ada_w, ada_b):
    depth = ada_w.shape[0]
    tn = 1024
    return pl.pallas_call(
        _adaln_kernel,
        grid=(depth, 6 * D // tn),
        in_specs=[
            pl.BlockSpec((1, D), lambda l, j: (0, 0)),
            pl.BlockSpec((N_LAT_SEQ, D), lambda l, j: (0, 0)),
            pl.BlockSpec((1, D, tn), lambda l, j: (l, 0, j)),
            pl.BlockSpec((depth, tn), lambda l, j: (0, j)),
        ],
        out_specs=pl.BlockSpec((1, 8, tn), lambda l, j: (l, 0, j)),
        out_shape=jax.ShapeDtypeStruct((depth, 8, 6 * D), F32),
        compiler_params=_params("arbitrary", "arbitrary"),
        name="adaln",
    )(c_ctx.reshape(1, D), c, ada_w, ada_b)


def _rope_tables():
    half = QK_ROPE // 2
    nf = half // 2
    pos = np.arange(LAT_LEN)
    inv = ROPE_THETA ** (-np.arange(nf, dtype=np.float64) / nf)
    k = np.arange(QK_ROPE)
    part, idx = k // half, k % half
    p = np.where(part[None, :] == 0, (pos // GRID_W)[:, None], (pos % GRID_W)[:, None])
    ang = p * inv[idx % nf][None, :]
    cos, sin = np.cos(ang), np.sin(ang)
    first = (idx < nf)[None, :]
    s1 = np.where(first, -sin, 0.0)
    s2 = np.where(first, 0.0, sin)

    def place(t, fill):
        tab = np.full((2 * LAT_LEN, HEAD_PAD), fill, np.float32)
        tab[:LAT_LEN, QK_NOPE:QK_HEAD] = t
        return jnp.asarray(tab)

    return place(cos, 1.0), place(s1, 0.0), place(s2, 0.0)


def _rope(blk, cos, s1, s2):
    return blk * cos + pltpu.roll(blk, 8, 1) * s2 + pltpu.roll(blk, HEAD_PAD - 8, 1) * s1


def _head_norm(blk, g):
    ms = jnp.sum(blk * blk, axis=-1, keepdims=True) * (1.0 / QK_HEAD)
    return blk * lax.rsqrt(ms + EPS) * g


def _even_proj_kernel(xp_ref, xs_ref, mod_ref, nm_ref, wa_ref, wb_ref, qan_ref, wqb_ref, kvan_ref,
                      qn_ref, wsc_ref, cos_ref, s1_ref, s2_ref,
                      q_ref, ckv_ref, kpe_ref, sc_ref, sckv_ref, skpe_ref):
    i = pl.program_id(0)
    x = jnp.where(i < CTX_TILES, xp_ref[...], xs_ref[...])
    m = _mod_row(mod_ref, i * TM)
    h = _rms(x, nm_ref[0:1, :]) * (1.0 + m[:, D:2 * D]) + m[:, 0:D]
    hb = h.astype(BF16)

    za = _dot(hb, wa_ref[...])
    ckv = _rms(za[:, Q_LORA:Q_LORA + KV_LORA], kvan_ref[...])
    kpe = za[:, Q_LORA + KV_LORA:]
    ckv_ref[...] = ckv
    kpe_ref[...] = kpe

    @pl.when(i < CTX_TILES)
    def _():
        sckv_ref[...] = ckv
        skpe_ref[...] = kpe[:, :QK_ROPE]

    qa = _rms(za[:, :Q_LORA], qan_ref[...]).astype(BF16)
    cos, s1, s2 = cos_ref[...], s1_ref[...], s2_ref[...]
    qn = qn_ref[...]
    scale = QK_HEAD ** -0.5
    for hp in range(HEADS // 2):
        qq = _dot(qa, wqb_ref[:, hp * 256:(hp + 1) * 256])
        for j in range(2):
            blk = _head_norm(qq[:, j * HEAD_PAD:(j + 1) * HEAD_PAD], qn)
            blk = _rope(blk, cos, s1, s2) * scale
            h0 = (2 * hp + j) * HEAD_PAD
            q_ref[:, h0:h0 + HEAD_PAD] = blk.astype(BF16)

    gb = _dot(hb, wb_ref[:, 0:SC_W])
    v = _dot(hb, wb_ref[:, SC_W:2 * SC_W]) * _dot(hb, wb_ref[:, 2 * SC_W:3 * SC_W])
    seq = jnp.where(i < CTX_TILES, CTX_LEN, LAT_LEN)
    r = lax.broadcasted_iota(jnp.int32, (TM, 1), 0) & (seq - 1)
    vp = jnp.where(r == 0, 0.0, pltpu.roll(v, 1, 0))
    vn = jnp.where(r == seq - 1, 0.0, pltpu.roll(v, TM - 1, 0))
    w = wsc_ref[0]
    y = w[0:1] * vp + w[1:2] * v + w[2:3] * vn
    sc_ref[...] = (gb * y).astype(BF16)


def _even_proj(xp, xs, mods, norm_mix, wa, wb, q_a_norm, wqb, kv_a_norm, qn, w_sc, tabs):
    full = lambda shape: pl.BlockSpec(shape, lambda i: (0,) * len(shape))
    tab = pl.BlockSpec((TM, HEAD_PAD), lambda i: (jnp.where(i < CTX_TILES, 1, 0), 0))
    row = lambda n: pl.BlockSpec((TM, n), lambda i: (i, 0))
    ctx_row = lambda n: pl.BlockSpec((TM, n), lambda i: (jnp.minimum(i, CTX_TILES - 1), 0))
    return pl.pallas_call(
        _even_proj_kernel,
        grid=(N_TILES,),
        in_specs=[
            ctx_row(D),
            pl.BlockSpec((TM, D), lambda i: (jnp.maximum(i - CTX_TILES, 0), 0)),
            _mod_spec(0, 1),
            full((2, D)), full((D, 512)), full((D, 3 * SC_W)), full((1, Q_LORA)),
            full((Q_LORA, HEADS * HEAD_PAD)), full((1, KV_LORA)), full((1, HEAD_PAD)),
            full((1, 3, SC_W)), tab, tab, tab,
        ],
        out_specs=[row(HEADS * HEAD_PAD), row(KV_LORA), row(HEAD_PAD), row(SC_W),
                   ctx_row(KV_LORA), ctx_row(QK_ROPE)],
        out_shape=[
            jax.ShapeDtypeStruct((T, HEADS * HEAD_PAD), BF16),
            jax.ShapeDtypeStruct((T, KV_LORA), F32),
            jax.ShapeDtypeStruct((T, HEAD_PAD), F32),
            jax.ShapeDtypeStruct((T, SC_W), BF16),
            jax.ShapeDtypeStruct((T_CTX, KV_LORA), F32),
            jax.ShapeDtypeStruct((T_CTX, QK_ROPE), F32),
        ],
        compiler_params=_params("arbitrary"),
        name="even_proj",
    )(xp, xs, mods, norm_mix, wa, wb, q_a_norm, wqb, kv_a_norm, qn, w_sc, *tabs)


def _kv_proj_kernel(ckv_ref, kpe_ref, wkvb_ref, kn_ref, cos_ref, s1_ref, s2_ref, k_ref, kv_ref):
    kv = _dot(ckv_ref[...].astype(BF16), wkvb_ref[...])
    kv_ref[...] = kv.astype(BF16)
    kpe = pltpu.roll(kpe_ref[...], QK_NOPE, 1)
    lane = lax.broadcasted_iota(jnp.int32, (1, HEAD_PAD), 1)
    kn = kn_ref[...]
    pe_sq = jnp.sum(kpe * kpe, axis=-1, keepdims=True)
    pe = _rope(kpe * kn, cos_ref[...], s1_ref[...], s2_ref[...])
    for h in range(HEADS):
        blk = kv[:, h * HEAD_PAD:(h + 1) * HEAD_PAD]
        nope = jnp.where(lane < QK_NOPE, blk, 0.0)
        ms = (jnp.sum(nope * nope, axis=-1, keepdims=True) + pe_sq) * (1.0 / QK_HEAD)
        k = jnp.where(lane < QK_NOPE, blk * kn, pe) * lax.rsqrt(ms + EPS)
        k_ref[:, h * HEAD_PAD:(h + 1) * HEAD_PAD] = k.astype(BF16)


def _kv_proj(ckv, kpe, wkvb, kn, tabs, tab_index, name):
    n = ckv.shape[0]
    full = lambda shape: pl.BlockSpec(shape, lambda i: (0,) * len(shape))
    tab = pl.BlockSpec((TKV, HEAD_PAD), lambda i: (tab_index(i), 0))
    row = lambda w: pl.BlockSpec((TKV, w), lambda i: (i, 0))
    return pl.pallas_call(
        _kv_proj_kernel,
        grid=(n // TKV,),
        in_specs=[row(KV_LORA), row(HEAD_PAD), full((KV_LORA, HEADS * HEAD_PAD)), full((1, HEAD_PAD)),
                  tab, tab, tab],
        out_specs=[row(HEADS * HEAD_PAD), row(HEADS * HEAD_PAD)],
        out_shape=[jax.ShapeDtypeStruct((n, HEADS * HEAD_PAD), BF16)] * 2,
        compiler_params=_params("arbitrary"),
        name=name,
    )(ckv, kpe, wkvb, kn, *tabs)


def _pair_out(o0, o1):
    lane = lax.broadcasted_iota(jnp.int32, (1, HEAD_PAD), 1)
    return jnp.where(lane < V_HEAD, pltpu.roll(o0, V_HEAD, 1), o1).astype(BF16)


def _attn_ctx_kernel(q_ref, k_ref, kv_ref, o_ref):
    for hp in range(HEADS // 2):
        outs = []
        for j in range(2):
            h0 = (2 * hp + j) * HEAD_PAD
            s = _dot_nt(q_ref[:, h0:h0 + HEAD_PAD], k_ref[:, h0:h0 + HEAD_PAD])
            p = jnp.exp(s - jnp.max(s, axis=-1, keepdims=True))
            l = jnp.sum(p, axis=-1, keepdims=True)
            outs.append(_dot(p.astype(BF16), kv_ref[:, h0:h0 + HEAD_PAD]) / l)
        o_ref[:, hp * HEAD_PAD:(hp + 1) * HEAD_PAD] = _pair_out(*outs)


def _attn_ctx(q, k, kv):
    blk = pl.BlockSpec((CTX_LEN, HEADS * HEAD_PAD), lambda b: (b, 0))
    return pl.pallas_call(
        _attn_ctx_kernel,
        grid=(N_CTX_SEQ,),
        in_specs=[blk, blk, blk],
        out_specs=pl.BlockSpec((CTX_LEN, HEADS * V_HEAD), lambda b: (b, 0)),
        out_shape=jax.ShapeDtypeStruct((T_CTX, HEADS * V_HEAD), BF16),
        compiler_params=_params("arbitrary"),
        name="attn_ctx",
    )(q, k, kv)


def _attn_lat_kernel(q_ref, kc_ref, kvc_ref, kl_ref, kvl_ref, o_ref):
    outs = []
    for j in range(2):
        lanes = slice(j * HEAD_PAD, (j + 1) * HEAD_PAD)
        q = q_ref[:, lanes]
        sc = _dot_nt(q, kc_ref[:, lanes])
        sl = _dot_nt(q, kl_ref[:, lanes])
        m = jnp.maximum(jnp.max(sc, axis=-1, keepdims=True), jnp.max(sl, axis=-1, keepdims=True))
        pc, pl_ = jnp.exp(sc - m), jnp.exp(sl - m)
        l = jnp.sum(pc, axis=-1, keepdims=True) + jnp.sum(pl_, axis=-1, keepdims=True)
        o = _dot(pc.astype(BF16), kvc_ref[:, lanes]) + _dot(pl_.astype(BF16), kvl_ref[:, lanes])
        outs.append(o / l)
    o_ref[...] = _pair_out(*outs)


def _attn_lat(q, kc, kvc, k, kv):
    nq = LAT_LEN // TQ
    q0 = T_CTX // TQ
    kl0 = T_CTX // LAT_LEN
    pair = 2 * HEAD_PAD
    lat = pl.BlockSpec((LAT_LEN, pair), lambda b, hp, t: (kl0 + b, hp))
    ctx = pl.BlockSpec((PAST, pair), lambda b, hp, t: (b, hp))
    return pl.pallas_call(
        _attn_lat_kernel,
        grid=(N_LAT_SEQ, HEADS // 2, nq),
        in_specs=[pl.BlockSpec((TQ, pair), lambda b, hp, t: (q0 + b * nq + t, hp)), ctx, ctx, lat, lat],
        out_specs=pl.BlockSpec((TQ, HEAD_PAD), lambda b, hp, t: (b * nq + t, hp)),
        out_shape=jax.ShapeDtypeStruct((T_LAT, HEADS * V_HEAD), BF16),
        compiler_params=_params("arbitrary", "arbitrary", "arbitrary"),
        name="attn_lat",
    )(q, kc, kvc, k, kv)


def _even_out_kernel(oc_ref, ol_ref, sc_ref, xp_ref, xs_ref, mod_ref, wo_ref, nf_ref, x1_ref, h_ref):
    i = pl.program_id(0)
    ctx = i < CTX_TILES
    attn = jnp.where(ctx, oc_ref[...], ol_ref[...])
    x = jnp.where(ctx, xp_ref[...], xs_ref[...])
    m = _mod_row(mod_ref, i * TM)
    out = _dot(attn, wo_ref[0:HEADS * V_HEAD, :]) + _dot(sc_ref[...], wo_ref[HEADS * V_HEAD:, :])
    x1 = x + m[:, 2 * D:3 * D] * out
    x1_ref[...] = x1
    h_ref[...] = (_rms(x1, nf_ref[0:1, :]) * (1.0 + m[:, 4 * D:5 * D]) + m[:, 3 * D:4 * D]).astype(BF16)


def _even_out(oc, ol, sc, xp, xs, mods, wo, norm_ffn):
    full = lambda shape: pl.BlockSpec(shape, lambda i: (0,) * len(shape))
    first = lambda n: pl.BlockSpec((TM, n), lambda i: (jnp.minimum(i, CTX_TILES - 1), 0))
    second = lambda n: pl.BlockSpec((TM, n), lambda i: (jnp.maximum(i - CTX_TILES, 0), 0))
    row = lambda n: pl.BlockSpec((TM, n), lambda i: (i, 0))
    return pl.pallas_call(
        _even_out_kernel,
        grid=(N_TILES,),
        in_specs=[first(HEADS * V_HEAD), second(HEADS * V_HEAD), row(SC_W), first(D), second(D),
                  _mod_spec(0, 1),
                  full((HEADS * V_HEAD + SC_W, D)), full((2, D))],
        out_specs=[row(D), row(D)],
        out_shape=[jax.ShapeDtypeStruct((T, D), F32), jax.ShapeDtypeStruct((T, D), BF16)],
        compiler_params=_params("arbitrary"),
        name="even_out",
    )(oc, ol, sc, xp, xs, mods, wo, norm_ffn)


FFN_NC = D_FF // FFN_FC
FFN_TM = 512


def _ffn_kernel(h_ref, wg_ref, wu_ref, wd_ref, x1_ref, mod0_ref, mod1_ref, nm_ref, x2_ref, h3_ref,
                wg_all, wu_all, wd_all):
    t = pl.program_id(0)

    @pl.when(t < FFN_NC)
    def _stage():
        wg_all[t] = wg_ref[...].astype(BF16)
        wu_all[t] = wu_ref[...].astype(BF16)
        wd_all[pl.ds(pl.multiple_of(t * FFN_FC, FFN_FC), FFN_FC), :] = wd_ref[...].astype(BF16)

    @pl.when(t >= FFN_NC - 1)
    def _tile():
        h = h_ref[...]
        act = []
        for c in range(FFN_NC):
            act.append((_silu(_dot(h, wg_all[c])) * _dot(h, wu_all[c])).astype(BF16))
        f = _dot(jnp.concatenate(act, axis=1), wd_all[...])
        row0 = (t - (FFN_NC - 1)) * FFN_TM
        m0, m1 = _mod_row(mod0_ref, row0), _mod_row(mod1_ref, row0)
        x2 = x1_ref[...] + m0[:, 5 * D:6 * D] * f
        x2_ref[...] = x2
        h3_ref[...] = (_rms(x2, nm_ref[1:2, :]) * (1.0 + m1[:, D:2 * D]) + m1[:, 0:D]).astype(BF16)


def _ffn(h, w_gu, w_down, x1, mods, norm_mix):
    chunk = lambda t: jnp.minimum(t, FFN_NC - 1)
    row = lambda n: pl.BlockSpec((FFN_TM, n), lambda t: (jnp.maximum(t - (FFN_NC - 1), 0), 0))
    return pl.pallas_call(
        _ffn_kernel,
        grid=(FFN_NC - 1 + T // FFN_TM,),
        in_specs=[row(D),
                  pl.BlockSpec((D, FFN_FC), lambda t: (0, chunk(t))),
                  pl.BlockSpec((D, FFN_FC), lambda t: (0, FFN_NC + chunk(t))),
                  pl.BlockSpec((FFN_FC, D), lambda t: (chunk(t), 0)),
                  row(D), _mod_spec(0, 1), _mod_spec(1, 1), pl.BlockSpec((2, D), lambda t: (0, 0))],
        out_specs=[row(D), row(D)],
        out_shape=[jax.ShapeDtypeStruct((T, D), F32), jax.ShapeDtypeStruct((T, D), BF16)],
        scratch_shapes=[pltpu.VMEM((FFN_NC, D, FFN_FC), BF16), pltpu.VMEM((FFN_NC, D, FFN_FC), BF16),
                        pltpu.VMEM((D_FF, D), BF16)],
        compiler_params=_params("arbitrary"),
        name="ffn_dense",
    )(h, w_gu, w_gu, w_down, x1, mods, mods, norm_mix)


CONF_CB = 256
CONF_SEG = 256
CONF_HALO = 16
CONF_SEGP = CONF_SEG + 2 * CONF_HALO
CONF_PIECE = 64


def _conf_kernel(h_ref, x2_ref, w1_ref, b1_ref, wdw_ref, bdw_ref, lng_ref, lnb_ref, w2_ref, b2_ref,
                 mod_ref, nf_ref, rt_ref, x3_ref, h4_ref, lg_ref, pad_ref, conv_ref):
    i = pl.program_id(0)
    nseg = TM // CONF_SEG
    h = h_ref[...]
    joined = jnp.where(i < CTX_TILES, 0.0, 1.0)
    zeros_halo = jnp.zeros((CONF_HALO, CONF_CB), F32)
    for cb in range(D // CONF_CB):
        c0 = cb * CONF_CB
        a = _dot(h, w1_ref[:, c0:c0 + CONF_CB]) + b1_ref[:, c0:c0 + CONF_CB]
        g = _dot(h, w1_ref[:, D + c0:D + c0 + CONF_CB]) + b1_ref[:, D + c0:D + c0 + CONF_CB]
        u = a * jax.nn.sigmoid(g)
        for s in range(nseg):
            base = s * CONF_SEGP
            top = u[s * CONF_SEG - CONF_HALO:s * CONF_SEG] * joined if s > 0 else zeros_halo
            bot = (u[(s + 1) * CONF_SEG:(s + 1) * CONF_SEG + CONF_HALO] * joined
                   if s < nseg - 1 else zeros_halo)
            pad_ref[0, base:base + CONF_HALO, :] = top
            pad_ref[0, base + CONF_HALO:base + CONF_HALO + CONF_SEG, :] = u[s * CONF_SEG:(s + 1) * CONF_SEG]
            pad_ref[0, base + CONF_HALO + CONF_SEG:base + CONF_SEGP, :] = bot
        p0 = pad_ref[0]
        rows = nseg * CONF_SEGP
        for b in range(1, 8):
            pad_ref[b] = pltpu.roll(p0, rows - b, 0)

        def piece(t, carry):
            s = t // (CONF_SEG // CONF_PIECE)
            q0 = (t % (CONF_SEG // CONF_PIECE)) * CONF_PIECE
            src = pl.multiple_of(s * CONF_SEGP + q0, 8)
            acc = jnp.zeros((CONF_PIECE, CONF_CB), F32)
            for j in range(CONF_K):
                hi, lo = (j + 1) // 8, (j + 1) % 8
                acc = acc + wdw_ref[0, j:j + 1, c0:c0 + CONF_CB] * pad_ref[lo, pl.ds(src + 8 * hi, CONF_PIECE), :]
            dst = pl.multiple_of(s * CONF_SEG + q0, 8)
            conv_ref[pl.ds(dst, CONF_PIECE), c0:c0 + CONF_CB] = acc + bdw_ref[:, c0:c0 + CONF_CB]
            return carry

        lax.fori_loop(0, TM // CONF_PIECE, piece, 0)

    y = conv_ref[...]
    mu = jnp.mean(y, axis=-1, keepdims=True)
    yc = y - mu
    var = jnp.mean(yc * yc, axis=-1, keepdims=True)
    y = _silu(yc * lax.rsqrt(var + EPS) * lng_ref[...] + lnb_ref[...])
    out = _dot(y.astype(BF16), w2_ref[...]) + b2_ref[...]
    m = _mod_row(mod_ref, i * TM)
    x3 = x2_ref[...] + m[:, 2 * D:3 * D] * out
    x3_ref[...] = x3
    h4 = _rms(x3, nf_ref[1:2, :]) * (1.0 + m[:, 4 * D:5 * D]) + m[:, 3 * D:4 * D]
    h4_ref[...] = h4.astype(BF16)
    lg_ref[...] = lax.dot_general(rt_ref[...], h4, (((1,), (1,)), ((), ())),
                                  precision=lax.Precision.HIGHEST, preferred_element_type=F32)


def _conf(h3, x2, w1, b1, wdw, bdw, lng, lnb, w2, b2, mods, norm_ffn1, router_t):
    full = lambda shape: pl.BlockSpec(shape, lambda i: (0,) * len(shape))
    row = lambda n: pl.BlockSpec((TM, n), lambda i: (i, 0))
    return pl.pallas_call(
        _conf_kernel,
        grid=(N_TILES,),
        in_specs=[row(D), row(D), full((D, 2 * D)), full((1, 2 * D)), full((1, CONF_K, D)), full((1, D)),
                  full((1, D)), full((1, D)), full((D, D)), full((1, D)),
                  _mod_spec(1, 1), full((2, D)), full((N_EXP, D))],
        out_specs=[row(D), row(D), pl.BlockSpec((N_EXP, TM), lambda i: (0, i))],
        out_shape=[jax.ShapeDtypeStruct((T, D), F32), jax.ShapeDtypeStruct((T, D), BF16),
                   jax.ShapeDtypeStruct((N_EXP, T), F32)],
        scratch_shapes=[pltpu.VMEM((8, (TM // CONF_SEG) * CONF_SEGP, CONF_CB), F32),
                        pltpu.VMEM((TM, D), F32)],
        compiler_params=_params("arbitrary"),
        name="conformer_conv",
    )(h3, x2, w1, b1, wdw, bdw, lng, lnb, w2, b2, mods, norm_ffn1, router_t)


TB = 256
N_TB = T // TB
SUBS = 8
SM = SUBS * TB
N_SUB_MAX = 2 * T // TB + N_EXP
N_SUP_MAX = N_SUB_MAX // SUBS + N_EXP - 1
YS_ROWS = (N_SUB_MAX + 2) * TB
WIN_ALIGN = 16
WIN = TB + WIN_ALIGN
FIRST_STRIDE = 32
GATHER_BLOCKS = 6


def _route_kernel(lg_ref, g_ref, rank_ref, first_ref):
    lg = lg_ref[...]
    idx = lax.broadcasted_iota(jnp.int32, lg.shape, 0).astype(F32)
    none = float(N_EXP)
    m1 = jnp.max(lg, axis=0, keepdims=True)
    i1 = jnp.min(jnp.where(lg == m1, idx, none), axis=0, keepdims=True)
    rest = jnp.where(idx == i1, -jnp.inf, lg)
    m2 = jnp.max(rest, axis=0, keepdims=True)
    i2 = jnp.min(jnp.where(rest == m2, idx, none), axis=0, keepdims=True)
    e = jnp.exp(m2 - m1)
    w1 = 1.0 / (1.0 + e)
    w2 = e / (1.0 + e)
    g_ref[...] = jnp.where(idx == i1, w1, 0.0) + jnp.where(idx == i2, w2, 0.0)

    mask = jnp.where(idx == i1, 1.0, 0.0) + jnp.where(idx == i2, 1.0, 0.0)
    before = (lax.broadcasted_iota(jnp.int32, (TB, TB), 0) < lax.broadcasted_iota(jnp.int32, (TB, TB), 1))
    before = jnp.where(before, 1.0, 0.0).astype(BF16)
    lane = lax.broadcasted_iota(jnp.int32, (N_EXP, 128), 1)
    carry = jnp.zeros((N_EXP, 1), F32)
    first = jnp.zeros((N_EXP, 128), F32)
    for b in range(N_TB):
        mb = mask[:, b * TB:(b + 1) * TB]
        local = _dot(mb.astype(BF16), before)
        rank_ref[:, b * TB:(b + 1) * TB] = jnp.where(mb > 0.0, local + carry, -1.0)
        first = jnp.where(lane == b, carry, first)
        carry = carry + jnp.sum(mb, axis=1, keepdims=True)
    first_ref[...] = jnp.where(lane == N_TB, carry, first)


def _route(logits_t):
    return pl.pallas_call(
        _route_kernel,
        out_shape=[jax.ShapeDtypeStruct((N_EXP, T), F32), jax.ShapeDtypeStruct((N_EXP, T), F32),
                   jax.ShapeDtypeStruct((N_EXP, 128), F32)],
        compiler_params=pltpu.CompilerParams(vmem_limit_bytes=VMEM_LIMIT),
        name="route",
    )(logits_t)


def _moe_plan(first):
    first = first[:, :FIRST_STRIDE].astype(jnp.int32)
    cnt = first[:, N_TB]
    nt = (cnt + (TB - 1)) >> 8
    off_end = jnp.cumsum(nt)
    off = off_end - nt
    nsub = off_end[-1]
    nsup = (nt + (SUBS - 1)) >> 3
    sup_end = jnp.cumsum(nsup)
    sup_off = sup_end - nsup
    s = jnp.minimum(jnp.arange(N_SUP_MAX), sup_end[-1] - 1)
    valid = jnp.arange(N_SUP_MAX) < sup_end[-1]
    se = jnp.sum(s[:, None] >= sup_end[None, :], axis=1)
    sk0 = (s - sup_off[se]) * SUBS
    sns = jnp.where(valid, jnp.clip(nt[se] - sk0, 0, SUBS), 0)
    sj0 = off[se] + sk0
    base = (jnp.arange(FIRST_STRIDE) * TB)[None, :, None]
    blo = jnp.minimum(jnp.sum(first[:, None, 1:N_TB + 1] <= base, axis=2), N_TB - 1)
    end = jnp.minimum(base + TB, cnt[:, None, None])
    bhi = jnp.maximum(jnp.sum(first[:, None, :N_TB] < end, axis=2) - 1, blo)
    ng = (bhi - blo) // GATHER_BLOCKS + 1
    start = TB * off[:, None] + first[:, :N_TB]
    lead = start & (WIN_ALIGN - 1)
    win = start - lead
    rel = first[:, :N_TB] - lead
    i32 = lambda a: a.astype(jnp.int32)
    return dict(se=i32(se), sk0=i32(sk0), sns=i32(sns), sj0=i32(sj0), nsub=i32(nsub).reshape(1),
                first=i32(first.reshape(-1)), blo=i32(blo.reshape(-1)), ng=i32(ng.reshape(-1)),
                win=i32(win.T.reshape(-1)), rel=i32(rel.T.reshape(-1)))


def _moe_gmm_kernel(se_ref, sk0_ref, sns_ref, sj0_ref, nsub_ref, first_ref, blo_ref, ng_ref,
                    x_ref, rank_ref, gate_ref, wg_ref, wu_ref, wd_ref, ys_ref,
                    xs_ref, gs_ref, yacc_ref, ybuf_ref, acc_ref, gacc_ref, wgb_ref, wub_ref, wdb_ref, sem):
    s, c = pl.program_id(0), pl.program_id(1)
    nc = pl.num_programs(1)
    e, k0, ns = se_ref[s], sk0_ref[s], sns_ref[s]

    def sub_rows(k):
        return pl.ds(pl.multiple_of(k * TB, TB), TB)

    def out_copy(k, row0):
        dst = ys_ref.at[pl.ds(pl.multiple_of(row0 + k * TB, TB), TB)]
        return pltpu.make_async_copy(ybuf_ref.at[sub_rows(k)], dst, sem.at[k])

    @pl.when((ns > 0) & (c == 0))
    def _gather():
        def sub(k, carry):
            base = (k0 + k) * TB
            slot = (lax.broadcasted_iota(jnp.int32, (TB, 1), 0) + base).astype(F32)
            acc_ref[...] = jnp.zeros_like(acc_ref)
            gacc_ref[...] = jnp.zeros_like(gacc_ref)
            blo = blo_ref[e * FIRST_STRIDE + k0 + k]

            def group(g, carry2):
                b0 = blo + g * GATHER_BLOCKS
                t0 = pl.multiple_of(jnp.minimum(b0, N_TB - GATHER_BLOCKS) * TB, TB)
                lo = first_ref[e * FIRST_STRIDE + b0].astype(F32)
                cols = pl.ds(t0, GATHER_BLOCKS * TB)
                hit = rank_ref[pl.ds(e, 1), cols] == jnp.where(slot >= lo, slot, -2.0)
                onehot = jnp.where(hit, 1.0, 0.0).astype(BF16)
                acc_ref[...] += _dot(onehot, x_ref[cols, :])
                gacc_ref[...] += jnp.sum(jnp.where(hit, gate_ref[pl.ds(e, 1), cols], 0.0),
                                         axis=-1, keepdims=True)
                return carry2

            lax.fori_loop(0, ng_ref[e * FIRST_STRIDE + k0 + k], group, 0)
            xs_ref[sub_rows(k), :] = acc_ref[...].astype(BF16)
            gs_ref[sub_rows(k), :] = gacc_ref[...]
            yacc_ref[sub_rows(k), :] = jnp.zeros((TB, D), F32)
            return carry

        lax.fori_loop(0, ns, sub, 0)

    @pl.when(ns > 0)
    def _compute():
        wgb_ref[...] = wg_ref[0].astype(BF16)
        wub_ref[...] = wu_ref[0].astype(BF16)
        wdb_ref[...] = wd_ref[0].astype(BF16)

        def swiglu(rows):
            x = xs_ref[rows, :]
            g = _dot(x, wgb_ref[...])
            u = _dot(x, wub_ref[...])
            yacc_ref[rows, :] += _dot((_silu(g) * u).astype(BF16), wdb_ref[...])

        def pair(k, carry):
            swiglu(pl.ds(pl.multiple_of(k * (2 * TB), 2 * TB), 2 * TB))
            return carry

        lax.fori_loop(0, ns >> 1, pair, 0)

        @pl.when((ns & 1) == 1)
        def _():
            swiglu(sub_rows(ns - 1))

    @pl.when((ns > 0) & (c == nc - 1))
    def _store():
        row0 = sj0_ref[s] * TB

        def put(k, carry):
            ybuf_ref[sub_rows(k), :] = (yacc_ref[sub_rows(k), :] * gs_ref[sub_rows(k), :]).astype(BF16)
            out_copy(k, row0).start()
            return carry

        def done(k, carry):
            out_copy(k, row0).wait()
            return carry

        lax.fori_loop(0, ns, put, 0)
        lax.fori_loop(0, ns, done, 0)

    @pl.when((s == pl.num_programs(0) - 1) & (c == nc - 1))
    def _zero_tail():
        ybuf_ref[0:TB, :] = jnp.zeros((TB, D), BF16)
        nsub = nsub_ref[0]

        def fill(k, carry):
            cp = out_copy(0, (nsub + k) * TB)
            cp.start()
            cp.wait()
            return carry

        lax.fori_loop(0, YS_ROWS // TB - nsub, fill, 0)


def _moe_gmm(plan, h4, rank, gates, w_gu, w_down):
    nc = D_FFE // MOE_FC

    def chunk(s, c, sns):
        return jnp.where(sns[s] > 0, c, nc - 1)

    return pl.pallas_call(
        _moe_gmm_kernel,
        grid_spec=pltpu.PrefetchScalarGridSpec(
            num_scalar_prefetch=8,
            grid=(N_SUP_MAX, nc),
            in_specs=[
                pl.BlockSpec((T, D), lambda s, c, *_: (0, 0), pipeline_mode=pl.Buffered(1)),
                pl.BlockSpec((N_EXP, T), lambda s, c, *_: (0, 0)),
                pl.BlockSpec((N_EXP, T), lambda s, c, *_: (0, 0)),
                pl.BlockSpec((1, D, MOE_FC), lambda s, c, se, sk0, sns, *_: (se[s], 0, chunk(s, c, sns))),
                pl.BlockSpec((1, D, MOE_FC), lambda s, c, se, sk0, sns, *_: (se[s], 0, nc + chunk(s, c, sns))),
                pl.BlockSpec((1, MOE_FC, D), lambda s, c, se, sk0, sns, *_: (se[s], chunk(s, c, sns), 0)),
            ],
            out_specs=pl.BlockSpec(memory_space=pl.ANY),
            scratch_shapes=[
                pltpu.VMEM((SM, D), BF16), pltpu.VMEM((SM, 1), F32), pltpu.VMEM((SM, D), F32),
                pltpu.VMEM((SM, D), BF16), pltpu.VMEM((TB, D), F32), pltpu.VMEM((TB, 1), F32),
                pltpu.VMEM((D, MOE_FC), BF16), pltpu.VMEM((D, MOE_FC), BF16), pltpu.VMEM((MOE_FC, D), BF16),
                pltpu.SemaphoreType.DMA((SUBS,)),
            ],
        ),
        out_shape=jax.ShapeDtypeStruct((YS_ROWS, D), BF16),
        compiler_params=_params("arbitrary", "arbitrary"),
        name="moe_gmm",
    )(plan["se"], plan["sk0"], plan["sns"], plan["sj0"], plan["nsub"], plan["first"], plan["blo"], plan["ng"],
      h4, rank, gates, w_gu, w_gu, w_down)


def _moe_combine_kernel(win_ref, rel_ref, *refs):
    y_refs = refs[:N_EXP]
    rank_ref, x3_ref, mod_ref, yp_ref, ys_ref, ycat_ref = refs[N_EXP:]
    b = pl.program_id(0)
    row = lax.broadcasted_iota(jnp.int32, (WIN, 1), 0)
    onehot = []
    for e in range(N_EXP):
        ycat_ref[e * WIN:(e + 1) * WIN, :] = y_refs[e][...]
        slot = (row + rel_ref[b * N_EXP + e]).astype(F32)
        onehot.append(jnp.where(rank_ref[e:e + 1, :] == slot, 1.0, 0.0).astype(BF16))
    acc = lax.dot_general(jnp.concatenate(onehot, axis=0), ycat_ref[...], (((0,), (0,)), ((), ())),
                          preferred_element_type=F32)
    out = x3_ref[...] + _mod_row(mod_ref, b * TB)[:, 5 * D:6 * D] * acc

    @pl.when(b < T_CTX // TB)
    def _():
        yp_ref[...] = out

    @pl.when(b >= T_CTX // TB)
    def _():
        ys_ref[...] = out


def _moe_combine(plan, ysorted, rank, x3, mods):
    ctx_blocks = T_CTX // TB

    def window(e):
        return pl.BlockSpec((pl.Element(WIN), pl.Element(D)),
                            lambda b, win, rel: (pl.multiple_of(win[b * N_EXP + e], WIN_ALIGN), 0))

    return pl.pallas_call(
        _moe_combine_kernel,
        grid_spec=pltpu.PrefetchScalarGridSpec(
            num_scalar_prefetch=2,
            grid=(N_TB,),
            in_specs=[window(e) for e in range(N_EXP)] + [
                pl.BlockSpec((N_EXP, TB), lambda b, win, rel: (0, b)),
                pl.BlockSpec((TB, D), lambda b, win, rel: (b, 0)),
                _mod_spec(1, 1),
            ],
            out_specs=[pl.BlockSpec((TB, D), lambda b, win, rel: (jnp.minimum(b, ctx_blocks - 1), 0)),
                       pl.BlockSpec((TB, D), lambda b, win, rel: (jnp.maximum(b - ctx_blocks, 0), 0))],
            scratch_shapes=[pltpu.VMEM((N_EXP * WIN, D), BF16)],
        ),
        out_shape=[jax.ShapeDtypeStruct((T_CTX, D), F32), jax.ShapeDtypeStruct((T_LAT, D), F32)],
        compiler_params=_params("arbitrary"),
        name="moe_combine",
    )(plan["win"], plan["rel"], *([ysorted] * N_EXP), rank, x3, mods)


def _pad_heads(w, width):
    lead = w.shape[:-1]
    w = w.reshape(*lead, HEADS, width)
    w = jnp.pad(w, [(0, 0)] * len(lead) + [(0, 0), (0, HEAD_PAD - width)])
    return w.reshape(*lead, HEADS * HEAD_PAD)


def kernel(x_prompt, x_sample, cache_ckv, cache_kpe, c, c_ctx, ada_w, ada_b, norm_mix, norm_ffn, w_in, q_a_norm,
           w_qb, kv_a_norm, w_kvb, q_norm, k_norm, w_sc, w_o, ffn_gu, ffn_down, conv_pw1, conv_pw1_b, conv_dw,
           conv_dw_b, conv_ln_g, conv_ln_b, conv_pw2, conv_pw2_b, router, moe_gu, moe_down):
    xp = x_prompt.reshape(T_CTX, D)
    xs = x_sample.reshape(T_LAT, D)

    mods = _adaln(c_ctx, c, ada_w, ada_b)

    n_a = Q_LORA + KV_LORA + QK_ROPE
    wa = jnp.pad(w_in[0, :, :n_a], ((0, 0), (0, 512 - n_a))).astype(BF16)
    wb = w_in[0, :, n_a:].astype(BF16)
    wqb = _pad_heads(w_qb[0], QK_HEAD).astype(BF16)
    wkvb = w_kvb[0].astype(BF16)
    qn = jnp.pad(q_norm[0], (0, HEAD_PAD - QK_HEAD)).reshape(1, HEAD_PAD)
    kn = jnp.pad(k_norm[0], (0, HEAD_PAD - QK_HEAD)).reshape(1, HEAD_PAD)
    tabs = _rope_tables()

    q, ckv, kpe, sc, state_ckv, state_kpe = _even_proj(xp, xs, mods, norm_mix, wa, wb, q_a_norm, wqb, kv_a_norm,
                                                       qn, w_sc, tabs)

    lat_tile0 = T_CTX // TKV
    ident = LAT_LEN // TKV
    k, kv = _kv_proj(ckv, kpe, wkvb, kn, tabs,
                     lambda i: jnp.where(i < lat_tile0, ident, (i - lat_tile0) % ident), "kv_proj")
    cache_kpe_p = jnp.pad(cache_kpe[:, 0].reshape(N_LAT_SEQ * PAST, QK_ROPE), ((0, 0), (0, HEAD_PAD - QK_ROPE)))
    kc, kvc = _kv_proj(cache_ckv[:, 0].reshape(N_LAT_SEQ * PAST, KV_LORA), cache_kpe_p, wkvb, kn, tabs,
                       lambda i: ident, "kv_proj_cache")

    oc = _attn_ctx(q, k, kv)
    ol = _attn_lat(q, kc, kvc, k, kv)
    x1, h2 = _even_out(oc, ol, sc, xp, xs, mods, w_o[0].astype(BF16), norm_ffn)
    x2, h3 = _ffn(h2, ffn_gu[0], ffn_down[0], x1, mods, norm_mix)

    x3, h4, logits_t = _conf(h3, x2, conv_pw1[0].astype(BF16), conv_pw1_b, conv_dw, conv_dw_b, conv_ln_g,
                             conv_ln_b, conv_pw2[0].astype(BF16), conv_pw2_b, mods, norm_ffn, router[0].T)
    gates, rank, first = _route(logits_t)
    plan = _moe_plan(first)
    ysorted = _moe_gmm(plan, h4, rank, gates, moe_gu[0], moe_down[0])
    yp, ys = _moe_combine(plan, ysorted, rank, x3, mods)

    return (yp.reshape(N_CTX_SEQ, CTX_LEN, D), ys.reshape(N_LAT_SEQ, LAT_LEN, D),
            state_ckv.reshape(N_CTX_SEQ, 1, CTX_LEN, KV_LORA), state_kpe.reshape(N_CTX_SEQ, 1, CTX_LEN, QK_ROPE))
```

```python
import functools

import jax
import jax.numpy as jnp
import numpy as np
from jax import lax
from jax.experimental import pallas as pl
from jax.experimental.pallas import tpu as pltpu

F32 = jnp.float32
BF16 = jnp.bfloat16

D = 1024
N_CTX_SEQ, CTX_LEN = 16, 256
N_LAT_SEQ, LAT_LEN = 2, 1024
T_CTX = N_CTX_SEQ * CTX_LEN
T_LAT = N_LAT_SEQ * LAT_LEN
T = T_CTX + T_LAT
PAST = 256
GRID_W = 64
HEADS = 8
QK_NOPE, QK_ROPE, V_HEAD = 64, 32, 64
QK_HEAD = QK_NOPE + QK_ROPE
HEAD_PAD = 128
Q_LORA, KV_LORA = 256, 128
SC_W = 512
IN0_W = Q_LORA + KV_LORA + QK_ROPE + 3 * SC_W
CONF_K = 31
D_FF = 2816
N_EXP = 8
D_FFE = 3584
EPS = 1e-6
ROPE_THETA = 10000.0

TM = 1024
N_TILES = T // TM
CTX_TILES = T_CTX // TM
TKV = 512
TQ = 256
FFN_FC = 256
MOE_FC = 512
VMEM_LIMIT = 56 * 1024 * 1024


def _dot(a, b):
    return jnp.dot(a, b, preferred_element_type=F32)


def _dot_nt(a, b):
    return lax.dot_general(a, b, (((1,), (1,)), ((), ())), preferred_element_type=F32)


def _rms(x, g):
    return x * lax.rsqrt(jnp.mean(x * x, axis=-1, keepdims=True) + EPS) * g


def _silu(x):
    return x * jax.nn.sigmoid(x)


def _params(*sem):
    return pltpu.CompilerParams(dimension_semantics=sem, vmem_limit_bytes=VMEM_LIMIT)


def _mod_row(mod_ref, row0):
    cond = jnp.maximum(row0 - (T_CTX - LAT_LEN), 0) >> 10
    return mod_ref[0, pl.ds(cond, 1), :]


def _mod_spec(layer, ngrid):
    return pl.BlockSpec((1, 8, 6 * D), lambda *_: (layer, 0, 0))


def _adaln_kernel(cc_ref, c_ref, w_ref, b_ref, o_ref):
    l = pl.program_id(0)
    row = lax.broadcasted_iota(jnp.int32, (8, 1), 0)
    cond = jnp.where(row == 0, cc_ref[...], 0.0)
    for b in range(N_LAT_SEQ):
        cond = jnp.where(row == 1 + b, c_ref[b:b + 1, :], cond)
    o_ref[0] = _dot(_silu(cond).astype(BF16), w_ref[0].astype(BF16)) + b_ref[pl.ds(l, 1), :]


def _adaln(c_ctx, c, ada_w, ada_b):
    depth = ada_w.shape[0]
    tn = 1024
    return pl.pallas_call(
        _adaln_kernel,
        grid=(depth, 6 * D // tn),
        in_specs=[
            pl.BlockSpec((1, D), lambda l, j: (0, 0)),
            pl.BlockSpec((N_LAT_SEQ, D), lambda l, j: (0, 0)),
            pl.BlockSpec((1, D, tn), lambda l, j: (l, 0, j)),
            pl.BlockSpec((depth, tn), lambda l, j: (0, j)),
        ],
        out_specs=pl.BlockSpec((1, 8, tn), lambda l, j: (l, 0, j)),
        out_shape=jax.ShapeDtypeStruct((depth, 8, 6 * D), F32),
        compiler_params=_params("arbitrary", "arbitrary"),
        name="adaln",
    )(c_ctx.reshape(1, D), c, ada_w, ada_b)


def _rope_tables():
    half = QK_ROPE // 2
    nf = half // 2
    pos = np.arange(LAT_LEN)
    inv = ROPE_THETA ** (-np.arange(nf, dtype=np.float64) / nf)
    k = np.arange(QK_ROPE)
    part, idx = k // half, k % half
    p = np.where(part[None, :] == 0, (pos // GRID_W)[:, None], (pos % GRID_W)[:, None])
    ang = p * inv[idx % nf][None, :]
    cos, sin = np.cos(ang), np.sin(ang)
    first = (idx < nf)[None, :]
    s1 = np.where(first, -sin, 0.0)
    s2 = np.where(first, 0.0, sin)

    def place(t, fill):
        tab = np.full((2 * LAT_LEN, HEAD_PAD), fill, np.float32)
        tab[:LAT_LEN, QK_NOPE:QK_HEAD] = t
        return jnp.asarray(tab)

    return place(cos, 1.0), place(s1, 0.0), place(s2, 0.0)


def _rope(blk, cos, s1, s2):
    return blk * cos + pltpu.roll(blk, 8, 1) * s2 + pltpu.roll(blk, HEAD_PAD - 8, 1) * s1


def _head_norm(blk, g):
    ms = jnp.sum(blk * blk, axis=-1, keepdims=True) * (1.0 / QK_HEAD)
    return blk * lax.rsqrt(ms + EPS) * g


def _even_proj_kernel(xp_ref, xs_ref, mod_ref, nm_ref, win_ref, qan_ref, wqb_ref, kvan_ref,
                      qn_ref, wsc_ref, cos_ref, s1_ref, s2_ref,
                      q_ref, ckv_ref, kpe_ref, sc_ref, sckv_ref, skpe_ref, wa_ref, wb_ref):
    i = pl.program_id(0)
    n_a = Q_LORA + KV_LORA + QK_ROPE

    @pl.when(i == 0)
    def _():
        wa_ref[...] = win_ref[0, :, 0:512].astype(BF16)
        wb_ref[...] = win_ref[0, :, n_a:n_a + 3 * SC_W].astype(BF16)

    x = jnp.where(i < CTX_TILES, xp_ref[...], xs_ref[...])
    m = _mod_row(mod_ref, i * TM)
    h = _rms(x, nm_ref[0:1, :]) * (1.0 + m[:, D:2 * D]) + m[:, 0:D]
    hb = h.astype(BF16)

    za = _dot(hb, wa_ref[...])
    ckv = _rms(za[:, Q_LORA:Q_LORA + KV_LORA], kvan_ref[...])
    lane = lax.broadcasted_iota(jnp.int32, (1, HEAD_PAD), 1)
    kpe = jnp.where(lane < QK_ROPE, za[:, Q_LORA + KV_LORA:], 0.0)
    ckv_ref[...] = ckv
    kpe_ref[...] = kpe

    @pl.when(i < CTX_TILES)
    def _():
        sckv_ref[...] = ckv
        skpe_ref[...] = kpe[:, :QK_ROPE]

    qa = _rms(za[:, :Q_LORA], qan_ref[...]).astype(BF16)
    cos, s1, s2 = cos_ref[...], s1_ref[...], s2_ref[...]
    qn = qn_ref[...]
    scale = QK_HEAD ** -0.5
    for hp in range(HEADS // 2):
        qq = _dot(qa, wqb_ref[:, hp * 256:(hp + 1) * 256])
        for j in range(2):
            blk = _head_norm(qq[:, j * HEAD_PAD:(j + 1) * HEAD_PAD], qn)
            blk = _rope(blk, cos, s1, s2) * scale
            h0 = (2 * hp + j) * HEAD_PAD
            q_ref[:, h0:h0 + HEAD_PAD] = blk.astype(BF16)

    gb = _dot(hb, wb_ref[:, 0:SC_W])
    v = _dot(hb, wb_ref[:, SC_W:2 * SC_W]) * _dot(hb, wb_ref[:, 2 * SC_W:3 * SC_W])
    seq = jnp.where(i < CTX_TILES, CTX_LEN, LAT_LEN)
    r = lax.broadcasted_iota(jnp.int32, (TM, 1), 0) & (seq - 1)
    vp = jnp.where(r == 0, 0.0, pltpu.roll(v, 1, 0))
    vn = jnp.where(r == seq - 1, 0.0, pltpu.roll(v, TM - 1, 0))
    w = wsc_ref[0]
    y = w[0:1] * vp + w[1:2] * v + w[2:3] * vn
    sc_ref[...] = (gb * y).astype(BF16)


def _even_proj(xp, xs, mods, norm_mix, w_in, q_a_norm, wqb, kv_a_norm, qn, w_sc, tabs):
    full = lambda shape: pl.BlockSpec(shape, lambda i: (0,) * len(shape))
    tab = pl.BlockSpec((TM, HEAD_PAD), lambda i: (jnp.where(i < CTX_TILES, 1, 0), 0))
    row = lambda n: pl.BlockSpec((TM, n), lambda i: (i, 0))
    ctx_row = lambda n: pl.BlockSpec((TM, n), lambda i: (jnp.minimum(i, CTX_TILES - 1), 0))
    return pl.pallas_call(
        _even_proj_kernel,
        grid=(N_TILES,),
        in_specs=[
            ctx_row(D),
            pl.BlockSpec((TM, D), lambda i: (jnp.maximum(i - CTX_TILES, 0), 0)),
            _mod_spec(0, 1),
            full((2, D)),
            pl.BlockSpec((1, D, IN0_W), lambda i: (0, 0, 0), pipeline_mode=pl.Buffered(1)),
            full((1, Q_LORA)),
            full((Q_LORA, HEADS * HEAD_PAD)), full((1, KV_LORA)), full((1, HEAD_PAD)),
            full((1, 3, SC_W)), tab, tab, tab,
        ],
        out_specs=[row(HEADS * HEAD_PAD), row(KV_LORA), row(HEAD_PAD), row(SC_W),
                   ctx_row(KV_LORA), ctx_row(QK_ROPE)],
        out_shape=[
            jax.ShapeDtypeStruct((T, HEADS * HEAD_PAD), BF16),
            jax.ShapeDtypeStruct((T, KV_LORA), F32),
            jax.ShapeDtypeStruct((T, HEAD_PAD), F32),
            jax.ShapeDtypeStruct((T, SC_W), BF16),
            jax.ShapeDtypeStruct((T_CTX, KV_LORA), F32),
            jax.ShapeDtypeStruct((T_CTX, QK_ROPE), F32),
        ],
        scratch_shapes=[pltpu.VMEM((D, 512), BF16), pltpu.VMEM((D, 3 * SC_W), BF16)],
        compiler_params=_params("arbitrary"),
        name="even_proj",
    )(xp, xs, mods, norm_mix, w_in, q_a_norm, wqb, kv_a_norm, qn, w_sc, *tabs)


def _kv_proj_kernel(ckv_ref, kpe_ref, wkvb_ref, kn_ref, cos_ref, s1_ref, s2_ref, k_ref, kv_ref):
    kv = _dot(ckv_ref[...].astype(BF16), wkvb_ref[...])
    kv_ref[...] = kv.astype(BF16)
    kpe = pltpu.roll(kpe_ref[...], QK_NOPE, 1)
    lane = lax.broadcasted_iota(jnp.int32, (1, HEAD_PAD), 1)
    kn = kn_ref[...]
    pe_sq = jnp.sum(kpe * kpe, axis=-1, keepdims=True)
    pe = _rope(kpe * kn, cos_ref[...], s1_ref[...], s2_ref[...])
    for h in range(HEADS):
        blk = kv[:, h * HEAD_PAD:(h + 1) * HEAD_PAD]
        nope = jnp.where(lane < QK_NOPE, blk, 0.0)
        ms = (jnp.sum(nope * nope, axis=-1, keepdims=True) + pe_sq) * (1.0 / QK_HEAD)
        k = jnp.where(lane < QK_NOPE, blk * kn, pe) * lax.rsqrt(ms + EPS)
        k_ref[:, h * HEAD_PAD:(h + 1) * HEAD_PAD] = k.astype(BF16)


def _kv_proj(ckv, kpe, wkvb, kn, tabs, tab_index, name):
    n = ckv.shape[0]
    full = lambda shape: pl.BlockSpec(shape, lambda i: (0,) * len(shape))
    tab = pl.BlockSpec((TKV, HEAD_PAD), lambda i: (tab_index(i), 0))
    row = lambda w: pl.BlockSpec((TKV, w), lambda i: (i, 0))
    return pl.pallas_call(
        _kv_proj_kernel,
        grid=(n // TKV,),
        in_specs=[row(KV_LORA), row(HEAD_PAD), full((KV_LORA, HEADS * HEAD_PAD)), full((1, HEAD_PAD)),
                  tab, tab, tab],
        out_specs=[row(HEADS * HEAD_PAD), row(HEADS * HEAD_PAD)],
        out_shape=[jax.ShapeDtypeStruct((n, HEADS * HEAD_PAD), BF16)] * 2,
        compiler_params=_params("arbitrary"),
        name=name,
    )(ckv, kpe, wkvb, kn, *tabs)


def _pair_out(o0, o1):
    lane = lax.broadcasted_iota(jnp.int32, (1, HEAD_PAD), 1)
    return jnp.where(lane < V_HEAD, pltpu.roll(o0, V_HEAD, 1), o1).astype(BF16)


def _attn_ctx_kernel(q_ref, k_ref, kv_ref, o_ref):
    for hp in range(HEADS // 2):
        outs = []
        for j in range(2):
            h0 = (2 * hp + j) * HEAD_PAD
            s = _dot_nt(q_ref[:, h0:h0 + HEAD_PAD], k_ref[:, h0:h0 + HEAD_PAD])
            p = jnp.exp(s - jnp.max(s, axis=-1, keepdims=True))
            l = jnp.sum(p, axis=-1, keepdims=True)
            outs.append(_dot(p.astype(BF16), kv_ref[:, h0:h0 + HEAD_PAD]) / l)
        o_ref[:, hp * HEAD_PAD:(hp + 1) * HEAD_PAD] = _pair_out(*outs)


def _attn_ctx(q, k, kv):
    blk = pl.BlockSpec((CTX_LEN, HEADS * HEAD_PAD), lambda b: (b, 0))
    return pl.pallas_call(
        _attn_ctx_kernel,
        grid=(N_CTX_SEQ,),
        in_specs=[blk, blk, blk],
        out_specs=pl.BlockSpec((CTX_LEN, HEADS * V_HEAD), lambda b: (b, 0)),
        out_shape=jax.ShapeDtypeStruct((T_CTX, HEADS * V_HEAD), BF16),
        compiler_params=_params("arbitrary"),
        name="attn_ctx",
    )(q, k, kv)


LAT_HEADS = 4


def _attn_lat_kernel(q_ref, kc_ref, kvc_ref, kl_ref, kvl_ref, o_ref):
    for hp in range(LAT_HEADS // 2):
        outs = []
        for j in range(2):
            h0 = (2 * hp + j) * HEAD_PAD
            lanes = slice(h0, h0 + HEAD_PAD)
            q = q_ref[:, lanes]
            sc = _dot_nt(q, kc_ref[:, lanes])
            sl = _dot_nt(q, kl_ref[:, lanes])
            m = jnp.maximum(jnp.max(sc, axis=-1, keepdims=True), jnp.max(sl, axis=-1, keepdims=True))
            pc, pl_ = jnp.exp(sc - m), jnp.exp(sl - m)
            l = jnp.sum(pc, axis=-1, keepdims=True) + jnp.sum(pl_, axis=-1, keepdims=True)
            o = _dot(pc.astype(BF16), kvc_ref[:, lanes]) + _dot(pl_.astype(BF16), kvl_ref[:, lanes])
            outs.append(o / l)
        o_ref[:, hp * HEAD_PAD:(hp + 1) * HEAD_PAD] = _pair_out(*outs)


def _attn_lat(q, kc, kvc, k, kv):
    nq = LAT_LEN // TQ
    q0 = T_CTX // TQ
    kl0 = T_CTX // LAT_LEN
    width = LAT_HEADS * HEAD_PAD
    lat = pl.BlockSpec((LAT_LEN, width), lambda b, hg, t: (kl0 + b, hg))
    ctx = pl.BlockSpec((PAST, width), lambda b, hg, t: (b, hg))
    return pl.pallas_call(
        _attn_lat_kernel,
        grid=(N_LAT_SEQ, HEADS // LAT_HEADS, nq),
        in_specs=[pl.BlockSpec((TQ, width), lambda b, hg, t: (q0 + b * nq + t, hg)), ctx, ctx, lat, lat],
        out_specs=pl.BlockSpec((TQ, LAT_HEADS * V_HEAD), lambda b, hg, t: (b * nq + t, hg)),
        out_shape=jax.ShapeDtypeStruct((T_LAT, HEADS * V_HEAD), BF16),
        compiler_params=_params("arbitrary", "arbitrary", "arbitrary"),
        name="attn_lat",
    )(q, kc, kvc, k, kv)


def _even_out_kernel(oc_ref, ol_ref, sc_ref, xp_ref, xs_ref, mod_ref, wo_ref, nf_ref, x1_ref, h_ref):
    i = pl.program_id(0)
    ctx = i < CTX_TILES
    attn = jnp.where(ctx, oc_ref[...], ol_ref[...])
    x = jnp.where(ctx, xp_ref[...], xs_ref[...])
    m = _mod_row(mod_ref, i * TM)
    out = _dot(attn, wo_ref[0:HEADS * V_HEAD, :]) + _dot(sc_ref[...], wo_ref[HEADS * V_HEAD:, :])
    x1 = x + m[:, 2 * D:3 * D] * out
    x1_ref[...] = x1
    h_ref[...] = (_rms(x1, nf_ref[0:1, :]) * (1.0 + m[:, 4 * D:5 * D]) + m[:, 3 * D:4 * D]).astype(BF16)


def _even_out(oc, ol, sc, xp, xs, mods, wo, norm_ffn):
    full = lambda shape: pl.BlockSpec(shape, lambda i: (0,) * len(shape))
    first = lambda n: pl.BlockSpec((TM, n), lambda i: (jnp.minimum(i, CTX_TILES - 1), 0))
    second = lambda n: pl.BlockSpec((TM, n), lambda i: (jnp.maximum(i - CTX_TILES, 0), 0))
    row = lambda n: pl.BlockSpec((TM, n), lambda i: (i, 0))
    return pl.pallas_call(
        _even_out_kernel,
        grid=(N_TILES,),
        in_specs=[first(HEADS * V_HEAD), second(HEADS * V_HEAD), row(SC_W), first(D), second(D),
                  _mod_spec(0, 1),
                  full((HEADS * V_HEAD + SC_W, D)), full((2, D))],
        out_specs=[row(D), row(D)],
        out_shape=[jax.ShapeDtypeStruct((T, D), F32), jax.ShapeDtypeStruct((T, D), BF16)],
        compiler_params=_params("arbitrary"),
        name="even_out",
    )(oc, ol, sc, xp, xs, mods, wo, norm_ffn)


FFN_NC = D_FF // FFN_FC
FFN_TM = 512


def _ffn_kernel(h_ref, wg_ref, wu_ref, wd_ref, x1_ref, mod0_ref, mod1_ref, nm_ref, x2_ref, h3_ref,
                wg_all, wu_all, wd_all):
    t = pl.program_id(0)

    @pl.when(t < FFN_NC)
    def _stage():
        wg_all[t] = wg_ref[...].astype(BF16)
        wu_all[t] = wu_ref[...].astype(BF16)
        wd_all[pl.ds(pl.multiple_of(t * FFN_FC, FFN_FC), FFN_FC), :] = wd_ref[...].astype(BF16)

    @pl.when(t >= FFN_NC - 1)
    def _tile():
        h = h_ref[...]
        act = []
        for c in range(FFN_NC):
            act.append((_silu(_dot(h, wg_all[c])) * _dot(h, wu_all[c])).astype(BF16))
        f = _dot(jnp.concatenate(act, axis=1), wd_all[...])
        row0 = (t - (FFN_NC - 1)) * FFN_TM
        m0, m1 = _mod_row(mod0_ref, row0), _mod_row(mod1_ref, row0)
        x2 = x1_ref[...] + m0[:, 5 * D:6 * D] * f
        x2_ref[...] = x2
        h3_ref[...] = (_rms(x2, nm_ref[1:2, :]) * (1.0 + m1[:, D:2 * D]) + m1[:, 0:D]).astype(BF16)


def _ffn(h, w_gu, w_down, x1, mods, norm_mix):
    chunk = lambda t: jnp.minimum(t, FFN_NC - 1)
    row = lambda n: pl.BlockSpec((FFN_TM, n), lambda t: (jnp.maximum(t - (FFN_NC - 1), 0), 0))
    return pl.pallas_call(
        _ffn_kernel,
        grid=(FFN_NC - 1 + T // FFN_TM,),
        in_specs=[row(D),
                  pl.BlockSpec((D, FFN_FC), lambda t: (0, chunk(t))),
                  pl.BlockSpec((D, FFN_FC), lambda t: (0, FFN_NC + chunk(t))),
                  pl.BlockSpec((FFN_FC, D), lambda t: (chunk(t), 0)),
                  row(D), _mod_spec(0, 1), _mod_spec(1, 1), pl.BlockSpec((2, D), lambda t: (0, 0))],
        out_specs=[row(D), row(D)],
        out_shape=[jax.ShapeDtypeStruct((T, D), F32), jax.ShapeDtypeStruct((T, D), BF16)],
        scratch_shapes=[pltpu.VMEM((FFN_NC, D, FFN_FC), BF16), pltpu.VMEM((FFN_NC, D, FFN_FC), BF16),
                        pltpu.VMEM((D_FF, D), BF16)],
        compiler_params=_params("arbitrary"),
        name="ffn_dense",
    )(h, w_gu, w_gu, w_down, x1, mods, mods, norm_mix)


CONF_CB = 256
CONF_SEG = 256
CONF_HALO = 16
CONF_SEGP = CONF_SEG + 2 * CONF_HALO
CONF_PIECE = 64


def _conf_kernel(h_ref, x2_ref, w1_ref, b1_ref, wdw_ref, bdw_ref, lng_ref, lnb_ref, w2_ref, b2_ref,
                 mod_ref, nf_ref, rt_ref, x3_ref, h4_ref, lg_ref, pad_ref, conv_ref):
    i = pl.program_id(0)
    nseg = TM // CONF_SEG
    h = h_ref[...]
    joined = jnp.where(i < CTX_TILES, 0.0, 1.0)
    zeros_halo = jnp.zeros((CONF_HALO, CONF_CB), F32)
    for cb in range(D // CONF_CB):
        c0 = cb * CONF_CB
        a = _dot(h, w1_ref[:, c0:c0 + CONF_CB]) + b1_ref[:, c0:c0 + CONF_CB]
        g = _dot(h, w1_ref[:, D + c0:D + c0 + CONF_CB]) + b1_ref[:, D + c0:D + c0 + CONF_CB]
        u = a * jax.nn.sigmoid(g)
        for s in range(nseg):
            base = s * CONF_SEGP
            top = u[s * CONF_SEG - CONF_HALO:s * CONF_SEG] * joined if s > 0 else zeros_halo
            bot = (u[(s + 1) * CONF_SEG:(s + 1) * CONF_SEG + CONF_HALO] * joined
                   if s < nseg - 1 else zeros_halo)
            pad_ref[0, base:base + CONF_HALO, :] = top
            pad_ref[0, base + CONF_HALO:base + CONF_HALO + CONF_SEG, :] = u[s * CONF_SEG:(s + 1) * CONF_SEG]
            pad_ref[0, base + CONF_HALO + CONF_SEG:base + CONF_SEGP, :] = bot

        p0 = pad_ref[0]
        rows = nseg * CONF_SEGP
        for b in range(1, 8):
            pad_ref[b] = pltpu.roll(p0, rows - b, 0)

        def piece(t, carry):
            s = t // (CONF_SEG // CONF_PIECE)
            q0 = (t % (CONF_SEG // CONF_PIECE)) * CONF_PIECE
            src = pl.multiple_of(s * CONF_SEGP + q0, 8)
            acc = jnp.zeros((CONF_PIECE, CONF_CB), F32)
            for j in range(CONF_K):
                hi, lo = (j + 1) // 8, (j + 1) % 8
                acc = acc + wdw_ref[0, j:j + 1, c0:c0 + CONF_CB] * pad_ref[lo, pl.ds(src + 8 * hi, CONF_PIECE), :]
            dst = pl.multiple_of(s * CONF_SEG + q0, 8)
            conv_ref[pl.ds(dst, CONF_PIECE), c0:c0 + CONF_CB] = acc + bdw_ref[:, c0:c0 + CONF_CB]
            return carry

        lax.fori_loop(0, TM // CONF_PIECE, piece, 0)

    m = _mod_row(mod_ref, i * TM)
    half = TM // 2
    for r0 in (0, half):
        rows = slice(r0, r0 + half)
        y = conv_ref[rows, :]
        mu = jnp.mean(y, axis=-1, keepdims=True)
        yc = y - mu
        var = jnp.mean(yc * yc, axis=-1, keepdims=True)
        y = _silu(yc * lax.rsqrt(var + EPS) * lng_ref[...] + lnb_ref[...])
        out = _dot(y.astype(BF16), w2_ref[...]) + b2_ref[...]
        x3 = x2_ref[rows, :] + m[:, 2 * D:3 * D] * out
        x3_ref[rows, :] = x3
        h4 = _rms(x3, nf_ref[1:2, :]) * (1.0 + m[:, 4 * D:5 * D]) + m[:, 3 * D:4 * D]
        h4_ref[rows, :] = h4.astype(BF16)
        lg_ref[:, rows] = lax.dot_general(rt_ref[...], h4, (((1,), (1,)), ((), ())),
                                          precision=lax.Precision.HIGHEST, preferred_element_type=F32)


def _conf(h3, x2, w1, b1, wdw, bdw, lng, lnb, w2, b2, mods, norm_ffn1, router_t):
    full = lambda shape: pl.BlockSpec(shape, lambda i: (0,) * len(shape))
    row = lambda n: pl.BlockSpec((TM, n), lambda i: (i, 0))
    return pl.pallas_call(
        _conf_kernel,
        grid=(N_TILES,),
        in_specs=[row(D), row(D), full((D, 2 * D)), full((1, 2 * D)), full((1, CONF_K, D)), full((1, D)),
                  full((1, D)), full((1, D)), full((D, D)), full((1, D)),
                  _mod_spec(1, 1), full((2, D)), full((N_EXP, D))],
        out_specs=[row(D), row(D), pl.BlockSpec((N_EXP, TM), lambda i: (0, i))],
        out_shape=[jax.ShapeDtypeStruct((T, D), F32), jax.ShapeDtypeStruct((T, D), BF16),
                   jax.ShapeDtypeStruct((N_EXP, T), F32)],
        scratch_shapes=[pltpu.VMEM((8, (TM // CONF_SEG) * CONF_SEGP, CONF_CB), F32),
                        pltpu.VMEM((TM, D), F32)],
        compiler_params=_params("arbitrary"),
        name="conformer_conv",
    )(h3, x2, w1, b1, wdw, bdw, lng, lnb, w2, b2, mods, norm_ffn1, router_t)


TB = 256
N_TB = T // TB
SUBS = 8
SM = SUBS * TB
N_SUB_MAX = 2 * T // TB + N_EXP
N_SUP_MAX = N_SUB_MAX // SUBS + N_EXP - 1
YS_ROWS = (N_SUB_MAX + 2) * TB
WIN_ALIGN = 16
WIN = TB + WIN_ALIGN
FIRST_STRIDE = 32
GATHER_BLOCKS = 6


def _route_kernel(lg_ref, g_ref, rank_ref, first_ref):
    lg = lg_ref[...]
    idx = lax.broadcasted_iota(jnp.int32, lg.shape, 0).astype(F32)
    none = float(N_EXP)
    m1 = jnp.max(lg, axis=0, keepdims=True)
    i1 = jnp.min(jnp.where(lg == m1, idx, none), axis=0, keepdims=True)
    rest = jnp.where(idx == i1, -jnp.inf, lg)
    m2 = jnp.max(rest, axis=0, keepdims=True)
    i2 = jnp.min(jnp.where(rest == m2, idx, none), axis=0, keepdims=True)
    e = jnp.exp(m2 - m1)
    w1 = 1.0 / (1.0 + e)
    w2 = e / (1.0 + e)
    g_ref[...] = jnp.where(idx == i1, w1, 0.0) + jnp.where(idx == i2, w2, 0.0)

    mask = jnp.where(idx == i1, 1.0, 0.0) + jnp.where(idx == i2, 1.0, 0.0)
    before = (lax.broadcasted_iota(jnp.int32, (TB, TB), 0) < lax.broadcasted_iota(jnp.int32, (TB, TB), 1))
    before = jnp.where(before, 1.0, 0.0).astype(BF16)
    lane = lax.broadcasted_iota(jnp.int32, (N_EXP, 128), 1)
    carry = jnp.zeros((N_EXP, 1), F32)
    first = jnp.zeros((N_EXP, 128), F32)
    for b in range(N_TB):
        mb = mask[:, b * TB:(b + 1) * TB]
        local = _dot(mb.astype(BF16), before)
        rank_ref[:, b * TB:(b + 1) * TB] = jnp.where(mb > 0.0, local + carry, -1.0)
        first = jnp.where(lane == b, carry, first)
        carry = carry + jnp.sum(mb, axis=1, keepdims=True)
    first_ref[...] = jnp.where(lane == N_TB, carry, first)


def _route(logits_t):
    return pl.pallas_call(
        _route_kernel,
        out_shape=[jax.ShapeDtypeStruct((N_EXP, T), F32), jax.ShapeDtypeStruct((N_EXP, T), F32),
                   jax.ShapeDtypeStruct((N_EXP, 128), F32)],
        compiler_params=pltpu.CompilerParams(vmem_limit_bytes=VMEM_LIMIT),
        name="route",
    )(logits_t)


def _moe_plan(first):
    first = first[:, :FIRST_STRIDE].astype(jnp.int32)
    cnt = first[:, N_TB]
    nt = (cnt + (TB - 1)) >> 8
    off_end = jnp.cumsum(nt)
    off = off_end - nt
    nsub = off_end[-1]
    nsup = (nt + (SUBS - 1)) >> 3
    sup_end = jnp.cumsum(nsup)
    sup_off = sup_end - nsup
    s = jnp.minimum(jnp.arange(N_SUP_MAX), sup_end[-1] - 1)
    valid = jnp.arange(N_SUP_MAX) < sup_end[-1]
    se = jnp.sum(s[:, None] >= sup_end[None, :], axis=1)
    sk0 = (s - sup_off[se]) * SUBS
    sns = jnp.where(valid, jnp.clip(nt[se] - sk0, 0, SUBS), 0)
    sj0 = off[se] + sk0
    base = (jnp.arange(FIRST_STRIDE) * TB)[None, :, None]
    blo = jnp.minimum(jnp.sum(first[:, None, 1:N_TB + 1] <= base, axis=2), N_TB - 1)
    end = jnp.minimum(base + TB, cnt[:, None, None])
    bhi = jnp.maximum(jnp.sum(first[:, None, :N_TB] < end, axis=2) - 1, blo)
    ng = (bhi - blo) // GATHER_BLOCKS + 1
    start = TB * off[:, None] + first[:, :N_TB]
    lead = start & (WIN_ALIGN - 1)
    win = start - lead
    rel = first[:, :N_TB] - lead
    i32 = lambda a: a.astype(jnp.int32)
    return dict(se=i32(se), sk0=i32(sk0), sns=i32(sns), sj0=i32(sj0), nsub=i32(nsub).reshape(1),
                first=i32(first.reshape(-1)), blo=i32(blo.reshape(-1)), ng=i32(ng.reshape(-1)),
                win=i32(win.T.reshape(-1)), rel=i32(rel.T.reshape(-1)))


def _moe_gmm_kernel(se_ref, sk0_ref, sns_ref, sj0_ref, nsub_ref, first_ref, blo_ref, ng_ref,
                    x_ref, rank_ref, gate_ref, wg_ref, wu_ref, wd_ref, ys_ref,
                    xs_ref, gs_ref, yacc_ref, ybuf_ref, acc_ref, gacc_ref, wgb_ref, wub_ref, wdb_ref, sem):
    s, c = pl.program_id(0), pl.program_id(1)
    nc = pl.num_programs(1)
    e, k0, ns = se_ref[s], sk0_ref[s], sns_ref[s]

    def sub_rows(k):
        return pl.ds(pl.multiple_of(k * TB, TB), TB)

    def out_copy(k, row0):
        dst = ys_ref.at[pl.ds(pl.multiple_of(row0 + k * TB, TB), TB)]
        return pltpu.make_async_copy(ybuf_ref.at[sub_rows(k)], dst, sem.at[k])

    @pl.when((ns > 0) & (c == 0))
    def _gather():
        def sub(k, carry):
            base = (k0 + k) * TB
            slot = (lax.broadcasted_iota(jnp.int32, (TB, 1), 0) + base).astype(F32)
            acc_ref[...] = jnp.zeros_like(acc_ref)
            gacc_ref[...] = jnp.zeros_like(gacc_ref)
            blo = blo_ref[e * FIRST_STRIDE + k0 + k]

            def group(g, carry2):
                b0 = blo + g * GATHER_BLOCKS
                t0 = pl.multiple_of(jnp.minimum(b0, N_TB - GATHER_BLOCKS) * TB, TB)
                lo = first_ref[e * FIRST_STRIDE + b0].astype(F32)
                cols = pl.ds(t0, GATHER_BLOCKS * TB)
                hit = rank_ref[pl.ds(e, 1), cols] == jnp.where(slot >= lo, slot, -2.0)
                onehot = jnp.where(hit, 1.0, 0.0).astype(BF16)
                acc_ref[...] += _dot(onehot, x_ref[cols, :])
                gacc_ref[...] += jnp.sum(jnp.where(hit, gate_ref[pl.ds(e, 1), cols], 0.0),
                                         axis=-1, keepdims=True)
                return carry2

            lax.fori_loop(0, ng_ref[e * FIRST_STRIDE + k0 + k], group, 0)
            xs_ref[sub_rows(k), :] = acc_ref[...].astype(BF16)
            gs_ref[sub_rows(k), :] = gacc_ref[...]
            yacc_ref[sub_rows(k), :] = jnp.zeros((TB, D), F32)
            return carry

        lax.fori_loop(0, ns, sub, 0)

    @pl.when(ns > 0)
    def _compute():
        wgb_ref[...] = wg_ref[0].astype(BF16)
        wub_ref[...] = wu_ref[0].astype(BF16)
        wdb_ref[...] = wd_ref[0].astype(BF16)

        def swiglu(rows):
            x = xs_ref[rows, :]
            g = _dot(x, wgb_ref[...])
            u = _dot(x, wub_ref[...])
            yacc_ref[rows, :] += _dot((_silu(g) * u).astype(BF16), wdb_ref[...])

        def pair(k, carry):
            swiglu(pl.ds(pl.multiple_of(k * (2 * TB), 2 * TB), 2 * TB))
            return carry

        lax.fori_loop(0, ns >> 1, pair, 0)

        @pl.when((ns & 1) == 1)
        def _():
            swiglu(sub_rows(ns - 1))

    @pl.when((ns > 0) & (c == nc - 1))
    def _store():
        row0 = sj0_ref[s] * TB

        def put(k, carry):
            ybuf_ref[sub_rows(k), :] = (yacc_ref[sub_rows(k), :] * gs_ref[sub_rows(k), :]).astype(BF16)
            out_copy(k, row0).start()
            return carry

        def done(k, carry):
            out_copy(k, row0).wait()
            return carry

        lax.fori_loop(0, ns, put, 0)
        lax.fori_loop(0, ns, done, 0)

    @pl.when((s == pl.num_programs(0) - 1) & (c == nc - 1))
    def _zero_tail():
        ybuf_ref[0:TB, :] = jnp.zeros((TB, D), BF16)
        nsub = nsub_ref[0]

        def fill(k, carry):
            cp = out_copy(0, (nsub + k) * TB)
            cp.start()
            cp.wait()
            return carry

        lax.fori_loop(0, YS_ROWS // TB - nsub, fill, 0)


def _moe_gmm(plan, h4, rank, gates, w_gu, w_down):
    nc = D_FFE // MOE_FC

    def chunk(s, c, sns):
        return jnp.where(sns[s] > 0, c, nc - 1)

    return pl.pallas_call(
        _moe_gmm_kernel,
        grid_spec=pltpu.PrefetchScalarGridSpec(
            num_scalar_prefetch=8,
            grid=(N_SUP_MAX, nc),
            in_specs=[
                pl.BlockSpec((T, D), lambda s, c, *_: (0, 0), pipeline_mode=pl.Buffered(1)),
                pl.BlockSpec((N_EXP, T), lambda s, c, *_: (0, 0)),
                pl.BlockSpec((N_EXP, T), lambda s, c, *_: (0, 0)),
                pl.BlockSpec((1, D, MOE_FC), lambda s, c, se, sk0, sns, *_: (se[s], 0, chunk(s, c, sns))),
                pl.BlockSpec((1, D, MOE_FC), lambda s, c, se, sk0, sns, *_: (se[s], 0, nc + chunk(s, c, sns))),
                pl.BlockSpec((1, MOE_FC, D), lambda s, c, se, sk0, sns, *_: (se[s], chunk(s, c, sns), 0)),
            ],
            out_specs=pl.BlockSpec(memory_space=pl.ANY),
            scratch_shapes=[
                pltpu.VMEM((SM, D), BF16), pltpu.VMEM((SM, 1), F32), pltpu.VMEM((SM, D), F32),
                pltpu.VMEM((SM, D), BF16), pltpu.VMEM((TB, D), F32), pltpu.VMEM((TB, 1), F32),
                pltpu.VMEM((D, MOE_FC), BF16), pltpu.VMEM((D, MOE_FC), BF16), pltpu.VMEM((MOE_FC, D), BF16),
                pltpu.SemaphoreType.DMA((SUBS,)),
            ],
        ),
        out_shape=jax.ShapeDtypeStruct((YS_ROWS, D), BF16),
        compiler_params=_params("arbitrary", "arbitrary"),
        name="moe_gmm",
    )(plan["se"], plan["sk0"], plan["sns"], plan["sj0"], plan["nsub"], plan["first"], plan["blo"], plan["ng"],
      h4, rank, gates, w_gu, w_gu, w_down)


def _moe_combine_kernel(win_ref, rel_ref, *refs):
    y_refs = refs[:N_EXP]
    rank_ref, x3_ref, mod_ref, yp_ref, ys_ref, ycat_ref = refs[N_EXP:]
    b = pl.program_id(0)
    row = lax.broadcasted_iota(jnp.int32, (WIN, 1), 0)
    onehot = []
    for e in range(N_EXP):
        ycat_ref[e * WIN:(e + 1) * WIN, :] = y_refs[e][...]
        slot = (row + rel_ref[b * N_EXP + e]).astype(F32)
        onehot.append(jnp.where(rank_ref[e:e + 1, :] == slot, 1.0, 0.0).astype(BF16))
    acc = lax.dot_general(jnp.concatenate(onehot, axis=0), ycat_ref[...], (((0,), (0,)), ((), ())),
                          preferred_element_type=F32)
    out = x3_ref[...] + _mod_row(mod_ref, b * TB)[:, 5 * D:6 * D] * acc

    @pl.when(b < T_CTX // TB)
    def _():
        yp_ref[...] = out

    @pl.when(b >= T_CTX // TB)
    def _():
        ys_ref[...] = out


def _moe_combine(plan, ysorted, rank, x3, mods):
    ctx_blocks = T_CTX // TB

    def window(e):
        return pl.BlockSpec((pl.Element(WIN), pl.Element(D)),
                            lambda b, win, rel: (pl.multiple_of(win[b * N_EXP + e], WIN_ALIGN), 0))

    return pl.pallas_call(
        _moe_combine_kernel,
        grid_spec=pltpu.PrefetchScalarGridSpec(
            num_scalar_prefetch=2,
            grid=(N_TB,),
            in_specs=[window(e) for e in range(N_EXP)] + [
                pl.BlockSpec((N_EXP, TB), lambda b, win, rel: (0, b)),
                pl.BlockSpec((TB, D), lambda b, win, rel: (b, 0)),
                _mod_spec(1, 1),
            ],
            out_specs=[pl.BlockSpec((TB, D), lambda b, win, rel: (jnp.minimum(b, ctx_blocks - 1), 0)),
                       pl.BlockSpec((TB, D), lambda b, win, rel: (jnp.maximum(b - ctx_blocks, 0), 0))],
            scratch_shapes=[pltpu.VMEM((N_EXP * WIN, D), BF16)],
        ),
        out_shape=[jax.ShapeDtypeStruct((T_CTX, D), F32), jax.ShapeDtypeStruct((T_LAT, D), F32)],
        compiler_params=_params("arbitrary"),
        name="moe_combine",
    )(plan["win"], plan["rel"], *([ysorted] * N_EXP), rank, x3, mods)


def _pad_heads(w, width):
    lead = w.shape[:-1]
    w = w.reshape(*lead, HEADS, width)
    w = jnp.pad(w, [(0, 0)] * len(lead) + [(0, 0), (0, HEAD_PAD - width)])
    return w.reshape(*lead, HEADS * HEAD_PAD)


def kernel(x_prompt, x_sample, cache_ckv, cache_kpe, c, c_ctx, ada_w, ada_b, norm_mix, norm_ffn, w_in, q_a_norm,
           w_qb, kv_a_norm, w_kvb, q_norm, k_norm, w_sc, w_o, ffn_gu, ffn_down, conv_pw1, conv_pw1_b, conv_dw,
           conv_dw_b, conv_ln_g, conv_ln_b, conv_pw2, conv_pw2_b, router, moe_gu, moe_down):
    xp = x_prompt.reshape(T_CTX, D)
    xs = x_sample.reshape(T_LAT, D)

    mods = _adaln(c_ctx, c, ada_w, ada_b)

    wqb = _pad_heads(w_qb[0], QK_HEAD).astype(BF16)
    wkvb = w_kvb[0].astype(BF16)
    qn = jnp.pad(q_norm[0], (0, HEAD_PAD - QK_HEAD)).reshape(1, HEAD_PAD)
    kn = jnp.pad(k_norm[0], (0, HEAD_PAD - QK_HEAD)).reshape(1, HEAD_PAD)
    tabs = _rope_tables()

    q, ckv, kpe, sc, state_ckv, state_kpe = _even_proj(xp, xs, mods, norm_mix, w_in, q_a_norm, wqb, kv_a_norm,
                                                       qn, w_sc, tabs)

    lat_tile0 = T_CTX // TKV
    ident = LAT_LEN // TKV
    k, kv = _kv_proj(ckv, kpe, wkvb, kn, tabs,
                     lambda i: jnp.where(i < lat_tile0, ident, (i - lat_tile0) % ident), "kv_proj")
    cache_kpe_p = jnp.pad(cache_kpe[:, 0].reshape(N_LAT_SEQ * PAST, QK_ROPE), ((0, 0), (0, HEAD_PAD - QK_ROPE)))
    kc, kvc = _kv_proj(cache_ckv[:, 0].reshape(N_LAT_SEQ * PAST, KV_LORA), cache_kpe_p, wkvb, kn, tabs,
                       lambda i: ident, "kv_proj_cache")

    oc = _attn_ctx(q, k, kv)
    ol = _attn_lat(q, kc, kvc, k, kv)
    x1, h2 = _even_out(oc, ol, sc, xp, xs, mods, w_o[0].astype(BF16), norm_ffn)
    x2, h3 = _ffn(h2, ffn_gu[0], ffn_down[0], x1, mods, norm_mix)

    x3, h4, logits_t = _conf(h3, x2, conv_pw1[0].astype(BF16), conv_pw1_b, conv_dw, conv_dw_b, conv_ln_g,
                             conv_ln_b, conv_pw2[0].astype(BF16), conv_pw2_b, mods, norm_ffn, router[0].T)
    gates, rank, first = _route(logits_t)
    plan = _moe_plan(first)
    ysorted = _moe_gmm(plan, h4, rank, gates, moe_gu[0], moe_down[0])
    yp, ys = _moe_combine(plan, ysorted, rank, x3, mods)

    return (yp.reshape(N_CTX_SEQ, CTX_LEN, D), ys.reshape(N_LAT_SEQ, LAT_LEN, D),
            state_ckv.reshape(N_CTX_SEQ, 1, CTX_LEN, KV_LORA), state_kpe.reshape(N_CTX_SEQ, 1, CTX_LEN, QK_ROPE))
```

```python
import functools

import jax
import jax.numpy as jnp
import numpy as np
from jax import lax
from jax.experimental import pallas as pl
from jax.experimental.pallas import tpu as pltpu

F32 = jnp.float32
BF16 = jnp.bfloat16

D = 1024
N_CTX_SEQ, CTX_LEN = 16, 256
N_LAT_SEQ, LAT_LEN = 2, 1024
T_CTX = N_CTX_SEQ * CTX_LEN
T_LAT = N_LAT_SEQ * LAT_LEN
T = T_CTX + T_LAT
PAST = 256
GRID_W = 64
HEADS = 8
QK_NOPE, QK_ROPE, V_HEAD = 64, 32, 64
QK_HEAD = QK_NOPE + QK_ROPE
HEAD_PAD = 128
Q_LORA, KV_LORA = 256, 128
SC_W = 512
IN0_W = Q_LORA + KV_LORA + QK_ROPE + 3 * SC_W
CONF_K = 31
D_FF = 2816
N_EXP = 8
D_FFE = 3584
EPS = 1e-6
ROPE_THETA = 10000.0

TM = 1024
N_TILES = T // TM
CTX_TILES = T_CTX // TM
TKV = 512
TQ = 256
FFN_FC = 256
MOE_FC = 512
VMEM_LIMIT = 56 * 1024 * 1024


def _dot(a, b):
    return jnp.dot(a, b, preferred_element_type=F32)


def _dot_nt(a, b):
    return lax.dot_general(a, b, (((1,), (1,)), ((), ())), preferred_element_type=F32)


def _rms(x, g):
    return x * lax.rsqrt(jnp.mean(x * x, axis=-1, keepdims=True) + EPS) * g


def _silu(x):
    return x * jax.nn.sigmoid(x)


def _params(*sem):
    return pltpu.CompilerParams(dimension_semantics=sem, vmem_limit_bytes=VMEM_LIMIT)


def _mod_row(mod_ref, row0):
    cond = jnp.maximum(row0 - (T_CTX - LAT_LEN), 0) >> 10
    return mod_ref[0, pl.ds(cond, 1), :]


def _mod_spec(layer, ngrid):
    return pl.BlockSpec((1, 8, 6 * D), lambda *_: (layer, 0, 0))


def _adaln_kernel(cc_ref, c_ref, w_ref, b_ref, o_ref):
    l = pl.program_id(0)
    row = lax.broadcasted_iota(jnp.int32, (8, 1), 0)
    cond = jnp.where(row == 0, cc_ref[...], 0.0)
    for b in range(N_LAT_SEQ):
        cond = jnp.where(row == 1 + b, c_ref[b:b + 1, :], cond)
    o_ref[0] = _dot(_silu(cond).astype(BF16), w_ref[0].astype(BF16)) + b_ref[pl.ds(l, 1), :]


def _adaln(c_ctx, c, ada_w, ada_b):
    depth = ada_w.shape[0]
    tn = 2048
    return pl.pallas_call(
        _adaln_kernel,
        grid=(depth, 6 * D // tn),
        in_specs=[
            pl.BlockSpec((1, D), lambda l, j: (0, 0)),
            pl.BlockSpec((N_LAT_SEQ, D), lambda l, j: (0, 0)),
            pl.BlockSpec((1, D, tn), lambda l, j: (l, 0, j)),
            pl.BlockSpec((depth, tn), lambda l, j: (0, j)),
        ],
        out_specs=pl.BlockSpec((1, 8, tn), lambda l, j: (l, 0, j)),
        out_shape=jax.ShapeDtypeStruct((depth, 8, 6 * D), F32),
        compiler_params=_params("arbitrary", "arbitrary"),
        name="adaln",
    )(c_ctx.reshape(1, D), c, ada_w, ada_b)


def _rope_tables():
    half = QK_ROPE // 2
    nf = half // 2
    pos = np.arange(LAT_LEN)
    inv = ROPE_THETA ** (-np.arange(nf, dtype=np.float64) / nf)
    k = np.arange(QK_ROPE)
    part, idx = k // half, k % half
    p = np.where(part[None, :] == 0, (pos // GRID_W)[:, None], (pos % GRID_W)[:, None])
    ang = p * inv[idx % nf][None, :]
    cos, sin = np.cos(ang), np.sin(ang)
    first = (idx < nf)[None, :]
    s1 = np.where(first, -sin, 0.0)
    s2 = np.where(first, 0.0, sin)

    def place(t, fill):
        tab = np.full((2 * LAT_LEN, HEAD_PAD), fill, np.float32)
        tab[:LAT_LEN, QK_NOPE:QK_HEAD] = t
        return jnp.asarray(tab)

    return place(cos, 1.0), place(s1, 0.0), place(s2, 0.0)


def _rope(blk, cos, s1, s2):
    return blk * cos + pltpu.roll(blk, 8, 1) * s2 + pltpu.roll(blk, HEAD_PAD - 8, 1) * s1


def _head_norm(blk, g):
    ms = jnp.sum(blk * blk, axis=-1, keepdims=True) * (1.0 / QK_HEAD)
    return blk * lax.rsqrt(ms + EPS) * g


def _even_proj_kernel(xp_ref, xs_ref, mod_ref, nm_ref, win_ref, qan_ref, wqb_ref, kvan_ref,
                      qn_ref, wsc_ref, cos_ref, s1_ref, s2_ref,
                      q_ref, ckv_ref, kpe_ref, sc_ref, sckv_ref, skpe_ref, wa_ref, wb_ref):
    i = pl.program_id(0)
    n_a = Q_LORA + KV_LORA + QK_ROPE

    @pl.when(i == 0)
    def _():
        wa_ref[...] = win_ref[0, :, 0:512].astype(BF16)
        wb_ref[...] = win_ref[0, :, n_a:n_a + 3 * SC_W].astype(BF16)

    x = jnp.where(i < CTX_TILES, xp_ref[...], xs_ref[...])
    m = _mod_row(mod_ref, i * TM)
    h = _rms(x, nm_ref[0:1, :]) * (1.0 + m[:, D:2 * D]) + m[:, 0:D]
    hb = h.astype(BF16)

    za = _dot(hb, wa_ref[...])
    ckv = _rms(za[:, Q_LORA:Q_LORA + KV_LORA], kvan_ref[...])
    lane = lax.broadcasted_iota(jnp.int32, (1, HEAD_PAD), 1)
    kpe = jnp.where(lane < QK_ROPE, za[:, Q_LORA + KV_LORA:], 0.0)
    ckv_ref[...] = ckv
    kpe_ref[...] = kpe

    @pl.when(i < CTX_TILES)
    def _():
        sckv_ref[...] = ckv
        skpe_ref[...] = kpe[:, :QK_ROPE]

    qa = _rms(za[:, :Q_LORA], qan_ref[...]).astype(BF16)
    cos, s1, s2 = cos_ref[...], s1_ref[...], s2_ref[...]
    qn = qn_ref[...]
    scale = QK_HEAD ** -0.5
    for hp in range(HEADS // 2):
        qq = _dot(qa, wqb_ref[:, hp * 256:(hp + 1) * 256])
        for j in range(2):
            blk = _head_norm(qq[:, j * HEAD_PAD:(j + 1) * HEAD_PAD], qn)
            blk = _rope(blk, cos, s1, s2) * scale
            h0 = (2 * hp + j) * HEAD_PAD
            q_ref[:, h0:h0 + HEAD_PAD] = blk.astype(BF16)

    gb = _dot(hb, wb_ref[:, 0:SC_W])
    v = _dot(hb, wb_ref[:, SC_W:2 * SC_W]) * _dot(hb, wb_ref[:, 2 * SC_W:3 * SC_W])
    seq = jnp.where(i < CTX_TILES, CTX_LEN, LAT_LEN)
    r = lax.broadcasted_iota(jnp.int32, (TM, 1), 0) & (seq - 1)
    vp = jnp.where(r == 0, 0.0, pltpu.roll(v, 1, 0))
    vn = jnp.where(r == seq - 1, 0.0, pltpu.roll(v, TM - 1, 0))
    w = wsc_ref[0]
    y = w[0:1] * vp + w[1:2] * v + w[2:3] * vn
    sc_ref[...] = (gb * y).astype(BF16)


def _even_proj(xp, xs, mods, norm_mix, w_in, q_a_norm, wqb, kv_a_norm, qn, w_sc, tabs):
    full = lambda shape: pl.BlockSpec(shape, lambda i: (0,) * len(shape))
    tab = pl.BlockSpec((TM, HEAD_PAD), lambda i: (jnp.where(i < CTX_TILES, 1, 0), 0))
    row = lambda n: pl.BlockSpec((TM, n), lambda i: (i, 0))
    ctx_row = lambda n: pl.BlockSpec((TM, n), lambda i: (jnp.minimum(i, CTX_TILES - 1), 0))
    return pl.pallas_call(
        _even_proj_kernel,
        grid=(N_TILES,),
        in_specs=[
            ctx_row(D),
            pl.BlockSpec((TM, D), lambda i: (jnp.maximum(i - CTX_TILES, 0), 0)),
            _mod_spec(0, 1),
            full((2, D)),
            pl.BlockSpec((1, D, IN0_W), lambda i: (0, 0, 0), pipeline_mode=pl.Buffered(1)),
            full((1, Q_LORA)),
            full((Q_LORA, HEADS * HEAD_PAD)), full((1, KV_LORA)), full((1, HEAD_PAD)),
            full((1, 3, SC_W)), tab, tab, tab,
        ],
        out_specs=[row(HEADS * HEAD_PAD), row(KV_LORA), row(HEAD_PAD), row(SC_W),
                   ctx_row(KV_LORA), ctx_row(QK_ROPE)],
        out_shape=[
            jax.ShapeDtypeStruct((T, HEADS * HEAD_PAD), BF16),
            jax.ShapeDtypeStruct((T, KV_LORA), F32),
            jax.ShapeDtypeStruct((T, HEAD_PAD), F32),
            jax.ShapeDtypeStruct((T, SC_W), BF16),
            jax.ShapeDtypeStruct((T_CTX, KV_LORA), F32),
            jax.ShapeDtypeStruct((T_CTX, QK_ROPE), F32),
        ],
        scratch_shapes=[pltpu.VMEM((D, 512), BF16), pltpu.VMEM((D, 3 * SC_W), BF16)],
        compiler_params=_params("arbitrary"),
        name="even_proj",
    )(xp, xs, mods, norm_mix, w_in, q_a_norm, wqb, kv_a_norm, qn, w_sc, *tabs)


def _kv_proj_kernel(ckv_ref, kpe_ref, wkvb_ref, kn_ref, cos_ref, s1_ref, s2_ref, k_ref, kv_ref):
    kv = _dot(ckv_ref[...].astype(BF16), wkvb_ref[...])
    kv_ref[...] = kv.astype(BF16)
    kpe = pltpu.roll(kpe_ref[...], QK_NOPE, 1)
    lane = lax.broadcasted_iota(jnp.int32, (1, HEAD_PAD), 1)
    kn = kn_ref[...]
    pe_sq = jnp.sum(kpe * kpe, axis=-1, keepdims=True)
    pe = _rope(kpe * kn, cos_ref[...], s1_ref[...], s2_ref[...])
    for h in range(HEADS):
        blk = kv[:, h * HEAD_PAD:(h + 1) * HEAD_PAD]
        nope = jnp.where(lane < QK_NOPE, blk, 0.0)
        ms = (jnp.sum(nope * nope, axis=-1, keepdims=True) + pe_sq) * (1.0 / QK_HEAD)
        k = jnp.where(lane < QK_NOPE, blk * kn, pe) * lax.rsqrt(ms + EPS)
        k_ref[:, h * HEAD_PAD:(h + 1) * HEAD_PAD] = k.astype(BF16)


def _kv_proj(ckv, kpe, wkvb, kn, tabs, tab_index, name):
    n = ckv.shape[0]
    full = lambda shape: pl.BlockSpec(shape, lambda i: (0,) * len(shape))
    tab = pl.BlockSpec((TKV, HEAD_PAD), lambda i: (tab_index(i), 0))
    row = lambda w: pl.BlockSpec((TKV, w), lambda i: (i, 0))
    return pl.pallas_call(
        _kv_proj_kernel,
        grid=(n // TKV,),
        in_specs=[row(KV_LORA), row(HEAD_PAD), full((KV_LORA, HEADS * HEAD_PAD)), full((1, HEAD_PAD)),
                  tab, tab, tab],
        out_specs=[row(HEADS * HEAD_PAD), row(HEADS * HEAD_PAD)],
        out_shape=[jax.ShapeDtypeStruct((n, HEADS * HEAD_PAD), BF16)] * 2,
        compiler_params=_params("arbitrary"),
        name=name,
    )(ckv, kpe, wkvb, kn, *tabs)


def _pair_out(o0, o1):
    lane = lax.broadcasted_iota(jnp.int32, (1, HEAD_PAD), 1)
    return jnp.where(lane < V_HEAD, pltpu.roll(o0, V_HEAD, 1), o1).astype(BF16)


CTX_SEQS = 2


def _attn_ctx_kernel(q_ref, k_ref, kv_ref, o_ref):
    for b in range(CTX_SEQS):
        rows = slice(b * CTX_LEN, (b + 1) * CTX_LEN)
        for hp in range(HEADS // 2):
            outs = []
            for j in range(2):
                lanes = slice((2 * hp + j) * HEAD_PAD, (2 * hp + j + 1) * HEAD_PAD)
                s = _dot_nt(q_ref[rows, lanes], k_ref[rows, lanes])
                p = jnp.exp(s - jnp.max(s, axis=-1, keepdims=True))
                l = jnp.sum(p, axis=-1, keepdims=True)
                outs.append(_dot(p.astype(BF16), kv_ref[rows, lanes]) / l)
            o_ref[rows, hp * HEAD_PAD:(hp + 1) * HEAD_PAD] = _pair_out(*outs)


def _attn_ctx(q, k, kv):
    blk = pl.BlockSpec((CTX_SEQS * CTX_LEN, HEADS * HEAD_PAD), lambda b: (b, 0))
    return pl.pallas_call(
        _attn_ctx_kernel,
        grid=(N_CTX_SEQ // CTX_SEQS,),
        in_specs=[blk, blk, blk],
        out_specs=pl.BlockSpec((CTX_SEQS * CTX_LEN, HEADS * V_HEAD), lambda b: (b, 0)),
        out_shape=jax.ShapeDtypeStruct((T_CTX, HEADS * V_HEAD), BF16),
        compiler_params=_params("arbitrary"),
        name="attn_ctx",
    )(q, k, kv)


LAT_HEADS = 4


def _attn_lat_kernel(q_ref, kc_ref, kvc_ref, kl_ref, kvl_ref, o_ref):
    for hp in range(LAT_HEADS // 2):
        outs = []
        for j in range(2):
            h0 = (2 * hp + j) * HEAD_PAD
            lanes = slice(h0, h0 + HEAD_PAD)
            q = q_ref[:, lanes]
            sc = _dot_nt(q, kc_ref[:, lanes])
            sl = _dot_nt(q, kl_ref[:, lanes])
            m = jnp.maximum(jnp.max(sc, axis=-1, keepdims=True), jnp.max(sl, axis=-1, keepdims=True))
            pc, pl_ = jnp.exp(sc - m), jnp.exp(sl - m)
            l = jnp.sum(pc, axis=-1, keepdims=True) + jnp.sum(pl_, axis=-1, keepdims=True)
            o = _dot(pc.astype(BF16), kvc_ref[:, lanes]) + _dot(pl_.astype(BF16), kvl_ref[:, lanes])
            outs.append(o / l)
        o_ref[:, hp * HEAD_PAD:(hp + 1) * HEAD_PAD] = _pair_out(*outs)


def _attn_lat(q, kc, kvc, k, kv):
    nq = LAT_LEN // TQ
    q0 = T_CTX // TQ
    kl0 = T_CTX // LAT_LEN
    width = LAT_HEADS * HEAD_PAD
    lat = pl.BlockSpec((LAT_LEN, width), lambda b, hg, t: (kl0 + b, hg))
    ctx = pl.BlockSpec((PAST, width), lambda b, hg, t: (b, hg))
    return pl.pallas_call(
        _attn_lat_kernel,
        grid=(N_LAT_SEQ, HEADS // LAT_HEADS, nq),
        in_specs=[pl.BlockSpec((TQ, width), lambda b, hg, t: (q0 + b * nq + t, hg)), ctx, ctx, lat, lat],
        out_specs=pl.BlockSpec((TQ, LAT_HEADS * V_HEAD), lambda b, hg, t: (b * nq + t, hg)),
        out_shape=jax.ShapeDtypeStruct((T_LAT, HEADS * V_HEAD), BF16),
        compiler_params=_params("arbitrary", "arbitrary", "arbitrary"),
        name="attn_lat",
    )(q, kc, kvc, k, kv)


def _even_out_kernel(oc_ref, ol_ref, sc_ref, xp_ref, xs_ref, mod_ref, wo_ref, nf_ref, x1_ref, h_ref):
    i = pl.program_id(0)
    ctx = i < CTX_TILES
    attn = jnp.where(ctx, oc_ref[...], ol_ref[...])
    x = jnp.where(ctx, xp_ref[...], xs_ref[...])
    m = _mod_row(mod_ref, i * TM)
    out = _dot(attn, wo_ref[0:HEADS * V_HEAD, :]) + _dot(sc_ref[...], wo_ref[HEADS * V_HEAD:, :])
    x1 = x + m[:, 2 * D:3 * D] * out
    x1_ref[...] = x1
    h_ref[...] = (_rms(x1, nf_ref[0:1, :]) * (1.0 + m[:, 4 * D:5 * D]) + m[:, 3 * D:4 * D]).astype(BF16)


def _even_out(oc, ol, sc, xp, xs, mods, wo, norm_ffn):
    full = lambda shape: pl.BlockSpec(shape, lambda i: (0,) * len(shape))
    first = lambda n: pl.BlockSpec((TM, n), lambda i: (jnp.minimum(i, CTX_TILES - 1), 0))
    second = lambda n: pl.BlockSpec((TM, n), lambda i: (jnp.maximum(i - CTX_TILES, 0), 0))
    row = lambda n: pl.BlockSpec((TM, n), lambda i: (i, 0))
    return pl.pallas_call(
        _even_out_kernel,
        grid=(N_TILES,),
        in_specs=[first(HEADS * V_HEAD), second(HEADS * V_HEAD), row(SC_W), first(D), second(D),
                  _mod_spec(0, 1),
                  full((HEADS * V_HEAD + SC_W, D)), full((2, D))],
        out_specs=[row(D), row(D)],
        out_shape=[jax.ShapeDtypeStruct((T, D), F32), jax.ShapeDtypeStruct((T, D), BF16)],
        compiler_params=_params("arbitrary"),
        name="even_out",
    )(oc, ol, sc, xp, xs, mods, wo, norm_ffn)


FFN_NC = D_FF // FFN_FC
FFN_TM = 512


def _ffn_kernel(h_ref, wg_ref, wu_ref, wd_ref, x1_ref, mod0_ref, mod1_ref, nm_ref, x2_ref, h3_ref,
                wg_all, wu_all, wd_all):
    t = pl.program_id(0)

    @pl.when(t < FFN_NC)
    def _stage():
        wg_all[t] = wg_ref[...].astype(BF16)
        wu_all[t] = wu_ref[...].astype(BF16)
        wd_all[pl.ds(pl.multiple_of(t * FFN_FC, FFN_FC), FFN_FC), :] = wd_ref[...].astype(BF16)

    @pl.when(t >= FFN_NC - 1)
    def _tile():
        h = h_ref[...]
        act = []
        for c in range(FFN_NC):
            act.append((_silu(_dot(h, wg_all[c])) * _dot(h, wu_all[c])).astype(BF16))
        f = _dot(jnp.concatenate(act, axis=1), wd_all[...])
        row0 = (t - (FFN_NC - 1)) * FFN_TM
        m0, m1 = _mod_row(mod0_ref, row0), _mod_row(mod1_ref, row0)
        x2 = x1_ref[...] + m0[:, 5 * D:6 * D] * f
        x2_ref[...] = x2
        h3_ref[...] = (_rms(x2, nm_ref[1:2, :]) * (1.0 + m1[:, D:2 * D]) + m1[:, 0:D]).astype(BF16)


def _ffn(h, w_gu, w_down, x1, mods, norm_mix):
    chunk = lambda t: jnp.minimum(t, FFN_NC - 1)
    row = lambda n: pl.BlockSpec((FFN_TM, n), lambda t: (jnp.maximum(t - (FFN_NC - 1), 0), 0))
    return pl.pallas_call(
        _ffn_kernel,
        grid=(FFN_NC - 1 + T // FFN_TM,),
        in_specs=[row(D),
                  pl.BlockSpec((D, FFN_FC), lambda t: (0, chunk(t))),
                  pl.BlockSpec((D, FFN_FC), lambda t: (0, FFN_NC + chunk(t))),
                  pl.BlockSpec((FFN_FC, D), lambda t: (chunk(t), 0)),
                  row(D), _mod_spec(0, 1), _mod_spec(1, 1), pl.BlockSpec((2, D), lambda t: (0, 0))],
        out_specs=[row(D), row(D)],
        out_shape=[jax.ShapeDtypeStruct((T, D), F32), jax.ShapeDtypeStruct((T, D), BF16)],
        scratch_shapes=[pltpu.VMEM((FFN_NC, D, FFN_FC), BF16), pltpu.VMEM((FFN_NC, D, FFN_FC), BF16),
                        pltpu.VMEM((D_FF, D), BF16)],
        compiler_params=_params("arbitrary"),
        name="ffn_dense",
    )(h, w_gu, w_gu, w_down, x1, mods, mods, norm_mix)


CONF_CB = 256
CONF_SEG = 256
CONF_HALO = 16
CONF_SEGP = CONF_SEG + 2 * CONF_HALO
CONF_PIECE = 64


def _conf_kernel(h_ref, x2_ref, w1_ref, b1_ref, wdw_ref, bdw_ref, lng_ref, lnb_ref, w2_ref, b2_ref,
                 mod_ref, nf_ref, rt_ref, x3_ref, h4_ref, lg_ref, pad_ref, conv_ref):
    i = pl.program_id(0)
    nseg = TM // CONF_SEG
    h = h_ref[...]
    joined = jnp.where(i < CTX_TILES, 0.0, 1.0)
    zeros_halo = jnp.zeros((CONF_HALO, CONF_CB), F32)
    for cb in range(D // CONF_CB):
        c0 = cb * CONF_CB
        a = _dot(h, w1_ref[:, c0:c0 + CONF_CB]) + b1_ref[:, c0:c0 + CONF_CB]
        g = _dot(h, w1_ref[:, D + c0:D + c0 + CONF_CB]) + b1_ref[:, D + c0:D + c0 + CONF_CB]
        u = a * jax.nn.sigmoid(g)
        for s in range(nseg):
            base = s * CONF_SEGP
            top = u[s * CONF_SEG - CONF_HALO:s * CONF_SEG] * joined if s > 0 else zeros_halo
            bot = (u[(s + 1) * CONF_SEG:(s + 1) * CONF_SEG + CONF_HALO] * joined
                   if s < nseg - 1 else zeros_halo)
            pad_ref[0, base:base + CONF_HALO, :] = top
            pad_ref[0, base + CONF_HALO:base + CONF_HALO + CONF_SEG, :] = u[s * CONF_SEG:(s + 1) * CONF_SEG]
            pad_ref[0, base + CONF_HALO + CONF_SEG:base + CONF_SEGP, :] = bot

        p0 = pad_ref[0]
        rows = nseg * CONF_SEGP
        for b in range(1, 8):
            pad_ref[b] = pltpu.roll(p0, rows - b, 0)

        def piece(t, carry):
            s = t // (CONF_SEG // CONF_PIECE)
            q0 = (t % (CONF_SEG // CONF_PIECE)) * CONF_PIECE
            src = pl.multiple_of(s * CONF_SEGP + q0, 8)
            acc = jnp.zeros((CONF_PIECE, CONF_CB), F32)
            for j in range(CONF_K):
                hi, lo = (j + 1) // 8, (j + 1) % 8
                acc = acc + wdw_ref[0, j:j + 1, c0:c0 + CONF_CB] * pad_ref[lo, pl.ds(src + 8 * hi, CONF_PIECE), :]
            dst = pl.multiple_of(s * CONF_SEG + q0, 8)
            conv_ref[pl.ds(dst, CONF_PIECE), c0:c0 + CONF_CB] = acc + bdw_ref[:, c0:c0 + CONF_CB]
            return carry

        lax.fori_loop(0, TM // CONF_PIECE, piece, 0)

    m = _mod_row(mod_ref, i * TM)
    half = TM // 2
    for r0 in (0, half):
        rows = slice(r0, r0 + half)
        y = conv_ref[rows, :]
        mu = jnp.mean(y, axis=-1, keepdims=True)
        yc = y - mu
        var = jnp.mean(yc * yc, axis=-1, keepdims=True)
        y = _silu(yc * lax.rsqrt(var + EPS) * lng_ref[...] + lnb_ref[...])
        out = _dot(y.astype(BF16), w2_ref[...]) + b2_ref[...]
        x3 = x2_ref[rows, :] + m[:, 2 * D:3 * D] * out
        x3_ref[rows, :] = x3
        h4 = _rms(x3, nf_ref[1:2, :]) * (1.0 + m[:, 4 * D:5 * D]) + m[:, 3 * D:4 * D]
        h4_ref[rows, :] = h4.astype(BF16)
        lg_ref[:, rows] = lax.dot_general(rt_ref[...], h4, (((1,), (1,)), ((), ())),
                                          precision=lax.Precision.HIGHEST, preferred_element_type=F32)


def _conf(h3, x2, w1, b1, wdw, bdw, lng, lnb, w2, b2, mods, norm_ffn1, router_t):
    full = lambda shape: pl.BlockSpec(shape, lambda i: (0,) * len(shape))
    row = lambda n: pl.BlockSpec((TM, n), lambda i: (i, 0))
    return pl.pallas_call(
        _conf_kernel,
        grid=(N_TILES,),
        in_specs=[row(D), row(D), full((D, 2 * D)), full((1, 2 * D)), full((1, CONF_K, D)), full((1, D)),
                  full((1, D)), full((1, D)), full((D, D)), full((1, D)),
                  _mod_spec(1, 1), full((2, D)), full((N_EXP, D))],
        out_specs=[row(D), row(D), pl.BlockSpec((N_EXP, TM), lambda i: (0, i))],
        out_shape=[jax.ShapeDtypeStruct((T, D), F32), jax.ShapeDtypeStruct((T, D), BF16),
                   jax.ShapeDtypeStruct((N_EXP, T), F32)],
        scratch_shapes=[pltpu.VMEM((8, (TM // CONF_SEG) * CONF_SEGP, CONF_CB), F32),
                        pltpu.VMEM((TM, D), F32)],
        compiler_params=_params("arbitrary"),
        name="conformer_conv",
    )(h3, x2, w1, b1, wdw, bdw, lng, lnb, w2, b2, mods, norm_ffn1, router_t)


TB = 256
N_TB = T // TB
SUBS = 8
SM = SUBS * TB
N_SUB_MAX = 2 * T // TB + N_EXP
N_SUP_MAX = N_SUB_MAX // SUBS + N_EXP - 1
YS_ROWS = (N_SUB_MAX + 2) * TB
WIN_ALIGN = 16
WIN_HALF = TB // 2 + WIN_ALIGN
FIRST_STRIDE = 32
GATHER_BLOCKS = 6


def _route_kernel(lg_ref, g_ref, rank_ref, first_ref):
    lg = lg_ref[...]
    idx = lax.broadcasted_iota(jnp.int32, lg.shape, 0).astype(F32)
    none = float(N_EXP)
    m1 = jnp.max(lg, axis=0, keepdims=True)
    i1 = jnp.min(jnp.where(lg == m1, idx, none), axis=0, keepdims=True)
    rest = jnp.where(idx == i1, -jnp.inf, lg)
    m2 = jnp.max(rest, axis=0, keepdims=True)
    i2 = jnp.min(jnp.where(rest == m2, idx, none), axis=0, keepdims=True)
    e = jnp.exp(m2 - m1)
    w1 = 1.0 / (1.0 + e)
    w2 = e / (1.0 + e)
    g_ref[...] = jnp.where(idx == i1, w1, 0.0) + jnp.where(idx == i2, w2, 0.0)

    mask = jnp.where(idx == i1, 1.0, 0.0) + jnp.where(idx == i2, 1.0, 0.0)
    before = (lax.broadcasted_iota(jnp.int32, (TB, TB), 0) < lax.broadcasted_iota(jnp.int32, (TB, TB), 1))
    before = jnp.where(before, 1.0, 0.0).astype(BF16)
    lane = lax.broadcasted_iota(jnp.int32, (N_EXP, 128), 1)
    carry = jnp.zeros((N_EXP, 1), F32)
    first = jnp.zeros((N_EXP, 128), F32)
    for b in range(N_TB):
        mb = mask[:, b * TB:(b + 1) * TB]
        local = _dot(mb.astype(BF16), before)
        rank_ref[:, b * TB:(b + 1) * TB] = jnp.where(mb > 0.0, local + carry, -1.0)
        first = jnp.where(lane == b, carry, first)
        carry = carry + jnp.sum(mb, axis=1, keepdims=True)
    first_ref[...] = jnp.where(lane == N_TB, carry, first)


def _route(logits_t):
    return pl.pallas_call(
        _route_kernel,
        out_shape=[jax.ShapeDtypeStruct((N_EXP, T), F32), jax.ShapeDtypeStruct((N_EXP, T), F32),
                   jax.ShapeDtypeStruct((N_EXP, 128), F32)],
        compiler_params=pltpu.CompilerParams(vmem_limit_bytes=VMEM_LIMIT),
        name="route",
    )(logits_t)


def _moe_plan(first):
    first = first[:, :FIRST_STRIDE].astype(jnp.int32)
    cnt = first[:, N_TB]
    nt = (cnt + (TB - 1)) >> 8
    off_end = jnp.cumsum(nt)
    off = off_end - nt
    nsub = off_end[-1]
    nsup = (nt + (SUBS - 1)) >> 3
    sup_end = jnp.cumsum(nsup)
    sup_off = sup_end - nsup
    s = jnp.minimum(jnp.arange(N_SUP_MAX), sup_end[-1] - 1)
    valid = jnp.arange(N_SUP_MAX) < sup_end[-1]
    se = jnp.sum(s[:, None] >= sup_end[None, :], axis=1)
    sk0 = (s - sup_off[se]) * SUBS
    sns = jnp.where(valid, jnp.clip(nt[se] - sk0, 0, SUBS), 0)
    sj0 = off[se] + sk0
    base = (jnp.arange(FIRST_STRIDE) * TB)[None, :, None]
    blo = jnp.minimum(jnp.sum(first[:, None, 1:N_TB + 1] <= base, axis=2), N_TB - 1)
    end = jnp.minimum(base + TB, cnt[:, None, None])
    bhi = jnp.maximum(jnp.sum(first[:, None, :N_TB] < end, axis=2) - 1, blo)
    ng = (bhi - blo) // GATHER_BLOCKS + 1
    start = TB * off[:, None] + first[:, :N_TB]
    lead = start & (WIN_ALIGN - 1)
    wina = start - lead
    rel = first[:, :N_TB] - lead
    need = lead + (first[:, 1:N_TB + 1] - first[:, :N_TB]) > WIN_HALF
    winb = lax.cummax(jnp.where(need, wina + WIN_HALF, 0), axis=1)
    wide = jnp.any(need, axis=0)
    i32 = lambda a: a.astype(jnp.int32)
    return dict(se=i32(se), sk0=i32(sk0), sns=i32(sns), sj0=i32(sj0), nsub=i32(nsub).reshape(1),
                first=i32(first.reshape(-1)), blo=i32(blo.reshape(-1)), ng=i32(ng.reshape(-1)),
                wina=i32(wina.T.reshape(-1)), winb=i32(winb.T.reshape(-1)), rel=i32(rel.T.reshape(-1)),
                wide=i32(wide))


def _moe_gmm_kernel(se_ref, sk0_ref, sns_ref, sj0_ref, nsub_ref, first_ref, blo_ref, ng_ref,
                    x_ref, rank_ref, gate_ref, wg_ref, wu_ref, wd_ref, ys_ref,
                    xs_ref, gs_ref, yacc_ref, ybuf_ref, acc_ref, gacc_ref, wgb_ref, wub_ref, wdb_ref, sem):
    s, c = pl.program_id(0), pl.program_id(1)
    nc = pl.num_programs(1)
    e, k0, ns = se_ref[s], sk0_ref[s], sns_ref[s]

    def sub_rows(k):
        return pl.ds(pl.multiple_of(k * TB, TB), TB)

    def out_copy(k, row0):
        dst = ys_ref.at[pl.ds(pl.multiple_of(row0 + k * TB, TB), TB)]
        return pltpu.make_async_copy(ybuf_ref.at[sub_rows(k)], dst, sem.at[k])

    @pl.when((ns > 0) & (c == 0))
    def _gather():
        def sub(k, carry):
            base = (k0 + k) * TB
            slot = (lax.broadcasted_iota(jnp.int32, (TB, 1), 0) + base).astype(F32)
            acc_ref[...] = jnp.zeros_like(acc_ref)
            gacc_ref[...] = jnp.zeros_like(gacc_ref)
            blo = blo_ref[e * FIRST_STRIDE + k0 + k]

            def group(g, carry2):
                b0 = blo + g * GATHER_BLOCKS
                t0 = pl.multiple_of(jnp.minimum(b0, N_TB - GATHER_BLOCKS) * TB, TB)
                lo = first_ref[e * FIRST_STRIDE + b0].astype(F32)
                cols = pl.ds(t0, GATHER_BLOCKS * TB)
                hit = rank_ref[pl.ds(e, 1), cols] == jnp.where(slot >= lo, slot, -2.0)
                onehot = jnp.where(hit, 1.0, 0.0).astype(BF16)
                acc_ref[...] += _dot(onehot, x_ref[cols, :])
                gacc_ref[...] += jnp.sum(jnp.where(hit, gate_ref[pl.ds(e, 1), cols], 0.0),
                                         axis=-1, keepdims=True)
                return carry2

            lax.fori_loop(0, ng_ref[e * FIRST_STRIDE + k0 + k], group, 0)
            xs_ref[sub_rows(k), :] = acc_ref[...].astype(BF16)
            gs_ref[sub_rows(k), :] = gacc_ref[...]
            yacc_ref[sub_rows(k), :] = jnp.zeros((TB, D), F32)
            return carry

        lax.fori_loop(0, ns, sub, 0)

    @pl.when(ns > 0)
    def _compute():
        wgb_ref[...] = wg_ref[0].astype(BF16)
        wub_ref[...] = wu_ref[0].astype(BF16)
        wdb_ref[...] = wd_ref[0].astype(BF16)

        def swiglu(rows):
            x = xs_ref[rows, :]
            g = _dot(x, wgb_ref[...])
            u = _dot(x, wub_ref[...])
            yacc_ref[rows, :] += _dot((_silu(g) * u).astype(BF16), wdb_ref[...])

        def quad(k, carry):
            swiglu(pl.ds(pl.multiple_of(k * (4 * TB), 4 * TB), 4 * TB))
            return carry

        lax.fori_loop(0, ns >> 2, quad, 0)

        @pl.when((ns & 2) != 0)
        def _():
            swiglu(pl.ds(pl.multiple_of((ns >> 2) * (4 * TB), 2 * TB), 2 * TB))

        @pl.when((ns & 1) != 0)
        def _():
            swiglu(sub_rows(ns - 1))

    @pl.when((ns > 0) & (c == nc - 1))
    def _store():
        row0 = sj0_ref[s] * TB

        def put(k, carry):
            ybuf_ref[sub_rows(k), :] = (yacc_ref[sub_rows(k), :] * gs_ref[sub_rows(k), :]).astype(BF16)
            out_copy(k, row0).start()
            return carry

        def done(k, carry):
            out_copy(k, row0).wait()
            return carry

        lax.fori_loop(0, ns, put, 0)
        lax.fori_loop(0, ns, done, 0)

    @pl.when((s == pl.num_programs(0) - 1) & (c == nc - 1))
    def _zero_tail():
        ybuf_ref[0:TB, :] = jnp.zeros((TB, D), BF16)
        nsub = nsub_ref[0]

        def fill(k, carry):
            cp = out_copy(0, (nsub + k) * TB)
            cp.start()
            cp.wait()
            return carry

        lax.fori_loop(0, YS_ROWS // TB - nsub, fill, 0)


def _moe_gmm(plan, h4, rank, gates, w_gu, w_down):
    nc = D_FFE // MOE_FC

    def chunk(s, c, sns):
        return jnp.where(sns[s] > 0, c, nc - 1)

    return pl.pallas_call(
        _moe_gmm_kernel,
        grid_spec=pltpu.PrefetchScalarGridSpec(
            num_scalar_prefetch=8,
            grid=(N_SUP_MAX, nc),
            in_specs=[
                pl.BlockSpec((T, D), lambda s, c, *_: (0, 0), pipeline_mode=pl.Buffered(1)),
                pl.BlockSpec((N_EXP, T), lambda s, c, *_: (0, 0)),
                pl.BlockSpec((N_EXP, T), lambda s, c, *_: (0, 0)),
                pl.BlockSpec((1, D, MOE_FC), lambda s, c, se, sk0, sns, *_: (se[s], 0, chunk(s, c, sns))),
                pl.BlockSpec((1, D, MOE_FC), lambda s, c, se, sk0, sns, *_: (se[s], 0, nc + chunk(s, c, sns))),
                pl.BlockSpec((1, MOE_FC, D), lambda s, c, se, sk0, sns, *_: (se[s], chunk(s, c, sns), 0)),
            ],
            out_specs=pl.BlockSpec(memory_space=pl.ANY),
            scratch_shapes=[
                pltpu.VMEM((SM, D), BF16), pltpu.VMEM((SM, 1), F32), pltpu.VMEM((SM, D), F32),
                pltpu.VMEM((SM, D), BF16), pltpu.VMEM((TB, D), F32), pltpu.VMEM((TB, 1), F32),
                pltpu.VMEM((D, MOE_FC), BF16), pltpu.VMEM((D, MOE_FC), BF16), pltpu.VMEM((MOE_FC, D), BF16),
                pltpu.SemaphoreType.DMA((SUBS,)),
            ],
        ),
        out_shape=jax.ShapeDtypeStruct((YS_ROWS, D), BF16),
        compiler_params=_params("arbitrary", "arbitrary"),
        name="moe_gmm",
    )(plan["se"], plan["sk0"], plan["sns"], plan["sj0"], plan["nsub"], plan["first"], plan["blo"], plan["ng"],
      h4, rank, gates, w_gu, w_gu, w_down)


def _moe_combine_kernel(wina_ref, winb_ref, rel_ref, wide_ref, *refs):
    ya, yb = refs[:N_EXP], refs[N_EXP:2 * N_EXP]
    rank_ref, x3_ref, mod_ref, yp_ref, ys_ref, ycat_ref, acc_ref = refs[2 * N_EXP:]
    b = pl.program_id(0)
    row = lax.broadcasted_iota(jnp.int32, (WIN_HALF, 1), 0)

    def onehot(e, first_row):
        slot = (row + (rel_ref[b * N_EXP + e] + first_row)).astype(F32)
        return jnp.where(rank_ref[e:e + 1, :] == slot, 1.0, 0.0).astype(BF16)

    def gather(y_refs, first_row, base):
        pieces = []
        for e in range(N_EXP):
            ycat_ref[base + e * WIN_HALF:base + (e + 1) * WIN_HALF, :] = y_refs[e][...]
            pieces.append(onehot(e, first_row))
        return pieces

    def combine(pieces, rows):
        return lax.dot_general(jnp.concatenate(pieces, axis=0), ycat_ref[0:rows, :], (((0,), (0,)), ((), ())),
                               preferred_element_type=F32)

    @pl.when(wide_ref[b] == 0)
    def _():
        acc_ref[...] = combine(gather(ya, 0, 0), N_EXP * WIN_HALF)

    @pl.when(wide_ref[b] != 0)
    def _():
        pieces = gather(ya, 0, 0) + gather(yb, WIN_HALF, N_EXP * WIN_HALF)
        acc_ref[...] = combine(pieces, 2 * N_EXP * WIN_HALF)

    out = x3_ref[...] + _mod_row(mod_ref, b * TB)[:, 5 * D:6 * D] * acc_ref[...]

    @pl.when(b < T_CTX // TB)
    def _():
        yp_ref[...] = out

    @pl.when(b >= T_CTX // TB)
    def _():
        ys_ref[...] = out


def _moe_combine(plan, ysorted, rank, x3, mods):
    ctx_blocks = T_CTX // TB

    def window(e, second):
        def index(b, wina, winb, rel, wide):
            start = (winb if second else wina)[b * N_EXP + e]
            return pl.multiple_of(start, WIN_ALIGN), 0
        return pl.BlockSpec((pl.Element(WIN_HALF), pl.Element(D)), index)

    return pl.pallas_call(
        _moe_combine_kernel,
        grid_spec=pltpu.PrefetchScalarGridSpec(
            num_scalar_prefetch=4,
            grid=(N_TB,),
            in_specs=[window(e, False) for e in range(N_EXP)] + [window(e, True) for e in range(N_EXP)] + [
                pl.BlockSpec((N_EXP, TB), lambda b, *_: (0, b)),
                pl.BlockSpec((TB, D), lambda b, *_: (b, 0)),
                _mod_spec(1, 1),
            ],
            out_specs=[pl.BlockSpec((TB, D), lambda b, *_: (jnp.minimum(b, ctx_blocks - 1), 0)),
                       pl.BlockSpec((TB, D), lambda b, *_: (jnp.maximum(b - ctx_blocks, 0), 0))],
            scratch_shapes=[pltpu.VMEM((2 * N_EXP * WIN_HALF, D), BF16), pltpu.VMEM((TB, D), F32)],
        ),
        out_shape=[jax.ShapeDtypeStruct((T_CTX, D), F32), jax.ShapeDtypeStruct((T_LAT, D), F32)],
        compiler_params=_params("arbitrary"),
        name="moe_combine",
    )(plan["wina"], plan["winb"], plan["rel"], plan["wide"], *([ysorted] * (2 * N_EXP)), rank, x3, mods)


def _pad_heads(w, width):
    lead = w.shape[:-1]
    w = w.reshape(*lead, HEADS, width)
    w = jnp.pad(w, [(0, 0)] * len(lead) + [(0, 0), (0, HEAD_PAD - width)])
    return w.reshape(*lead, HEADS * HEAD_PAD)


def kernel(x_prompt, x_sample, cache_ckv, cache_kpe, c, c_ctx, ada_w, ada_b, norm_mix, norm_ffn, w_in, q_a_norm,
           w_qb, kv_a_norm, w_kvb, q_norm, k_norm, w_sc, w_o, ffn_gu, ffn_down, conv_pw1, conv_pw1_b, conv_dw,
           conv_dw_b, conv_ln_g, conv_ln_b, conv_pw2, conv_pw2_b, router, moe_gu, moe_down):
    xp = x_prompt.reshape(T_CTX, D)
    xs = x_sample.reshape(T_LAT, D)

    mods = _adaln(c_ctx, c, ada_w, ada_b)

    wqb = _pad_heads(w_qb[0], QK_HEAD).astype(BF16)
    wkvb = w_kvb[0].astype(BF16)
    qn = jnp.pad(q_norm[0], (0, HEAD_PAD - QK_HEAD)).reshape(1, HEAD_PAD)
    kn = jnp.pad(k_norm[0], (0, HEAD_PAD - QK_HEAD)).reshape(1, HEAD_PAD)
    tabs = _rope_tables()

    q, ckv, kpe, sc, state_ckv, state_kpe = _even_proj(xp, xs, mods, norm_mix, w_in, q_a_norm, wqb, kv_a_norm,
                                                       qn, w_sc, tabs)

    lat_tile0 = T_CTX // TKV
    ident = LAT_LEN // TKV
    k, kv = _kv_proj(ckv, kpe, wkvb, kn, tabs,
                     lambda i: jnp.where(i < lat_tile0, ident, (i - lat_tile0) % ident), "kv_proj")
    cache_kpe_p = jnp.pad(cache_kpe[:, 0].reshape(N_LAT_SEQ * PAST, QK_ROPE), ((0, 0), (0, HEAD_PAD - QK_ROPE)))
    kc, kvc = _kv_proj(cache_ckv[:, 0].reshape(N_LAT_SEQ * PAST, KV_LORA), cache_kpe_p, wkvb, kn, tabs,
                       lambda i: ident, "kv_proj_cache")

    oc = _attn_ctx(q, k, kv)
    ol = _attn_lat(q, kc, kvc, k, kv)
    x1, h2 = _even_out(oc, ol, sc, xp, xs, mods, w_o[0].astype(BF16), norm_ffn)
    x2, h3 = _ffn(h2, ffn_gu[0], ffn_down[0], x1, mods, norm_mix)

    x3, h4, logits_t = _conf(h3, x2, conv_pw1[0].astype(BF16), conv_pw1_b, conv_dw, conv_dw_b, conv_ln_g,
                             conv_ln_b, conv_pw2[0].astype(BF16), conv_pw2_b, mods, norm_ffn, router[0].T)
    gates, rank, first = _route(logits_t)
    plan = _moe_plan(first)
    ysorted = _moe_gmm(plan, h4, rank, gates, moe_gu[0], moe_down[0])
    yp, ys = _moe_combine(plan, ysorted, rank, x3, mods)

    return (yp.reshape(N_CTX_SEQ, CTX_LEN, D), ys.reshape(N_LAT_SEQ, LAT_LEN, D),
            state_ckv.reshape(N_CTX_SEQ, 1, CTX_LEN, KV_LORA), state_kpe.reshape(N_CTX_SEQ, 1, CTX_LEN, QK_ROPE))
```

```python
import functools

import jax
import jax.numpy as jnp
import numpy as np
from jax import lax
from jax.experimental import pallas as pl
from jax.experimental.pallas import tpu as pltpu

F32 = jnp.float32
BF16 = jnp.bfloat16

D = 1024
N_CTX_SEQ, CTX_LEN = 16, 256
N_LAT_SEQ, LAT_LEN = 2, 1024
T_CTX = N_CTX_SEQ * CTX_LEN
T_LAT = N_LAT_SEQ * LAT_LEN
T = T_CTX + T_LAT
PAST = 256
GRID_W = 64
HEADS = 8
QK_NOPE, QK_ROPE, V_HEAD = 64, 32, 64
QK_HEAD = QK_NOPE + QK_ROPE
HEAD_PAD = 128
Q_LORA, KV_LORA = 256, 128
SC_W = 512
IN0_W = Q_LORA + KV_LORA + QK_ROPE + 3 * SC_W
CONF_K = 31
D_FF = 2816
N_EXP = 8
D_FFE = 3584
EPS = 1e-6
ROPE_THETA = 10000.0

TM = 1024
N_TILES = T // TM
CTX_TILES = T_CTX // TM
TKV = 512
TQ = 256
FFN_FC = 256
MOE_FC = 512
VMEM_LIMIT = 56 * 1024 * 1024


def _dot(a, b):
    return jnp.dot(a, b, preferred_element_type=F32)


def _dot_nt(a, b):
    return lax.dot_general(a, b, (((1,), (1,)), ((), ())), preferred_element_type=F32)


def _rms(x, g):
    return x * lax.rsqrt(jnp.mean(x * x, axis=-1, keepdims=True) + EPS) * g


def _silu(x):
    return x * jax.nn.sigmoid(x)


def _params(*sem):
    return pltpu.CompilerParams(dimension_semantics=sem, vmem_limit_bytes=VMEM_LIMIT)


def _mod_row(mod_ref, row0):
    cond = jnp.maximum(row0 - (T_CTX - LAT_LEN), 0) >> 10
    return mod_ref[0, pl.ds(cond, 1), :]


def _mod_spec(layer, ngrid):
    return pl.BlockSpec((1, 8, 6 * D), lambda *_: (layer, 0, 0))


def _adaln_kernel(cc_ref, c_ref, w_ref, b_ref, o_ref):
    l = pl.program_id(0)
    row = lax.broadcasted_iota(jnp.int32, (8, 1), 0)
    cond = jnp.where(row == 0, cc_ref[...], 0.0)
    for b in range(N_LAT_SEQ):
        cond = jnp.where(row == 1 + b, c_ref[b:b + 1, :], cond)
    o_ref[0] = _dot(_silu(cond).astype(BF16), w_ref[0].astype(BF16)) + b_ref[pl.ds(l, 1), :]


def _adaln(c_ctx, c, ada_w, ada_b):
    depth = ada_w.shape[0]
    tn = 2048
    return pl.pallas_call(
        _adaln_kernel,
        grid=(depth, 6 * D // tn),
        in_specs=[
            pl.BlockSpec((1, D), lambda l, j: (0, 0)),
            pl.BlockSpec((N_LAT_SEQ, D), lambda l, j: (0, 0)),
            pl.BlockSpec((1, D, tn), lambda l, j: (l, 0, j)),
            pl.BlockSpec((depth, tn), lambda l, j: (0, j)),
        ],
        out_specs=pl.BlockSpec((1, 8, tn), lambda l, j: (l, 0, j)),
        out_shape=jax.ShapeDtypeStruct((depth, 8, 6 * D), F32),
        compiler_params=_params("arbitrary", "arbitrary"),
        name="adaln",
    )(c_ctx.reshape(1, D), c, ada_w, ada_b)


def _rope_tables():
    half = QK_ROPE // 2
    nf = half // 2
    pos = np.arange(LAT_LEN)
    inv = ROPE_THETA ** (-np.arange(nf, dtype=np.float64) / nf)
    k = np.arange(QK_ROPE)
    part, idx = k // half, k % half
    p = np.where(part[None, :] == 0, (pos // GRID_W)[:, None], (pos % GRID_W)[:, None])
    ang = p * inv[idx % nf][None, :]
    cos, sin = np.cos(ang), np.sin(ang)
    first = (idx < nf)[None, :]
    s1 = np.where(first, -sin, 0.0)
    s2 = np.where(first, 0.0, sin)

    def place(t, fill):
        tab = np.full((2 * LAT_LEN, HEAD_PAD), fill, np.float32)
        tab[:LAT_LEN, QK_NOPE:QK_HEAD] = t
        return jnp.asarray(tab)

    return place(cos, 1.0), place(s1, 0.0), place(s2, 0.0)


def _rope(blk, cos, s1, s2):
    return blk * cos + pltpu.roll(blk, 8, 1) * s2 + pltpu.roll(blk, HEAD_PAD - 8, 1) * s1


def _head_norm(blk, g):
    ms = jnp.sum(blk * blk, axis=-1, keepdims=True) * (1.0 / QK_HEAD)
    return blk * lax.rsqrt(ms + EPS) * g


def _even_proj_kernel(xp_ref, xs_ref, mod_ref, nm_ref, win_ref, qan_ref, wqb_ref, kvan_ref,
                      qn_ref, wsc_ref, cos_ref, s1_ref, s2_ref,
                      q_ref, ckv_ref, kpe_ref, sc_ref, sckv_ref, skpe_ref, wa_ref, wb_ref):
    i = pl.program_id(0)
    n_a = Q_LORA + KV_LORA + QK_ROPE

    @pl.when(i == 0)
    def _():
        wa_ref[...] = win_ref[0, :, 0:512].astype(BF16)
        wb_ref[...] = win_ref[0, :, n_a:n_a + 3 * SC_W].astype(BF16)

    x = jnp.where(i < CTX_TILES, xp_ref[...], xs_ref[...])
    m = _mod_row(mod_ref, i * TM)
    h = _rms(x, nm_ref[0:1, :]) * (1.0 + m[:, D:2 * D]) + m[:, 0:D]
    hb = h.astype(BF16)

    za = _dot(hb, wa_ref[...])
    ckv = _rms(za[:, Q_LORA:Q_LORA + KV_LORA], kvan_ref[...])
    lane = lax.broadcasted_iota(jnp.int32, (1, HEAD_PAD), 1)
    kpe = jnp.where(lane < QK_ROPE, za[:, Q_LORA + KV_LORA:], 0.0)
    ckv_ref[...] = ckv
    kpe_ref[...] = kpe

    @pl.when(i < CTX_TILES)
    def _():
        sckv_ref[...] = ckv
        skpe_ref[...] = kpe[:, :QK_ROPE]

    qa = _rms(za[:, :Q_LORA], qan_ref[...]).astype(BF16)
    cos, s1, s2 = cos_ref[...], s1_ref[...], s2_ref[...]
    qn = qn_ref[...]
    scale = QK_HEAD ** -0.5
    for hp in range(HEADS // 2):
        qq = _dot(qa, wqb_ref[:, hp * 256:(hp + 1) * 256])
        for j in range(2):
            blk = _head_norm(qq[:, j * HEAD_PAD:(j + 1) * HEAD_PAD], qn)
            blk = _rope(blk, cos, s1, s2) * scale
            h0 = (2 * hp + j) * HEAD_PAD
            q_ref[:, h0:h0 + HEAD_PAD] = blk.astype(BF16)

    gb = _dot(hb, wb_ref[:, 0:SC_W])
    v = _dot(hb, wb_ref[:, SC_W:2 * SC_W]) * _dot(hb, wb_ref[:, 2 * SC_W:3 * SC_W])
    seq = jnp.where(i < CTX_TILES, CTX_LEN, LAT_LEN)
    r = lax.broadcasted_iota(jnp.int32, (TM, 1), 0) & (seq - 1)
    vp = jnp.where(r == 0, 0.0, pltpu.roll(v, 1, 0))
    vn = jnp.where(r == seq - 1, 0.0, pltpu.roll(v, TM - 1, 0))
    w = wsc_ref[0]
    y = w[0:1] * vp + w[1:2] * v + w[2:3] * vn
    sc_ref[...] = (gb * y).astype(BF16)


def _even_proj(xp, xs, mods, norm_mix, w_in, q_a_norm, wqb, kv_a_norm, qn, w_sc, tabs):
    full = lambda shape: pl.BlockSpec(shape, lambda i: (0,) * len(shape))
    tab = pl.BlockSpec((TM, HEAD_PAD), lambda i: (jnp.where(i < CTX_TILES, 1, 0), 0))
    row = lambda n: pl.BlockSpec((TM, n), lambda i: (i, 0))
    ctx_row = lambda n: pl.BlockSpec((TM, n), lambda i: (jnp.minimum(i, CTX_TILES - 1), 0))
    return pl.pallas_call(
        _even_proj_kernel,
        grid=(N_TILES,),
        in_specs=[
            ctx_row(D),
            pl.BlockSpec((TM, D), lambda i: (jnp.maximum(i - CTX_TILES, 0), 0)),
            _mod_spec(0, 1),
            full((2, D)),
            pl.BlockSpec((1, D, IN0_W), lambda i: (0, 0, 0), pipeline_mode=pl.Buffered(1)),
            full((1, Q_LORA)),
            full((Q_LORA, HEADS * HEAD_PAD)), full((1, KV_LORA)), full((1, HEAD_PAD)),
            full((1, 3, SC_W)), tab, tab, tab,
        ],
        out_specs=[row(HEADS * HEAD_PAD), row(KV_LORA), row(HEAD_PAD), row(SC_W),
                   ctx_row(KV_LORA), ctx_row(QK_ROPE)],
        out_shape=[
            jax.ShapeDtypeStruct((T, HEADS * HEAD_PAD), BF16),
            jax.ShapeDtypeStruct((T, KV_LORA), F32),
            jax.ShapeDtypeStruct((T, HEAD_PAD), F32),
            jax.ShapeDtypeStruct((T, SC_W), BF16),
            jax.ShapeDtypeStruct((T_CTX, KV_LORA), F32),
            jax.ShapeDtypeStruct((T_CTX, QK_ROPE), F32),
        ],
        scratch_shapes=[pltpu.VMEM((D, 512), BF16), pltpu.VMEM((D, 3 * SC_W), BF16)],
        compiler_params=_params("arbitrary"),
        name="even_proj",
    )(xp, xs, mods, norm_mix, w_in, q_a_norm, wqb, kv_a_norm, qn, w_sc, *tabs)


def _kv_proj_kernel(ckv_ref, kpe_ref, wkvb_ref, kn_ref, cos_ref, s1_ref, s2_ref, k_ref, kv_ref):
    kv = _dot(ckv_ref[...].astype(BF16), wkvb_ref[...])
    kv_ref[...] = kv.astype(BF16)
    kpe = pltpu.roll(kpe_ref[...], QK_NOPE, 1)
    lane = lax.broadcasted_iota(jnp.int32, (1, HEAD_PAD), 1)
    kn = kn_ref[...]
    pe_sq = jnp.sum(kpe * kpe, axis=-1, keepdims=True)
    pe = _rope(kpe * kn, cos_ref[...], s1_ref[...], s2_ref[...])
    for h in range(HEADS):
        blk = kv[:, h * HEAD_PAD:(h + 1) * HEAD_PAD]
        nope = jnp.where(lane < QK_NOPE, blk, 0.0)
        ms = (jnp.sum(nope * nope, axis=-1, keepdims=True) + pe_sq) * (1.0 / QK_HEAD)
        k = jnp.where(lane < QK_NOPE, blk * kn, pe) * lax.rsqrt(ms + EPS)
        k_ref[:, h * HEAD_PAD:(h + 1) * HEAD_PAD] = k.astype(BF16)


def _kv_proj(ckv, kpe, wkvb, kn, tabs, tab_index, name):
    n = ckv.shape[0]
    full = lambda shape: pl.BlockSpec(shape, lambda i: (0,) * len(shape))
    tab = pl.BlockSpec((TKV, HEAD_PAD), lambda i: (tab_index(i), 0))
    row = lambda w: pl.BlockSpec((TKV, w), lambda i: (i, 0))
    return pl.pallas_call(
        _kv_proj_kernel,
        grid=(n // TKV,),
        in_specs=[row(KV_LORA), row(HEAD_PAD), full((KV_LORA, HEADS * HEAD_PAD)), full((1, HEAD_PAD)),
                  tab, tab, tab],
        out_specs=[row(HEADS * HEAD_PAD), row(HEADS * HEAD_PAD)],
        out_shape=[jax.ShapeDtypeStruct((n, HEADS * HEAD_PAD), BF16)] * 2,
        compiler_params=_params("arbitrary"),
        name=name,
    )(ckv, kpe, wkvb, kn, *tabs)


def _pair_out(o0, o1):
    lane = lax.broadcasted_iota(jnp.int32, (1, HEAD_PAD), 1)
    return jnp.where(lane < V_HEAD, pltpu.roll(o0, V_HEAD, 1), o1).astype(BF16)


CTX_SEQS = 2


def _attn_ctx_kernel(q_ref, k_ref, kv_ref, o_ref):
    for b in range(CTX_SEQS):
        rows = slice(b * CTX_LEN, (b + 1) * CTX_LEN)
        for hp in range(HEADS // 2):
            outs = []
            for j in range(2):
                lanes = slice((2 * hp + j) * HEAD_PAD, (2 * hp + j + 1) * HEAD_PAD)
                s = _dot_nt(q_ref[rows, lanes], k_ref[rows, lanes])
                p = jnp.exp(s - jnp.max(s, axis=-1, keepdims=True))
                l = jnp.sum(p, axis=-1, keepdims=True)
                outs.append(_dot(p.astype(BF16), kv_ref[rows, lanes]) / l)
            o_ref[rows, hp * HEAD_PAD:(hp + 1) * HEAD_PAD] = _pair_out(*outs)


def _attn_ctx(q, k, kv):
    blk = pl.BlockSpec((CTX_SEQS * CTX_LEN, HEADS * HEAD_PAD), lambda b: (b, 0))
    return pl.pallas_call(
        _attn_ctx_kernel,
        grid=(N_CTX_SEQ // CTX_SEQS,),
        in_specs=[blk, blk, blk],
        out_specs=pl.BlockSpec((CTX_SEQS * CTX_LEN, HEADS * V_HEAD), lambda b: (b, 0)),
        out_shape=jax.ShapeDtypeStruct((T_CTX, HEADS * V_HEAD), BF16),
        compiler_params=_params("arbitrary"),
        name="attn_ctx",
    )(q, k, kv)


LAT_HEADS = 4


def _attn_lat_kernel(q_ref, kc_ref, kvc_ref, kl_ref, kvl_ref, o_ref):
    for hp in range(LAT_HEADS // 2):
        outs = []
        for j in range(2):
            h0 = (2 * hp + j) * HEAD_PAD
            lanes = slice(h0, h0 + HEAD_PAD)
            q = q_ref[:, lanes]
            sc = _dot_nt(q, kc_ref[:, lanes])
            sl = _dot_nt(q, kl_ref[:, lanes])
            m = jnp.maximum(jnp.max(sc, axis=-1, keepdims=True), jnp.max(sl, axis=-1, keepdims=True))
            pc, pl_ = jnp.exp(sc - m), jnp.exp(sl - m)
            l = jnp.sum(pc, axis=-1, keepdims=True) + jnp.sum(pl_, axis=-1, keepdims=True)
            o = _dot(pc.astype(BF16), kvc_ref[:, lanes]) + _dot(pl_.astype(BF16), kvl_ref[:, lanes])
            outs.append(o / l)
        o_ref[:, hp * HEAD_PAD:(hp + 1) * HEAD_PAD] = _pair_out(*outs)


def _attn_lat(q, kc, kvc, k, kv):
    nq = LAT_LEN // TQ
    q0 = T_CTX // TQ
    kl0 = T_CTX // LAT_LEN
    width = LAT_HEADS * HEAD_PAD
    lat = pl.BlockSpec((LAT_LEN, width), lambda b, hg, t: (kl0 + b, hg))
    ctx = pl.BlockSpec((PAST, width), lambda b, hg, t: (b, hg))
    return pl.pallas_call(
        _attn_lat_kernel,
        grid=(N_LAT_SEQ, HEADS // LAT_HEADS, nq),
        in_specs=[pl.BlockSpec((TQ, width), lambda b, hg, t: (q0 + b * nq + t, hg)), ctx, ctx, lat, lat],
        out_specs=pl.BlockSpec((TQ, LAT_HEADS * V_HEAD), lambda b, hg, t: (b * nq + t, hg)),
        out_shape=jax.ShapeDtypeStruct((T_LAT, HEADS * V_HEAD), BF16),
        compiler_params=_params("arbitrary", "arbitrary", "arbitrary"),
        name="attn_lat",
    )(q, kc, kvc, k, kv)


def _even_out_kernel(oc_ref, ol_ref, sc_ref, xp_ref, xs_ref, mod_ref, wo_ref, nf_ref, x1_ref, h_ref):
    i = pl.program_id(0)
    ctx = i < CTX_TILES
    m = _mod_row(mod_ref, i * TM)
    for r0 in range(0, TM, TM // 4):
        rows = slice(r0, r0 + TM // 4)
        attn = jnp.where(ctx, oc_ref[rows, :], ol_ref[rows, :])
        x = jnp.where(ctx, xp_ref[rows, :], xs_ref[rows, :])
        out = _dot(jnp.concatenate([attn, sc_ref[rows, :]], axis=1), wo_ref[...])
        x1 = x + m[:, 2 * D:3 * D] * out
        x1_ref[rows, :] = x1
        h_ref[rows, :] = (_rms(x1, nf_ref[0:1, :]) * (1.0 + m[:, 4 * D:5 * D]) + m[:, 3 * D:4 * D]).astype(BF16)


def _even_out(oc, ol, sc, xp, xs, mods, wo, norm_ffn):
    full = lambda shape: pl.BlockSpec(shape, lambda i: (0,) * len(shape))
    first = lambda n: pl.BlockSpec((TM, n), lambda i: (jnp.minimum(i, CTX_TILES - 1), 0))
    second = lambda n: pl.BlockSpec((TM, n), lambda i: (jnp.maximum(i - CTX_TILES, 0), 0))
    row = lambda n: pl.BlockSpec((TM, n), lambda i: (i, 0))
    return pl.pallas_call(
        _even_out_kernel,
        grid=(N_TILES,),
        in_specs=[first(HEADS * V_HEAD), second(HEADS * V_HEAD), row(SC_W), first(D), second(D),
                  _mod_spec(0, 1),
                  full((HEADS * V_HEAD + SC_W, D)), full((2, D))],
        out_specs=[row(D), row(D)],
        out_shape=[jax.ShapeDtypeStruct((T, D), F32), jax.ShapeDtypeStruct((T, D), BF16)],
        compiler_params=_params("arbitrary"),
        name="even_out",
    )(oc, ol, sc, xp, xs, mods, wo, norm_ffn)


FFN_NC = D_FF // FFN_FC
FFN_TM = 512


def _ffn_kernel(h_ref, wg_ref, wu_ref, wd_ref, x1_ref, mod0_ref, mod1_ref, nm_ref, x2_ref, h3_ref,
                wg_all, wu_all, wd_all):
    t = pl.program_id(0)

    @pl.when(t < FFN_NC)
    def _stage():
        wg_all[t] = wg_ref[...].astype(BF16)
        wu_all[t] = wu_ref[...].astype(BF16)
        wd_all[pl.ds(pl.multiple_of(t * FFN_FC, FFN_FC), FFN_FC), :] = wd_ref[...].astype(BF16)

    @pl.when(t >= FFN_NC - 1)
    def _tile():
        h = h_ref[...]
        act = []
        for c in range(FFN_NC):
            act.append((_silu(_dot(h, wg_all[c])) * _dot(h, wu_all[c])).astype(BF16))
        f = _dot(jnp.concatenate(act, axis=1), wd_all[...])
        row0 = (t - (FFN_NC - 1)) * FFN_TM
        m0, m1 = _mod_row(mod0_ref, row0), _mod_row(mod1_ref, row0)
        x2 = x1_ref[...] + m0[:, 5 * D:6 * D] * f
        x2_ref[...] = x2
        h3_ref[...] = (_rms(x2, nm_ref[1:2, :]) * (1.0 + m1[:, D:2 * D]) + m1[:, 0:D]).astype(BF16)


def _ffn(h, w_gu, w_down, x1, mods, norm_mix):
    chunk = lambda t: jnp.minimum(t, FFN_NC - 1)
    row = lambda n: pl.BlockSpec((FFN_TM, n), lambda t: (jnp.maximum(t - (FFN_NC - 1), 0), 0))
    return pl.pallas_call(
        _ffn_kernel,
        grid=(FFN_NC - 1 + T // FFN_TM,),
        in_specs=[row(D),
                  pl.BlockSpec((D, FFN_FC), lambda t: (0, chunk(t))),
                  pl.BlockSpec((D, FFN_FC), lambda t: (0, FFN_NC + chunk(t))),
                  pl.BlockSpec((FFN_FC, D), lambda t: (chunk(t), 0)),
                  row(D), _mod_spec(0, 1), _mod_spec(1, 1), pl.BlockSpec((2, D), lambda t: (0, 0))],
        out_specs=[row(D), row(D)],
        out_shape=[jax.ShapeDtypeStruct((T, D), F32), jax.ShapeDtypeStruct((T, D), BF16)],
        scratch_shapes=[pltpu.VMEM((FFN_NC, D, FFN_FC), BF16), pltpu.VMEM((FFN_NC, D, FFN_FC), BF16),
                        pltpu.VMEM((D_FF, D), BF16)],
        compiler_params=_params("arbitrary"),
        name="ffn_dense",
    )(h, w_gu, w_gu, w_down, x1, mods, mods, norm_mix)


CONF_CB = 256
CONF_SEG = 256
CONF_HALO = 16
CONF_SEGP = CONF_SEG + 2 * CONF_HALO
CONF_PIECE = 64


def _conf_kernel(h_ref, x2_ref, w1_ref, b1_ref, wdw_ref, bdw_ref, lng_ref, lnb_ref, w2_ref, b2_ref,
                 mod_ref, nf_ref, rt_ref, x3_ref, h4_ref, lg_ref, pad_ref, conv_ref):
    i = pl.program_id(0)
    nseg = TM // CONF_SEG
    h = h_ref[...]
    joined = jnp.where(i < CTX_TILES, 0.0, 1.0)
    zeros_halo = jnp.zeros((CONF_HALO, CONF_CB), F32)
    for cb in range(D // CONF_CB):
        c0 = cb * CONF_CB
        a = _dot(h, w1_ref[:, c0:c0 + CONF_CB]) + b1_ref[:, c0:c0 + CONF_CB]
        g = _dot(h, w1_ref[:, D + c0:D + c0 + CONF_CB]) + b1_ref[:, D + c0:D + c0 + CONF_CB]
        u = a * jax.nn.sigmoid(g)
        for s in range(nseg):
            base = s * CONF_SEGP
            top = u[s * CONF_SEG - CONF_HALO:s * CONF_SEG] * joined if s > 0 else zeros_halo
            bot = (u[(s + 1) * CONF_SEG:(s + 1) * CONF_SEG + CONF_HALO] * joined
                   if s < nseg - 1 else zeros_halo)
            pad_ref[0, base:base + CONF_HALO, :] = top
            pad_ref[0, base + CONF_HALO:base + CONF_HALO + CONF_SEG, :] = u[s * CONF_SEG:(s + 1) * CONF_SEG]
            pad_ref[0, base + CONF_HALO + CONF_SEG:base + CONF_SEGP, :] = bot

        p0 = pad_ref[0]
        rows = nseg * CONF_SEGP
        for b in range(1, 8):
            pad_ref[b] = pltpu.roll(p0, rows - b, 0)

        def piece(t, carry):
            s = t // (CONF_SEG // CONF_PIECE)
            q0 = (t % (CONF_SEG // CONF_PIECE)) * CONF_PIECE
            src = pl.multiple_of(s * CONF_SEGP + q0, 8)
            acc = jnp.zeros((CONF_PIECE, CONF_CB), F32)
            for j in range(CONF_K):
                hi, lo = (j + 1) // 8, (j + 1) % 8
                acc = acc + wdw_ref[0, j:j + 1, c0:c0 + CONF_CB] * pad_ref[lo, pl.ds(src + 8 * hi, CONF_PIECE), :]
            dst = pl.multiple_of(s * CONF_SEG + q0, 8)
            conv_ref[pl.ds(dst, CONF_PIECE), c0:c0 + CONF_CB] = acc + bdw_ref[:, c0:c0 + CONF_CB]
            return carry

        lax.fori_loop(0, TM // CONF_PIECE, piece, 0)

    m = _mod_row(mod_ref, i * TM)
    half = TM // 2
    for r0 in (0, half):
        rows = slice(r0, r0 + half)
        y = conv_ref[rows, :]
        mu = jnp.mean(y, axis=-1, keepdims=True)
        yc = y - mu
        var = jnp.mean(yc * yc, axis=-1, keepdims=True)
        y = _silu(yc * lax.rsqrt(var + EPS) * lng_ref[...] + lnb_ref[...])
        out = _dot(y.astype(BF16), w2_ref[...]) + b2_ref[...]
        x3 = x2_ref[rows, :] + m[:, 2 * D:3 * D] * out
        x3_ref[rows, :] = x3
        h4 = _rms(x3, nf_ref[1:2, :]) * (1.0 + m[:, 4 * D:5 * D]) + m[:, 3 * D:4 * D]
        h4_ref[rows, :] = h4.astype(BF16)
        lg_ref[:, rows] = lax.dot_general(rt_ref[...], h4, (((1,), (1,)), ((), ())),
                                          precision=lax.Precision.HIGHEST, preferred_element_type=F32)


def _conf(h3, x2, w1, b1, wdw, bdw, lng, lnb, w2, b2, mods, norm_ffn1, router_t):
    full = lambda shape: pl.BlockSpec(shape, lambda i: (0,) * len(shape))
    row = lambda n: pl.BlockSpec((TM, n), lambda i: (i, 0))
    return pl.pallas_call(
        _conf_kernel,
        grid=(N_TILES,),
        in_specs=[row(D), row(D), full((D, 2 * D)), full((1, 2 * D)), full((1, CONF_K, D)), full((1, D)),
                  full((1, D)), full((1, D)), full((D, D)), full((1, D)),
                  _mod_spec(1, 1), full((2, D)), full((N_EXP, D))],
        out_specs=[row(D), row(D), pl.BlockSpec((N_EXP, TM), lambda i: (0, i))],
        out_shape=[jax.ShapeDtypeStruct((T, D), F32), jax.ShapeDtypeStruct((T, D), BF16),
                   jax.ShapeDtypeStruct((N_EXP, T), F32)],
        scratch_shapes=[pltpu.VMEM((8, (TM // CONF_SEG) * CONF_SEGP, CONF_CB), F32),
                        pltpu.VMEM((TM, D), F32)],
        compiler_params=_params("arbitrary"),
        name="conformer_conv",
    )(h3, x2, w1, b1, wdw, bdw, lng, lnb, w2, b2, mods, norm_ffn1, router_t)


TB = 256
N_TB = T // TB
SUBS = 8
SM = SUBS * TB
N_SUB_MAX = 2 * T // TB + N_EXP
N_SUP_MAX = N_SUB_MAX // SUBS + N_EXP - 1
YS_ROWS = (N_SUB_MAX + 2) * TB
WIN_ALIGN = 16
WIN_HALF = TB // 2 + WIN_ALIGN
FIRST_STRIDE = 32
GATHER_BLOCKS = 6


def _route_kernel(lg_ref, g_ref, rank_ref, first_ref):
    lg = lg_ref[...]
    idx = lax.broadcasted_iota(jnp.int32, lg.shape, 0).astype(F32)
    none = float(N_EXP)
    m1 = jnp.max(lg, axis=0, keepdims=True)
    i1 = jnp.min(jnp.where(lg == m1, idx, none), axis=0, keepdims=True)
    rest = jnp.where(idx == i1, -jnp.inf, lg)
    m2 = jnp.max(rest, axis=0, keepdims=True)
    i2 = jnp.min(jnp.where(rest == m2, idx, none), axis=0, keepdims=True)
    e = jnp.exp(m2 - m1)
    w1 = 1.0 / (1.0 + e)
    w2 = e / (1.0 + e)
    g_ref[...] = jnp.where(idx == i1, w1, 0.0) + jnp.where(idx == i2, w2, 0.0)

    mask = jnp.where(idx == i1, 1.0, 0.0) + jnp.where(idx == i2, 1.0, 0.0)
    before = (lax.broadcasted_iota(jnp.int32, (TB, TB), 0) < lax.broadcasted_iota(jnp.int32, (TB, TB), 1))
    before = jnp.where(before, 1.0, 0.0).astype(BF16)
    lane = lax.broadcasted_iota(jnp.int32, (N_EXP, 128), 1)
    carry = jnp.zeros((N_EXP, 1), F32)
    first = jnp.zeros((N_EXP, 128), F32)
    for b in range(N_TB):
        mb = mask[:, b * TB:(b + 1) * TB]
        local = _dot(mb.astype(BF16), before)
        rank_ref[:, b * TB:(b + 1) * TB] = jnp.where(mb > 0.0, local + carry, -1.0)
        first = jnp.where(lane == b, carry, first)
        carry = carry + jnp.sum(mb, axis=1, keepdims=True)
    first_ref[...] = jnp.where(lane == N_TB, carry, first)


def _route(logits_t):
    return pl.pallas_call(
        _route_kernel,
        out_shape=[jax.ShapeDtypeStruct((N_EXP, T), F32), jax.ShapeDtypeStruct((N_EXP, T), F32),
                   jax.ShapeDtypeStruct((N_EXP, 128), F32)],
        compiler_params=pltpu.CompilerParams(vmem_limit_bytes=VMEM_LIMIT),
        name="route",
    )(logits_t)


def _moe_plan(first):
    first = first[:, :FIRST_STRIDE].astype(jnp.int32)
    cnt = first[:, N_TB]
    nt = (cnt + (TB - 1)) >> 8
    off_end = jnp.cumsum(nt)
    off = off_end - nt
    nsub = off_end[-1]
    nsup = (nt + (SUBS - 1)) >> 3
    sup_end = jnp.cumsum(nsup)
    sup_off = sup_end - nsup
    s = jnp.minimum(jnp.arange(N_SUP_MAX), sup_end[-1] - 1)
    valid = jnp.arange(N_SUP_MAX) < sup_end[-1]
    se = jnp.sum(s[:, None] >= sup_end[None, :], axis=1)
    sk0 = (s - sup_off[se]) * SUBS
    sns = jnp.where(valid, jnp.clip(nt[se] - sk0, 0, SUBS), 0)
    sj0 = off[se] + sk0
    base = (jnp.arange(FIRST_STRIDE) * TB)[None, :, None]
    blo = jnp.minimum(jnp.sum(first[:, None, 1:N_TB + 1] <= base, axis=2), N_TB - 1)
    end = jnp.minimum(base + TB, cnt[:, None, None])
    bhi = jnp.maximum(jnp.sum(first[:, None, :N_TB] < end, axis=2) - 1, blo)
    ng = (bhi - blo) // GATHER_BLOCKS + 1
    start = TB * off[:, None] + first[:, :N_TB]
    lead = start & (WIN_ALIGN - 1)
    wina = start - lead
    rel = first[:, :N_TB] - lead
    need = lead + (first[:, 1:N_TB + 1] - first[:, :N_TB]) > WIN_HALF
    winb = lax.cummax(jnp.where(need, wina + WIN_HALF, 0), axis=1)
    wide = jnp.any(need, axis=0)
    i32 = lambda a: a.astype(jnp.int32)
    return dict(se=i32(se), sk0=i32(sk0), sns=i32(sns), sj0=i32(sj0), nsub=i32(nsub).reshape(1),
                first=i32(first.reshape(-1)), blo=i32(blo.reshape(-1)), ng=i32(ng.reshape(-1)),
                wina=i32(wina.T.reshape(-1)), winb=i32(winb.T.reshape(-1)), rel=i32(rel.T.reshape(-1)),
                wide=i32(wide))


def _moe_gmm_kernel(se_ref, sk0_ref, sns_ref, sj0_ref, nsub_ref, first_ref, blo_ref, ng_ref,
                    x_ref, rank_ref, gate_ref, wg_ref, wu_ref, wd_ref, ys_ref,
                    xs_ref, gs_ref, yacc_ref, ybuf_ref, acc_ref, gacc_ref, wgb_ref, wub_ref, wdb_ref, sem):
    s, c = pl.program_id(0), pl.program_id(1)
    nc = pl.num_programs(1)
    e, k0, ns = se_ref[s], sk0_ref[s], sns_ref[s]

    def sub_rows(k):
        return pl.ds(pl.multiple_of(k * TB, TB), TB)

    def out_copy(k, row0):
        dst = ys_ref.at[pl.ds(pl.multiple_of(row0 + k * TB, TB), TB)]
        return pltpu.make_async_copy(ybuf_ref.at[sub_rows(k)], dst, sem.at[k])

    @pl.when((ns > 0) & (c == 0))
    def _gather():
        def sub(k, carry):
            base = (k0 + k) * TB
            slot = (lax.broadcasted_iota(jnp.int32, (TB, 1), 0) + base).astype(F32)
            acc_ref[...] = jnp.zeros_like(acc_ref)
            gacc_ref[...] = jnp.zeros_like(gacc_ref)
            blo = blo_ref[e * FIRST_STRIDE + k0 + k]

            def group(g, carry2):
                b0 = blo + g * GATHER_BLOCKS
                t0 = pl.multiple_of(jnp.minimum(b0, N_TB - GATHER_BLOCKS) * TB, TB)
                lo = first_ref[e * FIRST_STRIDE + b0].astype(F32)
                cols = pl.ds(t0, GATHER_BLOCKS * TB)
                hit = rank_ref[pl.ds(e, 1), cols] == jnp.where(slot >= lo, slot, -2.0)
                onehot = jnp.where(hit, 1.0, 0.0).astype(BF16)
                acc_ref[...] += _dot(onehot, x_ref[cols, :])
                gacc_ref[...] += jnp.sum(jnp.where(hit, gate_ref[pl.ds(e, 1), cols], 0.0),
                                         axis=-1, keepdims=True)
                return carry2

            lax.fori_loop(0, ng_ref[e * FIRST_STRIDE + k0 + k], group, 0)
            xs_ref[sub_rows(k), :] = acc_ref[...].astype(BF16)
            gs_ref[sub_rows(k), :] = gacc_ref[...]
            yacc_ref[sub_rows(k), :] = jnp.zeros((TB, D), F32)
            return carry

        lax.fori_loop(0, ns, sub, 0)

    @pl.when(ns > 0)
    def _compute():
        def swiglu(rows, wg, wu, wd):
            x = xs_ref[rows, :]
            g = _dot(x, wg)
            u = _dot(x, wu)
            yacc_ref[rows, :] += _dot((_silu(g) * u).astype(BF16), wd)

        def first_chain(rows):
            wg, wu, wd = wg_ref[0].astype(BF16), wu_ref[0].astype(BF16), wd_ref[0].astype(BF16)
            wgb_ref[...] = wg
            wub_ref[...] = wu
            wdb_ref[...] = wd
            swiglu(rows, wg, wu, wd)

        @pl.when(ns >= 2)
        def _():
            first_chain(pl.ds(0, 2 * TB))

        @pl.when(ns == 1)
        def _():
            first_chain(pl.ds(0, TB))

        rest = jnp.maximum(ns - 2, 0)

        def chain(rows):
            swiglu(rows, wgb_ref[...], wub_ref[...], wdb_ref[...])

        def quad(k, carry):
            chain(pl.ds(pl.multiple_of(2 * TB + k * (4 * TB), 2 * TB), 4 * TB))
            return carry

        lax.fori_loop(0, rest >> 2, quad, 0)

        @pl.when((rest & 2) != 0)
        def _():
            chain(pl.ds(pl.multiple_of(2 * TB + (rest >> 2) * (4 * TB), 2 * TB), 2 * TB))

        @pl.when((rest & 1) != 0)
        def _():
            chain(sub_rows(ns - 1))

    @pl.when((ns > 0) & (c == nc - 1))
    def _store():
        row0 = sj0_ref[s] * TB

        def put(k, carry):
            ybuf_ref[sub_rows(k), :] = (yacc_ref[sub_rows(k), :] * gs_ref[sub_rows(k), :]).astype(BF16)
            out_copy(k, row0).start()
            return carry

        def done(k, carry):
            out_copy(k, row0).wait()
            return carry

        lax.fori_loop(0, ns, put, 0)
        lax.fori_loop(0, ns, done, 0)

    @pl.when((s == pl.num_programs(0) - 1) & (c == nc - 1))
    def _zero_tail():
        ybuf_ref[0:TB, :] = jnp.zeros((TB, D), BF16)
        nsub = nsub_ref[0]

        def fill(k, carry):
            cp = out_copy(0, (nsub + k) * TB)
            cp.start()
            cp.wait()
            return carry

        lax.fori_loop(0, YS_ROWS // TB - nsub, fill, 0)


def _moe_gmm(plan, h4, rank, gates, w_gu, w_down):
    nc = D_FFE // MOE_FC

    def chunk(s, c, sns):
        return jnp.where(sns[s] > 0, c, nc - 1)

    return pl.pallas_call(
        _moe_gmm_kernel,
        grid_spec=pltpu.PrefetchScalarGridSpec(
            num_scalar_prefetch=8,
            grid=(N_SUP_MAX, nc),
            in_specs=[
                pl.BlockSpec((T, D), lambda s, c, *_: (0, 0), pipeline_mode=pl.Buffered(1)),
                pl.BlockSpec((N_EXP, T), lambda s, c, *_: (0, 0)),
                pl.BlockSpec((N_EXP, T), lambda s, c, *_: (0, 0)),
                pl.BlockSpec((1, D, MOE_FC), lambda s, c, se, sk0, sns, *_: (se[s], 0, chunk(s, c, sns))),
                pl.BlockSpec((1, D, MOE_FC), lambda s, c, se, sk0, sns, *_: (se[s], 0, nc + chunk(s, c, sns))),
                pl.BlockSpec((1, MOE_FC, D), lambda s, c, se, sk0, sns, *_: (se[s], chunk(s, c, sns), 0)),
            ],
            out_specs=pl.BlockSpec(memory_space=pl.ANY),
            scratch_shapes=[
                pltpu.VMEM((SM, D), BF16), pltpu.VMEM((SM, 1), F32), pltpu.VMEM((SM, D), F32),
                pltpu.VMEM((SM, D), BF16), pltpu.VMEM((TB, D), F32), pltpu.VMEM((TB, 1), F32),
                pltpu.VMEM((D, MOE_FC), BF16), pltpu.VMEM((D, MOE_FC), BF16), pltpu.VMEM((MOE_FC, D), BF16),
                pltpu.SemaphoreType.DMA((SUBS,)),
            ],
        ),
        out_shape=jax.ShapeDtypeStruct((YS_ROWS, D), BF16),
        compiler_params=_params("arbitrary", "arbitrary"),
        name="moe_gmm",
    )(plan["se"], plan["sk0"], plan["sns"], plan["sj0"], plan["nsub"], plan["first"], plan["blo"], plan["ng"],
      h4, rank, gates, w_gu, w_gu, w_down)


def _moe_combine_kernel(wina_ref, winb_ref, rel_ref, wide_ref, *refs):
    ya, yb = refs[:N_EXP], refs[N_EXP:2 * N_EXP]
    rank_ref, x3_ref, mod_ref, yp_ref, ys_ref, ycat_ref, acc_ref = refs[2 * N_EXP:]
    b = pl.program_id(0)
    row = lax.broadcasted_iota(jnp.int32, (WIN_HALF, 1), 0)

    def onehot(e, first_row):
        slot = (row + (rel_ref[b * N_EXP + e] + first_row)).astype(F32)
        return jnp.where(rank_ref[e:e + 1, :] == slot, 1.0, 0.0).astype(BF16)

    def gather(y_refs, first_row, base):
        pieces = []
        for e in range(N_EXP):
            ycat_ref[base + e * WIN_HALF:base + (e + 1) * WIN_HALF, :] = y_refs[e][...]
            pieces.append(onehot(e, first_row))
        return pieces

    def combine(pieces, rows):
        return lax.dot_general(jnp.concatenate(pieces, axis=0), ycat_ref[0:rows, :], (((0,), (0,)), ((), ())),
                               preferred_element_type=F32)

    @pl.when(wide_ref[b] == 0)
    def _():
        acc_ref[...] = combine(gather(ya, 0, 0), N_EXP * WIN_HALF)

    @pl.when(wide_ref[b] != 0)
    def _():
        pieces = gather(ya, 0, 0) + gather(yb, WIN_HALF, N_EXP * WIN_HALF)
        acc_ref[...] = combine(pieces, 2 * N_EXP * WIN_HALF)

    out = x3_ref[...] + _mod_row(mod_ref, b * TB)[:, 5 * D:6 * D] * acc_ref[...]

    @pl.when(b < T_CTX // TB)
    def _():
        yp_ref[...] = out

    @pl.when(b >= T_CTX // TB)
    def _():
        ys_ref[...] = out


def _moe_combine(plan, ysorted, rank, x3, mods):
    ctx_blocks = T_CTX // TB

    def window(e, second):
        def index(b, wina, winb, rel, wide):
            start = (winb if second else wina)[b * N_EXP + e]
            return pl.multiple_of(start, WIN_ALIGN), 0
        return pl.BlockSpec((pl.Element(WIN_HALF), pl.Element(D)), index)

    return pl.pallas_call(
        _moe_combine_kernel,
        grid_spec=pltpu.PrefetchScalarGridSpec(
            num_scalar_prefetch=4,
            grid=(N_TB,),
            in_specs=[window(e, False) for e in range(N_EXP)] + [window(e, True) for e in range(N_EXP)] + [
                pl.BlockSpec((N_EXP, TB), lambda b, *_: (0, b)),
                pl.BlockSpec((TB, D), lambda b, *_: (b, 0)),
                _mod_spec(1, 1),
            ],
            out_specs=[pl.BlockSpec((TB, D), lambda b, *_: (jnp.minimum(b, ctx_blocks - 1), 0)),
                       pl.BlockSpec((TB, D), lambda b, *_: (jnp.maximum(b - ctx_blocks, 0), 0))],
            scratch_shapes=[pltpu.VMEM((2 * N_EXP * WIN_HALF, D), BF16), pltpu.VMEM((TB, D), F32)],
        ),
        out_shape=[jax.ShapeDtypeStruct((T_CTX, D), F32), jax.ShapeDtypeStruct((T_LAT, D), F32)],
        compiler_params=_params("arbitrary"),
        name="moe_combine",
    )(plan["wina"], plan["winb"], plan["rel"], plan["wide"], *([ysorted] * (2 * N_EXP)), rank, x3, mods)


def _pad_heads(w, width):
    lead = w.shape[:-1]
    w = w.reshape(*lead, HEADS, width)
    w = jnp.pad(w, [(0, 0)] * len(lead) + [(0, 0), (0, HEAD_PAD - width)])
    return w.reshape(*lead, HEADS * HEAD_PAD)


def kernel(x_prompt, x_sample, cache_ckv, cache_kpe, c, c_ctx, ada_w, ada_b, norm_mix, norm_ffn, w_in, q_a_norm,
           w_qb, kv_a_norm, w_kvb, q_norm, k_norm, w_sc, w_o, ffn_gu, ffn_down, conv_pw1, conv_pw1_b, conv_dw,
           conv_dw_b, conv_ln_g, conv_ln_b, conv_pw2, conv_pw2_b, router, moe_gu, moe_down):
    xp = x_prompt.reshape(T_CTX, D)
    xs = x_sample.reshape(T_LAT, D)

    mods = _adaln(c_ctx, c, ada_w, ada_b)

    wqb = _pad_heads(w_qb[0], QK_HEAD).astype(BF16)
    wkvb = w_kvb[0].astype(BF16)
    qn = jnp.pad(q_norm[0], (0, HEAD_PAD - QK_HEAD)).reshape(1, HEAD_PAD)
    kn = jnp.pad(k_norm[0], (0, HEAD_PAD - QK_HEAD)).reshape(1, HEAD_PAD)
    tabs = _rope_tables()

    q, ckv, kpe, sc, state_ckv, state_kpe = _even_proj(xp, xs, mods, norm_mix, w_in, q_a_norm, wqb, kv_a_norm,
                                                       qn, w_sc, tabs)

    lat_tile0 = T_CTX // TKV
    ident = LAT_LEN // TKV
    k, kv = _kv_proj(ckv, kpe, wkvb, kn, tabs,
                     lambda i: jnp.where(i < lat_tile0, ident, (i - lat_tile0) % ident), "kv_proj")
    cache_kpe_p = jnp.pad(cache_kpe[:, 0].reshape(N_LAT_SEQ * PAST, QK_ROPE), ((0, 0), (0, HEAD_PAD - QK_ROPE)))
    kc, kvc = _kv_proj(cache_ckv[:, 0].reshape(N_LAT_SEQ * PAST, KV_LORA), cache_kpe_p, wkvb, kn, tabs,
                       lambda i: ident, "kv_proj_cache")

    oc = _attn_ctx(q, k, kv)
    ol = _attn_lat(q, kc, kvc, k, kv)
    x1, h2 = _even_out(oc, ol, sc, xp, xs, mods, w_o[0].astype(BF16), norm_ffn)
    x2, h3 = _ffn(h2, ffn_gu[0], ffn_down[0], x1, mods, norm_mix)

    x3, h4, logits_t = _conf(h3, x2, conv_pw1[0].astype(BF16), conv_pw1_b, conv_dw, conv_dw_b, conv_ln_g,
                             conv_ln_b, conv_pw2[0].astype(BF16), conv_pw2_b, mods, norm_ffn, router[0].T)
    gates, rank, first = _route(logits_t)
    plan = _moe_plan(first)
    ysorted = _moe_gmm(plan, h4, rank, gates, moe_gu[0], moe_down[0])
    yp, ys = _moe_combine(plan, ysorted, rank, x3, mods)

    return (yp.reshape(N_CTX_SEQ, CTX_LEN, D), ys.reshape(N_LAT_SEQ, LAT_LEN, D),
            state_ckv.reshape(N_CTX_SEQ, 1, CTX_LEN, KV_LORA), state_kpe.reshape(N_CTX_SEQ, 1, CTX_LEN, QK_ROPE))
```

```python
import functools

import jax
import jax.numpy as jnp
import numpy as np
from jax import lax
from jax.experimental import pallas as pl
from jax.experimental.pallas import tpu as pltpu

F32 = jnp.float32
BF16 = jnp.bfloat16

D = 1024
N_CTX_SEQ, CTX_LEN = 16, 256
N_LAT_SEQ, LAT_LEN = 2, 1024
T_CTX = N_CTX_SEQ * CTX_LEN
T_LAT = N_LAT_SEQ * LAT_LEN
T = T_CTX + T_LAT
PAST = 256
GRID_W = 64
HEADS = 8
QK_NOPE, QK_ROPE, V_HEAD = 64, 32, 64
QK_HEAD = QK_NOPE + QK_ROPE
HEAD_PAD = 128
Q_LORA, KV_LORA = 256, 128
SC_W = 512
IN0_W = Q_LORA + KV_LORA + QK_ROPE + 3 * SC_W
CONF_K = 31
D_FF = 2816
N_EXP = 8
D_FFE = 3584
EPS = 1e-6
ROPE_THETA = 10000.0

TM = 1024
N_TILES = T // TM
CTX_TILES = T_CTX // TM
TKV = 512
TQ = 256
FFN_FC = 256
MOE_FC = 512
VMEM_LIMIT = 56 * 1024 * 1024


def _dot(a, b):
    return jnp.dot(a, b, preferred_element_type=F32)


def _dot_nt(a, b):
    return lax.dot_general(a, b, (((1,), (1,)), ((), ())), preferred_element_type=F32)


def _rms(x, g):
    return x * lax.rsqrt(jnp.mean(x * x, axis=-1, keepdims=True) + EPS) * g


def _silu(x):
    return x * jax.nn.sigmoid(x)


def _params(*sem):
    return pltpu.CompilerParams(dimension_semantics=sem, vmem_limit_bytes=VMEM_LIMIT)


def _mod_row(mod_ref, row0):
    cond = jnp.maximum(row0 - (T_CTX - LAT_LEN), 0) >> 10
    return mod_ref[0, pl.ds(cond, 1), :]


def _mod_spec(layer, ngrid):
    return pl.BlockSpec((1, 8, 6 * D), lambda *_: (layer, 0, 0))


def _adaln_kernel(cc_ref, c_ref, w_ref, b_ref, o_ref):
    l = pl.program_id(0)
    row = lax.broadcasted_iota(jnp.int32, (8, 1), 0)
    cond = jnp.where(row == 0, cc_ref[...], 0.0)
    for b in range(N_LAT_SEQ):
        cond = jnp.where(row == 1 + b, c_ref[b:b + 1, :], cond)
    o_ref[0] = _dot(_silu(cond).astype(BF16), w_ref[0].astype(BF16)) + b_ref[pl.ds(l, 1), :]


def _adaln(c_ctx, c, ada_w, ada_b):
    depth = ada_w.shape[0]
    tn = 2048
    return pl.pallas_call(
        _adaln_kernel,
        grid=(depth, 6 * D // tn),
        in_specs=[
            pl.BlockSpec((1, D), lambda l, j: (0, 0)),
            pl.BlockSpec((N_LAT_SEQ, D), lambda l, j: (0, 0)),
            pl.BlockSpec((1, D, tn), lambda l, j: (l, 0, j)),
            pl.BlockSpec((depth, tn), lambda l, j: (0, j)),
        ],
        out_specs=pl.BlockSpec((1, 8, tn), lambda l, j: (l, 0, j)),
        out_shape=jax.ShapeDtypeStruct((depth, 8, 6 * D), F32),
        compiler_params=_params("arbitrary", "arbitrary"),
        name="adaln",
    )(c_ctx.reshape(1, D), c, ada_w, ada_b)


def _rope_tables():
    half = QK_ROPE // 2
    nf = half // 2
    pos = np.arange(LAT_LEN)
    inv = ROPE_THETA ** (-np.arange(nf, dtype=np.float64) / nf)
    k = np.arange(QK_ROPE)
    part, idx = k // half, k % half
    p = np.where(part[None, :] == 0, (pos // GRID_W)[:, None], (pos % GRID_W)[:, None])
    ang = p * inv[idx % nf][None, :]
    cos, sin = np.cos(ang), np.sin(ang)
    first = (idx < nf)[None, :]
    s1 = np.where(first, -sin, 0.0)
    s2 = np.where(first, 0.0, sin)

    def place(t, fill):
        tab = np.full((2 * LAT_LEN, HEAD_PAD), fill, np.float32)
        tab[:LAT_LEN, QK_NOPE:QK_HEAD] = t
        return jnp.asarray(tab)

    return place(cos, 1.0), place(s1, 0.0), place(s2, 0.0)


def _rope(blk, cos, s1, s2):
    return blk * cos + pltpu.roll(blk, 8, 1) * s2 + pltpu.roll(blk, HEAD_PAD - 8, 1) * s1


def _head_norm(blk, g):
    ms = jnp.sum(blk * blk, axis=-1, keepdims=True) * (1.0 / QK_HEAD)
    return blk * lax.rsqrt(ms + EPS) * g


def _even_proj_kernel(xp_ref, xs_ref, mod_ref, nm_ref, win_ref, qan_ref, wqb_ref, kvan_ref,
                      qn_ref, wsc_ref, cos_ref, s1_ref, s2_ref,
                      q_ref, ckv_ref, kpe_ref, sc_ref, sckv_ref, skpe_ref, wt_ref):
    i = pl.program_id(0)
    n_a = Q_LORA + KV_LORA + QK_ROPE

    @pl.when(i == 0)
    def _():
        wt_ref[...] = win_ref[...].astype(BF16)

    x = jnp.where(i < CTX_TILES, xp_ref[...], xs_ref[...])
    m = _mod_row(mod_ref, i * TM)
    h = _rms(x, nm_ref[0:1, :]) * (1.0 + m[:, D:2 * D]) + m[:, 0:D]
    hb = h.astype(BF16)

    za = _dot_nt(hb, wt_ref[0:512, :])
    ckv = _rms(za[:, Q_LORA:Q_LORA + KV_LORA], kvan_ref[...])
    lane = lax.broadcasted_iota(jnp.int32, (1, HEAD_PAD), 1)
    kpe = jnp.where(lane < QK_ROPE, za[:, Q_LORA + KV_LORA:], 0.0)
    ckv_ref[...] = ckv
    kpe_ref[...] = kpe

    @pl.when(i < CTX_TILES)
    def _():
        sckv_ref[...] = ckv
        skpe_ref[...] = kpe[:, :QK_ROPE]

    qa = _rms(za[:, :Q_LORA], qan_ref[...]).astype(BF16)
    cos, s1, s2 = cos_ref[...], s1_ref[...], s2_ref[...]
    qn = qn_ref[...]
    scale = QK_HEAD ** -0.5
    for hp in range(HEADS // 2):
        qq = _dot(qa, wqb_ref[:, hp * 256:(hp + 1) * 256])
        for j in range(2):
            blk = _head_norm(qq[:, j * HEAD_PAD:(j + 1) * HEAD_PAD], qn)
            blk = _rope(blk, cos, s1, s2) * scale
            h0 = (2 * hp + j) * HEAD_PAD
            q_ref[:, h0:h0 + HEAD_PAD] = blk.astype(BF16)

    gb = _dot_nt(hb, wt_ref[n_a:n_a + SC_W, :])
    v = _dot_nt(hb, wt_ref[n_a + SC_W:n_a + 2 * SC_W, :]) * _dot_nt(hb, wt_ref[n_a + 2 * SC_W:n_a + 3 * SC_W, :])
    seq = jnp.where(i < CTX_TILES, CTX_LEN, LAT_LEN)
    r = lax.broadcasted_iota(jnp.int32, (TM, 1), 0) & (seq - 1)
    vp = jnp.where(r == 0, 0.0, pltpu.roll(v, 1, 0))
    vn = jnp.where(r == seq - 1, 0.0, pltpu.roll(v, TM - 1, 0))
    w = wsc_ref[0]
    y = w[0:1] * vp + w[1:2] * v + w[2:3] * vn
    sc_ref[...] = (gb * y).astype(BF16)


def _even_proj(xp, xs, mods, norm_mix, w_in, q_a_norm, wqb, kv_a_norm, qn, w_sc, tabs):
    full = lambda shape: pl.BlockSpec(shape, lambda i: (0,) * len(shape))
    tab = pl.BlockSpec((TM, HEAD_PAD), lambda i: (jnp.where(i < CTX_TILES, 1, 0), 0))
    row = lambda n: pl.BlockSpec((TM, n), lambda i: (i, 0))
    ctx_row = lambda n: pl.BlockSpec((TM, n), lambda i: (jnp.minimum(i, CTX_TILES - 1), 0))
    return pl.pallas_call(
        _even_proj_kernel,
        grid=(N_TILES,),
        in_specs=[
            ctx_row(D),
            pl.BlockSpec((TM, D), lambda i: (jnp.maximum(i - CTX_TILES, 0), 0)),
            _mod_spec(0, 1),
            full((2, D)),
            pl.BlockSpec((IN0_W, D), lambda i: (0, 0), pipeline_mode=pl.Buffered(1)),
            full((1, Q_LORA)),
            full((Q_LORA, HEADS * HEAD_PAD)), full((1, KV_LORA)), full((1, HEAD_PAD)),
            full((1, 3, SC_W)), tab, tab, tab,
        ],
        out_specs=[row(HEADS * HEAD_PAD), row(KV_LORA), row(HEAD_PAD), row(SC_W),
                   ctx_row(KV_LORA), ctx_row(QK_ROPE)],
        out_shape=[
            jax.ShapeDtypeStruct((T, HEADS * HEAD_PAD), BF16),
            jax.ShapeDtypeStruct((T, KV_LORA), F32),
            jax.ShapeDtypeStruct((T, HEAD_PAD), F32),
            jax.ShapeDtypeStruct((T, SC_W), BF16),
            jax.ShapeDtypeStruct((T_CTX, KV_LORA), F32),
            jax.ShapeDtypeStruct((T_CTX, QK_ROPE), F32),
        ],
        scratch_shapes=[pltpu.VMEM((IN0_W, D), BF16)],
        compiler_params=_params("arbitrary"),
        name="even_proj",
    )(xp, xs, mods, norm_mix, w_in, q_a_norm, wqb, kv_a_norm, qn, w_sc, *tabs)


def _kv_proj_kernel(ckv_ref, kpe_ref, wkvb_ref, kn_ref, cos_ref, s1_ref, s2_ref, k_ref, kv_ref):
    kv = _dot(ckv_ref[...].astype(BF16), wkvb_ref[...])
    kv_ref[...] = kv.astype(BF16)
    kpe = pltpu.roll(kpe_ref[...], QK_NOPE, 1)
    lane = lax.broadcasted_iota(jnp.int32, (1, HEAD_PAD), 1)
    kn = kn_ref[...]
    pe_sq = jnp.sum(kpe * kpe, axis=-1, keepdims=True)
    pe = _rope(kpe * kn, cos_ref[...], s1_ref[...], s2_ref[...])
    for h in range(HEADS):
        blk = kv[:, h * HEAD_PAD:(h + 1) * HEAD_PAD]
        nope = jnp.where(lane < QK_NOPE, blk, 0.0)
        ms = (jnp.sum(nope * nope, axis=-1, keepdims=True) + pe_sq) * (1.0 / QK_HEAD)
        k = jnp.where(lane < QK_NOPE, blk * kn, pe) * lax.rsqrt(ms + EPS)
        k_ref[:, h * HEAD_PAD:(h + 1) * HEAD_PAD] = k.astype(BF16)


def _kv_proj(ckv, kpe, wkvb, kn, tabs, tab_index, name):
    n = ckv.shape[0]
    full = lambda shape: pl.BlockSpec(shape, lambda i: (0,) * len(shape))
    tab = pl.BlockSpec((TKV, HEAD_PAD), lambda i: (tab_index(i), 0))
    row = lambda w: pl.BlockSpec((TKV, w), lambda i: (i, 0))
    return pl.pallas_call(
        _kv_proj_kernel,
        grid=(n // TKV,),
        in_specs=[row(KV_LORA), row(HEAD_PAD), full((KV_LORA, HEADS * HEAD_PAD)), full((1, HEAD_PAD)),
                  tab, tab, tab],
        out_specs=[row(HEADS * HEAD_PAD), row(HEADS * HEAD_PAD)],
        out_shape=[jax.ShapeDtypeStruct((n, HEADS * HEAD_PAD), BF16)] * 2,
        compiler_params=_params("arbitrary"),
        name=name,
    )(ckv, kpe, wkvb, kn, *tabs)


def _pair_out(o0, o1):
    lane = lax.broadcasted_iota(jnp.int32, (1, HEAD_PAD), 1)
    return jnp.where(lane < V_HEAD, pltpu.roll(o0, V_HEAD, 1), o1).astype(BF16)


CTX_SEQS = 2


def _attn_ctx_kernel(q_ref, k_ref, kv_ref, o_ref):
    for b in range(CTX_SEQS):
        rows = slice(b * CTX_LEN, (b + 1) * CTX_LEN)
        for hp in range(HEADS // 2):
            outs = []
            for j in range(2):
                lanes = slice((2 * hp + j) * HEAD_PAD, (2 * hp + j + 1) * HEAD_PAD)
                s = _dot_nt(q_ref[rows, lanes], k_ref[rows, lanes])
                p = jnp.exp(s - jnp.max(s, axis=-1, keepdims=True))
                l = jnp.sum(p, axis=-1, keepdims=True)
                outs.append(_dot(p.astype(BF16), kv_ref[rows, lanes]) / l)
            o_ref[rows, hp * HEAD_PAD:(hp + 1) * HEAD_PAD] = _pair_out(*outs)


def _attn_ctx(q, k, kv):
    blk = pl.BlockSpec((CTX_SEQS * CTX_LEN, HEADS * HEAD_PAD), lambda b: (b, 0))
    return pl.pallas_call(
        _attn_ctx_kernel,
        grid=(N_CTX_SEQ // CTX_SEQS,),
        in_specs=[blk, blk, blk],
        out_specs=pl.BlockSpec((CTX_SEQS * CTX_LEN, HEADS * V_HEAD), lambda b: (b, 0)),
        out_shape=jax.ShapeDtypeStruct((T_CTX, HEADS * V_HEAD), BF16),
        compiler_params=_params("arbitrary"),
        name="attn_ctx",
    )(q, k, kv)


LAT_HEADS = 4


def _attn_lat_kernel(q_ref, kc_ref, kvc_ref, kl_ref, kvl_ref, o_ref):
    for hp in range(LAT_HEADS // 2):
        outs = []
        for j in range(2):
            h0 = (2 * hp + j) * HEAD_PAD
            lanes = slice(h0, h0 + HEAD_PAD)
            q = q_ref[:, lanes]
            sc = _dot_nt(q, kc_ref[:, lanes])
            sl = _dot_nt(q, kl_ref[:, lanes])
            m = jnp.maximum(jnp.max(sc, axis=-1, keepdims=True), jnp.max(sl, axis=-1, keepdims=True))
            pc, pl_ = jnp.exp(sc - m), jnp.exp(sl - m)
            l = jnp.sum(pc, axis=-1, keepdims=True) + jnp.sum(pl_, axis=-1, keepdims=True)
            o = _dot(pc.astype(BF16), kvc_ref[:, lanes]) + _dot(pl_.astype(BF16), kvl_ref[:, lanes])
            outs.append(o / l)
        o_ref[:, hp * HEAD_PAD:(hp + 1) * HEAD_PAD] = _pair_out(*outs)


def _attn_lat(q, kc, kvc, k, kv):
    nq = LAT_LEN // TQ
    q0 = T_CTX // TQ
    kl0 = T_CTX // LAT_LEN
    width = LAT_HEADS * HEAD_PAD
    lat = pl.BlockSpec((LAT_LEN, width), lambda b, hg, t: (kl0 + b, hg))
    ctx = pl.BlockSpec((PAST, width), lambda b, hg, t: (b, hg))
    return pl.pallas_call(
        _attn_lat_kernel,
        grid=(N_LAT_SEQ, HEADS // LAT_HEADS, nq),
        in_specs=[pl.BlockSpec((TQ, width), lambda b, hg, t: (q0 + b * nq + t, hg)), ctx, ctx, lat, lat],
        out_specs=pl.BlockSpec((TQ, LAT_HEADS * V_HEAD), lambda b, hg, t: (b * nq + t, hg)),
        out_shape=jax.ShapeDtypeStruct((T_LAT, HEADS * V_HEAD), BF16),
        compiler_params=_params("arbitrary", "arbitrary", "arbitrary"),
        name="attn_lat",
    )(q, kc, kvc, k, kv)


FFN_NC = D_FF // FFN_FC
FFN_TM = 512
FFN_CTX_TILES = T_CTX // FFN_TM


def _ffn_kernel(oc_ref, ol_ref, sc_ref, xp_ref, xs_ref, wo_ref, nf_ref, wg_ref, wu_ref, wd_ref,
                mod0_ref, mod1_ref, nm_ref, x2_ref, h3_ref, wg_all, wu_all, wd_all):
    t = pl.program_id(0)

    @pl.when(t < FFN_NC)
    def _stage():
        wg_all[t] = wg_ref[...].astype(BF16)
        wu_all[t] = wu_ref[...].astype(BF16)
        wd_all[pl.ds(pl.multiple_of(t * FFN_FC, FFN_FC), FFN_FC), :] = wd_ref[...].astype(BF16)

    @pl.when(t >= FFN_NC - 1)
    def _tile():
        i = t - (FFN_NC - 1)
        ctx = i < FFN_CTX_TILES
        m0, m1 = _mod_row(mod0_ref, i * FFN_TM), _mod_row(mod1_ref, i * FFN_TM)
        attn = jnp.where(ctx, oc_ref[...], ol_ref[...])
        x = jnp.where(ctx, xp_ref[...], xs_ref[...])
        x1 = x + m0[:, 2 * D:3 * D] * _dot(jnp.concatenate([attn, sc_ref[...]], axis=1), wo_ref[...])
        h = (_rms(x1, nf_ref[0:1, :]) * (1.0 + m0[:, 4 * D:5 * D]) + m0[:, 3 * D:4 * D]).astype(BF16)
        act = []
        for c in range(FFN_NC):
            act.append((_silu(_dot(h, wg_all[c])) * _dot(h, wu_all[c])).astype(BF16))
        f = _dot(jnp.concatenate(act, axis=1), wd_all[...])
        x2 = x1 + m0[:, 5 * D:6 * D] * f
        x2_ref[...] = x2
        h3_ref[...] = (_rms(x2, nm_ref[1:2, :]) * (1.0 + m1[:, D:2 * D]) + m1[:, 0:D]).astype(BF16)


def _ffn(oc, ol, sc, xp, xs, wo, norm_ffn, w_gu, w_down, mods, norm_mix):
    chunk = lambda t: jnp.minimum(t, FFN_NC - 1)
    tile = lambda t: jnp.maximum(t - (FFN_NC - 1), 0)
    full = lambda shape: pl.BlockSpec(shape, lambda t: (0,) * len(shape))
    row = lambda n: pl.BlockSpec((FFN_TM, n), lambda t: (tile(t), 0))
    first = lambda n: pl.BlockSpec((FFN_TM, n), lambda t: (jnp.minimum(tile(t), FFN_CTX_TILES - 1), 0))
    second = lambda n: pl.BlockSpec((FFN_TM, n), lambda t: (jnp.maximum(tile(t) - FFN_CTX_TILES, 0), 0))
    return pl.pallas_call(
        _ffn_kernel,
        grid=(FFN_NC - 1 + T // FFN_TM,),
        in_specs=[first(HEADS * V_HEAD), second(HEADS * V_HEAD), row(SC_W), first(D), second(D),
                  full((HEADS * V_HEAD + SC_W, D)), full((2, D)),
                  pl.BlockSpec((D, FFN_FC), lambda t: (0, chunk(t))),
                  pl.BlockSpec((D, FFN_FC), lambda t: (0, FFN_NC + chunk(t))),
                  pl.BlockSpec((FFN_FC, D), lambda t: (chunk(t), 0)),
                  _mod_spec(0, 1), _mod_spec(1, 1), full((2, D))],
        out_specs=[row(D), row(D)],
        out_shape=[jax.ShapeDtypeStruct((T, D), F32), jax.ShapeDtypeStruct((T, D), BF16)],
        scratch_shapes=[pltpu.VMEM((FFN_NC, D, FFN_FC), BF16), pltpu.VMEM((FFN_NC, D, FFN_FC), BF16),
                        pltpu.VMEM((D_FF, D), BF16)],
        compiler_params=_params("arbitrary"),
        name="ffn_dense",
    )(oc, ol, sc, xp, xs, wo, norm_ffn, w_gu, w_gu, w_down, mods, mods, norm_mix)


CONF_CB = 256
CONF_SEG = 256
CONF_HALO = 16
CONF_SEGP = CONF_SEG + 2 * CONF_HALO
CONF_PIECE = 64


def _conf_kernel(h_ref, x2_ref, w1_ref, b1_ref, wdw_ref, bdw_ref, lng_ref, lnb_ref, w2_ref, b2_ref,
                 mod_ref, nf_ref, rt_ref, x3_ref, h4_ref, lg_ref, pad_ref, conv_ref):
    i = pl.program_id(0)
    nseg = TM // CONF_SEG
    h = h_ref[...]
    joined = jnp.where(i < CTX_TILES, 0.0, 1.0)
    zeros_halo = jnp.zeros((CONF_HALO, CONF_CB), F32)
    for cb in range(D // CONF_CB):
        c0 = cb * CONF_CB
        a = _dot(h, w1_ref[:, c0:c0 + CONF_CB]) + b1_ref[:, c0:c0 + CONF_CB]
        g = _dot(h, w1_ref[:, D + c0:D + c0 + CONF_CB]) + b1_ref[:, D + c0:D + c0 + CONF_CB]
        u = a * jax.nn.sigmoid(g)
        for s in range(nseg):
            base = s * CONF_SEGP
            top = u[s * CONF_SEG - CONF_HALO:s * CONF_SEG] * joined if s > 0 else zeros_halo
            bot = (u[(s + 1) * CONF_SEG:(s + 1) * CONF_SEG + CONF_HALO] * joined
                   if s < nseg - 1 else zeros_halo)
            pad_ref[0, base:base + CONF_HALO, :] = top
            pad_ref[0, base + CONF_HALO:base + CONF_HALO + CONF_SEG, :] = u[s * CONF_SEG:(s + 1) * CONF_SEG]
            pad_ref[0, base + CONF_HALO + CONF_SEG:base + CONF_SEGP, :] = bot

        p0 = pad_ref[0]
        rows = nseg * CONF_SEGP
        for b in range(1, 8):
            pad_ref[b] = pltpu.roll(p0, rows - b, 0)

        def piece(t, carry):
            s = t // (CONF_SEG // CONF_PIECE)
            q0 = (t % (CONF_SEG // CONF_PIECE)) * CONF_PIECE
            src = pl.multiple_of(s * CONF_SEGP + q0, 8)
            acc = jnp.zeros((CONF_PIECE, CONF_CB), F32)
            for j in range(CONF_K):
                hi, lo = (j + 1) // 8, (j + 1) % 8
                acc = acc + wdw_ref[0, j:j + 1, c0:c0 + CONF_CB] * pad_ref[lo, pl.ds(src + 8 * hi, CONF_PIECE), :]
            dst = pl.multiple_of(s * CONF_SEG + q0, 8)
            conv_ref[pl.ds(dst, CONF_PIECE), c0:c0 + CONF_CB] = acc + bdw_ref[:, c0:c0 + CONF_CB]
            return carry

        lax.fori_loop(0, TM // CONF_PIECE, piece, 0)

    m = _mod_row(mod_ref, i * TM)
    half = TM // 2
    for r0 in (0, half):
        rows = slice(r0, r0 + half)
        y = conv_ref[rows, :]
        mu = jnp.mean(y, axis=-1, keepdims=True)
        yc = y - mu
        var = jnp.mean(yc * yc, axis=-1, keepdims=True)
        y = _silu(yc * lax.rsqrt(var + EPS) * lng_ref[...] + lnb_ref[...])
        out = _dot(y.astype(BF16), w2_ref[...]) + b2_ref[...]
        x3 = x2_ref[rows, :] + m[:, 2 * D:3 * D] * out
        x3_ref[rows, :] = x3
        h4 = _rms(x3, nf_ref[1:2, :]) * (1.0 + m[:, 4 * D:5 * D]) + m[:, 3 * D:4 * D]
        h4_ref[rows, :] = h4.astype(BF16)
        lg_ref[:, rows] = lax.dot_general(rt_ref[...], h4, (((1,), (1,)), ((), ())),
                                          precision=lax.Precision.HIGHEST, preferred_element_type=F32)


def _conf(h3, x2, w1, b1, wdw, bdw, lng, lnb, w2, b2, mods, norm_ffn1, router_t):
    full = lambda shape: pl.BlockSpec(shape, lambda i: (0,) * len(shape))
    row = lambda n: pl.BlockSpec((TM, n), lambda i: (i, 0))
    return pl.pallas_call(
        _conf_kernel,
        grid=(N_TILES,),
        in_specs=[row(D), row(D), full((D, 2 * D)), full((1, 2 * D)), full((1, CONF_K, D)), full((1, D)),
                  full((1, D)), full((1, D)), full((D, D)), full((1, D)),
                  _mod_spec(1, 1), full((2, D)), full((N_EXP, D))],
        out_specs=[row(D), row(D), pl.BlockSpec((N_EXP, TM), lambda i: (0, i))],
        out_shape=[jax.ShapeDtypeStruct((T, D), F32), jax.ShapeDtypeStruct((T, D), BF16),
                   jax.ShapeDtypeStruct((N_EXP, T), F32)],
        scratch_shapes=[pltpu.VMEM((8, (TM // CONF_SEG) * CONF_SEGP, CONF_CB), F32),
                        pltpu.VMEM((TM, D), F32)],
        compiler_params=_params("arbitrary"),
        name="conformer_conv",
    )(h3, x2, w1, b1, wdw, bdw, lng, lnb, w2, b2, mods, norm_ffn1, router_t)


TB = 256
N_TB = T // TB
SUBS = 8
SM = SUBS * TB
N_SUB_MAX = 2 * T // TB + N_EXP
N_SUP_MAX = N_SUB_MAX // SUBS + N_EXP - 1
YS_ROWS = (N_SUB_MAX + 2) * TB
WIN_ALIGN = 16
WIN_HALF = TB // 2 + WIN_ALIGN
FIRST_STRIDE = 32
GATHER_BLOCKS = 6


def _route_kernel(lg_ref, g_ref, rank_ref, first_ref):
    lg = lg_ref[...]
    idx = lax.broadcasted_iota(jnp.int32, lg.shape, 0).astype(F32)
    none = float(N_EXP)
    m1 = jnp.max(lg, axis=0, keepdims=True)
    i1 = jnp.min(jnp.where(lg == m1, idx, none), axis=0, keepdims=True)
    rest = jnp.where(idx == i1, -jnp.inf, lg)
    m2 = jnp.max(rest, axis=0, keepdims=True)
    i2 = jnp.min(jnp.where(rest == m2, idx, none), axis=0, keepdims=True)
    e = jnp.exp(m2 - m1)
    w1 = 1.0 / (1.0 + e)
    w2 = e / (1.0 + e)
    g_ref[...] = jnp.where(idx == i1, w1, 0.0) + jnp.where(idx == i2, w2, 0.0)

    mask = jnp.where(idx == i1, 1.0, 0.0) + jnp.where(idx == i2, 1.0, 0.0)
    before = (lax.broadcasted_iota(jnp.int32, (TB, TB), 0) < lax.broadcasted_iota(jnp.int32, (TB, TB), 1))
    before = jnp.where(before, 1.0, 0.0).astype(BF16)
    lane = lax.broadcasted_iota(jnp.int32, (N_EXP, 128), 1)
    carry = jnp.zeros((N_EXP, 1), F32)
    first = jnp.zeros((N_EXP, 128), F32)
    for b in range(N_TB):
        mb = mask[:, b * TB:(b + 1) * TB]
        local = _dot(mb.astype(BF16), before)
        rank_ref[:, b * TB:(b + 1) * TB] = jnp.where(mb > 0.0, local + carry, -1.0)
        first = jnp.where(lane == b, carry, first)
        carry = carry + jnp.sum(mb, axis=1, keepdims=True)
    first_ref[...] = jnp.where(lane == N_TB, carry, first)


def _route(logits_t):
    return pl.pallas_call(
        _route_kernel,
        out_shape=[jax.ShapeDtypeStruct((N_EXP, T), F32), jax.ShapeDtypeStruct((N_EXP, T), F32),
                   jax.ShapeDtypeStruct((N_EXP, 128), F32)],
        compiler_params=pltpu.CompilerParams(vmem_limit_bytes=VMEM_LIMIT),
        name="route",
    )(logits_t)


def _moe_plan(first):
    first = first[:, :FIRST_STRIDE].astype(jnp.int32)
    cnt = first[:, N_TB]
    nt = (cnt + (TB - 1)) >> 8
    off_end = jnp.cumsum(nt)
    off = off_end - nt
    nsub = off_end[-1]
    nsup = (nt + (SUBS - 1)) >> 3
    sup_end = jnp.cumsum(nsup)
    sup_off = sup_end - nsup
    s = jnp.minimum(jnp.arange(N_SUP_MAX), sup_end[-1] - 1)
    valid = jnp.arange(N_SUP_MAX) < sup_end[-1]
    se = jnp.sum(s[:, None] >= sup_end[None, :], axis=1)
    sk0 = (s - sup_off[se]) * SUBS
    sns = jnp.where(valid, jnp.clip(nt[se] - sk0, 0, SUBS), 0)
    sj0 = off[se] + sk0
    base = (jnp.arange(FIRST_STRIDE) * TB)[None, :, None]
    blo = jnp.minimum(jnp.sum(first[:, None, 1:N_TB + 1] <= base, axis=2), N_TB - 1)
    end = jnp.minimum(base + TB, cnt[:, None, None])
    bhi = jnp.maximum(jnp.sum(first[:, None, :N_TB] < end, axis=2) - 1, blo)
    ng = (bhi - blo) // GATHER_BLOCKS + 1
    start = TB * off[:, None] + first[:, :N_TB]
    lead = start & (WIN_ALIGN - 1)
    wina = start - lead
    rel = first[:, :N_TB] - lead
    need = lead + (first[:, 1:N_TB + 1] - first[:, :N_TB]) > WIN_HALF
    winb = lax.cummax(jnp.where(need, wina + WIN_HALF, 0), axis=1)
    wide = jnp.any(need, axis=0)
    i32 = lambda a: a.astype(jnp.int32)
    return dict(se=i32(se), sk0=i32(sk0), sns=i32(sns), sj0=i32(sj0), nsub=i32(nsub).reshape(1),
                first=i32(first.reshape(-1)), blo=i32(blo.reshape(-1)), ng=i32(ng.reshape(-1)),
                wina=i32(wina.T.reshape(-1)), winb=i32(winb.T.reshape(-1)), rel=i32(rel.T.reshape(-1)),
                wide=i32(wide))


def _moe_gmm_kernel(se_ref, sk0_ref, sns_ref, sj0_ref, nsub_ref, first_ref, blo_ref, ng_ref,
                    x_ref, rank_ref, gate_ref, wg_ref, wu_ref, wd_ref, ys_ref,
                    xs_ref, gs_ref, yacc_ref, ybuf_ref, acc_ref, gacc_ref, wgb_ref, wub_ref, wdb_ref, sem):
    s, c = pl.program_id(0), pl.program_id(1)
    nc = pl.num_programs(1)
    e, k0, ns = se_ref[s], sk0_ref[s], sns_ref[s]

    def sub_rows(k):
        return pl.ds(pl.multiple_of(k * TB, TB), TB)

    def out_copy(k, row0):
        dst = ys_ref.at[pl.ds(pl.multiple_of(row0 + k * TB, TB), TB)]
        return pltpu.make_async_copy(ybuf_ref.at[sub_rows(k)], dst, sem.at[k])

    @pl.when((ns > 0) & (c == 0))
    def _gather():
        def sub(k, carry):
            base = (k0 + k) * TB
            slot = (lax.broadcasted_iota(jnp.int32, (TB, 1), 0) + base).astype(F32)
            acc_ref[...] = jnp.zeros_like(acc_ref)
            gacc_ref[...] = jnp.zeros_like(gacc_ref)
            blo = blo_ref[e * FIRST_STRIDE + k0 + k]

            def group(g, carry2):
                b0 = blo + g * GATHER_BLOCKS
                t0 = pl.multiple_of(jnp.minimum(b0, N_TB - GATHER_BLOCKS) * TB, TB)
                lo = first_ref[e * FIRST_STRIDE + b0].astype(F32)
                cols = pl.ds(t0, GATHER_BLOCKS * TB)
                hit = rank_ref[pl.ds(e, 1), cols] == jnp.where(slot >= lo, slot, -2.0)
                onehot = jnp.where(hit, 1.0, 0.0).astype(BF16)
                acc_ref[...] += _dot(onehot, x_ref[cols, :])
                gacc_ref[...] += jnp.sum(jnp.where(hit, gate_ref[pl.ds(e, 1), cols], 0.0),
                                         axis=-1, keepdims=True)
                return carry2

            lax.fori_loop(0, ng_ref[e * FIRST_STRIDE + k0 + k], group, 0)
            xs_ref[sub_rows(k), :] = acc_ref[...].astype(BF16)
            gs_ref[sub_rows(k), :] = gacc_ref[...]
            yacc_ref[sub_rows(k), :] = jnp.zeros((TB, D), F32)
            return carry

        lax.fori_loop(0, ns, sub, 0)

    @pl.when(ns > 0)
    def _compute():
        def swiglu(rows, wg, wu, wd):
            x = xs_ref[rows, :]
            g = _dot(x, wg)
            u = _dot(x, wu)
            yacc_ref[rows, :] += _dot((_silu(g) * u).astype(BF16), wd)

        def first_chain(rows):
            wg, wu, wd = wg_ref[0].astype(BF16), wu_ref[0].astype(BF16), wd_ref[0].astype(BF16)
            wgb_ref[...] = wg
            wub_ref[...] = wu
            wdb_ref[...] = wd
            swiglu(rows, wg, wu, wd)

        @pl.when(ns >= 2)
        def _():
            first_chain(pl.ds(0, 2 * TB))

        @pl.when(ns == 1)
        def _():
            first_chain(pl.ds(0, TB))

        rest = jnp.maximum(ns - 2, 0)

        def chain(rows):
            swiglu(rows, wgb_ref[...], wub_ref[...], wdb_ref[...])

        def quad(k, carry):
            chain(pl.ds(pl.multiple_of(2 * TB + k * (4 * TB), 2 * TB), 4 * TB))
            return carry

        lax.fori_loop(0, rest >> 2, quad, 0)

        @pl.when((rest & 2) != 0)
        def _():
            chain(pl.ds(pl.multiple_of(2 * TB + (rest >> 2) * (4 * TB), 2 * TB), 2 * TB))

        @pl.when((rest & 1) != 0)
        def _():
            chain(sub_rows(ns - 1))

    @pl.when((ns > 0) & (c == nc - 1))
    def _store():
        row0 = sj0_ref[s] * TB

        def put(k, carry):
            ybuf_ref[sub_rows(k), :] = (yacc_ref[sub_rows(k), :] * gs_ref[sub_rows(k), :]).astype(BF16)
            out_copy(k, row0).start()
            return carry

        def done(k, carry):
            out_copy(k, row0).wait()
            return carry

        lax.fori_loop(0, ns, put, 0)
        lax.fori_loop(0, ns, done, 0)

    @pl.when((s == pl.num_programs(0) - 1) & (c == nc - 1))
    def _zero_tail():
        ybuf_ref[0:TB, :] = jnp.zeros((TB, D), BF16)
        nsub = nsub_ref[0]

        def fill(k, carry):
            cp = out_copy(0, (nsub + k) * TB)
            cp.start()
            cp.wait()
            return carry

        lax.fori_loop(0, YS_ROWS // TB - nsub, fill, 0)


def _moe_gmm(plan, h4, rank, gates, w_gu, w_down):
    nc = D_FFE // MOE_FC

    def chunk(s, c, sns):
        return jnp.where(sns[s] > 0, c, nc - 1)

    return pl.pallas_call(
        _moe_gmm_kernel,
        grid_spec=pltpu.PrefetchScalarGridSpec(
            num_scalar_prefetch=8,
            grid=(N_SUP_MAX, nc),
            in_specs=[
                pl.BlockSpec((T, D), lambda s, c, *_: (0, 0), pipeline_mode=pl.Buffered(1)),
                pl.BlockSpec((N_EXP, T), lambda s, c, *_: (0, 0)),
                pl.BlockSpec((N_EXP, T), lambda s, c, *_: (0, 0)),
                pl.BlockSpec((1, D, MOE_FC), lambda s, c, se, sk0, sns, *_: (se[s], 0, chunk(s, c, sns))),
                pl.BlockSpec((1, D, MOE_FC), lambda s, c, se, sk0, sns, *_: (se[s], 0, nc + chunk(s, c, sns))),
                pl.BlockSpec((1, MOE_FC, D), lambda s, c, se, sk0, sns, *_: (se[s], chunk(s, c, sns), 0)),
            ],
            out_specs=pl.BlockSpec(memory_space=pl.ANY),
            scratch_shapes=[
                pltpu.VMEM((SM, D), BF16), pltpu.VMEM((SM, 1), F32), pltpu.VMEM((SM, D), F32),
                pltpu.VMEM((SM, D), BF16), pltpu.VMEM((TB, D), F32), pltpu.VMEM((TB, 1), F32),
                pltpu.VMEM((D, MOE_FC), BF16), pltpu.VMEM((D, MOE_FC), BF16), pltpu.VMEM((MOE_FC, D), BF16),
                pltpu.SemaphoreType.DMA((SUBS,)),
            ],
        ),
        out_shape=jax.ShapeDtypeStruct((YS_ROWS, D), BF16),
        compiler_params=_params("arbitrary", "arbitrary"),
        name="moe_gmm",
    )(plan["se"], plan["sk0"], plan["sns"], plan["sj0"], plan["nsub"], plan["first"], plan["blo"], plan["ng"],
      h4, rank, gates, w_gu, w_gu, w_down)


def _moe_combine_kernel(wina_ref, winb_ref, rel_ref, wide_ref, *refs):
    ya, yb = refs[:N_EXP], refs[N_EXP:2 * N_EXP]
    rank_ref, x3_ref, mod_ref, yp_ref, ys_ref, ycat_ref, acc_ref = refs[2 * N_EXP:]
    b = pl.program_id(0)
    row = lax.broadcasted_iota(jnp.int32, (WIN_HALF, 1), 0)

    def onehot(e, first_row):
        slot = (row + (rel_ref[b * N_EXP + e] + first_row)).astype(F32)
        return jnp.where(rank_ref[e:e + 1, :] == slot, 1.0, 0.0).astype(BF16)

    def gather(y_refs, first_row, base):
        pieces = []
        for e in range(N_EXP):
            ycat_ref[base + e * WIN_HALF:base + (e + 1) * WIN_HALF, :] = y_refs[e][...]
            pieces.append(onehot(e, first_row))
        return pieces

    def combine(pieces, rows):
        return lax.dot_general(jnp.concatenate(pieces, axis=0), ycat_ref[0:rows, :], (((0,), (0,)), ((), ())),
                               preferred_element_type=F32)

    @pl.when(wide_ref[b] == 0)
    def _():
        acc_ref[...] = combine(gather(ya, 0, 0), N_EXP * WIN_HALF)

    @pl.when(wide_ref[b] != 0)
    def _():
        pieces = gather(ya, 0, 0) + gather(yb, WIN_HALF, N_EXP * WIN_HALF)
        acc_ref[...] = combine(pieces, 2 * N_EXP * WIN_HALF)

    out = x3_ref[...] + _mod_row(mod_ref, b * TB)[:, 5 * D:6 * D] * acc_ref[...]

    @pl.when(b < T_CTX // TB)
    def _():
        yp_ref[...] = out

    @pl.when(b >= T_CTX // TB)
    def _():
        ys_ref[...] = out


def _moe_combine(plan, ysorted, rank, x3, mods):
    ctx_blocks = T_CTX // TB

    def window(e, second):
        def index(b, wina, winb, rel, wide):
            start = (winb if second else wina)[b * N_EXP + e]
            return pl.multiple_of(start, WIN_ALIGN), 0
        return pl.BlockSpec((pl.Element(WIN_HALF), pl.Element(D)), index)

    return pl.pallas_call(
        _moe_combine_kernel,
        grid_spec=pltpu.PrefetchScalarGridSpec(
            num_scalar_prefetch=4,
            grid=(N_TB,),
            in_specs=[window(e, False) for e in range(N_EXP)] + [window(e, True) for e in range(N_EXP)] + [
                pl.BlockSpec((N_EXP, TB), lambda b, *_: (0, b)),
                pl.BlockSpec((TB, D), lambda b, *_: (b, 0)),
                _mod_spec(1, 1),
            ],
            out_specs=[pl.BlockSpec((TB, D), lambda b, *_: (jnp.minimum(b, ctx_blocks - 1), 0)),
                       pl.BlockSpec((TB, D), lambda b, *_: (jnp.maximum(b - ctx_blocks, 0), 0))],
            scratch_shapes=[pltpu.VMEM((2 * N_EXP * WIN_HALF, D), BF16), pltpu.VMEM((TB, D), F32)],
        ),
        out_shape=[jax.ShapeDtypeStruct((T_CTX, D), F32), jax.ShapeDtypeStruct((T_LAT, D), F32)],
        compiler_params=_params("arbitrary"),
        name="moe_combine",
    )(plan["wina"], plan["winb"], plan["rel"], plan["wide"], *([ysorted] * (2 * N_EXP)), rank, x3, mods)


def _pad_heads(w, width):
    lead = w.shape[:-1]
    w = w.reshape(*lead, HEADS, width)
    w = jnp.pad(w, [(0, 0)] * len(lead) + [(0, 0), (0, HEAD_PAD - width)])
    return w.reshape(*lead, HEADS * HEAD_PAD)


def kernel(x_prompt, x_sample, cache_ckv, cache_kpe, c, c_ctx, ada_w, ada_b, norm_mix, norm_ffn, w_in, q_a_norm,
           w_qb, kv_a_norm, w_kvb, q_norm, k_norm, w_sc, w_o, ffn_gu, ffn_down, conv_pw1, conv_pw1_b, conv_dw,
           conv_dw_b, conv_ln_g, conv_ln_b, conv_pw2, conv_pw2_b, router, moe_gu, moe_down):
    xp = x_prompt.reshape(T_CTX, D)
    xs = x_sample.reshape(T_LAT, D)

    mods = _adaln(c_ctx, c, ada_w, ada_b)

    wqb = _pad_heads(w_qb[0], QK_HEAD).astype(BF16)
    wkvb = w_kvb[0].astype(BF16)
    qn = jnp.pad(q_norm[0], (0, HEAD_PAD - QK_HEAD)).reshape(1, HEAD_PAD)
    kn = jnp.pad(k_norm[0], (0, HEAD_PAD - QK_HEAD)).reshape(1, HEAD_PAD)
    tabs = _rope_tables()

    w_in_t = jnp.swapaxes(w_in[0], 0, 1)
    q, ckv, kpe, sc, state_ckv, state_kpe = _even_proj(xp, xs, mods, norm_mix, w_in_t, q_a_norm, wqb, kv_a_norm,
                                                       qn, w_sc, tabs)

    lat_tile0 = T_CTX // TKV
    ident = LAT_LEN // TKV
    k, kv = _kv_proj(ckv, kpe, wkvb, kn, tabs,
                     lambda i: jnp.where(i < lat_tile0, ident, (i - lat_tile0) % ident), "kv_proj")
    cache_kpe_p = jnp.pad(cache_kpe[:, 0].reshape(N_LAT_SEQ * PAST, QK_ROPE), ((0, 0), (0, HEAD_PAD - QK_ROPE)))
    kc, kvc = _kv_proj(cache_ckv[:, 0].reshape(N_LAT_SEQ * PAST, KV_LORA), cache_kpe_p, wkvb, kn, tabs,
                       lambda i: ident, "kv_proj_cache")

    oc = _attn_ctx(q, k, kv)
    ol = _attn_lat(q, kc, kvc, k, kv)
    x2, h3 = _ffn(oc, ol, sc, xp, xs, w_o[0].astype(BF16), norm_ffn, ffn_gu[0], ffn_down[0], mods, norm_mix)

    x3, h4, logits_t = _conf(h3, x2, conv_pw1[0].astype(BF16), conv_pw1_b, conv_dw, conv_dw_b, conv_ln_g,
                             conv_ln_b, conv_pw2[0].astype(BF16), conv_pw2_b, mods, norm_ffn, router[0].T)
    gates, rank, first = _route(logits_t)
    plan = _moe_plan(first)
    ysorted = _moe_gmm(plan, h4, rank, gates, moe_gu[0], moe_down[0])
    yp, ys = _moe_combine(plan, ysorted, rank, x3, mods)

    return (yp.reshape(N_CTX_SEQ, CTX_LEN, D), ys.reshape(N_LAT_SEQ, LAT_LEN, D),
            state_ckv.reshape(N_CTX_SEQ, 1, CTX_LEN, KV_LORA), state_kpe.reshape(N_CTX_SEQ, 1, CTX_LEN, QK_ROPE))
```

```python
import functools

import jax
import jax.numpy as jnp
import numpy as np
from jax import lax
from jax.experimental import pallas as pl
from jax.experimental.pallas import tpu as pltpu

F32 = jnp.float32
BF16 = jnp.bfloat16

D = 1024
N_CTX_SEQ, CTX_LEN = 16, 256
N_LAT_SEQ, LAT_LEN = 2, 1024
T_CTX = N_CTX_SEQ * CTX_LEN
T_LAT = N_LAT_SEQ * LAT_LEN
T = T_CTX + T_LAT
PAST = 256
GRID_W = 64
HEADS = 8
QK_NOPE, QK_ROPE, V_HEAD = 64, 32, 64
QK_HEAD = QK_NOPE + QK_ROPE
HEAD_PAD = 128
Q_LORA, KV_LORA = 256, 128
SC_W = 512
IN0_W = Q_LORA + KV_LORA + QK_ROPE + 3 * SC_W
CONF_K = 31
D_FF = 2816
N_EXP = 8
D_FFE = 3584
EPS = 1e-6
ROPE_THETA = 10000.0

TM = 1024
N_TILES = T // TM
CTX_TILES = T_CTX // TM
TKV = 512
TQ = 256
FFN_FC = 256
MOE_FC = 512
VMEM_LIMIT = 56 * 1024 * 1024


def _dot(a, b):
    return jnp.dot(a, b, preferred_element_type=F32)


def _dot_nt(a, b):
    return lax.dot_general(a, b, (((1,), (1,)), ((), ())), preferred_element_type=F32)


def _rms(x, g):
    return x * lax.rsqrt(jnp.mean(x * x, axis=-1, keepdims=True) + EPS) * g


def _silu(x):
    return x * jax.nn.sigmoid(x)


def _params(*sem):
    return pltpu.CompilerParams(dimension_semantics=sem, vmem_limit_bytes=VMEM_LIMIT)


def _mod_row(mod_ref, row0):
    cond = jnp.maximum(row0 - (T_CTX - LAT_LEN), 0) >> 10
    return mod_ref[0, pl.ds(cond, 1), :]


def _mod_spec(layer, ngrid):
    return pl.BlockSpec((1, 8, 6 * D), lambda *_: (layer, 0, 0))


def _adaln_kernel(cc_ref, c_ref, w_ref, b_ref, o_ref):
    l = pl.program_id(0)
    row = lax.broadcasted_iota(jnp.int32, (8, 1), 0)
    cond = jnp.where(row == 0, cc_ref[...], 0.0)
    for b in range(N_LAT_SEQ):
        cond = jnp.where(row == 1 + b, c_ref[b:b + 1, :], cond)
    o_ref[0] = _dot(_silu(cond).astype(BF16), w_ref[0].astype(BF16)) + b_ref[pl.ds(l, 1), :]


def _adaln(c_ctx, c, ada_w, ada_b):
    depth = ada_w.shape[0]
    tn = 2048
    return pl.pallas_call(
        _adaln_kernel,
        grid=(depth, 6 * D // tn),
        in_specs=[
            pl.BlockSpec((1, D), lambda l, j: (0, 0)),
            pl.BlockSpec((N_LAT_SEQ, D), lambda l, j: (0, 0)),
            pl.BlockSpec((1, D, tn), lambda l, j: (l, 0, j)),
            pl.BlockSpec((depth, tn), lambda l, j: (0, j)),
        ],
        out_specs=pl.BlockSpec((1, 8, tn), lambda l, j: (l, 0, j)),
        out_shape=jax.ShapeDtypeStruct((depth, 8, 6 * D), F32),
        compiler_params=_params("arbitrary", "arbitrary"),
        name="adaln",
    )(c_ctx.reshape(1, D), c, ada_w, ada_b)


def _rope_tables():
    half = QK_ROPE // 2
    nf = half // 2
    pos = np.arange(LAT_LEN)
    inv = ROPE_THETA ** (-np.arange(nf, dtype=np.float64) / nf)
    k = np.arange(QK_ROPE)
    part, idx = k // half, k % half
    p = np.where(part[None, :] == 0, (pos // GRID_W)[:, None], (pos % GRID_W)[:, None])
    ang = p * inv[idx % nf][None, :]
    cos, sin = np.cos(ang), np.sin(ang)
    first = (idx < nf)[None, :]
    s1 = np.where(first, -sin, 0.0)
    s2 = np.where(first, 0.0, sin)

    def place(t, fill):
        tab = np.full((2 * LAT_LEN, HEAD_PAD), fill, np.float32)
        tab[:LAT_LEN, QK_NOPE:QK_HEAD] = t
        return jnp.asarray(tab)

    return place(cos, 1.0), place(s1, 0.0), place(s2, 0.0)


def _rope(blk, cos, s1, s2):
    return blk * cos + pltpu.roll(blk, 8, 1) * s2 + pltpu.roll(blk, HEAD_PAD - 8, 1) * s1


def _head_norm(blk, g):
    ms = jnp.sum(blk * blk, axis=-1, keepdims=True) * (1.0 / QK_HEAD)
    return blk * lax.rsqrt(ms + EPS) * g


def _even_proj_kernel(xp_ref, xs_ref, mod_ref, nm_ref, win_ref, qan_ref, wqb_ref, kvan_ref,
                      qn_ref, wsc_ref, cos_ref, s1_ref, s2_ref,
                      q_ref, ckv_ref, kpe_ref, sc_ref, sckv_ref, skpe_ref, wt_ref):
    i = pl.program_id(0)
    n_a = Q_LORA + KV_LORA + QK_ROPE

    @pl.when(i == 0)
    def _():
        wt_ref[...] = win_ref[...].astype(BF16)

    x = jnp.where(i < CTX_TILES, xp_ref[...], xs_ref[...])
    m = _mod_row(mod_ref, i * TM)
    h = _rms(x, nm_ref[0:1, :]) * (1.0 + m[:, D:2 * D]) + m[:, 0:D]
    hb = h.astype(BF16)

    za = _dot_nt(hb, wt_ref[0:512, :])
    ckv = _rms(za[:, Q_LORA:Q_LORA + KV_LORA], kvan_ref[...])
    lane = lax.broadcasted_iota(jnp.int32, (1, HEAD_PAD), 1)
    kpe = jnp.where(lane < QK_ROPE, za[:, Q_LORA + KV_LORA:], 0.0)
    ckv_ref[...] = ckv
    kpe_ref[...] = kpe

    @pl.when(i < CTX_TILES)
    def _():
        sckv_ref[...] = ckv
        skpe_ref[...] = kpe[:, :QK_ROPE]

    qa = _rms(za[:, :Q_LORA], qan_ref[...]).astype(BF16)
    cos, s1, s2 = cos_ref[...], s1_ref[...], s2_ref[...]
    qn = qn_ref[...]
    scale = QK_HEAD ** -0.5
    for hp in range(HEADS // 2):
        qq = _dot(qa, wqb_ref[:, hp * 256:(hp + 1) * 256])
        for j in range(2):
            blk = _head_norm(qq[:, j * HEAD_PAD:(j + 1) * HEAD_PAD], qn)
            blk = _rope(blk, cos, s1, s2) * scale
            h0 = (2 * hp + j) * HEAD_PAD
            q_ref[:, h0:h0 + HEAD_PAD] = blk.astype(BF16)

    gb = _dot_nt(hb, wt_ref[n_a:n_a + SC_W, :])
    v = _dot_nt(hb, wt_ref[n_a + SC_W:n_a + 2 * SC_W, :]) * _dot_nt(hb, wt_ref[n_a + 2 * SC_W:n_a + 3 * SC_W, :])
    seq = jnp.where(i < CTX_TILES, CTX_LEN, LAT_LEN)
    r = lax.broadcasted_iota(jnp.int32, (TM, 1), 0) & (seq - 1)
    vp = jnp.where(r == 0, 0.0, pltpu.roll(v, 1, 0))
    vn = jnp.where(r == seq - 1, 0.0, pltpu.roll(v, TM - 1, 0))
    w = wsc_ref[0]
    y = w[0:1] * vp + w[1:2] * v + w[2:3] * vn
    sc_ref[...] = (gb * y).astype(BF16)


def _even_proj(xp, xs, mods, norm_mix, w_in, q_a_norm, wqb, kv_a_norm, qn, w_sc, tabs):
    full = lambda shape: pl.BlockSpec(shape, lambda i: (0,) * len(shape))
    tab = pl.BlockSpec((TM, HEAD_PAD), lambda i: (jnp.where(i < CTX_TILES, 1, 0), 0))
    row = lambda n: pl.BlockSpec((TM, n), lambda i: (i, 0))
    ctx_row = lambda n: pl.BlockSpec((TM, n), lambda i: (jnp.minimum(i, CTX_TILES - 1), 0))
    return pl.pallas_call(
        _even_proj_kernel,
        grid=(N_TILES,),
        in_specs=[
            ctx_row(D),
            pl.BlockSpec((TM, D), lambda i: (jnp.maximum(i - CTX_TILES, 0), 0)),
            _mod_spec(0, 1),
            full((2, D)),
            pl.BlockSpec((IN0_W, D), lambda i: (0, 0), pipeline_mode=pl.Buffered(1)),
            full((1, Q_LORA)),
            full((Q_LORA, HEADS * HEAD_PAD)), full((1, KV_LORA)), full((1, HEAD_PAD)),
            full((1, 3, SC_W)), tab, tab, tab,
        ],
        out_specs=[row(HEADS * HEAD_PAD), row(KV_LORA), row(HEAD_PAD), row(SC_W),
                   ctx_row(KV_LORA), ctx_row(QK_ROPE)],
        out_shape=[
            jax.ShapeDtypeStruct((T, HEADS * HEAD_PAD), BF16),
            jax.ShapeDtypeStruct((T, KV_LORA), F32),
            jax.ShapeDtypeStruct((T, HEAD_PAD), F32),
            jax.ShapeDtypeStruct((T, SC_W), BF16),
            jax.ShapeDtypeStruct((T_CTX, KV_LORA), F32),
            jax.ShapeDtypeStruct((T_CTX, QK_ROPE), F32),
        ],
        scratch_shapes=[pltpu.VMEM((IN0_W, D), BF16)],
        compiler_params=_params("arbitrary"),
        name="even_proj",
    )(xp, xs, mods, norm_mix, w_in, q_a_norm, wqb, kv_a_norm, qn, w_sc, *tabs)


def _kv_proj_kernel(ckv_ref, kpe_ref, wkvb_ref, kn_ref, cos_ref, s1_ref, s2_ref, k_ref, kv_ref):
    kv = _dot(ckv_ref[...].astype(BF16), wkvb_ref[...])
    kv_ref[...] = kv.astype(BF16)
    kpe = pltpu.roll(kpe_ref[...], QK_NOPE, 1)
    lane = lax.broadcasted_iota(jnp.int32, (1, HEAD_PAD), 1)
    kn = kn_ref[...]
    pe_sq = jnp.sum(kpe * kpe, axis=-1, keepdims=True)
    pe = _rope(kpe * kn, cos_ref[...], s1_ref[...], s2_ref[...])
    for h in range(HEADS):
        blk = kv[:, h * HEAD_PAD:(h + 1) * HEAD_PAD]
        nope = jnp.where(lane < QK_NOPE, blk, 0.0)
        ms = (jnp.sum(nope * nope, axis=-1, keepdims=True) + pe_sq) * (1.0 / QK_HEAD)
        k = jnp.where(lane < QK_NOPE, blk * kn, pe) * lax.rsqrt(ms + EPS)
        k_ref[:, h * HEAD_PAD:(h + 1) * HEAD_PAD] = k.astype(BF16)


def _kv_proj(ckv, kpe, wkvb, kn, tabs, tab_index, name):
    n = ckv.shape[0]
    full = lambda shape: pl.BlockSpec(shape, lambda i: (0,) * len(shape))
    tab = pl.BlockSpec((TKV, HEAD_PAD), lambda i: (tab_index(i), 0))
    row = lambda w: pl.BlockSpec((TKV, w), lambda i: (i, 0))
    return pl.pallas_call(
        _kv_proj_kernel,
        grid=(n // TKV,),
        in_specs=[row(KV_LORA), row(HEAD_PAD), full((KV_LORA, HEADS * HEAD_PAD)), full((1, HEAD_PAD)),
                  tab, tab, tab],
        out_specs=[row(HEADS * HEAD_PAD), row(HEADS * HEAD_PAD)],
        out_shape=[jax.ShapeDtypeStruct((n, HEADS * HEAD_PAD), BF16)] * 2,
        compiler_params=_params("arbitrary"),
        name=name,
    )(ckv, kpe, wkvb, kn, *tabs)


def _pair_out(o0, o1):
    lane = lax.broadcasted_iota(jnp.int32, (1, HEAD_PAD), 1)
    return jnp.where(lane < V_HEAD, pltpu.roll(o0, V_HEAD, 1), o1).astype(BF16)


CTX_SEQS = 2


def _attn_ctx_kernel(q_ref, k_ref, kv_ref, o_ref):
    for b in range(CTX_SEQS):
        rows = slice(b * CTX_LEN, (b + 1) * CTX_LEN)
        for hp in range(HEADS // 2):
            outs = []
            for j in range(2):
                lanes = slice((2 * hp + j) * HEAD_PAD, (2 * hp + j + 1) * HEAD_PAD)
                s = _dot_nt(q_ref[rows, lanes], k_ref[rows, lanes])
                p = jnp.exp(s - jnp.max(s, axis=-1, keepdims=True))
                l = jnp.sum(p, axis=-1, keepdims=True)
                outs.append(_dot(p.astype(BF16), kv_ref[rows, lanes]) / l)
            o_ref[rows, hp * HEAD_PAD:(hp + 1) * HEAD_PAD] = _pair_out(*outs)


def _attn_ctx(q, k, kv):
    blk = pl.BlockSpec((CTX_SEQS * CTX_LEN, HEADS * HEAD_PAD), lambda b: (b, 0))
    return pl.pallas_call(
        _attn_ctx_kernel,
        grid=(N_CTX_SEQ // CTX_SEQS,),
        in_specs=[blk, blk, blk],
        out_specs=pl.BlockSpec((CTX_SEQS * CTX_LEN, HEADS * V_HEAD), lambda b: (b, 0)),
        out_shape=jax.ShapeDtypeStruct((T_CTX, HEADS * V_HEAD), BF16),
        compiler_params=_params("arbitrary"),
        name="attn_ctx",
    )(q, k, kv)


LAT_HEADS = 4


def _attn_lat_kernel(q_ref, kc_ref, kvc_ref, kl_ref, kvl_ref, o_ref):
    for hp in range(LAT_HEADS // 2):
        outs = []
        for j in range(2):
            h0 = (2 * hp + j) * HEAD_PAD
            lanes = slice(h0, h0 + HEAD_PAD)
            q = q_ref[:, lanes]
            sc = _dot_nt(q, kc_ref[:, lanes])
            sl = _dot_nt(q, kl_ref[:, lanes])
            m = jnp.maximum(jnp.max(sc, axis=-1, keepdims=True), jnp.max(sl, axis=-1, keepdims=True))
            pc, pl_ = jnp.exp(sc - m), jnp.exp(sl - m)
            l = jnp.sum(pc, axis=-1, keepdims=True) + jnp.sum(pl_, axis=-1, keepdims=True)
            o = _dot(pc.astype(BF16), kvc_ref[:, lanes]) + _dot(pl_.astype(BF16), kvl_ref[:, lanes])
            outs.append(o / l)
        o_ref[:, hp * HEAD_PAD:(hp + 1) * HEAD_PAD] = _pair_out(*outs)


def _attn_lat(q, kc, kvc, k, kv):
    nq = LAT_LEN // TQ
    q0 = T_CTX // TQ
    kl0 = T_CTX // LAT_LEN
    width = LAT_HEADS * HEAD_PAD
    lat = pl.BlockSpec((LAT_LEN, width), lambda b, hg, t: (kl0 + b, hg))
    ctx = pl.BlockSpec((PAST, width), lambda b, hg, t: (b, hg))
    return pl.pallas_call(
        _attn_lat_kernel,
        grid=(N_LAT_SEQ, HEADS // LAT_HEADS, nq),
        in_specs=[pl.BlockSpec((TQ, width), lambda b, hg, t: (q0 + b * nq + t, hg)), ctx, ctx, lat, lat],
        out_specs=pl.BlockSpec((TQ, LAT_HEADS * V_HEAD), lambda b, hg, t: (b * nq + t, hg)),
        out_shape=jax.ShapeDtypeStruct((T_LAT, HEADS * V_HEAD), BF16),
        compiler_params=_params("arbitrary", "arbitrary", "arbitrary"),
        name="attn_lat",
    )(q, kc, kvc, k, kv)


FFN_NC = D_FF // FFN_FC
FFN_TM = 512
FFN_CTX_TILES = T_CTX // FFN_TM


def _ffn_kernel(oc_ref, ol_ref, sc_ref, xp_ref, xs_ref, wo_ref, nf_ref, wg_ref, wu_ref, wd_ref,
                mod0_ref, mod1_ref, nm_ref, x2_ref, h3_ref, wg_all, wu_all, wd_all):
    t = pl.program_id(0)

    @pl.when(t < FFN_NC)
    def _stage():
        wg_all[t] = wg_ref[...].astype(BF16)
        wu_all[t] = wu_ref[...].astype(BF16)
        wd_all[pl.ds(pl.multiple_of(t * FFN_FC, FFN_FC), FFN_FC), :] = wd_ref[...].astype(BF16)

    @pl.when(t >= FFN_NC - 1)
    def _tile():
        i = t - (FFN_NC - 1)
        ctx = i < FFN_CTX_TILES
        m0, m1 = _mod_row(mod0_ref, i * FFN_TM), _mod_row(mod1_ref, i * FFN_TM)
        attn = jnp.where(ctx, oc_ref[...], ol_ref[...])
        x = jnp.where(ctx, xp_ref[...], xs_ref[...])
        x1 = x + m0[:, 2 * D:3 * D] * _dot(jnp.concatenate([attn, sc_ref[...]], axis=1), wo_ref[...])
        h = (_rms(x1, nf_ref[0:1, :]) * (1.0 + m0[:, 4 * D:5 * D]) + m0[:, 3 * D:4 * D]).astype(BF16)
        act = []
        for c in range(FFN_NC):
            act.append((_silu(_dot(h, wg_all[c])) * _dot(h, wu_all[c])).astype(BF16))
        f = _dot(jnp.concatenate(act, axis=1), wd_all[...])
        x2 = x1 + m0[:, 5 * D:6 * D] * f
        x2_ref[...] = x2
        h3_ref[...] = (_rms(x2, nm_ref[1:2, :]) * (1.0 + m1[:, D:2 * D]) + m1[:, 0:D]).astype(BF16)


def _ffn(oc, ol, sc, xp, xs, wo, norm_ffn, w_gu, w_down, mods, norm_mix):
    chunk = lambda t: jnp.minimum(t, FFN_NC - 1)
    tile = lambda t: jnp.maximum(t - (FFN_NC - 1), 0)
    full = lambda shape: pl.BlockSpec(shape, lambda t: (0,) * len(shape))
    row = lambda n: pl.BlockSpec((FFN_TM, n), lambda t: (tile(t), 0))
    first = lambda n: pl.BlockSpec((FFN_TM, n), lambda t: (jnp.minimum(tile(t), FFN_CTX_TILES - 1), 0))
    second = lambda n: pl.BlockSpec((FFN_TM, n), lambda t: (jnp.maximum(tile(t) - FFN_CTX_TILES, 0), 0))
    return pl.pallas_call(
        _ffn_kernel,
        grid=(FFN_NC - 1 + T // FFN_TM,),
        in_specs=[first(HEADS * V_HEAD), second(HEADS * V_HEAD), row(SC_W), first(D), second(D),
                  full((HEADS * V_HEAD + SC_W, D)), full((2, D)),
                  pl.BlockSpec((D, FFN_FC), lambda t: (0, chunk(t))),
                  pl.BlockSpec((D, FFN_FC), lambda t: (0, FFN_NC + chunk(t))),
                  pl.BlockSpec((FFN_FC, D), lambda t: (chunk(t), 0)),
                  _mod_spec(0, 1), _mod_spec(1, 1), full((2, D))],
        out_specs=[row(D), row(D)],
        out_shape=[jax.ShapeDtypeStruct((T, D), F32), jax.ShapeDtypeStruct((T, D), BF16)],
        scratch_shapes=[pltpu.VMEM((FFN_NC, D, FFN_FC), BF16), pltpu.VMEM((FFN_NC, D, FFN_FC), BF16),
                        pltpu.VMEM((D_FF, D), BF16)],
        compiler_params=_params("arbitrary"),
        name="ffn_dense",
    )(oc, ol, sc, xp, xs, wo, norm_ffn, w_gu, w_gu, w_down, mods, mods, norm_mix)


CONF_CB = 256
CONF_SEG = 256
CONF_HALO = 16
CONF_SEGP = CONF_SEG + 2 * CONF_HALO
CONF_PIECE = 64


def _conf_kernel(h_ref, x2_ref, w1_ref, b1_ref, wdw_ref, bdw_ref, lng_ref, lnb_ref, w2_ref, b2_ref,
                 mod_ref, nf_ref, rt_ref, x3_ref, h4_ref, lg_ref, pad_ref, conv_ref):
    i = pl.program_id(0)
    nseg = TM // CONF_SEG
    h = h_ref[...]
    joined = jnp.where(i < CTX_TILES, 0.0, 1.0)
    zeros_halo = jnp.zeros((CONF_HALO, CONF_CB), F32)
    for cb in range(D // CONF_CB):
        c0 = cb * CONF_CB
        a = _dot(h, w1_ref[:, c0:c0 + CONF_CB]) + b1_ref[:, c0:c0 + CONF_CB]
        g = _dot(h, w1_ref[:, D + c0:D + c0 + CONF_CB]) + b1_ref[:, D + c0:D + c0 + CONF_CB]
        u = a * jax.nn.sigmoid(g)
        for s in range(nseg):
            base = s * CONF_SEGP
            top = u[s * CONF_SEG - CONF_HALO:s * CONF_SEG] * joined if s > 0 else zeros_halo
            bot = (u[(s + 1) * CONF_SEG:(s + 1) * CONF_SEG + CONF_HALO] * joined
                   if s < nseg - 1 else zeros_halo)
            pad_ref[0, base:base + CONF_HALO, :] = top
            pad_ref[0, base + CONF_HALO:base + CONF_HALO + CONF_SEG, :] = u[s * CONF_SEG:(s + 1) * CONF_SEG]
            pad_ref[0, base + CONF_HALO + CONF_SEG:base + CONF_SEGP, :] = bot

        p0 = pad_ref[0]
        rows = nseg * CONF_SEGP
        for b in range(1, 8):
            pad_ref[b] = pltpu.roll(p0, rows - b, 0)

        def piece(t, carry):
            s = t // (CONF_SEG // CONF_PIECE)
            q0 = (t % (CONF_SEG // CONF_PIECE)) * CONF_PIECE
            src = pl.multiple_of(s * CONF_SEGP + q0, 8)
            acc = jnp.zeros((CONF_PIECE, CONF_CB), F32)
            for j in range(CONF_K):
                hi, lo = (j + 1) // 8, (j + 1) % 8
                acc = acc + wdw_ref[0, j:j + 1, c0:c0 + CONF_CB] * pad_ref[lo, pl.ds(src + 8 * hi, CONF_PIECE), :]
            dst = pl.multiple_of(s * CONF_SEG + q0, 8)
            conv_ref[pl.ds(dst, CONF_PIECE), c0:c0 + CONF_CB] = acc + bdw_ref[:, c0:c0 + CONF_CB]
            return carry

        lax.fori_loop(0, TM // CONF_PIECE, piece, 0)

    m = _mod_row(mod_ref, i * TM)
    half = TM // 2
    for r0 in (0, half):
        rows = slice(r0, r0 + half)
        y = conv_ref[rows, :]
        mu = jnp.mean(y, axis=-1, keepdims=True)
        yc = y - mu
        var = jnp.mean(yc * yc, axis=-1, keepdims=True)
        y = _silu(yc * lax.rsqrt(var + EPS) * lng_ref[...] + lnb_ref[...])
        out = _dot(y.astype(BF16), w2_ref[...]) + b2_ref[...]
        x3 = x2_ref[rows, :] + m[:, 2 * D:3 * D] * out
        x3_ref[rows, :] = x3
        h4 = _rms(x3, nf_ref[1:2, :]) * (1.0 + m[:, 4 * D:5 * D]) + m[:, 3 * D:4 * D]
        h4_ref[rows, :] = h4.astype(BF16)
        lg_ref[:, rows] = lax.dot_general(rt_ref[...], h4, (((1,), (1,)), ((), ())),
                                          precision=lax.Precision.HIGHEST, preferred_element_type=F32)


def _conf(h3, x2, w1, b1, wdw, bdw, lng, lnb, w2, b2, mods, norm_ffn1, router_t):
    full = lambda shape: pl.BlockSpec(shape, lambda i: (0,) * len(shape))
    row = lambda n: pl.BlockSpec((TM, n), lambda i: (i, 0))
    return pl.pallas_call(
        _conf_kernel,
        grid=(N_TILES,),
        in_specs=[row(D), row(D), full((D, 2 * D)), full((1, 2 * D)), full((1, CONF_K, D)), full((1, D)),
                  full((1, D)), full((1, D)), full((D, D)), full((1, D)),
                  _mod_spec(1, 1), full((2, D)), full((N_EXP, D))],
        out_specs=[row(D), row(D), pl.BlockSpec((N_EXP, TM), lambda i: (0, i))],
        out_shape=[jax.ShapeDtypeStruct((T, D), F32), jax.ShapeDtypeStruct((T, D), BF16),
                   jax.ShapeDtypeStruct((N_EXP, T), F32)],
        scratch_shapes=[pltpu.VMEM((8, (TM // CONF_SEG) * CONF_SEGP, CONF_CB), F32),
                        pltpu.VMEM((TM, D), F32)],
        compiler_params=_params("arbitrary"),
        name="conformer_conv",
    )(h3, x2, w1, b1, wdw, bdw, lng, lnb, w2, b2, mods, norm_ffn1, router_t)


TB = 256
N_TB = T // TB
SUB = 128
SUBS = 16
SM = SUBS * SUB
N_SUB_MAX = 2 * T // SUB + N_EXP
N_SUP_MAX = (N_SUB_MAX + N_EXP * (SUBS - 1)) // SUBS
YS_ROWS = (N_SUB_MAX + 4) * SUB
WIN_ALIGN = 16
WIN_HALF = TB // 2 + WIN_ALIGN
FIRST_STRIDE = 32
UNIT_STRIDE = 64
GATHER_BLOCKS = 4


def _route_kernel(lg_ref, g_ref, rank_ref, first_ref):
    lg = lg_ref[...]
    idx = lax.broadcasted_iota(jnp.int32, lg.shape, 0).astype(F32)
    none = float(N_EXP)
    m1 = jnp.max(lg, axis=0, keepdims=True)
    i1 = jnp.min(jnp.where(lg == m1, idx, none), axis=0, keepdims=True)
    rest = jnp.where(idx == i1, -jnp.inf, lg)
    m2 = jnp.max(rest, axis=0, keepdims=True)
    i2 = jnp.min(jnp.where(rest == m2, idx, none), axis=0, keepdims=True)
    e = jnp.exp(m2 - m1)
    w1 = 1.0 / (1.0 + e)
    w2 = e / (1.0 + e)
    g_ref[...] = jnp.where(idx == i1, w1, 0.0) + jnp.where(idx == i2, w2, 0.0)

    mask = jnp.where(idx == i1, 1.0, 0.0) + jnp.where(idx == i2, 1.0, 0.0)
    before = (lax.broadcasted_iota(jnp.int32, (TB, TB), 0) < lax.broadcasted_iota(jnp.int32, (TB, TB), 1))
    before = jnp.where(before, 1.0, 0.0).astype(BF16)
    lane = lax.broadcasted_iota(jnp.int32, (N_EXP, 128), 1)
    carry = jnp.zeros((N_EXP, 1), F32)
    first = jnp.zeros((N_EXP, 128), F32)
    for b in range(N_TB):
        mb = mask[:, b * TB:(b + 1) * TB]
        local = _dot(mb.astype(BF16), before)
        rank_ref[:, b * TB:(b + 1) * TB] = jnp.where(mb > 0.0, local + carry, -1.0)
        first = jnp.where(lane == b, carry, first)
        carry = carry + jnp.sum(mb, axis=1, keepdims=True)
    first_ref[...] = jnp.where(lane == N_TB, carry, first)


def _route(logits_t):
    return pl.pallas_call(
        _route_kernel,
        out_shape=[jax.ShapeDtypeStruct((N_EXP, T), F32), jax.ShapeDtypeStruct((N_EXP, T), F32),
                   jax.ShapeDtypeStruct((N_EXP, 128), F32)],
        compiler_params=pltpu.CompilerParams(vmem_limit_bytes=VMEM_LIMIT),
        name="route",
    )(logits_t)


def _moe_plan(first):
    first = first[:, :FIRST_STRIDE].astype(jnp.int32)
    cnt = first[:, N_TB]
    nt = (cnt + (SUB - 1)) // SUB
    off_end = jnp.cumsum(nt)
    off = off_end - nt
    nsub = off_end[-1]
    nsup = (nt + (SUBS - 1)) // SUBS
    sup_end = jnp.cumsum(nsup)
    sup_off = sup_end - nsup
    s = jnp.minimum(jnp.arange(N_SUP_MAX), sup_end[-1] - 1)
    valid = jnp.arange(N_SUP_MAX) < sup_end[-1]
    se = jnp.sum(s[:, None] >= sup_end[None, :], axis=1)
    sk0 = (s - sup_off[se]) * SUBS
    sns = jnp.where(valid, jnp.clip(nt[se] - sk0, 0, SUBS), 0)
    sj0 = off[se] + sk0
    base = (jnp.arange(UNIT_STRIDE) * SUB)[None, :, None]
    blo = jnp.minimum(jnp.sum(first[:, None, 1:N_TB + 1] <= base, axis=2), N_TB - 1)
    end = jnp.minimum(base + SUB, cnt[:, None, None])
    bhi = jnp.maximum(jnp.sum(first[:, None, :N_TB] < end, axis=2) - 1, blo)
    ng = (bhi - blo) // GATHER_BLOCKS + 1
    start = SUB * off[:, None] + first[:, :N_TB]
    lead = start & (WIN_ALIGN - 1)
    wina = start - lead
    rel = first[:, :N_TB] - lead
    need = lead + (first[:, 1:N_TB + 1] - first[:, :N_TB]) > WIN_HALF
    winb = lax.cummax(jnp.where(need, wina + WIN_HALF, 0), axis=1)
    wide = jnp.any(need, axis=0)
    i32 = lambda a: a.astype(jnp.int32)
    return dict(se=i32(se), sk0=i32(sk0), sns=i32(sns), sj0=i32(sj0), nsub=i32(nsub).reshape(1),
                first=i32(first.reshape(-1)), blo=i32(blo.reshape(-1)), ng=i32(ng.reshape(-1)),
                wina=i32(wina.T.reshape(-1)), winb=i32(winb.T.reshape(-1)), rel=i32(rel.T.reshape(-1)),
                wide=i32(wide))


def _moe_gmm_kernel(se_ref, sk0_ref, sns_ref, sj0_ref, nsub_ref, first_ref, blo_ref, ng_ref,
                    x_ref, rank_ref, gate_ref, wg_ref, wu_ref, wd_ref, ys_ref,
                    xs_ref, gs_ref, yacc_ref, ybuf_ref, acc_ref, gacc_ref, wgb_ref, wub_ref, wdb_ref, sem):
    s, c = pl.program_id(0), pl.program_id(1)
    nc = pl.num_programs(1)
    e, k0, ns = se_ref[s], sk0_ref[s], sns_ref[s]

    def sub_rows(k):
        return pl.ds(pl.multiple_of(k * SUB, SUB), SUB)

    def out_copy(k, row0):
        dst = ys_ref.at[pl.ds(pl.multiple_of(row0 + k * SUB, SUB), SUB)]
        return pltpu.make_async_copy(ybuf_ref.at[sub_rows(k)], dst, sem.at[k])

    @pl.when((ns > 0) & (c == 0))
    def _gather():
        def sub(k, carry):
            base = (k0 + k) * SUB
            slot = (lax.broadcasted_iota(jnp.int32, (SUB, 1), 0) + base).astype(F32)
            acc_ref[...] = jnp.zeros_like(acc_ref)
            gacc_ref[...] = jnp.zeros_like(gacc_ref)
            blo = blo_ref[e * UNIT_STRIDE + k0 + k]

            def group(g, carry2):
                b0 = blo + g * GATHER_BLOCKS
                t0 = pl.multiple_of(jnp.minimum(b0, N_TB - GATHER_BLOCKS) * TB, TB)
                lo = first_ref[e * FIRST_STRIDE + b0].astype(F32)
                cols = pl.ds(t0, GATHER_BLOCKS * TB)
                hit = rank_ref[pl.ds(e, 1), cols] == jnp.where(slot >= lo, slot, -2.0)
                onehot = jnp.where(hit, 1.0, 0.0).astype(BF16)
                acc_ref[...] += _dot(onehot, x_ref[cols, :])
                gacc_ref[...] += jnp.sum(jnp.where(hit, gate_ref[pl.ds(e, 1), cols], 0.0),
                                         axis=-1, keepdims=True)
                return carry2

            lax.fori_loop(0, ng_ref[e * UNIT_STRIDE + k0 + k], group, 0)
            xs_ref[sub_rows(k), :] = acc_ref[...].astype(BF16)
            gs_ref[sub_rows(k), :] = gacc_ref[...]
            yacc_ref[sub_rows(k), :] = jnp.zeros((SUB, D), F32)
            return carry

        lax.fori_loop(0, ns, sub, 0)

    @pl.when(ns > 0)
    def _compute():
        def swiglu(rows, wg, wu, wd):
            x = xs_ref[rows, :]
            g = _dot(x, wg)
            u = _dot(x, wu)
            yacc_ref[rows, :] += _dot((_silu(g) * u).astype(BF16), wd)

        def first_chain(rows):
            wg, wu, wd = wg_ref[0].astype(BF16), wu_ref[0].astype(BF16), wd_ref[0].astype(BF16)
            wgb_ref[...] = wg
            wub_ref[...] = wu
            wdb_ref[...] = wd
            swiglu(rows, wg, wu, wd)

        @pl.when(ns >= 4)
        def _():
            first_chain(pl.ds(0, 4 * SUB))

        @pl.when(ns < 4)
        def _():
            first_chain(pl.ds(0, SUB))

        done = jnp.where(ns >= 4, 4, 1)
        rest = ns - done

        def chain(first_sub, n):
            rows = pl.ds(pl.multiple_of(first_sub * SUB, SUB), n * SUB)
            swiglu(rows, wgb_ref[...], wub_ref[...], wdb_ref[...])

        def eight(k, carry):
            chain(done + 8 * k, 8)
            return carry

        lax.fori_loop(0, rest >> 3, eight, 0)
        done8 = done + (rest & ~7)
        for n in (4, 2, 1):
            @pl.when((rest & n) != 0)
            def _(n=n):
                chain(done8 + (rest & (7 & ~(2 * n - 1))), n)

    @pl.when((ns > 0) & (c == nc - 1))
    def _store():
        row0 = sj0_ref[s] * SUB

        def put(k, carry):
            ybuf_ref[sub_rows(k), :] = (yacc_ref[sub_rows(k), :] * gs_ref[sub_rows(k), :]).astype(BF16)
            out_copy(k, row0).start()
            return carry

        def done(k, carry):
            out_copy(k, row0).wait()
            return carry

        lax.fori_loop(0, ns, put, 0)
        lax.fori_loop(0, ns, done, 0)

    @pl.when((s == pl.num_programs(0) - 1) & (c == nc - 1))
    def _zero_tail():
        ybuf_ref[0:SUB, :] = jnp.zeros((SUB, D), BF16)
        nsub = nsub_ref[0]

        def fill(k, carry):
            cp = out_copy(0, (nsub + k) * SUB)
            cp.start()
            cp.wait()
            return carry

        lax.fori_loop(0, YS_ROWS // SUB - nsub, fill, 0)


def _moe_gmm(plan, h4, rank, gates, w_gu, w_down):
    nc = D_FFE // MOE_FC

    def chunk(s, c, sns):
        return jnp.where(sns[s] > 0, c, nc - 1)

    return pl.pallas_call(
        _moe_gmm_kernel,
        grid_spec=pltpu.PrefetchScalarGridSpec(
            num_scalar_prefetch=8,
            grid=(N_SUP_MAX, nc),
            in_specs=[
                pl.BlockSpec((T, D), lambda s, c, *_: (0, 0), pipeline_mode=pl.Buffered(1)),
                pl.BlockSpec((N_EXP, T), lambda s, c, *_: (0, 0)),
                pl.BlockSpec((N_EXP, T), lambda s, c, *_: (0, 0)),
                pl.BlockSpec((1, D, MOE_FC), lambda s, c, se, sk0, sns, *_: (se[s], 0, chunk(s, c, sns))),
                pl.BlockSpec((1, D, MOE_FC), lambda s, c, se, sk0, sns, *_: (se[s], 0, nc + chunk(s, c, sns))),
                pl.BlockSpec((1, MOE_FC, D), lambda s, c, se, sk0, sns, *_: (se[s], chunk(s, c, sns), 0)),
            ],
            out_specs=pl.BlockSpec(memory_space=pl.ANY),
            scratch_shapes=[
                pltpu.VMEM((SM, D), BF16), pltpu.VMEM((SM, 1), F32), pltpu.VMEM((SM, D), F32),
                pltpu.VMEM((SM, D), BF16), pltpu.VMEM((SUB, D), F32), pltpu.VMEM((SUB, 1), F32),
                pltpu.VMEM((D, MOE_FC), BF16), pltpu.VMEM((D, MOE_FC), BF16), pltpu.VMEM((MOE_FC, D), BF16),
                pltpu.SemaphoreType.DMA((SUBS,)),
            ],
        ),
        out_shape=jax.ShapeDtypeStruct((YS_ROWS, D), BF16),
        compiler_params=_params("arbitrary", "arbitrary"),
        name="moe_gmm",
    )(plan["se"], plan["sk0"], plan["sns"], plan["sj0"], plan["nsub"], plan["first"], plan["blo"], plan["ng"],
      h4, rank, gates, w_gu, w_gu, w_down)


def _moe_combine_kernel(wina_ref, winb_ref, rel_ref, wide_ref, *refs):
    ya, yb = refs[:N_EXP], refs[N_EXP:2 * N_EXP]
    rank_ref, x3_ref, mod_ref, yp_ref, ys_ref, ycat_ref, acc_ref = refs[2 * N_EXP:]
    b = pl.program_id(0)
    row = lax.broadcasted_iota(jnp.int32, (WIN_HALF, 1), 0)

    def onehot(e, first_row):
        slot = (row + (rel_ref[b * N_EXP + e] + first_row)).astype(F32)
        return jnp.where(rank_ref[e:e + 1, :] == slot, 1.0, 0.0).astype(BF16)

    def gather(y_refs, first_row, base):
        pieces = []
        for e in range(N_EXP):
            ycat_ref[base + e * WIN_HALF:base + (e + 1) * WIN_HALF, :] = y_refs[e][...]
            pieces.append(onehot(e, first_row))
        return pieces

    def combine(pieces, rows):
        return lax.dot_general(jnp.concatenate(pieces, axis=0), ycat_ref[0:rows, :], (((0,), (0,)), ((), ())),
                               preferred_element_type=F32)

    @pl.when(wide_ref[b] == 0)
    def _():
        acc_ref[...] = combine(gather(ya, 0, 0), N_EXP * WIN_HALF)

    @pl.when(wide_ref[b] != 0)
    def _():
        pieces = gather(ya, 0, 0) + gather(yb, WIN_HALF, N_EXP * WIN_HALF)
        acc_ref[...] = combine(pieces, 2 * N_EXP * WIN_HALF)

    out = x3_ref[...] + _mod_row(mod_ref, b * TB)[:, 5 * D:6 * D] * acc_ref[...]

    @pl.when(b < T_CTX // TB)
    def _():
        yp_ref[...] = out

    @pl.when(b >= T_CTX // TB)
    def _():
        ys_ref[...] = out


def _moe_combine(plan, ysorted, rank, x3, mods):
    ctx_blocks = T_CTX // TB

    def window(e, second):
        def index(b, wina, winb, rel, wide):
            start = (winb if second else wina)[b * N_EXP + e]
            return pl.multiple_of(start, WIN_ALIGN), 0
        return pl.BlockSpec((pl.Element(WIN_HALF), pl.Element(D)), index)

    return pl.pallas_call(
        _moe_combine_kernel,
        grid_spec=pltpu.PrefetchScalarGridSpec(
            num_scalar_prefetch=4,
            grid=(N_TB,),
            in_specs=[window(e, False) for e in range(N_EXP)] + [window(e, True) for e in range(N_EXP)] + [
                pl.BlockSpec((N_EXP, TB), lambda b, *_: (0, b)),
                pl.BlockSpec((TB, D), lambda b, *_: (b, 0)),
                _mod_spec(1, 1),
            ],
            out_specs=[pl.BlockSpec((TB, D), lambda b, *_: (jnp.minimum(b, ctx_blocks - 1), 0)),
                       pl.BlockSpec((TB, D), lambda b, *_: (jnp.maximum(b - ctx_blocks, 0), 0))],
            scratch_shapes=[pltpu.VMEM((2 * N_EXP * WIN_HALF, D), BF16), pltpu.VMEM((TB, D), F32)],
        ),
        out_shape=[jax.ShapeDtypeStruct((T_CTX, D), F32), jax.ShapeDtypeStruct((T_LAT, D), F32)],
        compiler_params=_params("arbitrary"),
        name="moe_combine",
    )(plan["wina"], plan["winb"], plan["rel"], plan["wide"], *([ysorted] * (2 * N_EXP)), rank, x3, mods)


def _pad_heads(w, width):
    lead = w.shape[:-1]
    w = w.reshape(*lead, HEADS, width)
    w = jnp.pad(w, [(0, 0)] * len(lead) + [(0, 0), (0, HEAD_PAD - width)])
    return w.reshape(*lead, HEADS * HEAD_PAD)


def kernel(x_prompt, x_sample, cache_ckv, cache_kpe, c, c_ctx, ada_w, ada_b, norm_mix, norm_ffn, w_in, q_a_norm,
           w_qb, kv_a_norm, w_kvb, q_norm, k_norm, w_sc, w_o, ffn_gu, ffn_down, conv_pw1, conv_pw1_b, conv_dw,
           conv_dw_b, conv_ln_g, conv_ln_b, conv_pw2, conv_pw2_b, router, moe_gu, moe_down):
    xp = x_prompt.reshape(T_CTX, D)
    xs = x_sample.reshape(T_LAT, D)

    mods = _adaln(c_ctx, c, ada_w, ada_b)

    wqb = _pad_heads(w_qb[0], QK_HEAD).astype(BF16)
    wkvb = w_kvb[0].astype(BF16)
    qn = jnp.pad(q_norm[0], (0, HEAD_PAD - QK_HEAD)).reshape(1, HEAD_PAD)
    kn = jnp.pad(k_norm[0], (0, HEAD_PAD - QK_HEAD)).reshape(1, HEAD_PAD)
    tabs = _rope_tables()

    w_in_t = jnp.swapaxes(w_in[0], 0, 1)
    q, ckv, kpe, sc, state_ckv, state_kpe = _even_proj(xp, xs, mods, norm_mix, w_in_t, q_a_norm, wqb, kv_a_norm,
                                                       qn, w_sc, tabs)

    lat_tile0 = T_CTX // TKV
    ident = LAT_LEN // TKV
    k, kv = _kv_proj(ckv, kpe, wkvb, kn, tabs,
                     lambda i: jnp.where(i < lat_tile0, ident, (i - lat_tile0) % ident), "kv_proj")
    cache_kpe_p = jnp.pad(cache_kpe[:, 0].reshape(N_LAT_SEQ * PAST, QK_ROPE), ((0, 0), (0, HEAD_PAD - QK_ROPE)))
    kc, kvc = _kv_proj(cache_ckv[:, 0].reshape(N_LAT_SEQ * PAST, KV_LORA), cache_kpe_p, wkvb, kn, tabs,
                       lambda i: ident, "kv_proj_cache")

    oc = _attn_ctx(q, k, kv)
    ol = _attn_lat(q, kc, kvc, k, kv)
    x2, h3 = _ffn(oc, ol, sc, xp, xs, w_o[0].astype(BF16), norm_ffn, ffn_gu[0], ffn_down[0], mods, norm_mix)

    x3, h4, logits_t = _conf(h3, x2, conv_pw1[0].astype(BF16), conv_pw1_b, conv_dw, conv_dw_b, conv_ln_g,
                             conv_ln_b, conv_pw2[0].astype(BF16), conv_pw2_b, mods, norm_ffn, router[0].T)
    gates, rank, first = _route(logits_t)
    plan = _moe_plan(first)
    ysorted = _moe_gmm(plan, h4, rank, gates, moe_gu[0], moe_down[0])
    yp, ys = _moe_combine(plan, ysorted, rank, x3, mods)

    return (yp.reshape(N_CTX_SEQ, CTX_LEN, D), ys.reshape(N_LAT_SEQ, LAT_LEN, D),
            state_ckv.reshape(N_CTX_SEQ, 1, CTX_LEN, KV_LORA), state_kpe.reshape(N_CTX_SEQ, 1, CTX_LEN, QK_ROPE))
```

```python
import functools

import jax
import jax.numpy as jnp
import numpy as np
from jax import lax
from jax.experimental import pallas as pl
from jax.experimental.pallas import tpu as pltpu

F32 = jnp.float32
BF16 = jnp.bfloat16

D = 1024
N_CTX_SEQ, CTX_LEN = 16, 256
N_LAT_SEQ, LAT_LEN = 2, 1024
T_CTX = N_CTX_SEQ * CTX_LEN
T_LAT = N_LAT_SEQ * LAT_LEN
T = T_CTX + T_LAT
PAST = 256
GRID_W = 64
HEADS = 8
QK_NOPE, QK_ROPE, V_HEAD = 64, 32, 64
QK_HEAD = QK_NOPE + QK_ROPE
HEAD_PAD = 128
Q_LORA, KV_LORA = 256, 128
SC_W = 512
IN0_W = Q_LORA + KV_LORA + QK_ROPE + 3 * SC_W
CONF_K = 31
D_FF = 2816
N_EXP = 8
D_FFE = 3584
EPS = 1e-6
ROPE_THETA = 10000.0

TM = 1024
N_TILES = T // TM
CTX_TILES = T_CTX // TM
TKV = 512
TQ = 256
FFN_FC = 256
MOE_FC = 512
VMEM_LIMIT = 56 * 1024 * 1024


def _dot(a, b):
    return jnp.dot(a, b, preferred_element_type=F32)


def _dot_nt(a, b):
    return lax.dot_general(a, b, (((1,), (1,)), ((), ())), preferred_element_type=F32)


def _rms(x, g):
    return x * lax.rsqrt(jnp.mean(x * x, axis=-1, keepdims=True) + EPS) * g


def _silu(x):
    return x * jax.nn.sigmoid(x)


def _params(*sem):
    return pltpu.CompilerParams(dimension_semantics=sem, vmem_limit_bytes=VMEM_LIMIT)


def _mod_row(mod_ref, row0):
    cond = jnp.maximum(row0 - (T_CTX - LAT_LEN), 0) >> 10
    return mod_ref[0, pl.ds(cond, 1), :]


def _mod_spec(layer, ngrid):
    return pl.BlockSpec((1, 8, 6 * D), lambda *_: (layer, 0, 0))


def _adaln_kernel(cc_ref, c_ref, w_ref, b_ref, o_ref):
    l = pl.program_id(0)
    row = lax.broadcasted_iota(jnp.int32, (8, 1), 0)
    cond = jnp.where(row == 0, cc_ref[...], 0.0)
    for b in range(N_LAT_SEQ):
        cond = jnp.where(row == 1 + b, c_ref[b:b + 1, :], cond)
    o_ref[0] = _dot(_silu(cond).astype(BF16), w_ref[0].astype(BF16)) + b_ref[pl.ds(l, 1), :]


def _adaln(c_ctx, c, ada_w, ada_b):
    depth = ada_w.shape[0]
    tn = 2048
    return pl.pallas_call(
        _adaln_kernel,
        grid=(depth, 6 * D // tn),
        in_specs=[
            pl.BlockSpec((1, D), lambda l, j: (0, 0)),
            pl.BlockSpec((N_LAT_SEQ, D), lambda l, j: (0, 0)),
            pl.BlockSpec((1, D, tn), lambda l, j: (l, 0, j)),
            pl.BlockSpec((depth, tn), lambda l, j: (0, j)),
        ],
        out_specs=pl.BlockSpec((1, 8, tn), lambda l, j: (l, 0, j)),
        out_shape=jax.ShapeDtypeStruct((depth, 8, 6 * D), F32),
        compiler_params=_params("arbitrary", "arbitrary"),
        name="adaln",
    )(c_ctx.reshape(1, D), c, ada_w, ada_b)


def _rope_tables():
    half = QK_ROPE // 2
    nf = half // 2
    pos = np.arange(LAT_LEN)
    inv = ROPE_THETA ** (-np.arange(nf, dtype=np.float64) / nf)
    k = np.arange(QK_ROPE)
    part, idx = k // half, k % half
    p = np.where(part[None, :] == 0, (pos // GRID_W)[:, None], (pos % GRID_W)[:, None])
    ang = p * inv[idx % nf][None, :]
    cos, sin = np.cos(ang), np.sin(ang)
    first = (idx < nf)[None, :]
    s1 = np.where(first, -sin, 0.0)
    s2 = np.where(first, 0.0, sin)

    def place(t, fill):
        tab = np.full((2 * LAT_LEN, HEAD_PAD), fill, np.float32)
        tab[:LAT_LEN, QK_NOPE:QK_HEAD] = t
        return jnp.asarray(tab)

    return place(cos, 1.0), place(s1, 0.0), place(s2, 0.0)


def _rope(blk, cos, s1, s2):
    return blk * cos + pltpu.roll(blk, 8, 1) * s2 + pltpu.roll(blk, HEAD_PAD - 8, 1) * s1


def _head_norm(blk, g):
    ms = jnp.sum(blk * blk, axis=-1, keepdims=True) * (1.0 / QK_HEAD)
    return blk * lax.rsqrt(ms + EPS) * g


def _even_proj_kernel(xp_ref, xs_ref, mod_ref, nm_ref, win_ref, qan_ref, wqb_ref, kvan_ref,
                      qn_ref, wsc_ref, cos_ref, s1_ref, s2_ref,
                      q_ref, ckv_ref, kpe_ref, sc_ref, sckv_ref, skpe_ref, wt_ref):
    i = pl.program_id(0)
    n_a = Q_LORA + KV_LORA + QK_ROPE

    @pl.when(i == 0)
    def _():
        wt_ref[...] = win_ref[...].astype(BF16)

    x = jnp.where(i < CTX_TILES, xp_ref[...], xs_ref[...])
    m = _mod_row(mod_ref, i * TM)
    h = _rms(x, nm_ref[0:1, :]) * (1.0 + m[:, D:2 * D]) + m[:, 0:D]
    hb = h.astype(BF16)

    za = _dot_nt(hb, wt_ref[0:512, :])
    ckv = _rms(za[:, Q_LORA:Q_LORA + KV_LORA], kvan_ref[...])
    lane = lax.broadcasted_iota(jnp.int32, (1, HEAD_PAD), 1)
    kpe = jnp.where(lane < QK_ROPE, za[:, Q_LORA + KV_LORA:], 0.0)
    ckv_ref[...] = ckv
    kpe_ref[...] = kpe

    @pl.when(i < CTX_TILES)
    def _():
        sckv_ref[...] = ckv
        skpe_ref[...] = kpe[:, :QK_ROPE]

    qa = _rms(za[:, :Q_LORA], qan_ref[...]).astype(BF16)
    cos, s1, s2 = cos_ref[...], s1_ref[...], s2_ref[...]
    qn = qn_ref[...]
    scale = QK_HEAD ** -0.5
    for hp in range(HEADS // 2):
        qq = _dot(qa, wqb_ref[:, hp * 256:(hp + 1) * 256])
        for j in range(2):
            blk = _head_norm(qq[:, j * HEAD_PAD:(j + 1) * HEAD_PAD], qn)
            blk = _rope(blk, cos, s1, s2) * scale
            h0 = (2 * hp + j) * HEAD_PAD
            q_ref[:, h0:h0 + HEAD_PAD] = blk.astype(BF16)

    gb = _dot_nt(hb, wt_ref[n_a:n_a + SC_W, :])
    v = _dot_nt(hb, wt_ref[n_a + SC_W:n_a + 2 * SC_W, :]) * _dot_nt(hb, wt_ref[n_a + 2 * SC_W:n_a + 3 * SC_W, :])
    seq = jnp.where(i < CTX_TILES, CTX_LEN, LAT_LEN)
    r = lax.broadcasted_iota(jnp.int32, (TM, 1), 0) & (seq - 1)
    vp = jnp.where(r == 0, 0.0, pltpu.roll(v, 1, 0))
    vn = jnp.where(r == seq - 1, 0.0, pltpu.roll(v, TM - 1, 0))
    w = wsc_ref[0]
    y = w[0:1] * vp + w[1:2] * v + w[2:3] * vn
    sc_ref[...] = (gb * y).astype(BF16)


def _even_proj(xp, xs, mods, norm_mix, w_in, q_a_norm, wqb, kv_a_norm, qn, w_sc, tabs):
    full = lambda shape: pl.BlockSpec(shape, lambda i: (0,) * len(shape))
    tab = pl.BlockSpec((TM, HEAD_PAD), lambda i: (jnp.where(i < CTX_TILES, 1, 0), 0))
    row = lambda n: pl.BlockSpec((TM, n), lambda i: (i, 0))
    ctx_row = lambda n: pl.BlockSpec((TM, n), lambda i: (jnp.minimum(i, CTX_TILES - 1), 0))
    return pl.pallas_call(
        _even_proj_kernel,
        grid=(N_TILES,),
        in_specs=[
            ctx_row(D),
            pl.BlockSpec((TM, D), lambda i: (jnp.maximum(i - CTX_TILES, 0), 0)),
            _mod_spec(0, 1),
            full((2, D)),
            pl.BlockSpec((IN0_W, D), lambda i: (0, 0), pipeline_mode=pl.Buffered(1)),
            full((1, Q_LORA)),
            full((Q_LORA, HEADS * HEAD_PAD)), full((1, KV_LORA)), full((1, HEAD_PAD)),
            full((1, 3, SC_W)), tab, tab, tab,
        ],
        out_specs=[row(HEADS * HEAD_PAD), row(KV_LORA), row(HEAD_PAD), row(SC_W),
                   ctx_row(KV_LORA), ctx_row(QK_ROPE)],
        out_shape=[
            jax.ShapeDtypeStruct((T, HEADS * HEAD_PAD), BF16),
            jax.ShapeDtypeStruct((T, KV_LORA), F32),
            jax.ShapeDtypeStruct((T, HEAD_PAD), F32),
            jax.ShapeDtypeStruct((T, SC_W), BF16),
            jax.ShapeDtypeStruct((T_CTX, KV_LORA), F32),
            jax.ShapeDtypeStruct((T_CTX, QK_ROPE), F32),
        ],
        scratch_shapes=[pltpu.VMEM((IN0_W, D), BF16)],
        compiler_params=_params("arbitrary"),
        name="even_proj",
    )(xp, xs, mods, norm_mix, w_in, q_a_norm, wqb, kv_a_norm, qn, w_sc, *tabs)


def _kv_proj_kernel(ckv_ref, kpe_ref, wkvb_ref, kn_ref, cos_ref, s1_ref, s2_ref, k_ref, kv_ref):
    kv = _dot(ckv_ref[...].astype(BF16), wkvb_ref[...])
    kv_ref[...] = kv.astype(BF16)
    kpe = pltpu.roll(kpe_ref[...], QK_NOPE, 1)
    lane = lax.broadcasted_iota(jnp.int32, (1, HEAD_PAD), 1)
    kn = kn_ref[...]
    pe_sq = jnp.sum(kpe * kpe, axis=-1, keepdims=True)
    pe = _rope(kpe * kn, cos_ref[...], s1_ref[...], s2_ref[...])
    for h in range(HEADS):
        blk = kv[:, h * HEAD_PAD:(h + 1) * HEAD_PAD]
        nope = jnp.where(lane < QK_NOPE, blk, 0.0)
        ms = (jnp.sum(nope * nope, axis=-1, keepdims=True) + pe_sq) * (1.0 / QK_HEAD)
        k = jnp.where(lane < QK_NOPE, blk * kn, pe) * lax.rsqrt(ms + EPS)
        k_ref[:, h * HEAD_PAD:(h + 1) * HEAD_PAD] = k.astype(BF16)


def _kv_proj(ckv, kpe, wkvb, kn, tabs, tab_index, name):
    n = ckv.shape[0]
    full = lambda shape: pl.BlockSpec(shape, lambda i: (0,) * len(shape))
    tab = pl.BlockSpec((TKV, HEAD_PAD), lambda i: (tab_index(i), 0))
    row = lambda w: pl.BlockSpec((TKV, w), lambda i: (i, 0))
    return pl.pallas_call(
        _kv_proj_kernel,
        grid=(n // TKV,),
        in_specs=[row(KV_LORA), row(HEAD_PAD), full((KV_LORA, HEADS * HEAD_PAD)), full((1, HEAD_PAD)),
                  tab, tab, tab],
        out_specs=[row(HEADS * HEAD_PAD), row(HEADS * HEAD_PAD)],
        out_shape=[jax.ShapeDtypeStruct((n, HEADS * HEAD_PAD), BF16)] * 2,
        compiler_params=_params("arbitrary"),
        name=name,
    )(ckv, kpe, wkvb, kn, *tabs)


def _pair_out(o0, o1):
    lane = lax.broadcasted_iota(jnp.int32, (1, HEAD_PAD), 1)
    return jnp.where(lane < V_HEAD, pltpu.roll(o0, V_HEAD, 1), o1).astype(BF16)


CTX_SEQS = 2


def _attn_ctx_kernel(q_ref, k_ref, kv_ref, o_ref):
    for b in range(CTX_SEQS):
        rows = slice(b * CTX_LEN, (b + 1) * CTX_LEN)
        for hp in range(HEADS // 2):
            outs = []
            for j in range(2):
                lanes = slice((2 * hp + j) * HEAD_PAD, (2 * hp + j + 1) * HEAD_PAD)
                s = _dot_nt(q_ref[rows, lanes], k_ref[rows, lanes])
                p = jnp.exp(s - jnp.max(s, axis=-1, keepdims=True))
                l = jnp.sum(p, axis=-1, keepdims=True)
                outs.append(_dot(p.astype(BF16), kv_ref[rows, lanes]) / l)
            o_ref[rows, hp * HEAD_PAD:(hp + 1) * HEAD_PAD] = _pair_out(*outs)


def _attn_ctx(q, k, kv):
    blk = pl.BlockSpec((CTX_SEQS * CTX_LEN, HEADS * HEAD_PAD), lambda b: (b, 0))
    return pl.pallas_call(
        _attn_ctx_kernel,
        grid=(N_CTX_SEQ // CTX_SEQS,),
        in_specs=[blk, blk, blk],
        out_specs=pl.BlockSpec((CTX_SEQS * CTX_LEN, HEADS * V_HEAD), lambda b: (b, 0)),
        out_shape=jax.ShapeDtypeStruct((T_CTX, HEADS * V_HEAD), BF16),
        compiler_params=_params("arbitrary"),
        name="attn_ctx",
    )(q, k, kv)


LAT_HEADS = 4


def _attn_lat_kernel(q_ref, kc_ref, kvc_ref, kl_ref, kvl_ref, o_ref):
    for hp in range(LAT_HEADS // 2):
        outs = []
        for j in range(2):
            h0 = (2 * hp + j) * HEAD_PAD
            lanes = slice(h0, h0 + HEAD_PAD)
            q = q_ref[:, lanes]
            sc = _dot_nt(q, kc_ref[:, lanes])
            sl = _dot_nt(q, kl_ref[:, lanes])
            m = jnp.maximum(jnp.max(sc, axis=-1, keepdims=True), jnp.max(sl, axis=-1, keepdims=True))
            pc, pl_ = jnp.exp(sc - m), jnp.exp(sl - m)
            l = jnp.sum(pc, axis=-1, keepdims=True) + jnp.sum(pl_, axis=-1, keepdims=True)
            o = _dot(pc.astype(BF16), kvc_ref[:, lanes]) + _dot(pl_.astype(BF16), kvl_ref[:, lanes])
            outs.append(o / l)
        o_ref[:, hp * HEAD_PAD:(hp + 1) * HEAD_PAD] = _pair_out(*outs)


def _attn_lat(q, kc, kvc, k, kv):
    nq = LAT_LEN // TQ
    q0 = T_CTX // TQ
    kl0 = T_CTX // LAT_LEN
    width = LAT_HEADS * HEAD_PAD
    lat = pl.BlockSpec((LAT_LEN, width), lambda b, hg, t: (kl0 + b, hg))
    ctx = pl.BlockSpec((PAST, width), lambda b, hg, t: (b, hg))
    return pl.pallas_call(
        _attn_lat_kernel,
        grid=(N_LAT_SEQ, HEADS // LAT_HEADS, nq),
        in_specs=[pl.BlockSpec((TQ, width), lambda b, hg, t: (q0 + b * nq + t, hg)), ctx, ctx, lat, lat],
        out_specs=pl.BlockSpec((TQ, LAT_HEADS * V_HEAD), lambda b, hg, t: (b * nq + t, hg)),
        out_shape=jax.ShapeDtypeStruct((T_LAT, HEADS * V_HEAD), BF16),
        compiler_params=_params("arbitrary", "arbitrary", "arbitrary"),
        name="attn_lat",
    )(q, kc, kvc, k, kv)


FFN_NC = D_FF // FFN_FC
FFN_TM = 512
FFN_CTX_TILES = T_CTX // FFN_TM


def _ffn_kernel(oc_ref, ol_ref, sc_ref, xp_ref, xs_ref, wo_ref, nf_ref, wg_ref, wu_ref, wd_ref,
                mod0_ref, mod1_ref, nm_ref, x2_ref, h3_ref, wg_all, wu_all, wd_all):
    t = pl.program_id(0)

    @pl.when(t < FFN_NC)
    def _stage():
        wg_all[t] = wg_ref[...].astype(BF16)
        wu_all[t] = wu_ref[...].astype(BF16)
        wd_all[pl.ds(pl.multiple_of(t * FFN_FC, FFN_FC), FFN_FC), :] = wd_ref[...].astype(BF16)

    @pl.when(t >= FFN_NC - 1)
    def _tile():
        i = t - (FFN_NC - 1)
        ctx = i < FFN_CTX_TILES
        m0, m1 = _mod_row(mod0_ref, i * FFN_TM), _mod_row(mod1_ref, i * FFN_TM)
        attn = jnp.where(ctx, oc_ref[...], ol_ref[...])
        x = jnp.where(ctx, xp_ref[...], xs_ref[...])
        x1 = x + m0[:, 2 * D:3 * D] * _dot(jnp.concatenate([attn, sc_ref[...]], axis=1), wo_ref[...])
        h = (_rms(x1, nf_ref[0:1, :]) * (1.0 + m0[:, 4 * D:5 * D]) + m0[:, 3 * D:4 * D]).astype(BF16)
        act = []
        for c in range(FFN_NC):
            act.append((_silu(_dot(h, wg_all[c])) * _dot(h, wu_all[c])).astype(BF16))
        f = _dot(jnp.concatenate(act, axis=1), wd_all[...])
        x2 = x1 + m0[:, 5 * D:6 * D] * f
        x2_ref[...] = x2
        h3_ref[...] = (_rms(x2, nm_ref[1:2, :]) * (1.0 + m1[:, D:2 * D]) + m1[:, 0:D]).astype(BF16)


def _ffn(oc, ol, sc, xp, xs, wo, norm_ffn, w_gu, w_down, mods, norm_mix):
    chunk = lambda t: jnp.minimum(t, FFN_NC - 1)
    tile = lambda t: jnp.maximum(t - (FFN_NC - 1), 0)
    full = lambda shape: pl.BlockSpec(shape, lambda t: (0,) * len(shape))
    row = lambda n: pl.BlockSpec((FFN_TM, n), lambda t: (tile(t), 0))
    first = lambda n: pl.BlockSpec((FFN_TM, n), lambda t: (jnp.minimum(tile(t), FFN_CTX_TILES - 1), 0))
    second = lambda n: pl.BlockSpec((FFN_TM, n), lambda t: (jnp.maximum(tile(t) - FFN_CTX_TILES, 0), 0))
    return pl.pallas_call(
        _ffn_kernel,
        grid=(FFN_NC - 1 + T // FFN_TM,),
        in_specs=[first(HEADS * V_HEAD), second(HEADS * V_HEAD), row(SC_W), first(D), second(D),
                  full((HEADS * V_HEAD + SC_W, D)), full((2, D)),
                  pl.BlockSpec((D, FFN_FC), lambda t: (0, chunk(t))),
                  pl.BlockSpec((D, FFN_FC), lambda t: (0, FFN_NC + chunk(t))),
                  pl.BlockSpec((FFN_FC, D), lambda t: (chunk(t), 0)),
                  _mod_spec(0, 1), _mod_spec(1, 1), full((2, D))],
        out_specs=[row(D), row(D)],
        out_shape=[jax.ShapeDtypeStruct((T, D), F32), jax.ShapeDtypeStruct((T, D), BF16)],
        scratch_shapes=[pltpu.VMEM((FFN_NC, D, FFN_FC), BF16), pltpu.VMEM((FFN_NC, D, FFN_FC), BF16),
                        pltpu.VMEM((D_FF, D), BF16)],
        compiler_params=_params("arbitrary"),
        name="ffn_dense",
    )(oc, ol, sc, xp, xs, wo, norm_ffn, w_gu, w_gu, w_down, mods, mods, norm_mix)


CONF_CB = 256
CONF_SEG = 256
CONF_HALO = 16
CONF_SEGP = CONF_SEG + 2 * CONF_HALO
CONF_PIECE = 64


def _conf_kernel(h_ref, x2_ref, w1_ref, b1_ref, wdw_ref, bdw_ref, lng_ref, lnb_ref, w2_ref, b2_ref,
                 mod_ref, nf_ref, rt_ref, x3_ref, h4_ref, lg_ref, pad_ref, conv_ref):
    i = pl.program_id(0)
    nseg = TM // CONF_SEG
    h = h_ref[...]
    joined = jnp.where(i < CTX_TILES, 0.0, 1.0)
    zeros_halo = jnp.zeros((CONF_HALO, CONF_CB), F32)
    for cb in range(D // CONF_CB):
        c0 = cb * CONF_CB
        a = _dot(h, w1_ref[:, c0:c0 + CONF_CB]) + b1_ref[:, c0:c0 + CONF_CB]
        g = _dot(h, w1_ref[:, D + c0:D + c0 + CONF_CB]) + b1_ref[:, D + c0:D + c0 + CONF_CB]
        u = a * jax.nn.sigmoid(g)
        for s in range(nseg):
            base = s * CONF_SEGP
            top = u[s * CONF_SEG - CONF_HALO:s * CONF_SEG] * joined if s > 0 else zeros_halo
            bot = (u[(s + 1) * CONF_SEG:(s + 1) * CONF_SEG + CONF_HALO] * joined
                   if s < nseg - 1 else zeros_halo)
            pad_ref[0, base:base + CONF_HALO, :] = top
            pad_ref[0, base + CONF_HALO:base + CONF_HALO + CONF_SEG, :] = u[s * CONF_SEG:(s + 1) * CONF_SEG]
            pad_ref[0, base + CONF_HALO + CONF_SEG:base + CONF_SEGP, :] = bot

        p0 = pad_ref[0]
        rows = nseg * CONF_SEGP
        for b in range(1, 8):
            pad_ref[b] = pltpu.roll(p0, rows - b, 0)

        def piece(t, carry):
            s = t // (CONF_SEG // CONF_PIECE)
            q0 = (t % (CONF_SEG // CONF_PIECE)) * CONF_PIECE
            src = pl.multiple_of(s * CONF_SEGP + q0, 8)
            acc = jnp.zeros((CONF_PIECE, CONF_CB), F32)
            for j in range(CONF_K):
                hi, lo = (j + 1) // 8, (j + 1) % 8
                acc = acc + wdw_ref[0, j:j + 1, c0:c0 + CONF_CB] * pad_ref[lo, pl.ds(src + 8 * hi, CONF_PIECE), :]
            dst = pl.multiple_of(s * CONF_SEG + q0, 8)
            conv_ref[pl.ds(dst, CONF_PIECE), c0:c0 + CONF_CB] = acc + bdw_ref[:, c0:c0 + CONF_CB]
            return carry

        lax.fori_loop(0, TM // CONF_PIECE, piece, 0)

    m = _mod_row(mod_ref, i * TM)
    half = TM // 2
    for r0 in (0, half):
        rows = slice(r0, r0 + half)
        y = conv_ref[rows, :]
        mu = jnp.mean(y, axis=-1, keepdims=True)
        yc = y - mu
        var = jnp.mean(yc * yc, axis=-1, keepdims=True)
        y = _silu(yc * lax.rsqrt(var + EPS) * lng_ref[...] + lnb_ref[...])
        out = _dot(y.astype(BF16), w2_ref[...]) + b2_ref[...]
        x3 = x2_ref[rows, :] + m[:, 2 * D:3 * D] * out
        x3_ref[rows, :] = x3
        h4 = _rms(x3, nf_ref[1:2, :]) * (1.0 + m[:, 4 * D:5 * D]) + m[:, 3 * D:4 * D]
        h4_ref[rows, :] = h4.astype(BF16)
        lg_ref[:, rows] = lax.dot_general(rt_ref[...], h4, (((1,), (1,)), ((), ())),
                                          precision=lax.Precision.HIGHEST, preferred_element_type=F32)


def _conf(h3, x2, w1, b1, wdw, bdw, lng, lnb, w2, b2, mods, norm_ffn1, router_t):
    full = lambda shape: pl.BlockSpec(shape, lambda i: (0,) * len(shape))
    row = lambda n: pl.BlockSpec((TM, n), lambda i: (i, 0))
    return pl.pallas_call(
        _conf_kernel,
        grid=(N_TILES,),
        in_specs=[row(D), row(D), full((D, 2 * D)), full((1, 2 * D)), full((1, CONF_K, D)), full((1, D)),
                  full((1, D)), full((1, D)), full((D, D)), full((1, D)),
                  _mod_spec(1, 1), full((2, D)), full((N_EXP, D))],
        out_specs=[row(D), row(D), pl.BlockSpec((N_EXP, TM), lambda i: (0, i))],
        out_shape=[jax.ShapeDtypeStruct((T, D), F32), jax.ShapeDtypeStruct((T, D), BF16),
                   jax.ShapeDtypeStruct((N_EXP, T), F32)],
        scratch_shapes=[pltpu.VMEM((8, (TM // CONF_SEG) * CONF_SEGP, CONF_CB), F32),
                        pltpu.VMEM((TM, D), F32)],
        compiler_params=_params("arbitrary"),
        name="conformer_conv",
    )(h3, x2, w1, b1, wdw, bdw, lng, lnb, w2, b2, mods, norm_ffn1, router_t)


TB = 256
N_TB = T // TB
SUB = 128
SUBS = 24
SM = SUBS * SUB
N_SUB_MAX = 2 * T // SUB + N_EXP
N_SUP_MAX = (N_SUB_MAX + N_EXP * (SUBS - 1)) // SUBS
YS_ROWS = (N_SUB_MAX + 4) * SUB
WIN_ALIGN = 16
WIN_HALF = TB // 2 + WIN_ALIGN
FIRST_STRIDE = 32
UNIT_STRIDE = 64
GATHER_BLOCKS = 4


def _route_kernel(lg_ref, g_ref, rank_ref, first_ref):
    lg = lg_ref[...]
    idx = lax.broadcasted_iota(jnp.int32, lg.shape, 0).astype(F32)
    none = float(N_EXP)
    m1 = jnp.max(lg, axis=0, keepdims=True)
    i1 = jnp.min(jnp.where(lg == m1, idx, none), axis=0, keepdims=True)
    rest = jnp.where(idx == i1, -jnp.inf, lg)
    m2 = jnp.max(rest, axis=0, keepdims=True)
    i2 = jnp.min(jnp.where(rest == m2, idx, none), axis=0, keepdims=True)
    e = jnp.exp(m2 - m1)
    w1 = 1.0 / (1.0 + e)
    w2 = e / (1.0 + e)
    g_ref[...] = jnp.where(idx == i1, w1, 0.0) + jnp.where(idx == i2, w2, 0.0)

    mask = jnp.where(idx == i1, 1.0, 0.0) + jnp.where(idx == i2, 1.0, 0.0)
    before = (lax.broadcasted_iota(jnp.int32, (TB, TB), 0) < lax.broadcasted_iota(jnp.int32, (TB, TB), 1))
    before = jnp.where(before, 1.0, 0.0).astype(BF16)
    lane = lax.broadcasted_iota(jnp.int32, (N_EXP, 128), 1)
    carry = jnp.zeros((N_EXP, 1), F32)
    first = jnp.zeros((N_EXP, 128), F32)
    for b in range(N_TB):
        mb = mask[:, b * TB:(b + 1) * TB]
        local = _dot(mb.astype(BF16), before)
        rank_ref[:, b * TB:(b + 1) * TB] = jnp.where(mb > 0.0, local + carry, -1.0)
        first = jnp.where(lane == b, carry, first)
        carry = carry + jnp.sum(mb, axis=1, keepdims=True)
    first_ref[...] = jnp.where(lane == N_TB, carry, first)


def _route(logits_t):
    return pl.pallas_call(
        _route_kernel,
        out_shape=[jax.ShapeDtypeStruct((N_EXP, T), F32), jax.ShapeDtypeStruct((N_EXP, T), F32),
                   jax.ShapeDtypeStruct((N_EXP, 128), F32)],
        compiler_params=pltpu.CompilerParams(vmem_limit_bytes=VMEM_LIMIT),
        name="route",
    )(logits_t)


def _moe_plan(first):
    first = first[:, :FIRST_STRIDE].astype(jnp.int32)
    cnt = first[:, N_TB]
    nt = (cnt + (SUB - 1)) // SUB
    off_end = jnp.cumsum(nt)
    off = off_end - nt
    nsub = off_end[-1]
    nsup = (nt + (SUBS - 1)) // SUBS
    sup_end = jnp.cumsum(nsup)
    sup_off = sup_end - nsup
    s = jnp.minimum(jnp.arange(N_SUP_MAX), sup_end[-1] - 1)
    valid = jnp.arange(N_SUP_MAX) < sup_end[-1]
    se = jnp.sum(s[:, None] >= sup_end[None, :], axis=1)
    sk0 = (s - sup_off[se]) * SUBS
    sns = jnp.where(valid, jnp.clip(nt[se] - sk0, 0, SUBS), 0)
    sj0 = off[se] + sk0
    base = (jnp.arange(UNIT_STRIDE) * SUB)[None, :, None]
    blo = jnp.minimum(jnp.sum(first[:, None, 1:N_TB + 1] <= base, axis=2), N_TB - 1)
    end = jnp.minimum(base + SUB, cnt[:, None, None])
    bhi = jnp.maximum(jnp.sum(first[:, None, :N_TB] < end, axis=2) - 1, blo)
    ng = (bhi - blo) // GATHER_BLOCKS + 1
    start = SUB * off[:, None] + first[:, :N_TB]
    lead = start & (WIN_ALIGN - 1)
    wina = start - lead
    rel = first[:, :N_TB] - lead
    need = lead + (first[:, 1:N_TB + 1] - first[:, :N_TB]) > WIN_HALF
    winb = lax.cummax(jnp.where(need, wina + WIN_HALF, 0), axis=1)
    wide = jnp.any(need, axis=0)
    i32 = lambda a: a.astype(jnp.int32)
    return dict(se=i32(se), sk0=i32(sk0), sns=i32(sns), sj0=i32(sj0), nsub=i32(nsub).reshape(1),
                first=i32(first.reshape(-1)), blo=i32(blo.reshape(-1)), ng=i32(ng.reshape(-1)),
                wina=i32(wina.T.reshape(-1)), winb=i32(winb.T.reshape(-1)), rel=i32(rel.T.reshape(-1)),
                wide=i32(wide))


def _moe_gmm_kernel(se_ref, sk0_ref, sns_ref, sj0_ref, nsub_ref, first_ref, blo_ref, ng_ref,
                    x_ref, rank_ref, gate_ref, wg_ref, wu_ref, wd_ref, ys_ref,
                    xs_ref, gs_ref, yacc_ref, acc_ref, gacc_ref, wgb_ref, wub_ref, wdb_ref, sem):
    s, c = pl.program_id(0), pl.program_id(1)
    nc = pl.num_programs(1)
    e, k0, ns = se_ref[s], sk0_ref[s], sns_ref[s]

    def sub_rows(k):
        return pl.ds(pl.multiple_of(k * SUB, SUB), SUB)

    def out_copy(k, row0):
        dst = ys_ref.at[pl.ds(pl.multiple_of(row0 + k * SUB, SUB), SUB)]
        return pltpu.make_async_copy(xs_ref.at[sub_rows(k)], dst, sem.at[k])

    @pl.when((ns > 0) & (c == 0))
    def _gather():
        def sub(k, carry):
            base = (k0 + k) * SUB
            slot = (lax.broadcasted_iota(jnp.int32, (SUB, 1), 0) + base).astype(F32)
            acc_ref[...] = jnp.zeros_like(acc_ref)
            gacc_ref[...] = jnp.zeros_like(gacc_ref)
            blo = blo_ref[e * UNIT_STRIDE + k0 + k]

            def group(g, carry2):
                b0 = blo + g * GATHER_BLOCKS
                t0 = pl.multiple_of(jnp.minimum(b0, N_TB - GATHER_BLOCKS) * TB, TB)
                lo = first_ref[e * FIRST_STRIDE + b0].astype(F32)
                cols = pl.ds(t0, GATHER_BLOCKS * TB)
                hit = rank_ref[pl.ds(e, 1), cols] == jnp.where(slot >= lo, slot, -2.0)
                onehot = jnp.where(hit, 1.0, 0.0).astype(BF16)
                acc_ref[...] += _dot(onehot, x_ref[cols, :])
                gacc_ref[...] += jnp.sum(jnp.where(hit, gate_ref[pl.ds(e, 1), cols], 0.0),
                                         axis=-1, keepdims=True)
                return carry2

            lax.fori_loop(0, ng_ref[e * UNIT_STRIDE + k0 + k], group, 0)
            xs_ref[sub_rows(k), :] = acc_ref[...].astype(BF16)
            gs_ref[sub_rows(k), :] = gacc_ref[...]
            yacc_ref[sub_rows(k), :] = jnp.zeros((SUB, D), F32)
            return carry

        lax.fori_loop(0, ns, sub, 0)

    @pl.when(ns > 0)
    def _compute():
        row0 = sj0_ref[s] * SUB

        def swiglu(rows, wg, wu, wd):
            x = xs_ref[rows, :]
            g = _dot(x, wg)
            u = _dot(x, wu)
            yacc_ref[rows, :] += _dot((_silu(g) * u).astype(BF16), wd)

        def finish(first_sub, n):
            @pl.when(c == nc - 1)
            def _():
                for k in range(n):
                    rows = sub_rows(first_sub + k)
                    xs_ref[rows, :] = (yacc_ref[rows, :] * gs_ref[rows, :]).astype(BF16)
                    out_copy(first_sub + k, row0).start()

        def first_chain(rows):
            wg, wu, wd = wg_ref[0].astype(BF16), wu_ref[0].astype(BF16), wd_ref[0].astype(BF16)
            wgb_ref[...] = wg
            wub_ref[...] = wu
            wdb_ref[...] = wd
            swiglu(rows, wg, wu, wd)

        @pl.when(ns >= 4)
        def _():
            first_chain(pl.ds(0, 4 * SUB))
            finish(0, 4)

        @pl.when(ns < 4)
        def _():
            first_chain(pl.ds(0, SUB))
            finish(0, 1)

        done = jnp.where(ns >= 4, 4, 1)
        rest = ns - done

        def chain(first_sub, n):
            rows = pl.ds(pl.multiple_of(first_sub * SUB, SUB), n * SUB)
            swiglu(rows, wgb_ref[...], wub_ref[...], wdb_ref[...])
            finish(first_sub, n)

        def eight(k, carry):
            chain(done + 8 * k, 8)
            return carry

        lax.fori_loop(0, rest >> 3, eight, 0)
        done8 = done + (rest & ~7)
        for n in (4, 2, 1):
            @pl.when((rest & n) != 0)
            def _(n=n):
                chain(done8 + (rest & (7 & ~(2 * n - 1))), n)

    @pl.when((ns > 0) & (c == nc - 1))
    def _store_done():
        row0 = sj0_ref[s] * SUB

        def done(k, carry):
            out_copy(k, row0).wait()
            return carry

        lax.fori_loop(0, ns, done, 0)

    @pl.when((s == pl.num_programs(0) - 1) & (c == nc - 1))
    def _zero_tail():
        xs_ref[0:SUB, :] = jnp.zeros((SUB, D), BF16)
        nsub = nsub_ref[0]

        def fill(k, carry):
            cp = out_copy(0, (nsub + k) * SUB)
            cp.start()
            cp.wait()
            return carry

        lax.fori_loop(0, YS_ROWS // SUB - nsub, fill, 0)


def _moe_gmm(plan, h4, rank, gates, w_gu, w_down):
    nc = D_FFE // MOE_FC

    def chunk(s, c, sns):
        return jnp.where(sns[s] > 0, c, nc - 1)

    return pl.pallas_call(
        _moe_gmm_kernel,
        grid_spec=pltpu.PrefetchScalarGridSpec(
            num_scalar_prefetch=8,
            grid=(N_SUP_MAX, nc),
            in_specs=[
                pl.BlockSpec((T, D), lambda s, c, *_: (0, 0), pipeline_mode=pl.Buffered(1)),
                pl.BlockSpec((N_EXP, T), lambda s, c, *_: (0, 0)),
                pl.BlockSpec((N_EXP, T), lambda s, c, *_: (0, 0)),
                pl.BlockSpec((1, D, MOE_FC), lambda s, c, se, sk0, sns, *_: (se[s], 0, chunk(s, c, sns))),
                pl.BlockSpec((1, D, MOE_FC), lambda s, c, se, sk0, sns, *_: (se[s], 0, nc + chunk(s, c, sns))),
                pl.BlockSpec((1, MOE_FC, D), lambda s, c, se, sk0, sns, *_: (se[s], chunk(s, c, sns), 0)),
            ],
            out_specs=pl.BlockSpec(memory_space=pl.ANY),
            scratch_shapes=[
                pltpu.VMEM((SM, D), BF16), pltpu.VMEM((SM, 1), F32), pltpu.VMEM((SM, D), F32),
                pltpu.VMEM((SUB, D), F32), pltpu.VMEM((SUB, 1), F32),
                pltpu.VMEM((D, MOE_FC), BF16), pltpu.VMEM((D, MOE_FC), BF16), pltpu.VMEM((MOE_FC, D), BF16),
                pltpu.SemaphoreType.DMA((SUBS,)),
            ],
        ),
        out_shape=jax.ShapeDtypeStruct((YS_ROWS, D), BF16),
        compiler_params=_params("arbitrary", "arbitrary"),
        name="moe_gmm",
    )(plan["se"], plan["sk0"], plan["sns"], plan["sj0"], plan["nsub"], plan["first"], plan["blo"], plan["ng"],
      h4, rank, gates, w_gu, w_gu, w_down)


def _moe_combine_kernel(wina_ref, winb_ref, rel_ref, wide_ref, *refs):
    ya, yb = refs[:N_EXP], refs[N_EXP:2 * N_EXP]
    rank_ref, x3_ref, mod_ref, yp_ref, ys_ref, ycat_ref, acc_ref = refs[2 * N_EXP:]
    b = pl.program_id(0)
    row = lax.broadcasted_iota(jnp.int32, (WIN_HALF, 1), 0)

    def onehot(e, first_row):
        slot = (row + (rel_ref[b * N_EXP + e] + first_row)).astype(F32)
        return jnp.where(rank_ref[e:e + 1, :] == slot, 1.0, 0.0).astype(BF16)

    def gather(y_refs, first_row, base):
        pieces = []
        for e in range(N_EXP):
            ycat_ref[base + e * WIN_HALF:base + (e + 1) * WIN_HALF, :] = y_refs[e][...]
            pieces.append(onehot(e, first_row))
        return pieces

    def combine(pieces, rows):
        return lax.dot_general(jnp.concatenate(pieces, axis=0), ycat_ref[0:rows, :], (((0,), (0,)), ((), ())),
                               preferred_element_type=F32)

    @pl.when(wide_ref[b] == 0)
    def _():
        acc_ref[...] = combine(gather(ya, 0, 0), N_EXP * WIN_HALF)

    @pl.when(wide_ref[b] != 0)
    def _():
        pieces = gather(ya, 0, 0) + gather(yb, WIN_HALF, N_EXP * WIN_HALF)
        acc_ref[...] = combine(pieces, 2 * N_EXP * WIN_HALF)

    out = x3_ref[...] + _mod_row(mod_ref, b * TB)[:, 5 * D:6 * D] * acc_ref[...]

    @pl.when(b < T_CTX // TB)
    def _():
        yp_ref[...] = out

    @pl.when(b >= T_CTX // TB)
    def _():
        ys_ref[...] = out


def _moe_combine(plan, ysorted, rank, x3, mods):
    ctx_blocks = T_CTX // TB

    def window(e, second):
        def index(b, wina, winb, rel, wide):
            start = (winb if second else wina)[b * N_EXP + e]
            return pl.multiple_of(start, WIN_ALIGN), 0
        return pl.BlockSpec((pl.Element(WIN_HALF), pl.Element(D)), index)

    return pl.pallas_call(
        _moe_combine_kernel,
        grid_spec=pltpu.PrefetchScalarGridSpec(
            num_scalar_prefetch=4,
            grid=(N_TB,),
            in_specs=[window(e, False) for e in range(N_EXP)] + [window(e, True) for e in range(N_EXP)] + [
                pl.BlockSpec((N_EXP, TB), lambda b, *_: (0, b)),
                pl.BlockSpec((TB, D), lambda b, *_: (b, 0)),
                _mod_spec(1, 1),
            ],
            out_specs=[pl.BlockSpec((TB, D), lambda b, *_: (jnp.minimum(b, ctx_blocks - 1), 0)),
                       pl.BlockSpec((TB, D), lambda b, *_: (jnp.maximum(b - ctx_blocks, 0), 0))],
            scratch_shapes=[pltpu.VMEM((2 * N_EXP * WIN_HALF, D), BF16), pltpu.VMEM((TB, D), F32)],
        ),
        out_shape=[jax.ShapeDtypeStruct((T_CTX, D), F32), jax.ShapeDtypeStruct((T_LAT, D), F32)],
        compiler_params=_params("arbitrary"),
        name="moe_combine",
    )(plan["wina"], plan["winb"], plan["rel"], plan["wide"], *([ysorted] * (2 * N_EXP)), rank, x3, mods)


def _pad_heads(w, width):
    lead = w.shape[:-1]
    w = w.reshape(*lead, HEADS, width)
    w = jnp.pad(w, [(0, 0)] * len(lead) + [(0, 0), (0, HEAD_PAD - width)])
    return w.reshape(*lead, HEADS * HEAD_PAD)


def kernel(x_prompt, x_sample, cache_ckv, cache_kpe, c, c_ctx, ada_w, ada_b, norm_mix, norm_ffn, w_in, q_a_norm,
           w_qb, kv_a_norm, w_kvb, q_norm, k_norm, w_sc, w_o, ffn_gu, ffn_down, conv_pw1, conv_pw1_b, conv_dw,
           conv_dw_b, conv_ln_g, conv_ln_b, conv_pw2, conv_pw2_b, router, moe_gu, moe_down):
    xp = x_prompt.reshape(T_CTX, D)
    xs = x_sample.reshape(T_LAT, D)

    mods = _adaln(c_ctx, c, ada_w, ada_b)

    wqb = _pad_heads(w_qb[0], QK_HEAD).astype(BF16)
    wkvb = w_kvb[0].astype(BF16)
    qn = jnp.pad(q_norm[0], (0, HEAD_PAD - QK_HEAD)).reshape(1, HEAD_PAD)
    kn = jnp.pad(k_norm[0], (0, HEAD_PAD - QK_HEAD)).reshape(1, HEAD_PAD)
    tabs = _rope_tables()

    w_in_t = jnp.swapaxes(w_in[0], 0, 1)
    q, ckv, kpe, sc, state_ckv, state_kpe = _even_proj(xp, xs, mods, norm_mix, w_in_t, q_a_norm, wqb, kv_a_norm,
                                                       qn, w_sc, tabs)

    lat_tile0 = T_CTX // TKV
    ident = LAT_LEN // TKV
    k, kv = _kv_proj(ckv, kpe, wkvb, kn, tabs,
                     lambda i: jnp.where(i < lat_tile0, ident, (i - lat_tile0) % ident), "kv_proj")
    cache_kpe_p = jnp.pad(cache_kpe[:, 0].reshape(N_LAT_SEQ * PAST, QK_ROPE), ((0, 0), (0, HEAD_PAD - QK_ROPE)))
    kc, kvc = _kv_proj(cache_ckv[:, 0].reshape(N_LAT_SEQ * PAST, KV_LORA), cache_kpe_p, wkvb, kn, tabs,
                       lambda i: ident, "kv_proj_cache")

    oc = _attn_ctx(q, k, kv)
    ol = _attn_lat(q, kc, kvc, k, kv)
    x2, h3 = _ffn(oc, ol, sc, xp, xs, w_o[0].astype(BF16), norm_ffn, ffn_gu[0], ffn_down[0], mods, norm_mix)

    x3, h4, logits_t = _conf(h3, x2, conv_pw1[0].astype(BF16), conv_pw1_b, conv_dw, conv_dw_b, conv_ln_g,
                             conv_ln_b, conv_pw2[0].astype(BF16), conv_pw2_b, mods, norm_ffn, router[0].T)
    gates, rank, first = _route(logits_t)
    plan = _moe_plan(first)
    ysorted = _moe_gmm(plan, h4, rank, gates, moe_gu[0], moe_down[0])
    yp, ys = _moe_combine(plan, ysorted, rank, x3, mods)

    return (yp.reshape(N_CTX_SEQ, CTX_LEN, D), ys.reshape(N_LAT_SEQ, LAT_LEN, D),
            state_ckv.reshape(N_CTX_SEQ, 1, CTX_LEN, KV_LORA), state_kpe.reshape(N_CTX_SEQ, 1, CTX_LEN, QK_ROPE))
```

```python
import functools

import jax
import jax.numpy as jnp
import numpy as np
from jax import lax
from jax.experimental import pallas as pl
from jax.experimental.pallas import tpu as pltpu

F32 = jnp.float32
BF16 = jnp.bfloat16

D = 1024
N_CTX_SEQ, CTX_LEN = 16, 256
N_LAT_SEQ, LAT_LEN = 2, 1024
T_CTX = N_CTX_SEQ * CTX_LEN
T_LAT = N_LAT_SEQ * LAT_LEN
T = T_CTX + T_LAT
PAST = 256
GRID_W = 64
HEADS = 8
QK_NOPE, QK_ROPE, V_HEAD = 64, 32, 64
QK_HEAD = QK_NOPE + QK_ROPE
HEAD_PAD = 128
Q_LORA, KV_LORA = 256, 128
SC_W = 512
IN0_W = Q_LORA + KV_LORA + QK_ROPE + 3 * SC_W
CONF_K = 31
D_FF = 2816
N_EXP = 8
D_FFE = 3584
EPS = 1e-6
ROPE_THETA = 10000.0

TM = 1024
N_TILES = T // TM
CTX_TILES = T_CTX // TM
TKV = 512
TQ = 256
FFN_FC = 256
MOE_FC = 512
VMEM_LIMIT = 56 * 1024 * 1024


def _dot(a, b):
    return jnp.dot(a, b, preferred_element_type=F32)


def _dot_nt(a, b):
    return lax.dot_general(a, b, (((1,), (1,)), ((), ())), preferred_element_type=F32)


def _rms(x, g):
    return x * lax.rsqrt(jnp.mean(x * x, axis=-1, keepdims=True) + EPS) * g


def _silu(x):
    return x * jax.nn.sigmoid(x)


def _params(*sem):
    return pltpu.CompilerParams(dimension_semantics=sem, vmem_limit_bytes=VMEM_LIMIT)


def _mod_row(mod_ref, row0):
    cond = jnp.maximum(row0 - (T_CTX - LAT_LEN), 0) >> 10
    return mod_ref[0, pl.ds(cond, 1), :]


def _mod_spec(layer, ngrid):
    return pl.BlockSpec((1, 8, 6 * D), lambda *_: (layer, 0, 0))


def _adaln_kernel(cc_ref, c_ref, w_ref, b_ref, o_ref):
    l = pl.program_id(0)
    row = lax.broadcasted_iota(jnp.int32, (8, 1), 0)
    cond = jnp.where(row == 0, cc_ref[...], 0.0)
    for b in range(N_LAT_SEQ):
        cond = jnp.where(row == 1 + b, c_ref[b:b + 1, :], cond)
    o_ref[0] = _dot(_silu(cond).astype(BF16), w_ref[0].astype(BF16)) + b_ref[pl.ds(l, 1), :]


def _adaln(c_ctx, c, ada_w, ada_b):
    depth = ada_w.shape[0]
    tn = 2048
    return pl.pallas_call(
        _adaln_kernel,
        grid=(depth, 6 * D // tn),
        in_specs=[
            pl.BlockSpec((1, D), lambda l, j: (0, 0)),
            pl.BlockSpec((N_LAT_SEQ, D), lambda l, j: (0, 0)),
            pl.BlockSpec((1, D, tn), lambda l, j: (l, 0, j)),
            pl.BlockSpec((depth, tn), lambda l, j: (0, j)),
        ],
        out_specs=pl.BlockSpec((1, 8, tn), lambda l, j: (l, 0, j)),
        out_shape=jax.ShapeDtypeStruct((depth, 8, 6 * D), F32),
        compiler_params=_params("arbitrary", "arbitrary"),
        name="adaln",
    )(c_ctx.reshape(1, D), c, ada_w, ada_b)


def _rope_tables():
    half = QK_ROPE // 2
    nf = half // 2
    pos = np.arange(LAT_LEN)
    inv = ROPE_THETA ** (-np.arange(nf, dtype=np.float64) / nf)
    k = np.arange(QK_ROPE)
    part, idx = k // half, k % half
    p = np.where(part[None, :] == 0, (pos // GRID_W)[:, None], (pos % GRID_W)[:, None])
    ang = p * inv[idx % nf][None, :]
    cos, sin = np.cos(ang), np.sin(ang)
    first = (idx < nf)[None, :]
    s1 = np.where(first, -sin, 0.0)
    s2 = np.where(first, 0.0, sin)

    def place(t, fill):
        tab = np.full((2 * LAT_LEN, HEAD_PAD), fill, np.float32)
        tab[:LAT_LEN, QK_NOPE:QK_HEAD] = t
        return jnp.asarray(tab)

    return place(cos, 1.0), place(s1, 0.0), place(s2, 0.0)


def _rope(blk, cos, s1, s2):
    return blk * cos + pltpu.roll(blk, 8, 1) * s2 + pltpu.roll(blk, HEAD_PAD - 8, 1) * s1


def _head_norm(blk, g):
    ms = jnp.sum(blk * blk, axis=-1, keepdims=True) * (1.0 / QK_HEAD)
    return blk * lax.rsqrt(ms + EPS) * g


def _even_proj_kernel(xp_ref, xs_ref, mod_ref, nm_ref, win_ref, qan_ref, wqb_ref, kvan_ref,
                      qn_ref, wsc_ref, cos_ref, s1_ref, s2_ref,
                      q_ref, ckv_ref, kpe_ref, sc_ref, sckv_ref, skpe_ref, wt_ref):
    i = pl.program_id(0)
    n_a = Q_LORA + KV_LORA + QK_ROPE

    @pl.when(i == 0)
    def _():
        wt_ref[...] = win_ref[...].astype(BF16)

    x = jnp.where(i < CTX_TILES, xp_ref[...], xs_ref[...])
    m = _mod_row(mod_ref, i * TM)
    h = _rms(x, nm_ref[0:1, :]) * (1.0 + m[:, D:2 * D]) + m[:, 0:D]
    hb = h.astype(BF16)

    za = _dot_nt(hb, wt_ref[0:512, :])
    ckv = _rms(za[:, Q_LORA:Q_LORA + KV_LORA], kvan_ref[...])
    lane = lax.broadcasted_iota(jnp.int32, (1, HEAD_PAD), 1)
    kpe = jnp.where(lane < QK_ROPE, za[:, Q_LORA + KV_LORA:], 0.0)
    ckv_ref[...] = ckv
    kpe_ref[...] = kpe

    @pl.when(i < CTX_TILES)
    def _():
        sckv_ref[...] = ckv
        skpe_ref[...] = kpe[:, :QK_ROPE]

    qa = _rms(za[:, :Q_LORA], qan_ref[...]).astype(BF16)
    cos, s1, s2 = cos_ref[...], s1_ref[...], s2_ref[...]
    qn = qn_ref[...]
    scale = QK_HEAD ** -0.5
    for hp in range(HEADS // 2):
        qq = _dot(qa, wqb_ref[:, hp * 256:(hp + 1) * 256])
        for j in range(2):
            blk = _head_norm(qq[:, j * HEAD_PAD:(j + 1) * HEAD_PAD], qn)
            blk = _rope(blk, cos, s1, s2) * scale
            h0 = (2 * hp + j) * HEAD_PAD
            q_ref[:, h0:h0 + HEAD_PAD] = blk.astype(BF16)

    gb = _dot_nt(hb, wt_ref[n_a:n_a + SC_W, :])
    v = _dot_nt(hb, wt_ref[n_a + SC_W:n_a + 2 * SC_W, :]) * _dot_nt(hb, wt_ref[n_a + 2 * SC_W:n_a + 3 * SC_W, :])
    seq = jnp.where(i < CTX_TILES, CTX_LEN, LAT_LEN)
    r = lax.broadcasted_iota(jnp.int32, (TM, 1), 0) & (seq - 1)
    vp = jnp.where(r == 0, 0.0, pltpu.roll(v, 1, 0))
    vn = jnp.where(r == seq - 1, 0.0, pltpu.roll(v, TM - 1, 0))
    w = wsc_ref[0]
    y = w[0:1] * vp + w[1:2] * v + w[2:3] * vn
    sc_ref[...] = (gb * y).astype(BF16)


def _even_proj(xp, xs, mods, norm_mix, w_in, q_a_norm, wqb, kv_a_norm, qn, w_sc, tabs):
    full = lambda shape: pl.BlockSpec(shape, lambda i: (0,) * len(shape))
    tab = pl.BlockSpec((TM, HEAD_PAD), lambda i: (jnp.where(i < CTX_TILES, 1, 0), 0))
    row = lambda n: pl.BlockSpec((TM, n), lambda i: (i, 0))
    ctx_row = lambda n: pl.BlockSpec((TM, n), lambda i: (jnp.minimum(i, CTX_TILES - 1), 0))
    return pl.pallas_call(
        _even_proj_kernel,
        grid=(N_TILES,),
        in_specs=[
            ctx_row(D),
            pl.BlockSpec((TM, D), lambda i: (jnp.maximum(i - CTX_TILES, 0), 0)),
            _mod_spec(0, 1),
            full((2, D)),
            pl.BlockSpec((IN0_W, D), lambda i: (0, 0), pipeline_mode=pl.Buffered(1)),
            full((1, Q_LORA)),
            full((Q_LORA, HEADS * HEAD_PAD)), full((1, KV_LORA)), full((1, HEAD_PAD)),
            full((1, 3, SC_W)), tab, tab, tab,
        ],
        out_specs=[row(HEADS * HEAD_PAD), row(KV_LORA), row(HEAD_PAD), row(SC_W),
                   ctx_row(KV_LORA), ctx_row(QK_ROPE)],
        out_shape=[
            jax.ShapeDtypeStruct((T, HEADS * HEAD_PAD), BF16),
            jax.ShapeDtypeStruct((T, KV_LORA), F32),
            jax.ShapeDtypeStruct((T, HEAD_PAD), F32),
            jax.ShapeDtypeStruct((T, SC_W), BF16),
            jax.ShapeDtypeStruct((T_CTX, KV_LORA), F32),
            jax.ShapeDtypeStruct((T_CTX, QK_ROPE), F32),
        ],
        scratch_shapes=[pltpu.VMEM((IN0_W, D), BF16)],
        compiler_params=_params("arbitrary"),
        name="even_proj",
    )(xp, xs, mods, norm_mix, w_in, q_a_norm, wqb, kv_a_norm, qn, w_sc, *tabs)


def _kv_proj_kernel(ckv_ref, kpe_ref, wkvb_ref, kn_ref, cos_ref, s1_ref, s2_ref, k_ref, kv_ref):
    kv = _dot(ckv_ref[...].astype(BF16), wkvb_ref[...])
    kv_ref[...] = kv.astype(BF16)
    kpe = pltpu.roll(kpe_ref[...], QK_NOPE, 1)
    lane = lax.broadcasted_iota(jnp.int32, (1, HEAD_PAD), 1)
    kn = kn_ref[...]
    pe_sq = jnp.sum(kpe * kpe, axis=-1, keepdims=True)
    pe = _rope(kpe * kn, cos_ref[...], s1_ref[...], s2_ref[...])
    for h in range(HEADS):
        blk = kv[:, h * HEAD_PAD:(h + 1) * HEAD_PAD]
        nope = jnp.where(lane < QK_NOPE, blk, 0.0)
        ms = (jnp.sum(nope * nope, axis=-1, keepdims=True) + pe_sq) * (1.0 / QK_HEAD)
        k = jnp.where(lane < QK_NOPE, blk * kn, pe) * lax.rsqrt(ms + EPS)
        k_ref[:, h * HEAD_PAD:(h + 1) * HEAD_PAD] = k.astype(BF16)


def _kv_proj(ckv, kpe, wkvb, kn, tabs, tab_index, name):
    n = ckv.shape[0]
    full = lambda shape: pl.BlockSpec(shape, lambda i: (0,) * len(shape))
    tab = pl.BlockSpec((TKV, HEAD_PAD), lambda i: (tab_index(i), 0))
    row = lambda w: pl.BlockSpec((TKV, w), lambda i: (i, 0))
    return pl.pallas_call(
        _kv_proj_kernel,
        grid=(n // TKV,),
        in_specs=[row(KV_LORA), row(HEAD_PAD), full((KV_LORA, HEADS * HEAD_PAD)), full((1, HEAD_PAD)),
                  tab, tab, tab],
        out_specs=[row(HEADS * HEAD_PAD), row(HEADS * HEAD_PAD)],
        out_shape=[jax.ShapeDtypeStruct((n, HEADS * HEAD_PAD), BF16)] * 2,
        compiler_params=_params("arbitrary"),
        name=name,
    )(ckv, kpe, wkvb, kn, *tabs)


def _pair_out(o0, o1):
    lane = lax.broadcasted_iota(jnp.int32, (1, HEAD_PAD), 1)
    return jnp.where(lane < V_HEAD, pltpu.roll(o0, V_HEAD, 1), o1).astype(BF16)


CTX_SEQS = 2


def _attn_ctx_kernel(q_ref, k_ref, kv_ref, o_ref):
    for b in range(CTX_SEQS):
        rows = slice(b * CTX_LEN, (b + 1) * CTX_LEN)
        for hp in range(HEADS // 2):
            outs = []
            for j in range(2):
                lanes = slice((2 * hp + j) * HEAD_PAD, (2 * hp + j + 1) * HEAD_PAD)
                s = _dot_nt(q_ref[rows, lanes], k_ref[rows, lanes])
                p = jnp.exp(s - jnp.max(s, axis=-1, keepdims=True))
                l = jnp.sum(p, axis=-1, keepdims=True)
                outs.append(_dot(p.astype(BF16), kv_ref[rows, lanes]) / l)
            o_ref[rows, hp * HEAD_PAD:(hp + 1) * HEAD_PAD] = _pair_out(*outs)


def _attn_ctx(q, k, kv):
    blk = pl.BlockSpec((CTX_SEQS * CTX_LEN, HEADS * HEAD_PAD), lambda b: (b, 0))
    return pl.pallas_call(
        _attn_ctx_kernel,
        grid=(N_CTX_SEQ // CTX_SEQS,),
        in_specs=[blk, blk, blk],
        out_specs=pl.BlockSpec((CTX_SEQS * CTX_LEN, HEADS * V_HEAD), lambda b: (b, 0)),
        out_shape=jax.ShapeDtypeStruct((T_CTX, HEADS * V_HEAD), BF16),
        compiler_params=_params("arbitrary"),
        name="attn_ctx",
    )(q, k, kv)


LAT_HEADS = 4


def _attn_lat_kernel(q_ref, kc_ref, kvc_ref, kl_ref, kvl_ref, o_ref):
    for hp in range(LAT_HEADS // 2):
        outs = []
        for j in range(2):
            h0 = (2 * hp + j) * HEAD_PAD
            lanes = slice(h0, h0 + HEAD_PAD)
            q = q_ref[:, lanes]
            sc = _dot_nt(q, kc_ref[:, lanes])
            sl = _dot_nt(q, kl_ref[:, lanes])
            m = jnp.maximum(jnp.max(sc, axis=-1, keepdims=True), jnp.max(sl, axis=-1, keepdims=True))
            pc, pl_ = jnp.exp(sc - m), jnp.exp(sl - m)
            l = jnp.sum(pc, axis=-1, keepdims=True) + jnp.sum(pl_, axis=-1, keepdims=True)
            o = _dot(pc.astype(BF16), kvc_ref[:, lanes]) + _dot(pl_.astype(BF16), kvl_ref[:, lanes])
            outs.append(o / l)
        o_ref[:, hp * HEAD_PAD:(hp + 1) * HEAD_PAD] = _pair_out(*outs)


def _attn_lat(q, kc, kvc, k, kv):
    nq = LAT_LEN // TQ
    q0 = T_CTX // TQ
    kl0 = T_CTX // LAT_LEN
    width = LAT_HEADS * HEAD_PAD
    lat = pl.BlockSpec((LAT_LEN, width), lambda b, hg, t: (kl0 + b, hg))
    ctx = pl.BlockSpec((PAST, width), lambda b, hg, t: (b, hg))
    return pl.pallas_call(
        _attn_lat_kernel,
        grid=(N_LAT_SEQ, HEADS // LAT_HEADS, nq),
        in_specs=[pl.BlockSpec((TQ, width), lambda b, hg, t: (q0 + b * nq + t, hg)), ctx, ctx, lat, lat],
        out_specs=pl.BlockSpec((TQ, LAT_HEADS * V_HEAD), lambda b, hg, t: (b * nq + t, hg)),
        out_shape=jax.ShapeDtypeStruct((T_LAT, HEADS * V_HEAD), BF16),
        compiler_params=_params("arbitrary", "arbitrary", "arbitrary"),
        name="attn_lat",
    )(q, kc, kvc, k, kv)


FFN_NC = D_FF // FFN_FC
FFN_TM = 512
FFN_CTX_TILES = T_CTX // FFN_TM


def _ffn_kernel(oc_ref, ol_ref, sc_ref, xp_ref, xs_ref, wo_ref, nf_ref, wg_ref, wu_ref, wd_ref,
                mod0_ref, mod1_ref, nm_ref, x2_ref, h3_ref, wg_all, wu_all, wd_all):
    t = pl.program_id(0)

    @pl.when(t < FFN_NC)
    def _stage():
        wg_all[t] = wg_ref[...].astype(BF16)
        wu_all[t] = wu_ref[...].astype(BF16)
        wd_all[pl.ds(pl.multiple_of(t * FFN_FC, FFN_FC), FFN_FC), :] = wd_ref[...].astype(BF16)

    @pl.when(t >= FFN_NC - 1)
    def _tile():
        i = t - (FFN_NC - 1)
        ctx = i < FFN_CTX_TILES
        m0, m1 = _mod_row(mod0_ref, i * FFN_TM), _mod_row(mod1_ref, i * FFN_TM)
        attn = jnp.where(ctx, oc_ref[...], ol_ref[...])
        x = jnp.where(ctx, xp_ref[...], xs_ref[...])
        x1 = x + m0[:, 2 * D:3 * D] * _dot(jnp.concatenate([attn, sc_ref[...]], axis=1), wo_ref[...])
        h = (_rms(x1, nf_ref[0:1, :]) * (1.0 + m0[:, 4 * D:5 * D]) + m0[:, 3 * D:4 * D]).astype(BF16)
        act = []
        for c in range(FFN_NC):
            act.append((_silu(_dot(h, wg_all[c])) * _dot(h, wu_all[c])).astype(BF16))
        f = _dot(jnp.concatenate(act, axis=1), wd_all[...])
        x2 = x1 + m0[:, 5 * D:6 * D] * f
        x2_ref[...] = x2
        h3_ref[...] = (_rms(x2, nm_ref[1:2, :]) * (1.0 + m1[:, D:2 * D]) + m1[:, 0:D]).astype(BF16)


def _ffn(oc, ol, sc, xp, xs, wo, norm_ffn, w_gu, w_down, mods, norm_mix):
    chunk = lambda t: jnp.minimum(t, FFN_NC - 1)
    tile = lambda t: jnp.maximum(t - (FFN_NC - 1), 0)
    full = lambda shape: pl.BlockSpec(shape, lambda t: (0,) * len(shape))
    row = lambda n: pl.BlockSpec((FFN_TM, n), lambda t: (tile(t), 0))
    first = lambda n: pl.BlockSpec((FFN_TM, n), lambda t: (jnp.minimum(tile(t), FFN_CTX_TILES - 1), 0))
    second = lambda n: pl.BlockSpec((FFN_TM, n), lambda t: (jnp.maximum(tile(t) - FFN_CTX_TILES, 0), 0))
    return pl.pallas_call(
        _ffn_kernel,
        grid=(FFN_NC - 1 + T // FFN_TM,),
        in_specs=[first(HEADS * V_HEAD), second(HEADS * V_HEAD), row(SC_W), first(D), second(D),
                  full((HEADS * V_HEAD + SC_W, D)), full((2, D)),
                  pl.BlockSpec((D, FFN_FC), lambda t: (0, chunk(t))),
                  pl.BlockSpec((D, FFN_FC), lambda t: (0, FFN_NC + chunk(t))),
                  pl.BlockSpec((FFN_FC, D), lambda t: (chunk(t), 0)),
                  _mod_spec(0, 1), _mod_spec(1, 1), full((2, D))],
        out_specs=[row(D), row(D)],
        out_shape=[jax.ShapeDtypeStruct((T, D), F32), jax.ShapeDtypeStruct((T, D), BF16)],
        scratch_shapes=[pltpu.VMEM((FFN_NC, D, FFN_FC), BF16), pltpu.VMEM((FFN_NC, D, FFN_FC), BF16),
                        pltpu.VMEM((D_FF, D), BF16)],
        compiler_params=_params("arbitrary"),
        name="ffn_dense",
    )(oc, ol, sc, xp, xs, wo, norm_ffn, w_gu, w_gu, w_down, mods, mods, norm_mix)


CONF_CB = 256
CONF_SEG = 256
CONF_HALO = 16
CONF_SEGP = CONF_SEG + 2 * CONF_HALO
CONF_PIECE = 64


def _conf_kernel(h_ref, x2_ref, w1_ref, b1_ref, wdw_ref, bdw_ref, lng_ref, lnb_ref, w2_ref, b2_ref,
                 mod_ref, nf_ref, rt_ref, x3_ref, h4_ref, lg_ref, pad_ref, conv_ref):
    i = pl.program_id(0)
    nseg = TM // CONF_SEG
    h = h_ref[...]
    joined = jnp.where(i < CTX_TILES, 0.0, 1.0)
    zeros_halo = jnp.zeros((CONF_HALO, CONF_CB), F32)
    for cb in range(D // CONF_CB):
        c0 = cb * CONF_CB
        a = _dot(h, w1_ref[:, c0:c0 + CONF_CB]) + b1_ref[:, c0:c0 + CONF_CB]
        g = _dot(h, w1_ref[:, D + c0:D + c0 + CONF_CB]) + b1_ref[:, D + c0:D + c0 + CONF_CB]
        u = a * jax.nn.sigmoid(g)
        for s in range(nseg):
            base = s * CONF_SEGP
            top = u[s * CONF_SEG - CONF_HALO:s * CONF_SEG] * joined if s > 0 else zeros_halo
            bot = (u[(s + 1) * CONF_SEG:(s + 1) * CONF_SEG + CONF_HALO] * joined
                   if s < nseg - 1 else zeros_halo)
            pad_ref[0, base:base + CONF_HALO, :] = top
            pad_ref[0, base + CONF_HALO:base + CONF_HALO + CONF_SEG, :] = u[s * CONF_SEG:(s + 1) * CONF_SEG]
            pad_ref[0, base + CONF_HALO + CONF_SEG:base + CONF_SEGP, :] = bot

        p0 = pad_ref[0]
        rows = nseg * CONF_SEGP
        for b in range(1, 8):
            pad_ref[b] = pltpu.roll(p0, rows - b, 0)

        def piece(t, carry):
            s = t // (CONF_SEG // CONF_PIECE)
            q0 = (t % (CONF_SEG // CONF_PIECE)) * CONF_PIECE
            src = pl.multiple_of(s * CONF_SEGP + q0, 8)
            acc = jnp.zeros((CONF_PIECE, CONF_CB), F32)
            for j in range(CONF_K):
                hi, lo = (j + 1) // 8, (j + 1) % 8
                acc = acc + wdw_ref[0, j:j + 1, c0:c0 + CONF_CB] * pad_ref[lo, pl.ds(src + 8 * hi, CONF_PIECE), :]
            dst = pl.multiple_of(s * CONF_SEG + q0, 8)
            conv_ref[pl.ds(dst, CONF_PIECE), c0:c0 + CONF_CB] = acc + bdw_ref[:, c0:c0 + CONF_CB]
            return carry

        lax.fori_loop(0, TM // CONF_PIECE, piece, 0)

    m = _mod_row(mod_ref, i * TM)
    half = TM // 2
    for r0 in (0, half):
        rows = slice(r0, r0 + half)
        y = conv_ref[rows, :]
        mu = jnp.mean(y, axis=-1, keepdims=True)
        yc = y - mu
        var = jnp.mean(yc * yc, axis=-1, keepdims=True)
        y = _silu(yc * lax.rsqrt(var + EPS) * lng_ref[...] + lnb_ref[...])
        out = _dot(y.astype(BF16), w2_ref[...]) + b2_ref[...]
        x3 = x2_ref[rows, :] + m[:, 2 * D:3 * D] * out
        x3_ref[rows, :] = x3
        h4 = _rms(x3, nf_ref[1:2, :]) * (1.0 + m[:, 4 * D:5 * D]) + m[:, 3 * D:4 * D]
        h4_ref[rows, :] = h4.astype(BF16)
        lg_ref[:, rows] = lax.dot_general(rt_ref[...], h4, (((1,), (1,)), ((), ())),
                                          precision=lax.Precision.HIGHEST, preferred_element_type=F32)


def _conf(h3, x2, w1, b1, wdw, bdw, lng, lnb, w2, b2, mods, norm_ffn1, router_t):
    full = lambda shape: pl.BlockSpec(shape, lambda i: (0,) * len(shape))
    row = lambda n: pl.BlockSpec((TM, n), lambda i: (i, 0))
    return pl.pallas_call(
        _conf_kernel,
        grid=(N_TILES,),
        in_specs=[row(D), row(D), full((D, 2 * D)), full((1, 2 * D)), full((1, CONF_K, D)), full((1, D)),
                  full((1, D)), full((1, D)), full((D, D)), full((1, D)),
                  _mod_spec(1, 1), full((2, D)), full((N_EXP, D))],
        out_specs=[row(D), row(D), pl.BlockSpec((N_EXP, TM), lambda i: (0, i))],
        out_shape=[jax.ShapeDtypeStruct((T, D), F32), jax.ShapeDtypeStruct((T, D), BF16),
                   jax.ShapeDtypeStruct((N_EXP, T), F32)],
        scratch_shapes=[pltpu.VMEM((8, (TM // CONF_SEG) * CONF_SEGP, CONF_CB), F32),
                        pltpu.VMEM((TM, D), F32)],
        compiler_params=_params("arbitrary"),
        name="conformer_conv",
    )(h3, x2, w1, b1, wdw, bdw, lng, lnb, w2, b2, mods, norm_ffn1, router_t)


TB = 256
N_TB = T // TB
SUB = 128
SUBS = 24
SM = SUBS * SUB
N_SUB_MAX = 2 * T // SUB + N_EXP
N_SUP_MAX = (N_SUB_MAX + N_EXP * (SUBS - 1)) // SUBS
YS_ROWS = (N_SUB_MAX + 4) * SUB
WIN_ALIGN = 16
WIN_HALF = TB // 2 + WIN_ALIGN
FIRST_STRIDE = 32
UNIT_STRIDE = 64
GATHER_BLOCKS = 4


def _route_kernel(lg_ref, g_ref, rank_ref, first_ref):
    lg = lg_ref[...]
    idx = lax.broadcasted_iota(jnp.int32, lg.shape, 0).astype(F32)
    none = float(N_EXP)
    m1 = jnp.max(lg, axis=0, keepdims=True)
    i1 = jnp.min(jnp.where(lg == m1, idx, none), axis=0, keepdims=True)
    rest = jnp.where(idx == i1, -jnp.inf, lg)
    m2 = jnp.max(rest, axis=0, keepdims=True)
    i2 = jnp.min(jnp.where(rest == m2, idx, none), axis=0, keepdims=True)
    e = jnp.exp(m2 - m1)
    w1 = 1.0 / (1.0 + e)
    w2 = e / (1.0 + e)
    g_ref[...] = jnp.where(idx == i1, w1, 0.0) + jnp.where(idx == i2, w2, 0.0)

    mask = jnp.where(idx == i1, 1.0, 0.0) + jnp.where(idx == i2, 1.0, 0.0)
    before = (lax.broadcasted_iota(jnp.int32, (TB, TB), 0) < lax.broadcasted_iota(jnp.int32, (TB, TB), 1))
    before = jnp.where(before, 1.0, 0.0).astype(BF16)
    lane = lax.broadcasted_iota(jnp.int32, (N_EXP, 128), 1)
    carry = jnp.zeros((N_EXP, 1), F32)
    first = jnp.zeros((N_EXP, 128), F32)
    for b in range(N_TB):
        mb = mask[:, b * TB:(b + 1) * TB]
        local = _dot(mb.astype(BF16), before)
        rank_ref[:, b * TB:(b + 1) * TB] = jnp.where(mb > 0.0, local + carry, -1.0)
        first = jnp.where(lane == b, carry, first)
        carry = carry + jnp.sum(mb, axis=1, keepdims=True)
    first_ref[...] = jnp.where(lane == N_TB, carry, first)


def _route(logits_t):
    return pl.pallas_call(
        _route_kernel,
        out_shape=[jax.ShapeDtypeStruct((N_EXP, T), F32), jax.ShapeDtypeStruct((N_EXP, T), F32),
                   jax.ShapeDtypeStruct((N_EXP, 128), F32)],
        compiler_params=pltpu.CompilerParams(vmem_limit_bytes=VMEM_LIMIT),
        name="route",
    )(logits_t)


def _moe_plan(first):
    first = first[:, :FIRST_STRIDE].astype(jnp.int32)
    cnt = first[:, N_TB]
    nt = (cnt + (SUB - 1)) // SUB
    off_end = jnp.cumsum(nt)
    off = off_end - nt
    nsub = off_end[-1]
    nsup = (nt + (SUBS - 1)) // SUBS
    sup_end = jnp.cumsum(nsup)
    sup_off = sup_end - nsup
    s = jnp.minimum(jnp.arange(N_SUP_MAX), sup_end[-1] - 1)
    valid = jnp.arange(N_SUP_MAX) < sup_end[-1]
    se = jnp.sum(s[:, None] >= sup_end[None, :], axis=1)
    sk0 = (s - sup_off[se]) * SUBS
    sns = jnp.where(valid, jnp.clip(nt[se] - sk0, 0, SUBS), 0)
    sj0 = off[se] + sk0
    base = (jnp.arange(UNIT_STRIDE) * SUB)[None, :, None]
    blo = jnp.minimum(jnp.sum(first[:, None, 1:N_TB + 1] <= base, axis=2), N_TB - 1)
    end = jnp.minimum(base + SUB, cnt[:, None, None])
    bhi = jnp.maximum(jnp.sum(first[:, None, :N_TB] < end, axis=2) - 1, blo)
    ng = (bhi - blo) // GATHER_BLOCKS + 1
    start = SUB * off[:, None] + first[:, :N_TB]
    lead = start & (WIN_ALIGN - 1)
    wina = start - lead
    rel = first[:, :N_TB] - lead
    need = lead + (first[:, 1:N_TB + 1] - first[:, :N_TB]) > WIN_HALF
    winb = lax.cummax(jnp.where(need, wina + WIN_HALF, 0), axis=1)
    wide = jnp.any(need, axis=0)
    i32 = lambda a: a.astype(jnp.int32)
    return dict(se=i32(se), sk0=i32(sk0), sns=i32(sns), sj0=i32(sj0), nsub=i32(nsub).reshape(1),
                first=i32(first.reshape(-1)), blo=i32(blo.reshape(-1)), ng=i32(ng.reshape(-1)),
                wina=i32(wina.T.reshape(-1)), winb=i32(winb.T.reshape(-1)), rel=i32(rel.T.reshape(-1)),
                wide=i32(wide))


def _moe_gmm_kernel(se_ref, sk0_ref, sns_ref, sj0_ref, nsub_ref, first_ref, blo_ref, ng_ref,
                    x_ref, rank_ref, gate_ref, wg_ref, wu_ref, wd_ref, ys_ref,
                    xs_ref, gs_ref, yacc_ref, wgb_ref, wub_ref, wdb_ref, sem):
    s, c = pl.program_id(0), pl.program_id(1)
    nc = pl.num_programs(1)
    e, k0, ns = se_ref[s], sk0_ref[s], sns_ref[s]

    def sub_rows(k):
        return pl.ds(pl.multiple_of(k * SUB, SUB), SUB)

    def out_copy(k, row0):
        dst = ys_ref.at[pl.ds(pl.multiple_of(row0 + k * SUB, SUB), SUB)]
        return pltpu.make_async_copy(xs_ref.at[sub_rows(k)], dst, sem.at[k])

    @pl.when((ns > 0) & (c == 0))
    def _gather():
        def group(k, g):
            slot = (lax.broadcasted_iota(jnp.int32, (SUB, 1), 0) + (k0 + k) * SUB).astype(F32)
            b0 = blo_ref[e * UNIT_STRIDE + k0 + k] + g * GATHER_BLOCKS
            t0 = pl.multiple_of(jnp.minimum(b0, N_TB - GATHER_BLOCKS) * TB, TB)
            lo = first_ref[e * FIRST_STRIDE + b0].astype(F32)
            cols = pl.ds(t0, GATHER_BLOCKS * TB)
            hit = rank_ref[pl.ds(e, 1), cols] == jnp.where(slot >= lo, slot, -2.0)
            rows = _dot(jnp.where(hit, 1.0, 0.0).astype(BF16), x_ref[cols, :])
            gate = jnp.sum(jnp.where(hit, gate_ref[pl.ds(e, 1), cols], 0.0), axis=-1, keepdims=True)
            return rows, gate

        def first(k):
            rows, gate = group(k, 0)
            xs_ref[sub_rows(k), :] = rows.astype(BF16)
            gs_ref[sub_rows(k), :] = gate
            yacc_ref[sub_rows(k), :] = jnp.zeros((SUB, D), F32)

        def more(k):
            def body(g, carry):
                rows, gate = group(k, g)
                xs_ref[sub_rows(k), :] = (xs_ref[sub_rows(k), :].astype(F32) + rows).astype(BF16)
                gs_ref[sub_rows(k), :] += gate
                return carry

            lax.fori_loop(1, ng_ref[e * UNIT_STRIDE + k0 + k], body, 0)

        def pair(p, carry):
            first(2 * p)
            first(2 * p + 1)
            more(2 * p)
            more(2 * p + 1)
            return carry

        lax.fori_loop(0, ns >> 1, pair, 0)

        @pl.when((ns & 1) != 0)
        def _():
            first(ns - 1)
            more(ns - 1)

    @pl.when(ns > 0)
    def _compute():
        row0 = sj0_ref[s] * SUB

        def swiglu(rows, wg, wu, wd):
            x = xs_ref[rows, :]
            g = _dot(x, wg)
            u = _dot(x, wu)
            yacc_ref[rows, :] += _dot((_silu(g) * u).astype(BF16), wd)

        def finish(first_sub, n):
            @pl.when(c == nc - 1)
            def _():
                for k in range(n):
                    rows = sub_rows(first_sub + k)
                    xs_ref[rows, :] = (yacc_ref[rows, :] * gs_ref[rows, :]).astype(BF16)
                    out_copy(first_sub + k, row0).start()

        def first_chain(rows):
            wg, wu, wd = wg_ref[0].astype(BF16), wu_ref[0].astype(BF16), wd_ref[0].astype(BF16)
            wgb_ref[...] = wg
            wub_ref[...] = wu
            wdb_ref[...] = wd
            swiglu(rows, wg, wu, wd)

        @pl.when(ns >= 4)
        def _():
            first_chain(pl.ds(0, 4 * SUB))
            finish(0, 4)

        @pl.when(ns < 4)
        def _():
            first_chain(pl.ds(0, SUB))
            finish(0, 1)

        done = jnp.where(ns >= 4, 4, 1)
        rest = ns - done

        def chain(first_sub, n):
            rows = pl.ds(pl.multiple_of(first_sub * SUB, SUB), n * SUB)
            swiglu(rows, wgb_ref[...], wub_ref[...], wdb_ref[...])
            finish(first_sub, n)

        def eight(k, carry):
            chain(done + 8 * k, 8)
            return carry

        lax.fori_loop(0, rest >> 3, eight, 0)
        done8 = done + (rest & ~7)
        for n in (4, 2, 1):
            @pl.when((rest & n) != 0)
            def _(n=n):
                chain(done8 + (rest & (7 & ~(2 * n - 1))), n)

    @pl.when((ns > 0) & (c == nc - 1))
    def _store_done():
        row0 = sj0_ref[s] * SUB

        def done(k, carry):
            out_copy(k, row0).wait()
            return carry

        lax.fori_loop(0, ns, done, 0)

    @pl.when((s == pl.num_programs(0) - 1) & (c == nc - 1))
    def _zero_tail():
        xs_ref[0:SUB, :] = jnp.zeros((SUB, D), BF16)
        nsub = nsub_ref[0]

        def fill(k, carry):
            cp = out_copy(0, (nsub + k) * SUB)
            cp.start()
            cp.wait()
            return carry

        lax.fori_loop(0, YS_ROWS // SUB - nsub, fill, 0)


def _moe_gmm(plan, h4, rank, gates, w_gu, w_down):
    nc = D_FFE // MOE_FC

    def chunk(s, c, sns):
        return jnp.where(sns[s] > 0, c, nc - 1)

    return pl.pallas_call(
        _moe_gmm_kernel,
        grid_spec=pltpu.PrefetchScalarGridSpec(
            num_scalar_prefetch=8,
            grid=(N_SUP_MAX, nc),
            in_specs=[
                pl.BlockSpec((T, D), lambda s, c, *_: (0, 0), pipeline_mode=pl.Buffered(1)),
                pl.BlockSpec((N_EXP, T), lambda s, c, *_: (0, 0)),
                pl.BlockSpec((N_EXP, T), lambda s, c, *_: (0, 0)),
                pl.BlockSpec((1, D, MOE_FC), lambda s, c, se, sk0, sns, *_: (se[s], 0, chunk(s, c, sns))),
                pl.BlockSpec((1, D, MOE_FC), lambda s, c, se, sk0, sns, *_: (se[s], 0, nc + chunk(s, c, sns))),
                pl.BlockSpec((1, MOE_FC, D), lambda s, c, se, sk0, sns, *_: (se[s], chunk(s, c, sns), 0)),
            ],
            out_specs=pl.BlockSpec(memory_space=pl.ANY),
            scratch_shapes=[
                pltpu.VMEM((SM, D), BF16), pltpu.VMEM((SM, 1), F32), pltpu.VMEM((SM, D), F32),
                pltpu.VMEM((D, MOE_FC), BF16), pltpu.VMEM((D, MOE_FC), BF16), pltpu.VMEM((MOE_FC, D), BF16),
                pltpu.SemaphoreType.DMA((SUBS,)),
            ],
        ),
        out_shape=jax.ShapeDtypeStruct((YS_ROWS, D), BF16),
        compiler_params=_params("arbitrary", "arbitrary"),
        name="moe_gmm",
    )(plan["se"], plan["sk0"], plan["sns"], plan["sj0"], plan["nsub"], plan["first"], plan["blo"], plan["ng"],
      h4, rank, gates, w_gu, w_gu, w_down)


def _moe_combine_kernel(wina_ref, winb_ref, rel_ref, wide_ref, *refs):
    ya, yb = refs[:N_EXP], refs[N_EXP:2 * N_EXP]
    rank_ref, x3_ref, mod_ref, yp_ref, ys_ref, ycat_ref, acc_ref = refs[2 * N_EXP:]
    b = pl.program_id(0)
    row = lax.broadcasted_iota(jnp.int32, (WIN_HALF, 1), 0)

    def onehot(e, first_row):
        slot = (row + (rel_ref[b * N_EXP + e] + first_row)).astype(F32)
        return jnp.where(rank_ref[e:e + 1, :] == slot, 1.0, 0.0).astype(BF16)

    def gather(y_refs, first_row, base):
        pieces = []
        for e in range(N_EXP):
            ycat_ref[base + e * WIN_HALF:base + (e + 1) * WIN_HALF, :] = y_refs[e][...]
            pieces.append(onehot(e, first_row))
        return pieces

    def combine(pieces, rows):
        return lax.dot_general(jnp.concatenate(pieces, axis=0), ycat_ref[0:rows, :], (((0,), (0,)), ((), ())),
                               preferred_element_type=F32)

    @pl.when(wide_ref[b] == 0)
    def _():
        acc_ref[...] = combine(gather(ya, 0, 0), N_EXP * WIN_HALF)

    @pl.when(wide_ref[b] != 0)
    def _():
        pieces = gather(ya, 0, 0) + gather(yb, WIN_HALF, N_EXP * WIN_HALF)
        acc_ref[...] = combine(pieces, 2 * N_EXP * WIN_HALF)

    out = x3_ref[...] + _mod_row(mod_ref, b * TB)[:, 5 * D:6 * D] * acc_ref[...]

    @pl.when(b < T_CTX // TB)
    def _():
        yp_ref[...] = out

    @pl.when(b >= T_CTX // TB)
    def _():
        ys_ref[...] = out


def _moe_combine(plan, ysorted, rank, x3, mods):
    ctx_blocks = T_CTX // TB

    def window(e, second):
        def index(b, wina, winb, rel, wide):
            start = (winb if second else wina)[b * N_EXP + e]
            return pl.multiple_of(start, WIN_ALIGN), 0
        return pl.BlockSpec((pl.Element(WIN_HALF), pl.Element(D)), index)

    return pl.pallas_call(
        _moe_combine_kernel,
        grid_spec=pltpu.PrefetchScalarGridSpec(
            num_scalar_prefetch=4,
            grid=(N_TB,),
            in_specs=[window(e, False) for e in range(N_EXP)] + [window(e, True) for e in range(N_EXP)] + [
                pl.BlockSpec((N_EXP, TB), lambda b, *_: (0, b)),
                pl.BlockSpec((TB, D), lambda b, *_: (b, 0)),
                _mod_spec(1, 1),
            ],
            out_specs=[pl.BlockSpec((TB, D), lambda b, *_: (jnp.minimum(b, ctx_blocks - 1), 0)),
                       pl.BlockSpec((TB, D), lambda b, *_: (jnp.maximum(b - ctx_blocks, 0), 0))],
            scratch_shapes=[pltpu.VMEM((2 * N_EXP * WIN_HALF, D), BF16), pltpu.VMEM((TB, D), F32)],
        ),
        out_shape=[jax.ShapeDtypeStruct((T_CTX, D), F32), jax.ShapeDtypeStruct((T_LAT, D), F32)],
        compiler_params=_params("arbitrary"),
        name="moe_combine",
    )(plan["wina"], plan["winb"], plan["rel"], plan["wide"], *([ysorted] * (2 * N_EXP)), rank, x3, mods)


def _pad_heads(w, width):
    lead = w.shape[:-1]
    w = w.reshape(*lead, HEADS, width)
    w = jnp.pad(w, [(0, 0)] * len(lead) + [(0, 0), (0, HEAD_PAD - width)])
    return w.reshape(*lead, HEADS * HEAD_PAD)


def kernel(x_prompt, x_sample, cache_ckv, cache_kpe, c, c_ctx, ada_w, ada_b, norm_mix, norm_ffn, w_in, q_a_norm,
           w_qb, kv_a_norm, w_kvb, q_norm, k_norm, w_sc, w_o, ffn_gu, ffn_down, conv_pw1, conv_pw1_b, conv_dw,
           conv_dw_b, conv_ln_g, conv_ln_b, conv_pw2, conv_pw2_b, router, moe_gu, moe_down):
    xp = x_prompt.reshape(T_CTX, D)
    xs = x_sample.reshape(T_LAT, D)

    mods = _adaln(c_ctx, c, ada_w, ada_b)

    wqb = _pad_heads(w_qb[0], QK_HEAD).astype(BF16)
    wkvb = w_kvb[0].astype(BF16)
    qn = jnp.pad(q_norm[0], (0, HEAD_PAD - QK_HEAD)).reshape(1, HEAD_PAD)
    kn = jnp.pad(k_norm[0], (0, HEAD_PAD - QK_HEAD)).reshape(1, HEAD_PAD)
    tabs = _rope_tables()

    w_in_t = jnp.swapaxes(w_in[0], 0, 1)
    q, ckv, kpe, sc, state_ckv, state_kpe = _even_proj(xp, xs, mods, norm_mix, w_in_t, q_a_norm, wqb, kv_a_norm,
                                                       qn, w_sc, tabs)

    lat_tile0 = T_CTX // TKV
    ident = LAT_LEN // TKV
    k, kv = _kv_proj(ckv, kpe, wkvb, kn, tabs,
                     lambda i: jnp.where(i < lat_tile0, ident, (i - lat_tile0) % ident), "kv_proj")
    cache_kpe_p = jnp.pad(cache_kpe[:, 0].reshape(N_LAT_SEQ * PAST, QK_ROPE), ((0, 0), (0, HEAD_PAD - QK_ROPE)))
    kc, kvc = _kv_proj(cache_ckv[:, 0].reshape(N_LAT_SEQ * PAST, KV_LORA), cache_kpe_p, wkvb, kn, tabs,
                       lambda i: ident, "kv_proj_cache")

    oc = _attn_ctx(q, k, kv)
    ol = _attn_lat(q, kc, kvc, k, kv)
    x2, h3 = _ffn(oc, ol, sc, xp, xs, w_o[0].astype(BF16), norm_ffn, ffn_gu[0], ffn_down[0], mods, norm_mix)

    x3, h4, logits_t = _conf(h3, x2, conv_pw1[0].astype(BF16), conv_pw1_b, conv_dw, conv_dw_b, conv_ln_g,
                             conv_ln_b, conv_pw2[0].astype(BF16), conv_pw2_b, mods, norm_ffn, router[0].T)
    gates, rank, first = _route(logits_t)
    plan = _moe_plan(first)
    ysorted = _moe_gmm(plan, h4, rank, gates, moe_gu[0], moe_down[0])
    yp, ys = _moe_combine(plan, ysorted, rank, x3, mods)

    return (yp.reshape(N_CTX_SEQ, CTX_LEN, D), ys.reshape(N_LAT_SEQ, LAT_LEN, D),
            state_ckv.reshape(N_CTX_SEQ, 1, CTX_LEN, KV_LORA), state_kpe.reshape(N_CTX_SEQ, 1, CTX_LEN, QK_ROPE))
```

```python
import functools

import jax
import jax.numpy as jnp
import numpy as np
from jax import lax
from jax.experimental import pallas as pl
from jax.experimental.pallas import tpu as pltpu

F32 = jnp.float32
BF16 = jnp.bfloat16

D = 1024
N_CTX_SEQ, CTX_LEN = 16, 256
N_LAT_SEQ, LAT_LEN = 2, 1024
T_CTX = N_CTX_SEQ * CTX_LEN
T_LAT = N_LAT_SEQ * LAT_LEN
T = T_CTX + T_LAT
PAST = 256
GRID_W = 64
HEADS = 8
QK_NOPE, QK_ROPE, V_HEAD = 64, 32, 64
QK_HEAD = QK_NOPE + QK_ROPE
HEAD_PAD = 128
Q_LORA, KV_LORA = 256, 128
SC_W = 512
IN0_W = Q_LORA + KV_LORA + QK_ROPE + 3 * SC_W
CONF_K = 31
D_FF = 2816
N_EXP = 8
D_FFE = 3584
EPS = 1e-6
ROPE_THETA = 10000.0

TM = 1024
N_TILES = T // TM
CTX_TILES = T_CTX // TM
TKV = 512
TQ = 256
FFN_FC = 256
MOE_FC = 512
VMEM_LIMIT = 56 * 1024 * 1024


def _dot(a, b):
    return jnp.dot(a, b, preferred_element_type=F32)


def _dot_nt(a, b):
    return lax.dot_general(a, b, (((1,), (1,)), ((), ())), preferred_element_type=F32)


def _rms(x, g):
    return x * lax.rsqrt(jnp.mean(x * x, axis=-1, keepdims=True) + EPS) * g


def _silu(x):
    return x * jax.nn.sigmoid(x)


def _params(*sem):
    return pltpu.CompilerParams(dimension_semantics=sem, vmem_limit_bytes=VMEM_LIMIT)


def _mod_row(mod_ref, row0):
    cond = jnp.maximum(row0 - (T_CTX - LAT_LEN), 0) >> 10
    return mod_ref[0, pl.ds(cond, 1), :]


def _mod_spec(layer, ngrid):
    return pl.BlockSpec((1, 8, 6 * D), lambda *_: (layer, 0, 0))


def _adaln_kernel(cc_ref, c_ref, w_ref, b_ref, o_ref):
    l = pl.program_id(0)
    row = lax.broadcasted_iota(jnp.int32, (8, 1), 0)
    cond = jnp.where(row == 0, cc_ref[...], 0.0)
    for b in range(N_LAT_SEQ):
        cond = jnp.where(row == 1 + b, c_ref[b:b + 1, :], cond)
    o_ref[0] = _dot(_silu(cond).astype(BF16), w_ref[0].astype(BF16)) + b_ref[pl.ds(l, 1), :]


def _adaln(c_ctx, c, ada_w, ada_b):
    depth = ada_w.shape[0]
    tn = 2048
    return pl.pallas_call(
        _adaln_kernel,
        grid=(depth, 6 * D // tn),
        in_specs=[
            pl.BlockSpec((1, D), lambda l, j: (0, 0)),
            pl.BlockSpec((N_LAT_SEQ, D), lambda l, j: (0, 0)),
            pl.BlockSpec((1, D, tn), lambda l, j: (l, 0, j)),
            pl.BlockSpec((depth, tn), lambda l, j: (0, j)),
        ],
        out_specs=pl.BlockSpec((1, 8, tn), lambda l, j: (l, 0, j)),
        out_shape=jax.ShapeDtypeStruct((depth, 8, 6 * D), F32),
        compiler_params=_params("arbitrary", "arbitrary"),
        name="adaln",
    )(c_ctx.reshape(1, D), c, ada_w, ada_b)


def _rope_tables():
    half = QK_ROPE // 2
    nf = half // 2
    pos = np.arange(LAT_LEN)
    inv = ROPE_THETA ** (-np.arange(nf, dtype=np.float64) / nf)
    k = np.arange(QK_ROPE)
    part, idx = k // half, k % half
    p = np.where(part[None, :] == 0, (pos // GRID_W)[:, None], (pos % GRID_W)[:, None])
    ang = p * inv[idx % nf][None, :]
    cos, sin = np.cos(ang), np.sin(ang)
    first = (idx < nf)[None, :]
    s1 = np.where(first, -sin, 0.0)
    s2 = np.where(first, 0.0, sin)

    def place(t, fill):
        tab = np.full((2 * LAT_LEN, HEAD_PAD), fill, np.float32)
        tab[:LAT_LEN, QK_NOPE:QK_HEAD] = t
        return jnp.asarray(tab)

    return place(cos, 1.0), place(s1, 0.0), place(s2, 0.0)


def _rope(blk, cos, s1, s2):
    return blk * cos + pltpu.roll(blk, 8, 1) * s2 + pltpu.roll(blk, HEAD_PAD - 8, 1) * s1


def _head_norm(blk, g):
    ms = jnp.sum(blk * blk, axis=-1, keepdims=True) * (1.0 / QK_HEAD)
    return blk * lax.rsqrt(ms + EPS) * g


def _even_proj_kernel(xp_ref, xs_ref, mod_ref, nm_ref, win_ref, qan_ref, wqb_ref, kvan_ref,
                      qn_ref, wsc_ref, cos_ref, s1_ref, s2_ref,
                      q_ref, ckv_ref, kpe_ref, sc_ref, sckv_ref, skpe_ref, wt_ref):
    i = pl.program_id(0)
    n_a = Q_LORA + KV_LORA + QK_ROPE

    @pl.when(i == 0)
    def _():
        wt_ref[...] = win_ref[...].astype(BF16)

    x = jnp.where(i < CTX_TILES, xp_ref[...], xs_ref[...])
    m = _mod_row(mod_ref, i * TM)
    h = _rms(x, nm_ref[0:1, :]) * (1.0 + m[:, D:2 * D]) + m[:, 0:D]
    hb = h.astype(BF16)

    za = _dot_nt(hb, wt_ref[0:512, :])
    ckv = _rms(za[:, Q_LORA:Q_LORA + KV_LORA], kvan_ref[...])
    lane = lax.broadcasted_iota(jnp.int32, (1, HEAD_PAD), 1)
    kpe = jnp.where(lane < QK_ROPE, za[:, Q_LORA + KV_LORA:], 0.0)
    ckv_ref[...] = ckv
    kpe_ref[...] = kpe

    @pl.when(i < CTX_TILES)
    def _():
        sckv_ref[...] = ckv
        skpe_ref[...] = kpe[:, :QK_ROPE]

    qa = _rms(za[:, :Q_LORA], qan_ref[...]).astype(BF16)
    cos, s1, s2 = cos_ref[...], s1_ref[...], s2_ref[...]
    qn = qn_ref[...]
    scale = QK_HEAD ** -0.5
    for hp in range(HEADS // 2):
        qq = _dot(qa, wqb_ref[:, hp * 256:(hp + 1) * 256])
        for j in range(2):
            blk = _head_norm(qq[:, j * HEAD_PAD:(j + 1) * HEAD_PAD], qn)
            blk = _rope(blk, cos, s1, s2) * scale
            h0 = (2 * hp + j) * HEAD_PAD
            q_ref[:, h0:h0 + HEAD_PAD] = blk.astype(BF16)

    gb = _dot_nt(hb, wt_ref[n_a:n_a + SC_W, :])
    v = _dot_nt(hb, wt_ref[n_a + SC_W:n_a + 2 * SC_W, :]) * _dot_nt(hb, wt_ref[n_a + 2 * SC_W:n_a + 3 * SC_W, :])
    seq = jnp.where(i < CTX_TILES, CTX_LEN, LAT_LEN)
    r = lax.broadcasted_iota(jnp.int32, (TM, 1), 0) & (seq - 1)
    vp = jnp.where(r == 0, 0.0, pltpu.roll(v, 1, 0))
    vn = jnp.where(r == seq - 1, 0.0, pltpu.roll(v, TM - 1, 0))
    w = wsc_ref[0]
    y = w[0:1] * vp + w[1:2] * v + w[2:3] * vn
    sc_ref[...] = (gb * y).astype(BF16)


def _even_proj(xp, xs, mods, norm_mix, w_in, q_a_norm, wqb, kv_a_norm, qn, w_sc, tabs):
    full = lambda shape: pl.BlockSpec(shape, lambda i: (0,) * len(shape))
    tab = pl.BlockSpec((TM, HEAD_PAD), lambda i: (jnp.where(i < CTX_TILES, 1, 0), 0))
    row = lambda n: pl.BlockSpec((TM, n), lambda i: (i, 0))
    ctx_row = lambda n: pl.BlockSpec((TM, n), lambda i: (jnp.minimum(i, CTX_TILES - 1), 0))
    return pl.pallas_call(
        _even_proj_kernel,
        grid=(N_TILES,),
        in_specs=[
            ctx_row(D),
            pl.BlockSpec((TM, D), lambda i: (jnp.maximum(i - CTX_TILES, 0), 0)),
            _mod_spec(0, 1),
            full((2, D)),
            pl.BlockSpec((IN0_W, D), lambda i: (0, 0), pipeline_mode=pl.Buffered(1)),
            full((1, Q_LORA)),
            full((Q_LORA, HEADS * HEAD_PAD)), full((1, KV_LORA)), full((1, HEAD_PAD)),
            full((1, 3, SC_W)), tab, tab, tab,
        ],
        out_specs=[row(HEADS * HEAD_PAD), row(KV_LORA), row(HEAD_PAD), row(SC_W),
                   ctx_row(KV_LORA), ctx_row(QK_ROPE)],
        out_shape=[
            jax.ShapeDtypeStruct((T, HEADS * HEAD_PAD), BF16),
            jax.ShapeDtypeStruct((T, KV_LORA), F32),
            jax.ShapeDtypeStruct((T, HEAD_PAD), F32),
            jax.ShapeDtypeStruct((T, SC_W), BF16),
            jax.ShapeDtypeStruct((T_CTX, KV_LORA), F32),
            jax.ShapeDtypeStruct((T_CTX, QK_ROPE), F32),
        ],
        scratch_shapes=[pltpu.VMEM((IN0_W, D), BF16)],
        compiler_params=_params("arbitrary"),
        name="even_proj",
    )(xp, xs, mods, norm_mix, w_in, q_a_norm, wqb, kv_a_norm, qn, w_sc, *tabs)


def _kv_proj_kernel(ckv_ref, kpe_ref, wkvb_ref, kn_ref, cos_ref, s1_ref, s2_ref, k_ref, kv_ref):
    kv = _dot(ckv_ref[...].astype(BF16), wkvb_ref[...])
    kv_ref[...] = kv.astype(BF16)
    kpe = pltpu.roll(kpe_ref[...], QK_NOPE, 1)
    lane = lax.broadcasted_iota(jnp.int32, (1, HEAD_PAD), 1)
    kn = kn_ref[...]
    pe_sq = jnp.sum(kpe * kpe, axis=-1, keepdims=True)
    pe = _rope(kpe * kn, cos_ref[...], s1_ref[...], s2_ref[...])
    for h in range(HEADS):
        blk = kv[:, h * HEAD_PAD:(h + 1) * HEAD_PAD]
        nope = jnp.where(lane < QK_NOPE, blk, 0.0)
        ms = (jnp.sum(nope * nope, axis=-1, keepdims=True) + pe_sq) * (1.0 / QK_HEAD)
        k = jnp.where(lane < QK_NOPE, blk * kn, pe) * lax.rsqrt(ms + EPS)
        k_ref[:, h * HEAD_PAD:(h + 1) * HEAD_PAD] = k.astype(BF16)


def _kv_proj(ckv, kpe, wkvb, kn, tabs, tab_index, name):
    n = ckv.shape[0]
    full = lambda shape: pl.BlockSpec(shape, lambda i: (0,) * len(shape))
    tab = pl.BlockSpec((TKV, HEAD_PAD), lambda i: (tab_index(i), 0))
    row = lambda w: pl.BlockSpec((TKV, w), lambda i: (i, 0))
    return pl.pallas_call(
        _kv_proj_kernel,
        grid=(n // TKV,),
        in_specs=[row(KV_LORA), row(HEAD_PAD), full((KV_LORA, HEADS * HEAD_PAD)), full((1, HEAD_PAD)),
                  tab, tab, tab],
        out_specs=[row(HEADS * HEAD_PAD), row(HEADS * HEAD_PAD)],
        out_shape=[jax.ShapeDtypeStruct((n, HEADS * HEAD_PAD), BF16)] * 2,
        compiler_params=_params("arbitrary"),
        name=name,
    )(ckv, kpe, wkvb, kn, *tabs)


def _pair_out(o0, o1):
    lane = lax.broadcasted_iota(jnp.int32, (1, HEAD_PAD), 1)
    return jnp.where(lane < V_HEAD, pltpu.roll(o0, V_HEAD, 1), o1).astype(BF16)


CTX_SEQS = 2


def _attn_ctx_kernel(q_ref, k_ref, kv_ref, o_ref):
    for b in range(CTX_SEQS):
        rows = slice(b * CTX_LEN, (b + 1) * CTX_LEN)
        for hp in range(HEADS // 2):
            outs = []
            for j in range(2):
                lanes = slice((2 * hp + j) * HEAD_PAD, (2 * hp + j + 1) * HEAD_PAD)
                s = _dot_nt(q_ref[rows, lanes], k_ref[rows, lanes])
                p = jnp.exp(s - jnp.max(s, axis=-1, keepdims=True))
                l = jnp.sum(p, axis=-1, keepdims=True)
                outs.append(_dot(p.astype(BF16), kv_ref[rows, lanes]) / l)
            o_ref[rows, hp * HEAD_PAD:(hp + 1) * HEAD_PAD] = _pair_out(*outs)


def _attn_ctx(q, k, kv):
    blk = pl.BlockSpec((CTX_SEQS * CTX_LEN, HEADS * HEAD_PAD), lambda b: (b, 0))
    return pl.pallas_call(
        _attn_ctx_kernel,
        grid=(N_CTX_SEQ // CTX_SEQS,),
        in_specs=[blk, blk, blk],
        out_specs=pl.BlockSpec((CTX_SEQS * CTX_LEN, HEADS * V_HEAD), lambda b: (b, 0)),
        out_shape=jax.ShapeDtypeStruct((T_CTX, HEADS * V_HEAD), BF16),
        compiler_params=_params("arbitrary"),
        name="attn_ctx",
    )(q, k, kv)


LAT_HEADS = 4


def _attn_lat_kernel(q_ref, kc_ref, kvc_ref, kl_ref, kvl_ref, o_ref):
    for hp in range(LAT_HEADS // 2):
        outs = []
        for j in range(2):
            h0 = (2 * hp + j) * HEAD_PAD
            lanes = slice(h0, h0 + HEAD_PAD)
            q = q_ref[:, lanes]
            sc = _dot_nt(q, kc_ref[:, lanes])
            sl = _dot_nt(q, kl_ref[:, lanes])
            m = jnp.maximum(jnp.max(sc, axis=-1, keepdims=True), jnp.max(sl, axis=-1, keepdims=True))
            pc, pl_ = jnp.exp(sc - m), jnp.exp(sl - m)
            l = jnp.sum(pc, axis=-1, keepdims=True) + jnp.sum(pl_, axis=-1, keepdims=True)
            o = _dot(pc.astype(BF16), kvc_ref[:, lanes]) + _dot(pl_.astype(BF16), kvl_ref[:, lanes])
            outs.append(o / l)
        o_ref[:, hp * HEAD_PAD:(hp + 1) * HEAD_PAD] = _pair_out(*outs)


def _attn_lat(q, kc, kvc, k, kv):
    nq = LAT_LEN // TQ
    q0 = T_CTX // TQ
    kl0 = T_CTX // LAT_LEN
    width = LAT_HEADS * HEAD_PAD
    lat = pl.BlockSpec((LAT_LEN, width), lambda b, hg, t: (kl0 + b, hg))
    ctx = pl.BlockSpec((PAST, width), lambda b, hg, t: (b, hg))
    return pl.pallas_call(
        _attn_lat_kernel,
        grid=(N_LAT_SEQ, HEADS // LAT_HEADS, nq),
        in_specs=[pl.BlockSpec((TQ, width), lambda b, hg, t: (q0 + b * nq + t, hg)), ctx, ctx, lat, lat],
        out_specs=pl.BlockSpec((TQ, LAT_HEADS * V_HEAD), lambda b, hg, t: (b * nq + t, hg)),
        out_shape=jax.ShapeDtypeStruct((T_LAT, HEADS * V_HEAD), BF16),
        compiler_params=_params("arbitrary", "arbitrary", "arbitrary"),
        name="attn_lat",
    )(q, kc, kvc, k, kv)


FFN_NC = D_FF // FFN_FC
FFN_TM = 512
FFN_CTX_TILES = T_CTX // FFN_TM


def _ffn_kernel(oc_ref, ol_ref, sc_ref, xp_ref, xs_ref, wo_ref, nf_ref, wg_ref, wu_ref, wd_ref,
                mod0_ref, mod1_ref, nm_ref, x2_ref, h3_ref, wg_all, wu_all, wd_all, x1_ref, hs_ref, act_ref):
    t = pl.program_id(0)

    i = jnp.maximum(t - (FFN_NC - 1), 0)
    m0, m1 = _mod_row(mod0_ref, i * FFN_TM), _mod_row(mod1_ref, i * FFN_TM)

    def mixer():
        ctx = i < FFN_CTX_TILES
        attn = jnp.where(ctx, oc_ref[...], ol_ref[...])
        x = jnp.where(ctx, xp_ref[...], xs_ref[...])
        x1 = x + m0[:, 2 * D:3 * D] * _dot(jnp.concatenate([attn, sc_ref[...]], axis=1), wo_ref[...])
        h = (_rms(x1, nf_ref[0:1, :]) * (1.0 + m0[:, 4 * D:5 * D]) + m0[:, 3 * D:4 * D]).astype(BF16)
        return x1, h

    def up(h, c):
        return (_silu(_dot(h, wg_all[c])) * _dot(h, wu_all[c])).astype(BF16)

    def down(x1, act):
        x2 = x1 + m0[:, 5 * D:6 * D] * _dot(act, wd_all[...])
        x2_ref[...] = x2
        h3_ref[...] = (_rms(x2, nm_ref[1:2, :]) * (1.0 + m1[:, D:2 * D]) + m1[:, 0:D]).astype(BF16)

    @pl.when(t == 0)
    def _():
        x1, h = mixer()
        x1_ref[...] = x1
        hs_ref[...] = h

    @pl.when(t < FFN_NC)
    def _stage():
        wg_all[t] = wg_ref[...].astype(BF16)
        wu_all[t] = wu_ref[...].astype(BF16)
        wd_all[pl.ds(pl.multiple_of(t * FFN_FC, FFN_FC), FFN_FC), :] = wd_ref[...].astype(BF16)
        act_ref[t] = up(hs_ref[...], t)

    @pl.when(t == FFN_NC - 1)
    def _():
        down(x1_ref[...], jnp.concatenate([act_ref[c] for c in range(FFN_NC)], axis=1))

    @pl.when(t > FFN_NC - 1)
    def _tile():
        x1, h = mixer()
        down(x1, jnp.concatenate([up(h, c) for c in range(FFN_NC)], axis=1))


def _ffn(oc, ol, sc, xp, xs, wo, norm_ffn, w_gu, w_down, mods, norm_mix):
    chunk = lambda t: jnp.minimum(t, FFN_NC - 1)
    tile = lambda t: jnp.maximum(t - (FFN_NC - 1), 0)
    full = lambda shape: pl.BlockSpec(shape, lambda t: (0,) * len(shape))
    row = lambda n: pl.BlockSpec((FFN_TM, n), lambda t: (tile(t), 0))
    first = lambda n: pl.BlockSpec((FFN_TM, n), lambda t: (jnp.minimum(tile(t), FFN_CTX_TILES - 1), 0))
    second = lambda n: pl.BlockSpec((FFN_TM, n), lambda t: (jnp.maximum(tile(t) - FFN_CTX_TILES, 0), 0))
    return pl.pallas_call(
        _ffn_kernel,
        grid=(FFN_NC - 1 + T // FFN_TM,),
        in_specs=[first(HEADS * V_HEAD), second(HEADS * V_HEAD), row(SC_W), first(D), second(D),
                  full((HEADS * V_HEAD + SC_W, D)), full((2, D)),
                  pl.BlockSpec((D, FFN_FC), lambda t: (0, chunk(t))),
                  pl.BlockSpec((D, FFN_FC), lambda t: (0, FFN_NC + chunk(t))),
                  pl.BlockSpec((FFN_FC, D), lambda t: (chunk(t), 0)),
                  _mod_spec(0, 1), _mod_spec(1, 1), full((2, D))],
        out_specs=[row(D), row(D)],
        out_shape=[jax.ShapeDtypeStruct((T, D), F32), jax.ShapeDtypeStruct((T, D), BF16)],
        scratch_shapes=[pltpu.VMEM((FFN_NC, D, FFN_FC), BF16), pltpu.VMEM((FFN_NC, D, FFN_FC), BF16),
                        pltpu.VMEM((D_FF, D), BF16), pltpu.VMEM((FFN_TM, D), F32), pltpu.VMEM((FFN_TM, D), BF16),
                        pltpu.VMEM((FFN_NC, FFN_TM, FFN_FC), BF16)],
        compiler_params=_params("arbitrary"),
        name="ffn_dense",
    )(oc, ol, sc, xp, xs, wo, norm_ffn, w_gu, w_gu, w_down, mods, mods, norm_mix)


CONF_CB = 256
CONF_SEG = 256
CONF_HALO = 16
CONF_SEGP = CONF_SEG + 2 * CONF_HALO
CONF_PIECE = 64


def _conf_kernel(h_ref, x2_ref, w1_ref, b1_ref, wdw_ref, bdw_ref, lng_ref, lnb_ref, w2_ref, b2_ref,
                 mod_ref, nf_ref, rt_ref, x3_ref, h4_ref, lg_ref, pad_ref, conv_ref):
    i = pl.program_id(0)
    nseg = TM // CONF_SEG
    h = h_ref[...]
    joined = jnp.where(i < CTX_TILES, 0.0, 1.0)
    zeros_halo = jnp.zeros((CONF_HALO, CONF_CB), F32)
    for cb in range(D // CONF_CB):
        c0 = cb * CONF_CB
        a = _dot(h, w1_ref[:, c0:c0 + CONF_CB]) + b1_ref[:, c0:c0 + CONF_CB]
        g = _dot(h, w1_ref[:, D + c0:D + c0 + CONF_CB]) + b1_ref[:, D + c0:D + c0 + CONF_CB]
        u = a * jax.nn.sigmoid(g)
        for s in range(nseg):
            base = s * CONF_SEGP
            top = u[s * CONF_SEG - CONF_HALO:s * CONF_SEG] * joined if s > 0 else zeros_halo
            bot = (u[(s + 1) * CONF_SEG:(s + 1) * CONF_SEG + CONF_HALO] * joined
                   if s < nseg - 1 else zeros_halo)
            pad_ref[0, base:base + CONF_HALO, :] = top
            pad_ref[0, base + CONF_HALO:base + CONF_HALO + CONF_SEG, :] = u[s * CONF_SEG:(s + 1) * CONF_SEG]
            pad_ref[0, base + CONF_HALO + CONF_SEG:base + CONF_SEGP, :] = bot

        p0 = pad_ref[0]
        rows = nseg * CONF_SEGP
        for b in range(1, 8):
            pad_ref[b] = pltpu.roll(p0, rows - b, 0)

        def piece(t, carry):
            s = t // (CONF_SEG // CONF_PIECE)
            q0 = (t % (CONF_SEG // CONF_PIECE)) * CONF_PIECE
            src = pl.multiple_of(s * CONF_SEGP + q0, 8)
            acc = jnp.zeros((CONF_PIECE, CONF_CB), F32)
            for j in range(CONF_K):
                hi, lo = (j + 1) // 8, (j + 1) % 8
                acc = acc + wdw_ref[0, j:j + 1, c0:c0 + CONF_CB] * pad_ref[lo, pl.ds(src + 8 * hi, CONF_PIECE), :]
            dst = pl.multiple_of(s * CONF_SEG + q0, 8)
            conv_ref[pl.ds(dst, CONF_PIECE), c0:c0 + CONF_CB] = acc + bdw_ref[:, c0:c0 + CONF_CB]
            return carry

        lax.fori_loop(0, TM // CONF_PIECE, piece, 0)

    m = _mod_row(mod_ref, i * TM)
    half = TM // 2
    for r0 in (0, half):
        rows = slice(r0, r0 + half)
        y = conv_ref[rows, :]
        mu = jnp.mean(y, axis=-1, keepdims=True)
        yc = y - mu
        var = jnp.mean(yc * yc, axis=-1, keepdims=True)
        y = _silu(yc * lax.rsqrt(var + EPS) * lng_ref[...] + lnb_ref[...])
        out = _dot(y.astype(BF16), w2_ref[...]) + b2_ref[...]
        x3 = x2_ref[rows, :] + m[:, 2 * D:3 * D] * out
        x3_ref[rows, :] = x3
        h4 = _rms(x3, nf_ref[1:2, :]) * (1.0 + m[:, 4 * D:5 * D]) + m[:, 3 * D:4 * D]
        h4_ref[rows, :] = h4.astype(BF16)
        lg_ref[:, rows] = lax.dot_general(rt_ref[...], h4, (((1,), (1,)), ((), ())),
                                          precision=lax.Precision.HIGHEST, preferred_element_type=F32)


def _conf(h3, x2, w1, b1, wdw, bdw, lng, lnb, w2, b2, mods, norm_ffn1, router_t):
    full = lambda shape: pl.BlockSpec(shape, lambda i: (0,) * len(shape))
    row = lambda n: pl.BlockSpec((TM, n), lambda i: (i, 0))
    return pl.pallas_call(
        _conf_kernel,
        grid=(N_TILES,),
        in_specs=[row(D), row(D), full((D, 2 * D)), full((1, 2 * D)), full((1, CONF_K, D)), full((1, D)),
                  full((1, D)), full((1, D)), full((D, D)), full((1, D)),
                  _mod_spec(1, 1), full((2, D)), full((N_EXP, D))],
        out_specs=[row(D), row(D), pl.BlockSpec((N_EXP, TM), lambda i: (0, i))],
        out_shape=[jax.ShapeDtypeStruct((T, D), F32), jax.ShapeDtypeStruct((T, D), BF16),
                   jax.ShapeDtypeStruct((N_EXP, T), F32)],
        scratch_shapes=[pltpu.VMEM((8, (TM // CONF_SEG) * CONF_SEGP, CONF_CB), F32),
                        pltpu.VMEM((TM, D), F32)],
        compiler_params=_params("arbitrary"),
        name="conformer_conv",
    )(h3, x2, w1, b1, wdw, bdw, lng, lnb, w2, b2, mods, norm_ffn1, router_t)


TB = 256
N_TB = T // TB
SUB = 128
SUBS = 24
SM = SUBS * SUB
N_SUB_MAX = 2 * T // SUB + N_EXP
N_SUP_MAX = (N_SUB_MAX + N_EXP * (SUBS - 1)) // SUBS
YS_ROWS = (N_SUB_MAX + 4) * SUB
WIN_ALIGN = 16
WIN_HALF = TB // 2 + WIN_ALIGN
FIRST_STRIDE = 32
UNIT_STRIDE = 64
GATHER_BLOCKS = 4


def _route_kernel(lg_ref, g_ref, rank_ref, first_ref):
    lg = lg_ref[...]
    idx = lax.broadcasted_iota(jnp.int32, lg.shape, 0).astype(F32)
    none = float(N_EXP)
    m1 = jnp.max(lg, axis=0, keepdims=True)
    i1 = jnp.min(jnp.where(lg == m1, idx, none), axis=0, keepdims=True)
    rest = jnp.where(idx == i1, -jnp.inf, lg)
    m2 = jnp.max(rest, axis=0, keepdims=True)
    i2 = jnp.min(jnp.where(rest == m2, idx, none), axis=0, keepdims=True)
    e = jnp.exp(m2 - m1)
    w1 = 1.0 / (1.0 + e)
    w2 = e / (1.0 + e)
    g_ref[...] = jnp.where(idx == i1, w1, 0.0) + jnp.where(idx == i2, w2, 0.0)

    mask = jnp.where(idx == i1, 1.0, 0.0) + jnp.where(idx == i2, 1.0, 0.0)
    before = (lax.broadcasted_iota(jnp.int32, (TB, TB), 0) < lax.broadcasted_iota(jnp.int32, (TB, TB), 1))
    before = jnp.where(before, 1.0, 0.0).astype(BF16)
    lane = lax.broadcasted_iota(jnp.int32, (N_EXP, 128), 1)
    carry = jnp.zeros((N_EXP, 1), F32)
    first = jnp.zeros((N_EXP, 128), F32)
    for b in range(N_TB):
        mb = mask[:, b * TB:(b + 1) * TB]
        local = _dot(mb.astype(BF16), before)
        rank_ref[:, b * TB:(b + 1) * TB] = jnp.where(mb > 0.0, local + carry, -1.0)
        first = jnp.where(lane == b, carry, first)
        carry = carry + jnp.sum(mb, axis=1, keepdims=True)
    first_ref[...] = jnp.where(lane == N_TB, carry, first)


def _route(logits_t):
    return pl.pallas_call(
        _route_kernel,
        out_shape=[jax.ShapeDtypeStruct((N_EXP, T), F32), jax.ShapeDtypeStruct((N_EXP, T), F32),
                   jax.ShapeDtypeStruct((N_EXP, 128), F32)],
        compiler_params=pltpu.CompilerParams(vmem_limit_bytes=VMEM_LIMIT),
        name="route",
    )(logits_t)


def _moe_plan(first):
    first = first[:, :FIRST_STRIDE].astype(jnp.int32)
    cnt = first[:, N_TB]
    nt = (cnt + (SUB - 1)) // SUB
    off_end = jnp.cumsum(nt)
    off = off_end - nt
    nsub = off_end[-1]
    nsup = (nt + (SUBS - 1)) // SUBS
    sup_end = jnp.cumsum(nsup)
    sup_off = sup_end - nsup
    s = jnp.minimum(jnp.arange(N_SUP_MAX), sup_end[-1] - 1)
    valid = jnp.arange(N_SUP_MAX) < sup_end[-1]
    se = jnp.sum(s[:, None] >= sup_end[None, :], axis=1)
    mine = se[:, None] == jnp.arange(N_EXP)[None, :]
    pick = lambda v: jnp.sum(jnp.where(mine, v[None, :], 0), axis=1)
    sk0 = (s - pick(sup_off)) * SUBS
    sns = jnp.where(valid, jnp.clip(pick(nt) - sk0, 0, SUBS), 0)
    sj0 = pick(off) + sk0
    base = (jnp.arange(UNIT_STRIDE) * SUB)[None, :, None]
    blo = jnp.minimum(jnp.sum(first[:, None, 1:N_TB + 1] <= base, axis=2), N_TB - 1)
    end = jnp.minimum(base + SUB, cnt[:, None, None])
    bhi = jnp.maximum(jnp.sum(first[:, None, :N_TB] < end, axis=2) - 1, blo)
    ng = (bhi - blo) // GATHER_BLOCKS + 1
    start = SUB * off[:, None] + first[:, :N_TB]
    lead = start & (WIN_ALIGN - 1)
    wina = start - lead
    rel = first[:, :N_TB] - lead
    need = lead + (first[:, 1:N_TB + 1] - first[:, :N_TB]) > WIN_HALF
    winb = lax.cummax(jnp.where(need, wina + WIN_HALF, 0), axis=1)
    wide = jnp.any(need, axis=0)
    flat = lambda parts: jnp.concatenate([p.astype(jnp.int32).reshape(-1) for p in parts])
    gmm = dict(se=se, sk0=sk0, sns=sns, sj0=sj0, nsub=nsub, first=first, blo=blo, ng=ng)
    comb = dict(wina=wina.T, winb=winb.T, rel=rel.T, wide=wide)
    return flat([gmm[k] for k in _GMM_TAB]), flat([comb[k] for k in _COMB_TAB])


def _offsets(sizes):
    out, pos = {}, 0
    for name, n in sizes.items():
        out[name], pos = pos, pos + n
    return out


_GMM_TAB = _offsets(dict(se=N_SUP_MAX, sk0=N_SUP_MAX, sns=N_SUP_MAX, sj0=N_SUP_MAX, nsub=1,
                         first=N_EXP * FIRST_STRIDE, blo=N_EXP * UNIT_STRIDE, ng=N_EXP * UNIT_STRIDE))
_COMB_TAB = _offsets(dict(wina=N_TB * N_EXP, winb=N_TB * N_EXP, rel=N_TB * N_EXP, wide=N_TB))


class _Section:
    def __init__(self, ref, offset):
        self.ref, self.offset = ref, offset

    def __getitem__(self, i):
        return self.ref[self.offset + i]


def _moe_gmm_kernel(tab_ref, x_ref, rank_ref, gate_ref, wg_ref, wu_ref, wd_ref, ys_ref,
                    xs_ref, gs_ref, yacc_ref, wgb_ref, wub_ref, wdb_ref, sem):
    se_ref, sk0_ref, sns_ref, sj0_ref, nsub_ref, first_ref, blo_ref, ng_ref = (
        _Section(tab_ref, _GMM_TAB[k]) for k in ("se", "sk0", "sns", "sj0", "nsub", "first", "blo", "ng"))
    s, c = pl.program_id(0), pl.program_id(1)
    nc = pl.num_programs(1)
    e, k0, ns = se_ref[s], sk0_ref[s], sns_ref[s]

    def sub_rows(k):
        return pl.ds(pl.multiple_of(k * SUB, SUB), SUB)

    def out_copy(k, row0):
        dst = ys_ref.at[pl.ds(pl.multiple_of(row0 + k * SUB, SUB), SUB)]
        return pltpu.make_async_copy(xs_ref.at[sub_rows(k)], dst, sem.at[k])

    @pl.when((ns > 0) & (c == 0))
    def _gather():
        def group(k, g):
            slot = (lax.broadcasted_iota(jnp.int32, (SUB, 1), 0) + (k0 + k) * SUB).astype(F32)
            b0 = blo_ref[e * UNIT_STRIDE + k0 + k] + g * GATHER_BLOCKS
            t0 = pl.multiple_of(jnp.minimum(b0, N_TB - GATHER_BLOCKS) * TB, TB)
            lo = first_ref[e * FIRST_STRIDE + b0].astype(F32)
            cols = pl.ds(t0, GATHER_BLOCKS * TB)
            hit = rank_ref[pl.ds(e, 1), cols] == jnp.where(slot >= lo, slot, -2.0)
            rows = _dot(jnp.where(hit, 1.0, 0.0).astype(BF16), x_ref[cols, :])
            gate = jnp.sum(jnp.where(hit, gate_ref[pl.ds(e, 1), cols], 0.0), axis=-1, keepdims=True)
            return rows, gate

        def first(k):
            rows, gate = group(k, 0)
            xs_ref[sub_rows(k), :] = rows.astype(BF16)
            gs_ref[sub_rows(k), :] = gate
            yacc_ref[sub_rows(k), :] = jnp.zeros((SUB, D), F32)

        def more(k):
            def body(g, carry):
                rows, gate = group(k, g)
                xs_ref[sub_rows(k), :] = (xs_ref[sub_rows(k), :].astype(F32) + rows).astype(BF16)
                gs_ref[sub_rows(k), :] += gate
                return carry

            lax.fori_loop(1, ng_ref[e * UNIT_STRIDE + k0 + k], body, 0)

        def pair(p, carry):
            first(2 * p)
            first(2 * p + 1)
            more(2 * p)
            more(2 * p + 1)
            return carry

        lax.fori_loop(0, ns >> 1, pair, 0)

        @pl.when((ns & 1) != 0)
        def _():
            first(ns - 1)
            more(ns - 1)

    @pl.when(ns > 0)
    def _compute():
        row0 = sj0_ref[s] * SUB

        def swiglu(rows, wg, wu, wd):
            x = xs_ref[rows, :]
            g = _dot(x, wg)
            u = _dot(x, wu)
            yacc_ref[rows, :] += _dot((_silu(g) * u).astype(BF16), wd)

        def finish(first_sub, n):
            @pl.when(c == nc - 1)
            def _():
                for k in range(n):
                    rows = sub_rows(first_sub + k)
                    xs_ref[rows, :] = (yacc_ref[rows, :] * gs_ref[rows, :]).astype(BF16)
                    out_copy(first_sub + k, row0).start()

        def first_chain(rows):
            wg, wu, wd = wg_ref[0].astype(BF16), wu_ref[0].astype(BF16), wd_ref[0].astype(BF16)
            wgb_ref[...] = wg
            wub_ref[...] = wu
            wdb_ref[...] = wd
            swiglu(rows, wg, wu, wd)

        @pl.when(ns >= 4)
        def _():
            first_chain(pl.ds(0, 4 * SUB))
            finish(0, 4)

        @pl.when(ns < 4)
        def _():
            first_chain(pl.ds(0, SUB))
            finish(0, 1)

        done = jnp.where(ns >= 4, 4, 1)
        rest = ns - done

        def chain(first_sub, n):
            rows = pl.ds(pl.multiple_of(first_sub * SUB, SUB), n * SUB)
            swiglu(rows, wgb_ref[...], wub_ref[...], wdb_ref[...])
            finish(first_sub, n)

        def eight(k, carry):
            chain(done + 8 * k, 8)
            return carry

        lax.fori_loop(0, rest >> 3, eight, 0)
        done8 = done + (rest & ~7)
        for n in (4, 2, 1):
            @pl.when((rest & n) != 0)
            def _(n=n):
                chain(done8 + (rest & (7 & ~(2 * n - 1))), n)

    @pl.when((ns > 0) & (c == nc - 1))
    def _store_done():
        row0 = sj0_ref[s] * SUB

        def done(k, carry):
            out_copy(k, row0).wait()
            return carry

        lax.fori_loop(0, ns, done, 0)

    @pl.when((s == pl.num_programs(0) - 1) & (c == nc - 1))
    def _zero_tail():
        xs_ref[0:SUB, :] = jnp.zeros((SUB, D), BF16)
        nsub = nsub_ref[0]

        def fill(k, carry):
            cp = out_copy(0, (nsub + k) * SUB)
            cp.start()
            cp.wait()
            return carry

        lax.fori_loop(0, YS_ROWS // SUB - nsub, fill, 0)


def _moe_gmm(tab, h4, rank, gates, w_gu, w_down):
    nc = D_FFE // MOE_FC

    def expert(s, tab):
        return tab[_GMM_TAB["se"] + s]

    def chunk(s, c, tab):
        return jnp.where(tab[_GMM_TAB["sns"] + s] > 0, c, nc - 1)

    return pl.pallas_call(
        _moe_gmm_kernel,
        grid_spec=pltpu.PrefetchScalarGridSpec(
            num_scalar_prefetch=1,
            grid=(N_SUP_MAX, nc),
            in_specs=[
                pl.BlockSpec((T, D), lambda s, c, tab: (0, 0), pipeline_mode=pl.Buffered(1)),
                pl.BlockSpec((N_EXP, T), lambda s, c, tab: (0, 0)),
                pl.BlockSpec((N_EXP, T), lambda s, c, tab: (0, 0)),
                pl.BlockSpec((1, D, MOE_FC), lambda s, c, tab: (expert(s, tab), 0, chunk(s, c, tab))),
                pl.BlockSpec((1, D, MOE_FC), lambda s, c, tab: (expert(s, tab), 0, nc + chunk(s, c, tab))),
                pl.BlockSpec((1, MOE_FC, D), lambda s, c, tab: (expert(s, tab), chunk(s, c, tab), 0)),
            ],
            out_specs=pl.BlockSpec(memory_space=pl.ANY),
            scratch_shapes=[
                pltpu.VMEM((SM, D), BF16), pltpu.VMEM((SM, 1), F32), pltpu.VMEM((SM, D), F32),
                pltpu.VMEM((D, MOE_FC), BF16), pltpu.VMEM((D, MOE_FC), BF16), pltpu.VMEM((MOE_FC, D), BF16),
                pltpu.SemaphoreType.DMA((SUBS,)),
            ],
        ),
        out_shape=jax.ShapeDtypeStruct((YS_ROWS, D), BF16),
        compiler_params=_params("arbitrary", "arbitrary"),
        name="moe_gmm",
    )(tab, h4, rank, gates, w_gu, w_gu, w_down)


def _moe_combine_kernel(tab_ref, *refs):
    rel_ref, wide_ref = _Section(tab_ref, _COMB_TAB["rel"]), _Section(tab_ref, _COMB_TAB["wide"])
    ya, yb = refs[:N_EXP], refs[N_EXP:2 * N_EXP]
    rank_ref, x3_ref, mod_ref, yp_ref, ys_ref, ycat_ref, acc_ref = refs[2 * N_EXP:]
    b = pl.program_id(0)
    row = lax.broadcasted_iota(jnp.int32, (WIN_HALF, 1), 0)

    def onehot(e, first_row):
        slot = (row + (rel_ref[b * N_EXP + e] + first_row)).astype(F32)
        return jnp.where(rank_ref[e:e + 1, :] == slot, 1.0, 0.0).astype(BF16)

    def gather(y_refs, first_row, base):
        pieces = []
        for e in range(N_EXP):
            ycat_ref[base + e * WIN_HALF:base + (e + 1) * WIN_HALF, :] = y_refs[e][...]
            pieces.append(onehot(e, first_row))
        return pieces

    def combine(pieces, rows):
        return lax.dot_general(jnp.concatenate(pieces, axis=0), ycat_ref[0:rows, :], (((0,), (0,)), ((), ())),
                               preferred_element_type=F32)

    @pl.when(wide_ref[b] == 0)
    def _():
        acc_ref[...] = combine(gather(ya, 0, 0), N_EXP * WIN_HALF)

    @pl.when(wide_ref[b] != 0)
    def _():
        pieces = gather(ya, 0, 0) + gather(yb, WIN_HALF, N_EXP * WIN_HALF)
        acc_ref[...] = combine(pieces, 2 * N_EXP * WIN_HALF)

    out = x3_ref[...] + _mod_row(mod_ref, b * TB)[:, 5 * D:6 * D] * acc_ref[...]

    @pl.when(b < T_CTX // TB)
    def _():
        yp_ref[...] = out

    @pl.when(b >= T_CTX // TB)
    def _():
        ys_ref[...] = out


def _moe_combine(tab, ysorted, rank, x3, mods):
    ctx_blocks = T_CTX // TB

    def window(e, second):
        def index(b, tab):
            start = tab[_COMB_TAB["winb" if second else "wina"] + b * N_EXP + e]
            return pl.multiple_of(start, WIN_ALIGN), 0
        return pl.BlockSpec((pl.Element(WIN_HALF), pl.Element(D)), index)

    return pl.pallas_call(
        _moe_combine_kernel,
        grid_spec=pltpu.PrefetchScalarGridSpec(
            num_scalar_prefetch=1,
            grid=(N_TB,),
            in_specs=[window(e, False) for e in range(N_EXP)] + [window(e, True) for e in range(N_EXP)] + [
                pl.BlockSpec((N_EXP, TB), lambda b, *_: (0, b)),
                pl.BlockSpec((TB, D), lambda b, *_: (b, 0)),
                _mod_spec(1, 1),
            ],
            out_specs=[pl.BlockSpec((TB, D), lambda b, *_: (jnp.minimum(b, ctx_blocks - 1), 0)),
                       pl.BlockSpec((TB, D), lambda b, *_: (jnp.maximum(b - ctx_blocks, 0), 0))],
            scratch_shapes=[pltpu.VMEM((2 * N_EXP * WIN_HALF, D), BF16), pltpu.VMEM((TB, D), F32)],
        ),
        out_shape=[jax.ShapeDtypeStruct((T_CTX, D), F32), jax.ShapeDtypeStruct((T_LAT, D), F32)],
        compiler_params=_params("arbitrary"),
        name="moe_combine",
    )(tab, *([ysorted] * (2 * N_EXP)), rank, x3, mods)


def _pad_heads(w, width):
    lead = w.shape[:-1]
    w = w.reshape(*lead, HEADS, width)
    w = jnp.pad(w, [(0, 0)] * len(lead) + [(0, 0), (0, HEAD_PAD - width)])
    return w.reshape(*lead, HEADS * HEAD_PAD)


def kernel(x_prompt, x_sample, cache_ckv, cache_kpe, c, c_ctx, ada_w, ada_b, norm_mix, norm_ffn, w_in, q_a_norm,
           w_qb, kv_a_norm, w_kvb, q_norm, k_norm, w_sc, w_o, ffn_gu, ffn_down, conv_pw1, conv_pw1_b, conv_dw,
           conv_dw_b, conv_ln_g, conv_ln_b, conv_pw2, conv_pw2_b, router, moe_gu, moe_down):
    xp = x_prompt.reshape(T_CTX, D)
    xs = x_sample.reshape(T_LAT, D)

    mods = _adaln(c_ctx, c, ada_w, ada_b)

    wqb = _pad_heads(w_qb[0], QK_HEAD).astype(BF16)
    wkvb = w_kvb[0].astype(BF16)
    qn = jnp.pad(q_norm[0], (0, HEAD_PAD - QK_HEAD)).reshape(1, HEAD_PAD)
    kn = jnp.pad(k_norm[0], (0, HEAD_PAD - QK_HEAD)).reshape(1, HEAD_PAD)
    tabs = _rope_tables()

    w_in_t = jnp.swapaxes(w_in[0], 0, 1)
    q, ckv, kpe, sc, state_ckv, state_kpe = _even_proj(xp, xs, mods, norm_mix, w_in_t, q_a_norm, wqb, kv_a_norm,
                                                       qn, w_sc, tabs)

    lat_tile0 = T_CTX // TKV
    ident = LAT_LEN // TKV
    k, kv = _kv_proj(ckv, kpe, wkvb, kn, tabs,
                     lambda i: jnp.where(i < lat_tile0, ident, (i - lat_tile0) % ident), "kv_proj")
    cache_kpe_p = jnp.pad(cache_kpe[:, 0].reshape(N_LAT_SEQ * PAST, QK_ROPE), ((0, 0), (0, HEAD_PAD - QK_ROPE)))
    kc, kvc = _kv_proj(cache_ckv[:, 0].reshape(N_LAT_SEQ * PAST, KV_LORA), cache_kpe_p, wkvb, kn, tabs,
                       lambda i: ident, "kv_proj_cache")

    oc = _attn_ctx(q, k, kv)
    ol = _attn_lat(q, kc, kvc, k, kv)
    x2, h3 = _ffn(oc, ol, sc, xp, xs, w_o[0].astype(BF16), norm_ffn, ffn_gu[0], ffn_down[0], mods, norm_mix)

    x3, h4, logits_t = _conf(h3, x2, conv_pw1[0].astype(BF16), conv_pw1_b, conv_dw, conv_dw_b, conv_ln_g,
                             conv_ln_b, conv_pw2[0].astype(BF16), conv_pw2_b, mods, norm_ffn, router[0].T)
    gates, rank, first = _route(logits_t)
    gmm_tab, comb_tab = _moe_plan(first)
    ysorted = _moe_gmm(gmm_tab, h4, rank, gates, moe_gu[0], moe_down[0])
    yp, ys = _moe_combine(comb_tab, ysorted, rank, x3, mods)

    return (yp.reshape(N_CTX_SEQ, CTX_LEN, D), ys.reshape(N_LAT_SEQ, LAT_LEN, D),
            state_ckv.reshape(N_CTX_SEQ, 1, CTX_LEN, KV_LORA), state_kpe.reshape(N_CTX_SEQ, 1, CTX_LEN, QK_ROPE))
```

```python
import functools

import jax
import jax.numpy as jnp
import numpy as np
from jax import lax
from jax.experimental import pallas as pl
from jax.experimental.pallas import tpu as pltpu

F32 = jnp.float32
BF16 = jnp.bfloat16

D = 1024
N_CTX_SEQ, CTX_LEN = 16, 256
N_LAT_SEQ, LAT_LEN = 2, 1024
T_CTX = N_CTX_SEQ * CTX_LEN
T_LAT = N_LAT_SEQ * LAT_LEN
T = T_CTX + T_LAT
PAST = 256
GRID_W = 64
HEADS = 8
QK_NOPE, QK_ROPE, V_HEAD = 64, 32, 64
QK_HEAD = QK_NOPE + QK_ROPE
HEAD_PAD = 128
Q_LORA, KV_LORA = 256, 128
SC_W = 512
IN0_W = Q_LORA + KV_LORA + QK_ROPE + 3 * SC_W
CONF_K = 31
D_FF = 2816
N_EXP = 8
D_FFE = 3584
EPS = 1e-6
ROPE_THETA = 10000.0

TM = 1024
N_TILES = T // TM
CTX_TILES = T_CTX // TM
TKV = 512
TQ = 256
FFN_FC = 256
MOE_FC = 512
VMEM_LIMIT = 56 * 1024 * 1024


def _dot(a, b):
    return jnp.dot(a, b, preferred_element_type=F32)


def _dot_nt(a, b):
    return lax.dot_general(a, b, (((1,), (1,)), ((), ())), preferred_element_type=F32)


def _rms(x, g):
    return x * lax.rsqrt(jnp.mean(x * x, axis=-1, keepdims=True) + EPS) * g


def _silu(x):
    return x * jax.nn.sigmoid(x)


def _params(*sem):
    return pltpu.CompilerParams(dimension_semantics=sem, vmem_limit_bytes=VMEM_LIMIT)


def _mod_row(mod_ref, row0):
    cond = jnp.maximum(row0 - (T_CTX - LAT_LEN), 0) >> 10
    return mod_ref[0, pl.ds(cond, 1), :]


def _mod_spec(layer, ngrid):
    return pl.BlockSpec((1, 8, 6 * D), lambda *_: (layer, 0, 0))


def _adaln_kernel(cc_ref, c_ref, w_ref, b_ref, o_ref):
    l = pl.program_id(0)
    row = lax.broadcasted_iota(jnp.int32, (8, 1), 0)
    cond = jnp.where(row == 0, cc_ref[...], 0.0)
    for b in range(N_LAT_SEQ):
        cond = jnp.where(row == 1 + b, c_ref[b:b + 1, :], cond)
    o_ref[0] = _dot(_silu(cond).astype(BF16), w_ref[0].astype(BF16)) + b_ref[pl.ds(l, 1), :]


def _adaln(c_ctx, c, ada_w, ada_b):
    depth = ada_w.shape[0]
    tn = 2048
    return pl.pallas_call(
        _adaln_kernel,
        grid=(depth, 6 * D // tn),
        in_specs=[
            pl.BlockSpec((1, D), lambda l, j: (0, 0)),
            pl.BlockSpec((N_LAT_SEQ, D), lambda l, j: (0, 0)),
            pl.BlockSpec((1, D, tn), lambda l, j: (l, 0, j)),
            pl.BlockSpec((depth, tn), lambda l, j: (0, j)),
        ],
        out_specs=pl.BlockSpec((1, 8, tn), lambda l, j: (l, 0, j)),
        out_shape=jax.ShapeDtypeStruct((depth, 8, 6 * D), F32),
        compiler_params=_params("arbitrary", "arbitrary"),
        name="adaln",
    )(c_ctx.reshape(1, D), c, ada_w, ada_b)


def _rope_tables():
    half = QK_ROPE // 2
    nf = half // 2
    pos = np.arange(LAT_LEN)
    inv = ROPE_THETA ** (-np.arange(nf, dtype=np.float64) / nf)
    k = np.arange(QK_ROPE)
    part, idx = k // half, k % half
    p = np.where(part[None, :] == 0, (pos // GRID_W)[:, None], (pos % GRID_W)[:, None])
    ang = p * inv[idx % nf][None, :]
    cos, sin = np.cos(ang), np.sin(ang)
    first = (idx < nf)[None, :]
    s1 = np.where(first, -sin, 0.0)
    s2 = np.where(first, 0.0, sin)

    def place(t, fill):
        tab = np.full((2 * LAT_LEN, HEAD_PAD), fill, np.float32)
        tab[:LAT_LEN, QK_NOPE:QK_HEAD] = t
        return jnp.asarray(tab)

    return place(cos, 1.0), place(s1, 0.0), place(s2, 0.0)


def _rope(blk, cos, s1, s2):
    return blk * cos + pltpu.roll(blk, 8, 1) * s2 + pltpu.roll(blk, HEAD_PAD - 8, 1) * s1


def _head_norm(blk, g):
    ms = jnp.sum(blk * blk, axis=-1, keepdims=True) * (1.0 / QK_HEAD)
    return blk * lax.rsqrt(ms + EPS) * g


def _even_proj_kernel(xp_ref, xs_ref, mod_ref, nm_ref, win_ref, qan_ref, wqb_ref, kvan_ref,
                      qn_ref, wsc_ref, cos_ref, s1_ref, s2_ref,
                      q_ref, ckv_ref, kpe_ref, sc_ref, sckv_ref, skpe_ref, wt_ref):
    i = pl.program_id(0)
    n_a = Q_LORA + KV_LORA + QK_ROPE

    @pl.when(i == 0)
    def _():
        wt_ref[...] = win_ref[...].astype(BF16)

    x = jnp.where(i < CTX_TILES, xp_ref[...], xs_ref[...])
    m = _mod_row(mod_ref, i * TM)
    h = _rms(x, nm_ref[0:1, :]) * (1.0 + m[:, D:2 * D]) + m[:, 0:D]
    hb = h.astype(BF16)

    za = _dot_nt(hb, wt_ref[0:512, :])
    ckv = _rms(za[:, Q_LORA:Q_LORA + KV_LORA], kvan_ref[...])
    lane = lax.broadcasted_iota(jnp.int32, (1, HEAD_PAD), 1)
    kpe = jnp.where(lane < QK_ROPE, za[:, Q_LORA + KV_LORA:], 0.0)
    ckv_ref[...] = ckv
    kpe_ref[...] = kpe

    @pl.when(i < CTX_TILES)
    def _():
        sckv_ref[...] = ckv
        skpe_ref[...] = kpe[:, :QK_ROPE]

    qa = _rms(za[:, :Q_LORA], qan_ref[...]).astype(BF16)
    cos, s1, s2 = cos_ref[...], s1_ref[...], s2_ref[...]
    qn = qn_ref[...]
    scale = QK_HEAD ** -0.5
    for hp in range(HEADS // 2):
        qq = _dot(qa, wqb_ref[:, hp * 256:(hp + 1) * 256])
        for j in range(2):
            blk = _head_norm(qq[:, j * HEAD_PAD:(j + 1) * HEAD_PAD], qn)
            blk = _rope(blk, cos, s1, s2) * scale
            h0 = (2 * hp + j) * HEAD_PAD
            q_ref[:, h0:h0 + HEAD_PAD] = blk.astype(BF16)

    gb = _dot_nt(hb, wt_ref[n_a:n_a + SC_W, :])
    v = _dot_nt(hb, wt_ref[n_a + SC_W:n_a + 2 * SC_W, :]) * _dot_nt(hb, wt_ref[n_a + 2 * SC_W:n_a + 3 * SC_W, :])
    seq = jnp.where(i < CTX_TILES, CTX_LEN, LAT_LEN)
    r = lax.broadcasted_iota(jnp.int32, (TM, 1), 0) & (seq - 1)
    vp = jnp.where(r == 0, 0.0, pltpu.roll(v, 1, 0))
    vn = jnp.where(r == seq - 1, 0.0, pltpu.roll(v, TM - 1, 0))
    w = wsc_ref[0]
    y = w[0:1] * vp + w[1:2] * v + w[2:3] * vn
    sc_ref[...] = (gb * y).astype(BF16)


def _even_proj(xp, xs, mods, norm_mix, w_in, q_a_norm, wqb, kv_a_norm, qn, w_sc, tabs):
    full = lambda shape: pl.BlockSpec(shape, lambda i: (0,) * len(shape))
    tab = pl.BlockSpec((TM, HEAD_PAD), lambda i: (jnp.where(i < CTX_TILES, 1, 0), 0))
    row = lambda n: pl.BlockSpec((TM, n), lambda i: (i, 0))
    ctx_row = lambda n: pl.BlockSpec((TM, n), lambda i: (jnp.minimum(i, CTX_TILES - 1), 0))
    return pl.pallas_call(
        _even_proj_kernel,
        grid=(N_TILES,),
        in_specs=[
            ctx_row(D),
            pl.BlockSpec((TM, D), lambda i: (jnp.maximum(i - CTX_TILES, 0), 0)),
            _mod_spec(0, 1),
            full((2, D)),
            pl.BlockSpec((IN0_W, D), lambda i: (0, 0), pipeline_mode=pl.Buffered(1)),
            full((1, Q_LORA)),
            full((Q_LORA, HEADS * HEAD_PAD)), full((1, KV_LORA)), full((1, HEAD_PAD)),
            full((1, 3, SC_W)), tab, tab, tab,
        ],
        out_specs=[row(HEADS * HEAD_PAD), row(KV_LORA), row(HEAD_PAD), row(SC_W),
                   ctx_row(KV_LORA), ctx_row(QK_ROPE)],
        out_shape=[
            jax.ShapeDtypeStruct((T, HEADS * HEAD_PAD), BF16),
            jax.ShapeDtypeStruct((T, KV_LORA), F32),
            jax.ShapeDtypeStruct((T, HEAD_PAD), F32),
            jax.ShapeDtypeStruct((T, SC_W), BF16),
            jax.ShapeDtypeStruct((T_CTX, KV_LORA), F32),
            jax.ShapeDtypeStruct((T_CTX, QK_ROPE), F32),
        ],
        scratch_shapes=[pltpu.VMEM((IN0_W, D), BF16)],
        compiler_params=_params("arbitrary"),
        name="even_proj",
    )(xp, xs, mods, norm_mix, w_in, q_a_norm, wqb, kv_a_norm, qn, w_sc, *tabs)


def _kv_proj_kernel(ckv_ref, kpe_ref, wkvb_ref, kn_ref, cos_ref, s1_ref, s2_ref, k_ref, kv_ref):
    kv = _dot(ckv_ref[...].astype(BF16), wkvb_ref[...])
    kv_ref[...] = kv.astype(BF16)
    kpe = pltpu.roll(kpe_ref[...], QK_NOPE, 1)
    lane = lax.broadcasted_iota(jnp.int32, (1, HEAD_PAD), 1)
    kn = kn_ref[...]
    pe_sq = jnp.sum(kpe * kpe, axis=-1, keepdims=True)
    pe = _rope(kpe * kn, cos_ref[...], s1_ref[...], s2_ref[...])
    for h in range(HEADS):
        blk = kv[:, h * HEAD_PAD:(h + 1) * HEAD_PAD]
        nope = jnp.where(lane < QK_NOPE, blk, 0.0)
        ms = (jnp.sum(nope * nope, axis=-1, keepdims=True) + pe_sq) * (1.0 / QK_HEAD)
        k = jnp.where(lane < QK_NOPE, blk * kn, pe) * lax.rsqrt(ms + EPS)
        k_ref[:, h * HEAD_PAD:(h + 1) * HEAD_PAD] = k.astype(BF16)


def _kv_proj(ckv, kpe, wkvb, kn, tabs, tab_index, name):
    n = ckv.shape[0]
    full = lambda shape: pl.BlockSpec(shape, lambda i: (0,) * len(shape))
    tab = pl.BlockSpec((TKV, HEAD_PAD), lambda i: (tab_index(i), 0))
    row = lambda w: pl.BlockSpec((TKV, w), lambda i: (i, 0))
    return pl.pallas_call(
        _kv_proj_kernel,
        grid=(n // TKV,),
        in_specs=[row(KV_LORA), row(HEAD_PAD), full((KV_LORA, HEADS * HEAD_PAD)), full((1, HEAD_PAD)),
                  tab, tab, tab],
        out_specs=[row(HEADS * HEAD_PAD), row(HEADS * HEAD_PAD)],
        out_shape=[jax.ShapeDtypeStruct((n, HEADS * HEAD_PAD), BF16)] * 2,
        compiler_params=_params("arbitrary"),
        name=name,
    )(ckv, kpe, wkvb, kn, *tabs)


def _pair_out(o0, o1):
    lane = lax.broadcasted_iota(jnp.int32, (1, HEAD_PAD), 1)
    return jnp.where(lane < V_HEAD, pltpu.roll(o0, V_HEAD, 1), o1).astype(BF16)


CTX_SEQS = 4


def _attn_ctx_kernel(q_ref, k_ref, kv_ref, o_ref):
    for b in range(CTX_SEQS):
        rows = slice(b * CTX_LEN, (b + 1) * CTX_LEN)
        for hp in range(HEADS // 2):
            outs = []
            for j in range(2):
                lanes = slice((2 * hp + j) * HEAD_PAD, (2 * hp + j + 1) * HEAD_PAD)
                s = _dot_nt(q_ref[rows, lanes], k_ref[rows, lanes])
                p = jnp.exp(s - jnp.max(s, axis=-1, keepdims=True))
                l = jnp.sum(p, axis=-1, keepdims=True)
                outs.append(_dot(p.astype(BF16), kv_ref[rows, lanes]) / l)
            o_ref[rows, hp * HEAD_PAD:(hp + 1) * HEAD_PAD] = _pair_out(*outs)


def _attn_ctx(q, k, kv):
    blk = pl.BlockSpec((CTX_SEQS * CTX_LEN, HEADS * HEAD_PAD), lambda b: (b, 0))
    return pl.pallas_call(
        _attn_ctx_kernel,
        grid=(N_CTX_SEQ // CTX_SEQS,),
        in_specs=[blk, blk, blk],
        out_specs=pl.BlockSpec((CTX_SEQS * CTX_LEN, HEADS * V_HEAD), lambda b: (b, 0)),
        out_shape=jax.ShapeDtypeStruct((T_CTX, HEADS * V_HEAD), BF16),
        compiler_params=_params("arbitrary"),
        name="attn_ctx",
    )(q, k, kv)


LAT_HEADS = 8


def _attn_lat_kernel(q_ref, kc_ref, kvc_ref, kl_ref, kvl_ref, o_ref):
    for hp in range(LAT_HEADS // 2):
        outs = []
        for j in range(2):
            h0 = (2 * hp + j) * HEAD_PAD
            lanes = slice(h0, h0 + HEAD_PAD)
            q = q_ref[:, lanes]
            sc = _dot_nt(q, kc_ref[:, lanes])
            sl = _dot_nt(q, kl_ref[:, lanes])
            m = jnp.maximum(jnp.max(sc, axis=-1, keepdims=True), jnp.max(sl, axis=-1, keepdims=True))
            pc, pl_ = jnp.exp(sc - m), jnp.exp(sl - m)
            l = jnp.sum(pc, axis=-1, keepdims=True) + jnp.sum(pl_, axis=-1, keepdims=True)
            o = _dot(pc.astype(BF16), kvc_ref[:, lanes]) + _dot(pl_.astype(BF16), kvl_ref[:, lanes])
            outs.append(o / l)
        o_ref[:, hp * HEAD_PAD:(hp + 1) * HEAD_PAD] = _pair_out(*outs)


def _attn_lat(q, kc, kvc, k, kv):
    nq = LAT_LEN // TQ
    q0 = T_CTX // TQ
    kl0 = T_CTX // LAT_LEN
    width = LAT_HEADS * HEAD_PAD
    lat = pl.BlockSpec((LAT_LEN, width), lambda b, hg, t: (kl0 + b, hg))
    ctx = pl.BlockSpec((PAST, width), lambda b, hg, t: (b, hg))
    return pl.pallas_call(
        _attn_lat_kernel,
        grid=(N_LAT_SEQ, HEADS // LAT_HEADS, nq),
        in_specs=[pl.BlockSpec((TQ, width), lambda b, hg, t: (q0 + b * nq + t, hg)), ctx, ctx, lat, lat],
        out_specs=pl.BlockSpec((TQ, LAT_HEADS * V_HEAD), lambda b, hg, t: (b * nq + t, hg)),
        out_shape=jax.ShapeDtypeStruct((T_LAT, HEADS * V_HEAD), BF16),
        compiler_params=_params("arbitrary", "arbitrary", "arbitrary"),
        name="attn_lat",
    )(q, kc, kvc, k, kv)


FFN_NC = D_FF // FFN_FC
FFN_TM = 512
FFN_CTX_TILES = T_CTX // FFN_TM


def _ffn_kernel(oc_ref, ol_ref, sc_ref, xp_ref, xs_ref, wo_ref, nf_ref, wg_ref, wu_ref, wd_ref,
                mod0_ref, mod1_ref, nm_ref, x2_ref, h3_ref, wg_all, wu_all, wd_all, x1_ref, hs_ref, act_ref):
    t = pl.program_id(0)

    i = jnp.maximum(t - (FFN_NC - 1), 0)
    m0, m1 = _mod_row(mod0_ref, i * FFN_TM), _mod_row(mod1_ref, i * FFN_TM)

    def mixer():
        ctx = i < FFN_CTX_TILES
        attn = jnp.where(ctx, oc_ref[...], ol_ref[...])
        x = jnp.where(ctx, xp_ref[...], xs_ref[...])
        x1 = x + m0[:, 2 * D:3 * D] * _dot(jnp.concatenate([attn, sc_ref[...]], axis=1), wo_ref[...])
        h = (_rms(x1, nf_ref[0:1, :]) * (1.0 + m0[:, 4 * D:5 * D]) + m0[:, 3 * D:4 * D]).astype(BF16)
        return x1, h

    def up(h, c):
        return (_silu(_dot(h, wg_all[c])) * _dot(h, wu_all[c])).astype(BF16)

    def down(x1, act):
        x2 = x1 + m0[:, 5 * D:6 * D] * _dot(act, wd_all[...])
        x2_ref[...] = x2
        h3_ref[...] = (_rms(x2, nm_ref[1:2, :]) * (1.0 + m1[:, D:2 * D]) + m1[:, 0:D]).astype(BF16)

    @pl.when(t == 0)
    def _():
        x1, h = mixer()
        x1_ref[...] = x1
        hs_ref[...] = h

    @pl.when(t < FFN_NC)
    def _stage():
        wg_all[t] = wg_ref[...].astype(BF16)
        wu_all[t] = wu_ref[...].astype(BF16)
        wd_all[pl.ds(pl.multiple_of(t * FFN_FC, FFN_FC), FFN_FC), :] = wd_ref[...].astype(BF16)
        act_ref[t] = up(hs_ref[...], t)

    @pl.when(t == FFN_NC - 1)
    def _():
        down(x1_ref[...], jnp.concatenate([act_ref[c] for c in range(FFN_NC)], axis=1))

    @pl.when(t > FFN_NC - 1)
    def _tile():
        x1, h = mixer()
        down(x1, jnp.concatenate([up(h, c) for c in range(FFN_NC)], axis=1))


def _ffn(oc, ol, sc, xp, xs, wo, norm_ffn, w_gu, w_down, mods, norm_mix):
    chunk = lambda t: jnp.minimum(t, FFN_NC - 1)
    tile = lambda t: jnp.maximum(t - (FFN_NC - 1), 0)
    full = lambda shape: pl.BlockSpec(shape, lambda t: (0,) * len(shape))
    row = lambda n: pl.BlockSpec((FFN_TM, n), lambda t: (tile(t), 0))
    first = lambda n: pl.BlockSpec((FFN_TM, n), lambda t: (jnp.minimum(tile(t), FFN_CTX_TILES - 1), 0))
    second = lambda n: pl.BlockSpec((FFN_TM, n), lambda t: (jnp.maximum(tile(t) - FFN_CTX_TILES, 0), 0))
    return pl.pallas_call(
        _ffn_kernel,
        grid=(FFN_NC - 1 + T // FFN_TM,),
        in_specs=[first(HEADS * V_HEAD), second(HEADS * V_HEAD), row(SC_W), first(D), second(D),
                  full((HEADS * V_HEAD + SC_W, D)), full((2, D)),
                  pl.BlockSpec((D, FFN_FC), lambda t: (0, chunk(t))),
                  pl.BlockSpec((D, FFN_FC), lambda t: (0, FFN_NC + chunk(t))),
                  pl.BlockSpec((FFN_FC, D), lambda t: (chunk(t), 0)),
                  _mod_spec(0, 1), _mod_spec(1, 1), full((2, D))],
        out_specs=[row(D), row(D)],
        out_shape=[jax.ShapeDtypeStruct((T, D), F32), jax.ShapeDtypeStruct((T, D), BF16)],
        scratch_shapes=[pltpu.VMEM((FFN_NC, D, FFN_FC), BF16), pltpu.VMEM((FFN_NC, D, FFN_FC), BF16),
                        pltpu.VMEM((D_FF, D), BF16), pltpu.VMEM((FFN_TM, D), F32), pltpu.VMEM((FFN_TM, D), BF16),
                        pltpu.VMEM((FFN_NC, FFN_TM, FFN_FC), BF16)],
        compiler_params=_params("arbitrary"),
        name="ffn_dense",
    )(oc, ol, sc, xp, xs, wo, norm_ffn, w_gu, w_gu, w_down, mods, mods, norm_mix)


CONF_CB = 256
CONF_SEG = 256
CONF_HALO = 16
CONF_SEGP = CONF_SEG + 2 * CONF_HALO
CONF_PIECE = 256


def _conf_kernel(h_ref, x2_ref, w1_ref, b1_ref, wdw_ref, bdw_ref, lng_ref, lnb_ref, w2_ref, b2_ref,
                 mod_ref, nf_ref, rt_ref, x3_ref, h4_ref, lg_ref, pad_ref, conv_ref):
    i = pl.program_id(0)
    nseg = TM // CONF_SEG
    h = h_ref[...]
    joined = jnp.where(i < CTX_TILES, 0.0, 1.0)
    zeros_halo = jnp.zeros((CONF_HALO, CONF_CB), F32)
    for cb in range(D // CONF_CB):
        c0 = cb * CONF_CB
        a = _dot(h, w1_ref[:, c0:c0 + CONF_CB]) + b1_ref[:, c0:c0 + CONF_CB]
        g = _dot(h, w1_ref[:, D + c0:D + c0 + CONF_CB]) + b1_ref[:, D + c0:D + c0 + CONF_CB]
        u = a * jax.nn.sigmoid(g)
        for s in range(nseg):
            base = s * CONF_SEGP
            top = u[s * CONF_SEG - CONF_HALO:s * CONF_SEG] * joined if s > 0 else zeros_halo
            bot = (u[(s + 1) * CONF_SEG:(s + 1) * CONF_SEG + CONF_HALO] * joined
                   if s < nseg - 1 else zeros_halo)
            pad_ref[0, base:base + CONF_HALO, :] = top
            pad_ref[0, base + CONF_HALO:base + CONF_HALO + CONF_SEG, :] = u[s * CONF_SEG:(s + 1) * CONF_SEG]
            pad_ref[0, base + CONF_HALO + CONF_SEG:base + CONF_SEGP, :] = bot

        p0 = pad_ref[0]
        rows = nseg * CONF_SEGP
        for b in range(1, 8):
            pad_ref[b] = pltpu.roll(p0, rows - b, 0)

        def piece(t, carry):
            s = t // (CONF_SEG // CONF_PIECE)
            q0 = (t % (CONF_SEG // CONF_PIECE)) * CONF_PIECE
            src = pl.multiple_of(s * CONF_SEGP + q0, 8)
            acc = jnp.zeros((CONF_PIECE, CONF_CB), F32)
            for j in range(CONF_K):
                hi, lo = (j + 1) // 8, (j + 1) % 8
                acc = acc + wdw_ref[0, j:j + 1, c0:c0 + CONF_CB] * pad_ref[lo, pl.ds(src + 8 * hi, CONF_PIECE), :]
            dst = pl.multiple_of(s * CONF_SEG + q0, 8)
            conv_ref[pl.ds(dst, CONF_PIECE), c0:c0 + CONF_CB] = acc + bdw_ref[:, c0:c0 + CONF_CB]
            return carry

        lax.fori_loop(0, TM // CONF_PIECE, piece, 0)

    m = _mod_row(mod_ref, i * TM)
    half = TM // 2
    for r0 in (0, half):
        rows = slice(r0, r0 + half)
        y = conv_ref[rows, :]
        mu = jnp.mean(y, axis=-1, keepdims=True)
        yc = y - mu
        var = jnp.mean(yc * yc, axis=-1, keepdims=True)
        y = _silu(yc * lax.rsqrt(var + EPS) * lng_ref[...] + lnb_ref[...])
        out = _dot(y.astype(BF16), w2_ref[...]) + b2_ref[...]
        x3 = x2_ref[rows, :] + m[:, 2 * D:3 * D] * out
        x3_ref[rows, :] = x3
        h4 = _rms(x3, nf_ref[1:2, :]) * (1.0 + m[:, 4 * D:5 * D]) + m[:, 3 * D:4 * D]
        h4_ref[rows, :] = h4.astype(BF16)
        lg_ref[:, rows] = lax.dot_general(rt_ref[...], h4, (((1,), (1,)), ((), ())),
                                          precision=lax.Precision.HIGHEST, preferred_element_type=F32)


def _conf(h3, x2, w1, b1, wdw, bdw, lng, lnb, w2, b2, mods, norm_ffn1, router_t):
    full = lambda shape: pl.BlockSpec(shape, lambda i: (0,) * len(shape))
    row = lambda n: pl.BlockSpec((TM, n), lambda i: (i, 0))
    return pl.pallas_call(
        _conf_kernel,
        grid=(N_TILES,),
        in_specs=[row(D), row(D), full((D, 2 * D)), full((1, 2 * D)), full((1, CONF_K, D)), full((1, D)),
                  full((1, D)), full((1, D)), full((D, D)), full((1, D)),
                  _mod_spec(1, 1), full((2, D)), full((N_EXP, D))],
        out_specs=[row(D), row(D), pl.BlockSpec((N_EXP, TM), lambda i: (0, i))],
        out_shape=[jax.ShapeDtypeStruct((T, D), F32), jax.ShapeDtypeStruct((T, D), BF16),
                   jax.ShapeDtypeStruct((N_EXP, T), F32)],
        scratch_shapes=[pltpu.VMEM((8, (TM // CONF_SEG) * CONF_SEGP, CONF_CB), F32),
                        pltpu.VMEM((TM, D), F32)],
        compiler_params=_params("arbitrary"),
        name="conformer_conv",
    )(h3, x2, w1, b1, wdw, bdw, lng, lnb, w2, b2, mods, norm_ffn1, router_t)


TB = 256
N_TB = T // TB
SUB = 128
SUBS = 24
SM = SUBS * SUB
N_SUB_MAX = 2 * T // SUB + N_EXP
N_SUP_MAX = (N_SUB_MAX + N_EXP * (SUBS - 1)) // SUBS
YS_ROWS = (N_SUB_MAX + 4) * SUB
WIN_ALIGN = 16
WIN_HALF = TB // 2 + WIN_ALIGN
FIRST_STRIDE = 32
UNIT_STRIDE = 64
GATHER_BLOCKS = 4


def _route_kernel(lg_ref, g_ref, rank_ref, first_ref):
    lg = lg_ref[...]
    idx = lax.broadcasted_iota(jnp.int32, lg.shape, 0).astype(F32)
    none = float(N_EXP)
    m1 = jnp.max(lg, axis=0, keepdims=True)
    i1 = jnp.min(jnp.where(lg == m1, idx, none), axis=0, keepdims=True)
    rest = jnp.where(idx == i1, -jnp.inf, lg)
    m2 = jnp.max(rest, axis=0, keepdims=True)
    i2 = jnp.min(jnp.where(rest == m2, idx, none), axis=0, keepdims=True)
    e = jnp.exp(m2 - m1)
    w1 = 1.0 / (1.0 + e)
    w2 = e / (1.0 + e)
    g_ref[...] = jnp.where(idx == i1, w1, 0.0) + jnp.where(idx == i2, w2, 0.0)

    mask = jnp.where(idx == i1, 1.0, 0.0) + jnp.where(idx == i2, 1.0, 0.0)
    before = (lax.broadcasted_iota(jnp.int32, (TB, TB), 0) < lax.broadcasted_iota(jnp.int32, (TB, TB), 1))
    before = jnp.where(before, 1.0, 0.0).astype(BF16)
    lane = lax.broadcasted_iota(jnp.int32, (N_EXP, 128), 1)
    carry = jnp.zeros((N_EXP, 1), F32)
    first = jnp.zeros((N_EXP, 128), F32)
    for b in range(N_TB):
        mb = mask[:, b * TB:(b + 1) * TB]
        local = _dot(mb.astype(BF16), before)
        rank_ref[:, b * TB:(b + 1) * TB] = jnp.where(mb > 0.0, local + carry, -1.0)
        first = jnp.where(lane == b, carry, first)
        carry = carry + jnp.sum(mb, axis=1, keepdims=True)
    first_ref[...] = jnp.where(lane == N_TB, carry, first)


def _route(logits_t):
    return pl.pallas_call(
        _route_kernel,
        out_shape=[jax.ShapeDtypeStruct((N_EXP, T), F32), jax.ShapeDtypeStruct((N_EXP, T), F32),
                   jax.ShapeDtypeStruct((N_EXP, 128), F32)],
        compiler_params=pltpu.CompilerParams(vmem_limit_bytes=VMEM_LIMIT),
        name="route",
    )(logits_t)


def _moe_plan(first):
    first = first[:, :FIRST_STRIDE].astype(jnp.int32)
    cnt = first[:, N_TB]
    nt = (cnt + (SUB - 1)) // SUB
    off_end = jnp.cumsum(nt)
    off = off_end - nt
    nsub = off_end[-1]
    nsup = (nt + (SUBS - 1)) // SUBS
    sup_end = jnp.cumsum(nsup)
    sup_off = sup_end - nsup
    s = jnp.minimum(jnp.arange(N_SUP_MAX), sup_end[-1] - 1)
    valid = jnp.arange(N_SUP_MAX) < sup_end[-1]
    se = jnp.sum(s[:, None] >= sup_end[None, :], axis=1)
    mine = se[:, None] == jnp.arange(N_EXP)[None, :]
    pick = lambda v: jnp.sum(jnp.where(mine, v[None, :], 0), axis=1)
    sk0 = (s - pick(sup_off)) * SUBS
    sns = jnp.where(valid, jnp.clip(pick(nt) - sk0, 0, SUBS), 0)
    sj0 = pick(off) + sk0
    base = (jnp.arange(UNIT_STRIDE) * SUB)[None, :, None]
    blo = jnp.minimum(jnp.sum(first[:, None, 1:N_TB + 1] <= base, axis=2), N_TB - 1)
    end = jnp.minimum(base + SUB, cnt[:, None, None])
    bhi = jnp.maximum(jnp.sum(first[:, None, :N_TB] < end, axis=2) - 1, blo)
    ng = (bhi - blo) // GATHER_BLOCKS + 1
    start = SUB * off[:, None] + first[:, :N_TB]
    lead = start & (WIN_ALIGN - 1)
    wina = start - lead
    rel = first[:, :N_TB] - lead
    need = lead + (first[:, 1:N_TB + 1] - first[:, :N_TB]) > WIN_HALF
    winb = lax.cummax(jnp.where(need, wina + WIN_HALF, 0), axis=1)
    wide = jnp.any(need, axis=0)
    flat = lambda parts: jnp.concatenate([p.astype(jnp.int32).reshape(-1) for p in parts])
    gmm = dict(se=se, sk0=sk0, sns=sns, sj0=sj0, nsub=nsub, first=first, blo=blo, ng=ng)
    comb = dict(wina=wina.T, winb=winb.T, rel=rel.T, wide=wide)
    return flat([gmm[k] for k in _GMM_TAB]), flat([comb[k] for k in _COMB_TAB])


def _offsets(sizes):
    out, pos = {}, 0
    for name, n in sizes.items():
        out[name], pos = pos, pos + n
    return out


_GMM_TAB = _offsets(dict(se=N_SUP_MAX, sk0=N_SUP_MAX, sns=N_SUP_MAX, sj0=N_SUP_MAX, nsub=1,
                         first=N_EXP * FIRST_STRIDE, blo=N_EXP * UNIT_STRIDE, ng=N_EXP * UNIT_STRIDE))
_COMB_TAB = _offsets(dict(wina=N_TB * N_EXP, winb=N_TB * N_EXP, rel=N_TB * N_EXP, wide=N_TB))


class _Section:
    def __init__(self, ref, offset):
        self.ref, self.offset = ref, offset

    def __getitem__(self, i):
        return self.ref[self.offset + i]


def _moe_gmm_kernel(tab_ref, x_ref, rank_ref, gate_ref, wg_ref, wu_ref, wd_ref, ys_ref,
                    xs_ref, gs_ref, yacc_ref, wgb_ref, wub_ref, wdb_ref, sem):
    se_ref, sk0_ref, sns_ref, sj0_ref, nsub_ref, first_ref, blo_ref, ng_ref = (
        _Section(tab_ref, _GMM_TAB[k]) for k in ("se", "sk0", "sns", "sj0", "nsub", "first", "blo", "ng"))
    s, c = pl.program_id(0), pl.program_id(1)
    nc = pl.num_programs(1)
    e, k0, ns = se_ref[s], sk0_ref[s], sns_ref[s]

    def sub_rows(k):
        return pl.ds(pl.multiple_of(k * SUB, SUB), SUB)

    def out_copy(k, row0):
        dst = ys_ref.at[pl.ds(pl.multiple_of(row0 + k * SUB, SUB), SUB)]
        return pltpu.make_async_copy(xs_ref.at[sub_rows(k)], dst, sem.at[k])

    @pl.when((ns > 0) & (c == 0))
    def _gather():
        def group(k, g):
            slot = (lax.broadcasted_iota(jnp.int32, (SUB, 1), 0) + (k0 + k) * SUB).astype(F32)
            b0 = blo_ref[e * UNIT_STRIDE + k0 + k] + g * GATHER_BLOCKS
            t0 = pl.multiple_of(jnp.minimum(b0, N_TB - GATHER_BLOCKS) * TB, TB)
            lo = first_ref[e * FIRST_STRIDE + b0].astype(F32)
            cols = pl.ds(t0, GATHER_BLOCKS * TB)
            hit = rank_ref[pl.ds(e, 1), cols] == jnp.where(slot >= lo, slot, -2.0)
            rows = _dot(jnp.where(hit, 1.0, 0.0).astype(BF16), x_ref[cols, :])
            gate = jnp.sum(jnp.where(hit, gate_ref[pl.ds(e, 1), cols], 0.0), axis=-1, keepdims=True)
            return rows, gate

        def first(k):
            rows, gate = group(k, 0)
            xs_ref[sub_rows(k), :] = rows.astype(BF16)
            gs_ref[sub_rows(k), :] = gate
            yacc_ref[sub_rows(k), :] = jnp.zeros((SUB, D), F32)

        def more(k):
            def body(g, carry):
                rows, gate = group(k, g)
                xs_ref[sub_rows(k), :] = (xs_ref[sub_rows(k), :].astype(F32) + rows).astype(BF16)
                gs_ref[sub_rows(k), :] += gate
                return carry

            lax.fori_loop(1, ng_ref[e * UNIT_STRIDE + k0 + k], body, 0)

        def pair(p, carry):
            first(2 * p)
            first(2 * p + 1)
            more(2 * p)
            more(2 * p + 1)
            return carry

        lax.fori_loop(0, ns >> 1, pair, 0)

        @pl.when((ns & 1) != 0)
        def _():
            first(ns - 1)
            more(ns - 1)

    @pl.when(ns > 0)
    def _compute():
        row0 = sj0_ref[s] * SUB

        def swiglu(rows, wg, wu, wd):
            x = xs_ref[rows, :]
            g = _dot(x, wg)
            u = _dot(x, wu)
            yacc_ref[rows, :] += _dot((_silu(g) * u).astype(BF16), wd)

        def finish(first_sub, n):
            @pl.when(c == nc - 1)
            def _():
                for k in range(n):
                    rows = sub_rows(first_sub + k)
                    xs_ref[rows, :] = (yacc_ref[rows, :] * gs_ref[rows, :]).astype(BF16)
                    out_copy(first_sub + k, row0).start()

        def first_chain(rows):
            wg, wu, wd = wg_ref[0].astype(BF16), wu_ref[0].astype(BF16), wd_ref[0].astype(BF16)
            wgb_ref[...] = wg
            wub_ref[...] = wu
            wdb_ref[...] = wd
            swiglu(rows, wg, wu, wd)

        @pl.when(ns >= 4)
        def _():
            first_chain(pl.ds(0, 4 * SUB))
            finish(0, 4)

        @pl.when(ns < 4)
        def _():
            first_chain(pl.ds(0, SUB))
            finish(0, 1)

        done = jnp.where(ns >= 4, 4, 1)
        rest = ns - done

        def chain(first_sub, n):
            rows = pl.ds(pl.multiple_of(first_sub * SUB, SUB), n * SUB)
            swiglu(rows, wgb_ref[...], wub_ref[...], wdb_ref[...])
            finish(first_sub, n)

        def eight(k, carry):
            chain(done + 8 * k, 8)
            return carry

        lax.fori_loop(0, rest >> 3, eight, 0)
        done8 = done + (rest & ~7)
        for n in (4, 2, 1):
            @pl.when((rest & n) != 0)
            def _(n=n):
                chain(done8 + (rest & (7 & ~(2 * n - 1))), n)

    @pl.when((ns > 0) & (c == nc - 1))
    def _store_done():
        row0 = sj0_ref[s] * SUB

        def done(k, carry):
            out_copy(k, row0).wait()
            return carry

        lax.fori_loop(0, ns, done, 0)

    @pl.when((s == pl.num_programs(0) - 1) & (c == nc - 1))
    def _zero_tail():
        xs_ref[0:SUB, :] = jnp.zeros((SUB, D), BF16)
        nsub = nsub_ref[0]

        def fill(k, carry):
            cp = out_copy(0, (nsub + k) * SUB)
            cp.start()
            cp.wait()
            return carry

        lax.fori_loop(0, YS_ROWS // SUB - nsub, fill, 0)


def _moe_gmm(tab, h4, rank, gates, w_gu, w_down):
    nc = D_FFE // MOE_FC

    def expert(s, tab):
        return tab[_GMM_TAB["se"] + s]

    def chunk(s, c, tab):
        return jnp.where(tab[_GMM_TAB["sns"] + s] > 0, c, nc - 1)

    return pl.pallas_call(
        _moe_gmm_kernel,
        grid_spec=pltpu.PrefetchScalarGridSpec(
            num_scalar_prefetch=1,
            grid=(N_SUP_MAX, nc),
            in_specs=[
                pl.BlockSpec((T, D), lambda s, c, tab: (0, 0), pipeline_mode=pl.Buffered(1)),
                pl.BlockSpec((N_EXP, T), lambda s, c, tab: (0, 0)),
                pl.BlockSpec((N_EXP, T), lambda s, c, tab: (0, 0)),
                pl.BlockSpec((1, D, MOE_FC), lambda s, c, tab: (expert(s, tab), 0, chunk(s, c, tab))),
                pl.BlockSpec((1, D, MOE_FC), lambda s, c, tab: (expert(s, tab), 0, nc + chunk(s, c, tab))),
                pl.BlockSpec((1, MOE_FC, D), lambda s, c, tab: (expert(s, tab), chunk(s, c, tab), 0)),
            ],
            out_specs=pl.BlockSpec(memory_space=pl.ANY),
            scratch_shapes=[
                pltpu.VMEM((SM, D), BF16), pltpu.VMEM((SM, 1), F32), pltpu.VMEM((SM, D), F32),
                pltpu.VMEM((D, MOE_FC), BF16), pltpu.VMEM((D, MOE_FC), BF16), pltpu.VMEM((MOE_FC, D), BF16),
                pltpu.SemaphoreType.DMA((SUBS,)),
            ],
        ),
        out_shape=jax.ShapeDtypeStruct((YS_ROWS, D), BF16),
        compiler_params=_params("arbitrary", "arbitrary"),
        name="moe_gmm",
    )(tab, h4, rank, gates, w_gu, w_gu, w_down)


def _moe_combine_kernel(tab_ref, *refs):
    rel_ref, wide_ref = _Section(tab_ref, _COMB_TAB["rel"]), _Section(tab_ref, _COMB_TAB["wide"])
    ya, yb = refs[:N_EXP], refs[N_EXP:2 * N_EXP]
    rank_ref, x3_ref, mod_ref, yp_ref, ys_ref, ycat_ref, acc_ref = refs[2 * N_EXP:]
    b = pl.program_id(0)
    row = lax.broadcasted_iota(jnp.int32, (WIN_HALF, 1), 0)

    def onehot(e, first_row):
        slot = (row + (rel_ref[b * N_EXP + e] + first_row)).astype(F32)
        return jnp.where(rank_ref[e:e + 1, :] == slot, 1.0, 0.0).astype(BF16)

    def gather(y_refs, first_row, base):
        pieces = []
        for e in range(N_EXP):
            ycat_ref[base + e * WIN_HALF:base + (e + 1) * WIN_HALF, :] = y_refs[e][...]
            pieces.append(onehot(e, first_row))
        return pieces

    def combine(pieces, rows):
        return lax.dot_general(jnp.concatenate(pieces, axis=0), ycat_ref[0:rows, :], (((0,), (0,)), ((), ())),
                               preferred_element_type=F32)

    @pl.when(wide_ref[b] == 0)
    def _():
        acc_ref[...] = combine(gather(ya, 0, 0), N_EXP * WIN_HALF)

    @pl.when(wide_ref[b] != 0)
    def _():
        pieces = gather(ya, 0, 0) + gather(yb, WIN_HALF, N_EXP * WIN_HALF)
        acc_ref[...] = combine(pieces, 2 * N_EXP * WIN_HALF)

    out = x3_ref[...] + _mod_row(mod_ref, b * TB)[:, 5 * D:6 * D] * acc_ref[...]

    @pl.when(b < T_CTX // TB)
    def _():
        yp_ref[...] = out

    @pl.when(b >= T_CTX // TB)
    def _():
        ys_ref[...] = out


def _moe_combine(tab, ysorted, rank, x3, mods):
    ctx_blocks = T_CTX // TB

    def window(e, second):
        def index(b, tab):
            start = tab[_COMB_TAB["winb" if second else "wina"] + b * N_EXP + e]
            return pl.multiple_of(start, WIN_ALIGN), 0
        return pl.BlockSpec((pl.Element(WIN_HALF), pl.Element(D)), index)

    return pl.pallas_call(
        _moe_combine_kernel,
        grid_spec=pltpu.PrefetchScalarGridSpec(
            num_scalar_prefetch=1,
            grid=(N_TB,),
            in_specs=[window(e, False) for e in range(N_EXP)] + [window(e, True) for e in range(N_EXP)] + [
                pl.BlockSpec((N_EXP, TB), lambda b, *_: (0, b)),
                pl.BlockSpec((TB, D), lambda b, *_: (b, 0)),
                _mod_spec(1, 1),
            ],
            out_specs=[pl.BlockSpec((TB, D), lambda b, *_: (jnp.minimum(b, ctx_blocks - 1), 0)),
                       pl.BlockSpec((TB, D), lambda b, *_: (jnp.maximum(b - ctx_blocks, 0), 0))],
            scratch_shapes=[pltpu.VMEM((2 * N_EXP * WIN_HALF, D), BF16), pltpu.VMEM((TB, D), F32)],
        ),
        out_shape=[jax.ShapeDtypeStruct((T_CTX, D), F32), jax.ShapeDtypeStruct((T_LAT, D), F32)],
        compiler_params=_params("arbitrary"),
        name="moe_combine",
    )(tab, *([ysorted] * (2 * N_EXP)), rank, x3, mods)


def _pad_heads(w, width):
    lead = w.shape[:-1]
    w = w.reshape(*lead, HEADS, width)
    w = jnp.pad(w, [(0, 0)] * len(lead) + [(0, 0), (0, HEAD_PAD - width)])
    return w.reshape(*lead, HEADS * HEAD_PAD)


def kernel(x_prompt, x_sample, cache_ckv, cache_kpe, c, c_ctx, ada_w, ada_b, norm_mix, norm_ffn, w_in, q_a_norm,
           w_qb, kv_a_norm, w_kvb, q_norm, k_norm, w_sc, w_o, ffn_gu, ffn_down, conv_pw1, conv_pw1_b, conv_dw,
           conv_dw_b, conv_ln_g, conv_ln_b, conv_pw2, conv_pw2_b, router, moe_gu, moe_down):
    xp = x_prompt.reshape(T_CTX, D)
    xs = x_sample.reshape(T_LAT, D)

    mods = _adaln(c_ctx, c, ada_w, ada_b)

    wqb = _pad_heads(w_qb[0], QK_HEAD).astype(BF16)
    wkvb = w_kvb[0].astype(BF16)
    qn = jnp.pad(q_norm[0], (0, HEAD_PAD - QK_HEAD)).reshape(1, HEAD_PAD)
    kn = jnp.pad(k_norm[0], (0, HEAD_PAD - QK_HEAD)).reshape(1, HEAD_PAD)
    tabs = _rope_tables()

    w_in_t = jnp.swapaxes(w_in[0], 0, 1)
    q, ckv, kpe, sc, state_ckv, state_kpe = _even_proj(xp, xs, mods, norm_mix, w_in_t, q_a_norm, wqb, kv_a_norm,
                                                       qn, w_sc, tabs)

    lat_tile0 = T_CTX // TKV
    ident = LAT_LEN // TKV
    k, kv = _kv_proj(ckv, kpe, wkvb, kn, tabs,
                     lambda i: jnp.where(i < lat_tile0, ident, (i - lat_tile0) % ident), "kv_proj")
    cache_kpe_p = jnp.pad(cache_kpe[:, 0].reshape(N_LAT_SEQ * PAST, QK_ROPE), ((0, 0), (0, HEAD_PAD - QK_ROPE)))
    kc, kvc = _kv_proj(cache_ckv[:, 0].reshape(N_LAT_SEQ * PAST, KV_LORA), cache_kpe_p, wkvb, kn, tabs,
                       lambda i: ident, "kv_proj_cache")

    oc = _attn_ctx(q, k, kv)
    ol = _attn_lat(q, kc, kvc, k, kv)
    x2, h3 = _ffn(oc, ol, sc, xp, xs, w_o[0].astype(BF16), norm_ffn, ffn_gu[0], ffn_down[0], mods, norm_mix)

    x3, h4, logits_t = _conf(h3, x2, conv_pw1[0].astype(BF16), conv_pw1_b, conv_dw, conv_dw_b, conv_ln_g,
                             conv_ln_b, conv_pw2[0].astype(BF16), conv_pw2_b, mods, norm_ffn, router[0].T)
    gates, rank, first = _route(logits_t)
    gmm_tab, comb_tab = _moe_plan(first)
    ysorted = _moe_gmm(gmm_tab, h4, rank, gates, moe_gu[0], moe_down[0])
    yp, ys = _moe_combine(comb_tab, ysorted, rank, x3, mods)

    return (yp.reshape(N_CTX_SEQ, CTX_LEN, D), ys.reshape(N_LAT_SEQ, LAT_LEN, D),
            state_ckv.reshape(N_CTX_SEQ, 1, CTX_LEN, KV_LORA), state_kpe.reshape(N_CTX_SEQ, 1, CTX_LEN, QK_ROPE))
```

```python
import functools

import jax
import jax.numpy as jnp
import numpy as np
from jax import lax
from jax.experimental import pallas as pl
from jax.experimental.pallas import tpu as pltpu

F32 = jnp.float32
BF16 = jnp.bfloat16

D = 1024
N_CTX_SEQ, CTX_LEN = 16, 256
N_LAT_SEQ, LAT_LEN = 2, 1024
T_CTX = N_CTX_SEQ * CTX_LEN
T_LAT = N_LAT_SEQ * LAT_LEN
T = T_CTX + T_LAT
PAST = 256
GRID_W = 64
HEADS = 8
QK_NOPE, QK_ROPE, V_HEAD = 64, 32, 64
QK_HEAD = QK_NOPE + QK_ROPE
HEAD_PAD = 128
Q_LORA, KV_LORA = 256, 128
SC_W = 512
IN0_W = Q_LORA + KV_LORA + QK_ROPE + 3 * SC_W
CONF_K = 31
D_FF = 2816
N_EXP = 8
D_FFE = 3584
EPS = 1e-6
ROPE_THETA = 10000.0

TM = 1024
N_TILES = T // TM
CTX_TILES = T_CTX // TM
TKV = 1024
TQ = 512
FFN_FC = 256
MOE_FC = 512
VMEM_LIMIT = 56 * 1024 * 1024


def _dot(a, b):
    return jnp.dot(a, b, preferred_element_type=F32)


def _dot_nt(a, b):
    return lax.dot_general(a, b, (((1,), (1,)), ((), ())), preferred_element_type=F32)


def _rms(x, g):
    return x * lax.rsqrt(jnp.mean(x * x, axis=-1, keepdims=True) + EPS) * g


def _silu(x):
    return x * jax.nn.sigmoid(x)


def _params(*sem):
    return pltpu.CompilerParams(dimension_semantics=sem, vmem_limit_bytes=VMEM_LIMIT)


def _mod_row(mod_ref, row0):
    cond = jnp.maximum(row0 - (T_CTX - LAT_LEN), 0) >> 10
    return mod_ref[0, pl.ds(cond, 1), :]


def _mod_spec(layer, ngrid):
    return pl.BlockSpec((1, 8, 6 * D), lambda *_: (layer, 0, 0))


def _adaln_kernel(cc_ref, c_ref, w_ref, b_ref, o_ref):
    l = pl.program_id(0)
    row = lax.broadcasted_iota(jnp.int32, (8, 1), 0)
    cond = jnp.where(row == 0, cc_ref[...], 0.0)
    for b in range(N_LAT_SEQ):
        cond = jnp.where(row == 1 + b, c_ref[b:b + 1, :], cond)
    o_ref[0] = _dot(_silu(cond).astype(BF16), w_ref[0].astype(BF16)) + b_ref[pl.ds(l, 1), :]


def _adaln(c_ctx, c, ada_w, ada_b):
    depth = ada_w.shape[0]
    tn = 2048
    return pl.pallas_call(
        _adaln_kernel,
        grid=(depth, 6 * D // tn),
        in_specs=[
            pl.BlockSpec((1, D), lambda l, j: (0, 0)),
            pl.BlockSpec((N_LAT_SEQ, D), lambda l, j: (0, 0)),
            pl.BlockSpec((1, D, tn), lambda l, j: (l, 0, j)),
            pl.BlockSpec((depth, tn), lambda l, j: (0, j)),
        ],
        out_specs=pl.BlockSpec((1, 8, tn), lambda l, j: (l, 0, j)),
        out_shape=jax.ShapeDtypeStruct((depth, 8, 6 * D), F32),
        compiler_params=_params("arbitrary", "arbitrary"),
        name="adaln",
    )(c_ctx.reshape(1, D), c, ada_w, ada_b)


def _rope_tables():
    half = QK_ROPE // 2
    nf = half // 2
    pos = np.arange(LAT_LEN)
    inv = ROPE_THETA ** (-np.arange(nf, dtype=np.float64) / nf)
    k = np.arange(QK_ROPE)
    part, idx = k // half, k % half
    p = np.where(part[None, :] == 0, (pos // GRID_W)[:, None], (pos % GRID_W)[:, None])
    ang = p * inv[idx % nf][None, :]
    cos, sin = np.cos(ang), np.sin(ang)
    first = (idx < nf)[None, :]
    s1 = np.where(first, -sin, 0.0)
    s2 = np.where(first, 0.0, sin)

    def place(t, fill):
        tab = np.full((2 * LAT_LEN, HEAD_PAD), fill, np.float32)
        tab[:LAT_LEN, QK_NOPE:QK_HEAD] = t
        return jnp.asarray(tab)

    return place(cos, 1.0), place(s1, 0.0), place(s2, 0.0)


def _rope(blk, cos, s1, s2):
    return blk * cos + pltpu.roll(blk, 8, 1) * s2 + pltpu.roll(blk, HEAD_PAD - 8, 1) * s1


def _head_norm(blk, g):
    ms = jnp.sum(blk * blk, axis=-1, keepdims=True) * (1.0 / QK_HEAD)
    return blk * lax.rsqrt(ms + EPS) * g


def _even_proj_kernel(xp_ref, xs_ref, mod_ref, nm_ref, win_ref, qan_ref, wqb_ref, kvan_ref,
                      qn_ref, wsc_ref, cos_ref, s1_ref, s2_ref,
                      q_ref, ckv_ref, kpe_ref, sc_ref, sckv_ref, skpe_ref, wt_ref):
    i = pl.program_id(0)
    n_a = Q_LORA + KV_LORA + QK_ROPE

    @pl.when(i == 0)
    def _():
        wt_ref[...] = win_ref[...].astype(BF16)

    x = jnp.where(i < CTX_TILES, xp_ref[...], xs_ref[...])
    m = _mod_row(mod_ref, i * TM)
    h = _rms(x, nm_ref[0:1, :]) * (1.0 + m[:, D:2 * D]) + m[:, 0:D]
    hb = h.astype(BF16)

    za = _dot_nt(hb, wt_ref[0:512, :])
    ckv = _rms(za[:, Q_LORA:Q_LORA + KV_LORA], kvan_ref[...])
    lane = lax.broadcasted_iota(jnp.int32, (1, HEAD_PAD), 1)
    kpe = jnp.where(lane < QK_ROPE, za[:, Q_LORA + KV_LORA:], 0.0)
    ckv_ref[...] = ckv
    kpe_ref[...] = kpe

    @pl.when(i < CTX_TILES)
    def _():
        sckv_ref[...] = ckv
        skpe_ref[...] = kpe[:, :QK_ROPE]

    qa = _rms(za[:, :Q_LORA], qan_ref[...]).astype(BF16)
    cos, s1, s2 = cos_ref[...], s1_ref[...], s2_ref[...]
    qn = qn_ref[...]
    scale = QK_HEAD ** -0.5
    for hp in range(HEADS // 2):
        qq = _dot(qa, wqb_ref[:, hp * 256:(hp + 1) * 256])
        for j in range(2):
            blk = _head_norm(qq[:, j * HEAD_PAD:(j + 1) * HEAD_PAD], qn)
            blk = _rope(blk, cos, s1, s2) * scale
            h0 = (2 * hp + j) * HEAD_PAD
            q_ref[:, h0:h0 + HEAD_PAD] = blk.astype(BF16)

    gb = _dot_nt(hb, wt_ref[n_a:n_a + SC_W, :])
    v = _dot_nt(hb, wt_ref[n_a + SC_W:n_a + 2 * SC_W, :]) * _dot_nt(hb, wt_ref[n_a + 2 * SC_W:n_a + 3 * SC_W, :])
    seq = jnp.where(i < CTX_TILES, CTX_LEN, LAT_LEN)
    r = lax.broadcasted_iota(jnp.int32, (TM, 1), 0) & (seq - 1)
    vp = jnp.where(r == 0, 0.0, pltpu.roll(v, 1, 0))
    vn = jnp.where(r == seq - 1, 0.0, pltpu.roll(v, TM - 1, 0))
    w = wsc_ref[0]
    y = w[0:1] * vp + w[1:2] * v + w[2:3] * vn
    sc_ref[...] = (gb * y).astype(BF16)


def _even_proj(xp, xs, mods, norm_mix, w_in, q_a_norm, wqb, kv_a_norm, qn, w_sc, tabs):
    full = lambda shape: pl.BlockSpec(shape, lambda i: (0,) * len(shape))
    tab = pl.BlockSpec((TM, HEAD_PAD), lambda i: (jnp.where(i < CTX_TILES, 1, 0), 0))
    row = lambda n: pl.BlockSpec((TM, n), lambda i: (i, 0))
    ctx_row = lambda n: pl.BlockSpec((TM, n), lambda i: (jnp.minimum(i, CTX_TILES - 1), 0))
    return pl.pallas_call(
        _even_proj_kernel,
        grid=(N_TILES,),
        in_specs=[
            ctx_row(D),
            pl.BlockSpec((TM, D), lambda i: (jnp.maximum(i - CTX_TILES, 0), 0)),
            _mod_spec(0, 1),
            full((2, D)),
            pl.BlockSpec((IN0_W, D), lambda i: (0, 0), pipeline_mode=pl.Buffered(1)),
            full((1, Q_LORA)),
            full((Q_LORA, HEADS * HEAD_PAD)), full((1, KV_LORA)), full((1, HEAD_PAD)),
            full((1, 3, SC_W)), tab, tab, tab,
        ],
        out_specs=[row(HEADS * HEAD_PAD), row(KV_LORA), row(HEAD_PAD), row(SC_W),
                   ctx_row(KV_LORA), ctx_row(QK_ROPE)],
        out_shape=[
            jax.ShapeDtypeStruct((T, HEADS * HEAD_PAD), BF16),
            jax.ShapeDtypeStruct((T, KV_LORA), F32),
            jax.ShapeDtypeStruct((T, HEAD_PAD), F32),
            jax.ShapeDtypeStruct((T, SC_W), BF16),
            jax.ShapeDtypeStruct((T_CTX, KV_LORA), F32),
            jax.ShapeDtypeStruct((T_CTX, QK_ROPE), F32),
        ],
        scratch_shapes=[pltpu.VMEM((IN0_W, D), BF16)],
        compiler_params=_params("arbitrary"),
        name="even_proj",
    )(xp, xs, mods, norm_mix, w_in, q_a_norm, wqb, kv_a_norm, qn, w_sc, *tabs)


def _kv_proj_kernel(ckv_ref, kpe_ref, wkvb_ref, kn_ref, cos_ref, s1_ref, s2_ref, k_ref, kv_ref):
    kv = _dot(ckv_ref[...].astype(BF16), wkvb_ref[...])
    kv_ref[...] = kv.astype(BF16)
    kpe = pltpu.roll(kpe_ref[...], QK_NOPE, 1)
    lane = lax.broadcasted_iota(jnp.int32, (1, HEAD_PAD), 1)
    kn = kn_ref[...]
    pe_sq = jnp.sum(kpe * kpe, axis=-1, keepdims=True)
    pe = _rope(kpe * kn, cos_ref[...], s1_ref[...], s2_ref[...])
    for h in range(HEADS):
        blk = kv[:, h * HEAD_PAD:(h + 1) * HEAD_PAD]
        nope = jnp.where(lane < QK_NOPE, blk, 0.0)
        ms = (jnp.sum(nope * nope, axis=-1, keepdims=True) + pe_sq) * (1.0 / QK_HEAD)
        k = jnp.where(lane < QK_NOPE, blk * kn, pe) * lax.rsqrt(ms + EPS)
        k_ref[:, h * HEAD_PAD:(h + 1) * HEAD_PAD] = k.astype(BF16)


def _kv_proj(ckv, kpe, wkvb, kn, tabs, tab_index, name, TKV):
    n = ckv.shape[0]
    full = lambda shape: pl.BlockSpec(shape, lambda i: (0,) * len(shape))
    tab = pl.BlockSpec((TKV, HEAD_PAD), lambda i: (tab_index(i), 0))
    row = lambda w: pl.BlockSpec((TKV, w), lambda i: (i, 0))
    return pl.pallas_call(
        _kv_proj_kernel,
        grid=(n // TKV,),
        in_specs=[row(KV_LORA), row(HEAD_PAD), full((KV_LORA, HEADS * HEAD_PAD)), full((1, HEAD_PAD)),
                  tab, tab, tab],
        out_specs=[row(HEADS * HEAD_PAD), row(HEADS * HEAD_PAD)],
        out_shape=[jax.ShapeDtypeStruct((n, HEADS * HEAD_PAD), BF16)] * 2,
        compiler_params=_params("arbitrary"),
        name=name,
    )(ckv, kpe, wkvb, kn, *tabs)


def _pair_out(o0, o1):
    lane = lax.broadcasted_iota(jnp.int32, (1, HEAD_PAD), 1)
    return jnp.where(lane < V_HEAD, pltpu.roll(o0, V_HEAD, 1), o1).astype(BF16)


CTX_SEQS = 4


def _attn_ctx_kernel(q_ref, k_ref, kv_ref, o_ref):
    for b in range(CTX_SEQS):
        rows = slice(b * CTX_LEN, (b + 1) * CTX_LEN)
        for hp in range(HEADS // 2):
            outs = []
            for j in range(2):
                lanes = slice((2 * hp + j) * HEAD_PAD, (2 * hp + j + 1) * HEAD_PAD)
                s = _dot_nt(q_ref[rows, lanes], k_ref[rows, lanes])
                p = jnp.exp(s - jnp.max(s, axis=-1, keepdims=True))
                l = jnp.sum(p, axis=-1, keepdims=True)
                outs.append(_dot(p.astype(BF16), kv_ref[rows, lanes]) / l)
            o_ref[rows, hp * HEAD_PAD:(hp + 1) * HEAD_PAD] = _pair_out(*outs)


def _attn_ctx(q, k, kv):
    blk = pl.BlockSpec((CTX_SEQS * CTX_LEN, HEADS * HEAD_PAD), lambda b: (b, 0))
    return pl.pallas_call(
        _attn_ctx_kernel,
        grid=(N_CTX_SEQ // CTX_SEQS,),
        in_specs=[blk, blk, blk],
        out_specs=pl.BlockSpec((CTX_SEQS * CTX_LEN, HEADS * V_HEAD), lambda b: (b, 0)),
        out_shape=jax.ShapeDtypeStruct((T_CTX, HEADS * V_HEAD), BF16),
        compiler_params=_params("arbitrary"),
        name="attn_ctx",
    )(q, k, kv)


LAT_HEADS = 8


def _attn_lat_kernel(q_ref, kc_ref, kvc_ref, kl_ref, kvl_ref, o_ref):
    for hp in range(LAT_HEADS // 2):
        outs = []
        for j in range(2):
            h0 = (2 * hp + j) * HEAD_PAD
            lanes = slice(h0, h0 + HEAD_PAD)
            q = q_ref[:, lanes]
            sc = _dot_nt(q, kc_ref[:, lanes])
            sl = _dot_nt(q, kl_ref[:, lanes])
            m = jnp.maximum(jnp.max(sc, axis=-1, keepdims=True), jnp.max(sl, axis=-1, keepdims=True))
            pc, pl_ = jnp.exp(sc - m), jnp.exp(sl - m)
            l = jnp.sum(pc, axis=-1, keepdims=True) + jnp.sum(pl_, axis=-1, keepdims=True)
            o = _dot(pc.astype(BF16), kvc_ref[:, lanes]) + _dot(pl_.astype(BF16), kvl_ref[:, lanes])
            outs.append(o / l)
        o_ref[:, hp * HEAD_PAD:(hp + 1) * HEAD_PAD] = _pair_out(*outs)


def _attn_lat(q, kc, kvc, k, kv):
    nq = LAT_LEN // TQ
    q0 = T_CTX // TQ
    kl0 = T_CTX // LAT_LEN
    width = LAT_HEADS * HEAD_PAD
    lat = pl.BlockSpec((LAT_LEN, width), lambda b, hg, t: (kl0 + b, hg))
    ctx = pl.BlockSpec((PAST, width), lambda b, hg, t: (b, hg))
    return pl.pallas_call(
        _attn_lat_kernel,
        grid=(N_LAT_SEQ, HEADS // LAT_HEADS, nq),
        in_specs=[pl.BlockSpec((TQ, width), lambda b, hg, t: (q0 + b * nq + t, hg)), ctx, ctx, lat, lat],
        out_specs=pl.BlockSpec((TQ, LAT_HEADS * V_HEAD), lambda b, hg, t: (b * nq + t, hg)),
        out_shape=jax.ShapeDtypeStruct((T_LAT, HEADS * V_HEAD), BF16),
        compiler_params=_params("arbitrary", "arbitrary", "arbitrary"),
        name="attn_lat",
    )(q, kc, kvc, k, kv)


FFN_NC = D_FF // FFN_FC
FFN_TM = 512
FFN_CTX_TILES = T_CTX // FFN_TM


def _ffn_kernel(oc_ref, ol_ref, sc_ref, xp_ref, xs_ref, wo_ref, nf_ref, wg_ref, wu_ref, wd_ref,
                mod0_ref, mod1_ref, nm_ref, x2_ref, h3_ref, wg_all, wu_all, wd_all, x1_ref, hs_ref, act_ref):
    t = pl.program_id(0)

    i = jnp.maximum(t - (FFN_NC - 1), 0)
    m0, m1 = _mod_row(mod0_ref, i * FFN_TM), _mod_row(mod1_ref, i * FFN_TM)

    def mixer():
        ctx = i < FFN_CTX_TILES
        attn = jnp.where(ctx, oc_ref[...], ol_ref[...])
        x = jnp.where(ctx, xp_ref[...], xs_ref[...])
        x1 = x + m0[:, 2 * D:3 * D] * _dot(jnp.concatenate([attn, sc_ref[...]], axis=1), wo_ref[...])
        h = (_rms(x1, nf_ref[0:1, :]) * (1.0 + m0[:, 4 * D:5 * D]) + m0[:, 3 * D:4 * D]).astype(BF16)
        return x1, h

    def up(h, c):
        return (_silu(_dot(h, wg_all[c])) * _dot(h, wu_all[c])).astype(BF16)

    def down(x1, act):
        x2 = x1 + m0[:, 5 * D:6 * D] * _dot(act, wd_all[...])
        x2_ref[...] = x2
        h3_ref[...] = (_rms(x2, nm_ref[1:2, :]) * (1.0 + m1[:, D:2 * D]) + m1[:, 0:D]).astype(BF16)

    @pl.when(t == 0)
    def _():
        x1, h = mixer()
        x1_ref[...] = x1
        hs_ref[...] = h

    @pl.when(t < FFN_NC)
    def _stage():
        wg_all[t] = wg_ref[...].astype(BF16)
        wu_all[t] = wu_ref[...].astype(BF16)
        wd_all[pl.ds(pl.multiple_of(t * FFN_FC, FFN_FC), FFN_FC), :] = wd_ref[...].astype(BF16)
        act_ref[t] = up(hs_ref[...], t)

    @pl.when(t == FFN_NC - 1)
    def _():
        down(x1_ref[...], jnp.concatenate([act_ref[c] for c in range(FFN_NC)], axis=1))

    @pl.when(t > FFN_NC - 1)
    def _tile():
        x1, h = mixer()
        down(x1, jnp.concatenate([up(h, c) for c in range(FFN_NC)], axis=1))


def _ffn(oc, ol, sc, xp, xs, wo, norm_ffn, w_gu, w_down, mods, norm_mix):
    chunk = lambda t: jnp.minimum(t, FFN_NC - 1)
    tile = lambda t: jnp.maximum(t - (FFN_NC - 1), 0)
    full = lambda shape: pl.BlockSpec(shape, lambda t: (0,) * len(shape))
    row = lambda n: pl.BlockSpec((FFN_TM, n), lambda t: (tile(t), 0))
    first = lambda n: pl.BlockSpec((FFN_TM, n), lambda t: (jnp.minimum(tile(t), FFN_CTX_TILES - 1), 0))
    second = lambda n: pl.BlockSpec((FFN_TM, n), lambda t: (jnp.maximum(tile(t) - FFN_CTX_TILES, 0), 0))
    return pl.pallas_call(
        _ffn_kernel,
        grid=(FFN_NC - 1 + T // FFN_TM,),
        in_specs=[first(HEADS * V_HEAD), second(HEADS * V_HEAD), row(SC_W), first(D), second(D),
                  full((HEADS * V_HEAD + SC_W, D)), full((2, D)),
                  pl.BlockSpec((D, FFN_FC), lambda t: (0, chunk(t))),
                  pl.BlockSpec((D, FFN_FC), lambda t: (0, FFN_NC + chunk(t))),
                  pl.BlockSpec((FFN_FC, D), lambda t: (chunk(t), 0)),
                  _mod_spec(0, 1), _mod_spec(1, 1), full((2, D))],
        out_specs=[row(D), row(D)],
        out_shape=[jax.ShapeDtypeStruct((T, D), F32), jax.ShapeDtypeStruct((T, D), BF16)],
        scratch_shapes=[pltpu.VMEM((FFN_NC, D, FFN_FC), BF16), pltpu.VMEM((FFN_NC, D, FFN_FC), BF16),
                        pltpu.VMEM((D_FF, D), BF16), pltpu.VMEM((FFN_TM, D), F32), pltpu.VMEM((FFN_TM, D), BF16),
                        pltpu.VMEM((FFN_NC, FFN_TM, FFN_FC), BF16)],
        compiler_params=_params("arbitrary"),
        name="ffn_dense",
    )(oc, ol, sc, xp, xs, wo, norm_ffn, w_gu, w_gu, w_down, mods, mods, norm_mix)


CONF_CB = 256
CONF_SEG = 256
CONF_HALO = 16
CONF_SEGP = CONF_SEG + 2 * CONF_HALO
CONF_PIECE = 256


def _conf_kernel(h_ref, x2_ref, w1_ref, b1_ref, wdw_ref, bdw_ref, lng_ref, lnb_ref, w2_ref, b2_ref,
                 mod_ref, nf_ref, rt_ref, x3_ref, h4_ref, lg_ref, pad_ref, conv_ref):
    i = pl.program_id(0)
    nseg = TM // CONF_SEG
    h = h_ref[...]
    joined = jnp.where(i < CTX_TILES, 0.0, 1.0)
    zeros_halo = jnp.zeros((CONF_HALO, CONF_CB), F32)
    for cb in range(D // CONF_CB):
        c0 = cb * CONF_CB
        a = _dot(h, w1_ref[:, c0:c0 + CONF_CB]) + b1_ref[:, c0:c0 + CONF_CB]
        g = _dot(h, w1_ref[:, D + c0:D + c0 + CONF_CB]) + b1_ref[:, D + c0:D + c0 + CONF_CB]
        u = a * jax.nn.sigmoid(g)
        for s in range(nseg):
            base = s * CONF_SEGP
            top = u[s * CONF_SEG - CONF_HALO:s * CONF_SEG] * joined if s > 0 else zeros_halo
            bot = (u[(s + 1) * CONF_SEG:(s + 1) * CONF_SEG + CONF_HALO] * joined
                   if s < nseg - 1 else zeros_halo)
            pad_ref[0, base:base + CONF_HALO, :] = top
            pad_ref[0, base + CONF_HALO:base + CONF_HALO + CONF_SEG, :] = u[s * CONF_SEG:(s + 1) * CONF_SEG]
            pad_ref[0, base + CONF_HALO + CONF_SEG:base + CONF_SEGP, :] = bot

        p0 = pad_ref[0]
        rows = nseg * CONF_SEGP
        for b in range(1, 8):
            pad_ref[b] = pltpu.roll(p0, rows - b, 0)

        def piece(t, carry):
            s = t // (CONF_SEG // CONF_PIECE)
            q0 = (t % (CONF_SEG // CONF_PIECE)) * CONF_PIECE
            src = pl.multiple_of(s * CONF_SEGP + q0, 8)
            acc = jnp.zeros((CONF_PIECE, CONF_CB), F32)
            for j in range(CONF_K):
                hi, lo = (j + 1) // 8, (j + 1) % 8
                acc = acc + wdw_ref[0, j:j + 1, c0:c0 + CONF_CB] * pad_ref[lo, pl.ds(src + 8 * hi, CONF_PIECE), :]
            dst = pl.multiple_of(s * CONF_SEG + q0, 8)
            conv_ref[pl.ds(dst, CONF_PIECE), c0:c0 + CONF_CB] = acc + bdw_ref[:, c0:c0 + CONF_CB]
            return carry

        lax.fori_loop(0, TM // CONF_PIECE, piece, 0)

    m = _mod_row(mod_ref, i * TM)
    half = TM // 2
    for r0 in (0, half):
        rows = slice(r0, r0 + half)
        y = conv_ref[rows, :]
        mu = jnp.mean(y, axis=-1, keepdims=True)
        yc = y - mu
        var = jnp.mean(yc * yc, axis=-1, keepdims=True)
        y = _silu(yc * lax.rsqrt(var + EPS) * lng_ref[...] + lnb_ref[...])
        out = _dot(y.astype(BF16), w2_ref[...]) + b2_ref[...]
        x3 = x2_ref[rows, :] + m[:, 2 * D:3 * D] * out
        x3_ref[rows, :] = x3
        h4 = _rms(x3, nf_ref[1:2, :]) * (1.0 + m[:, 4 * D:5 * D]) + m[:, 3 * D:4 * D]
        h4_ref[rows, :] = h4.astype(BF16)
        lg_ref[:, rows] = lax.dot_general(rt_ref[...], h4, (((1,), (1,)), ((), ())),
                                          precision=lax.Precision.HIGHEST, preferred_element_type=F32)


def _conf(h3, x2, w1, b1, wdw, bdw, lng, lnb, w2, b2, mods, norm_ffn1, router_t):
    full = lambda shape: pl.BlockSpec(shape, lambda i: (0,) * len(shape))
    row = lambda n: pl.BlockSpec((TM, n), lambda i: (i, 0))
    return pl.pallas_call(
        _conf_kernel,
        grid=(N_TILES,),
        in_specs=[row(D), row(D), full((D, 2 * D)), full((1, 2 * D)), full((1, CONF_K, D)), full((1, D)),
                  full((1, D)), full((1, D)), full((D, D)), full((1, D)),
                  _mod_spec(1, 1), full((2, D)), full((N_EXP, D))],
        out_specs=[row(D), row(D), pl.BlockSpec((N_EXP, TM), lambda i: (0, i))],
        out_shape=[jax.ShapeDtypeStruct((T, D), F32), jax.ShapeDtypeStruct((T, D), BF16),
                   jax.ShapeDtypeStruct((N_EXP, T), F32)],
        scratch_shapes=[pltpu.VMEM((8, (TM // CONF_SEG) * CONF_SEGP, CONF_CB), F32),
                        pltpu.VMEM((TM, D), F32)],
        compiler_params=_params("arbitrary"),
        name="conformer_conv",
    )(h3, x2, w1, b1, wdw, bdw, lng, lnb, w2, b2, mods, norm_ffn1, router_t)


TB = 256
N_TB = T // TB
SUB = 128
SUBS = 24
SM = SUBS * SUB
N_SUB_MAX = 2 * T // SUB + N_EXP
N_SUP_MAX = (N_SUB_MAX + N_EXP * (SUBS - 1)) // SUBS
YS_ROWS = (N_SUB_MAX + 4) * SUB
WIN_ALIGN = 16
WIN_HALF = TB // 2 + WIN_ALIGN
FIRST_STRIDE = 32
UNIT_STRIDE = 64
GATHER_BLOCKS = 4


def _route_kernel(lg_ref, g_ref, rank_ref, first_ref):
    lg = lg_ref[...]
    idx = lax.broadcasted_iota(jnp.int32, lg.shape, 0).astype(F32)
    none = float(N_EXP)
    m1 = jnp.max(lg, axis=0, keepdims=True)
    i1 = jnp.min(jnp.where(lg == m1, idx, none), axis=0, keepdims=True)
    rest = jnp.where(idx == i1, -jnp.inf, lg)
    m2 = jnp.max(rest, axis=0, keepdims=True)
    i2 = jnp.min(jnp.where(rest == m2, idx, none), axis=0, keepdims=True)
    e = jnp.exp(m2 - m1)
    w1 = 1.0 / (1.0 + e)
    w2 = e / (1.0 + e)
    g_ref[...] = jnp.where(idx == i1, w1, 0.0) + jnp.where(idx == i2, w2, 0.0)

    mask = jnp.where(idx == i1, 1.0, 0.0) + jnp.where(idx == i2, 1.0, 0.0)
    before = (lax.broadcasted_iota(jnp.int32, (TB, TB), 0) < lax.broadcasted_iota(jnp.int32, (TB, TB), 1))
    before = jnp.where(before, 1.0, 0.0).astype(BF16)
    lane = lax.broadcasted_iota(jnp.int32, (N_EXP, 128), 1)
    carry = jnp.zeros((N_EXP, 1), F32)
    first = jnp.zeros((N_EXP, 128), F32)
    for b in range(N_TB):
        mb = mask[:, b * TB:(b + 1) * TB]
        local = _dot(mb.astype(BF16), before)
        rank_ref[:, b * TB:(b + 1) * TB] = jnp.where(mb > 0.0, local + carry, -1.0)
        first = jnp.where(lane == b, carry, first)
        carry = carry + jnp.sum(mb, axis=1, keepdims=True)
    first_ref[...] = jnp.where(lane == N_TB, carry, first)


def _route(logits_t):
    return pl.pallas_call(
        _route_kernel,
        out_shape=[jax.ShapeDtypeStruct((N_EXP, T), F32), jax.ShapeDtypeStruct((N_EXP, T), F32),
                   jax.ShapeDtypeStruct((N_EXP, 128), F32)],
        compiler_params=pltpu.CompilerParams(vmem_limit_bytes=VMEM_LIMIT),
        name="route",
    )(logits_t)


def _moe_plan(first):
    first = first[:, :FIRST_STRIDE].astype(jnp.int32)
    cnt = first[:, N_TB]
    nt = (cnt + (SUB - 1)) // SUB
    off_end = jnp.cumsum(nt)
    off = off_end - nt
    nsub = off_end[-1]
    nsup = (nt + (SUBS - 1)) // SUBS
    sup_end = jnp.cumsum(nsup)
    sup_off = sup_end - nsup
    s = jnp.minimum(jnp.arange(N_SUP_MAX), sup_end[-1] - 1)
    valid = jnp.arange(N_SUP_MAX) < sup_end[-1]
    se = jnp.sum(s[:, None] >= sup_end[None, :], axis=1)
    mine = se[:, None] == jnp.arange(N_EXP)[None, :]
    pick = lambda v: jnp.sum(jnp.where(mine, v[None, :], 0), axis=1)
    sk0 = (s - pick(sup_off)) * SUBS
    sns = jnp.where(valid, jnp.clip(pick(nt) - sk0, 0, SUBS), 0)
    sj0 = pick(off) + sk0
    base = (jnp.arange(UNIT_STRIDE) * SUB)[None, :, None]
    blo = jnp.minimum(jnp.sum(first[:, None, 1:N_TB + 1] <= base, axis=2), N_TB - 1)
    end = jnp.minimum(base + SUB, cnt[:, None, None])
    bhi = jnp.maximum(jnp.sum(first[:, None, :N_TB] < end, axis=2) - 1, blo)
    ng = (bhi - blo) // GATHER_BLOCKS + 1
    start = SUB * off[:, None] + first[:, :N_TB]
    lead = start & (WIN_ALIGN - 1)
    wina = start - lead
    rel = first[:, :N_TB] - lead
    need = lead + (first[:, 1:N_TB + 1] - first[:, :N_TB]) > WIN_HALF
    winb = lax.cummax(jnp.where(need, wina + WIN_HALF, 0), axis=1)
    wide = jnp.any(need, axis=0)
    flat = lambda parts: jnp.concatenate([p.astype(jnp.int32).reshape(-1) for p in parts])
    gmm = dict(se=se, sk0=sk0, sns=sns, sj0=sj0, nsub=nsub, first=first, blo=blo, ng=ng)
    comb = dict(wina=wina.T, winb=winb.T, rel=rel.T, wide=wide)
    return flat([gmm[k] for k in _GMM_TAB]), flat([comb[k] for k in _COMB_TAB])


def _offsets(sizes):
    out, pos = {}, 0
    for name, n in sizes.items():
        out[name], pos = pos, pos + n
    return out


_GMM_TAB = _offsets(dict(se=N_SUP_MAX, sk0=N_SUP_MAX, sns=N_SUP_MAX, sj0=N_SUP_MAX, nsub=1,
                         first=N_EXP * FIRST_STRIDE, blo=N_EXP * UNIT_STRIDE, ng=N_EXP * UNIT_STRIDE))
_COMB_TAB = _offsets(dict(wina=N_TB * N_EXP, winb=N_TB * N_EXP, rel=N_TB * N_EXP, wide=N_TB))


class _Section:
    def __init__(self, ref, offset):
        self.ref, self.offset = ref, offset

    def __getitem__(self, i):
        return self.ref[self.offset + i]


def _moe_gmm_kernel(tab_ref, x_ref, rank_ref, gate_ref, wg_ref, wu_ref, wd_ref, ys_ref,
                    xs_ref, gs_ref, yacc_ref, wgb_ref, wub_ref, wdb_ref, sem):
    se_ref, sk0_ref, sns_ref, sj0_ref, nsub_ref, first_ref, blo_ref, ng_ref = (
        _Section(tab_ref, _GMM_TAB[k]) for k in ("se", "sk0", "sns", "sj0", "nsub", "first", "blo", "ng"))
    s, c = pl.program_id(0), pl.program_id(1)
    nc = pl.num_programs(1)
    e, k0, ns = se_ref[s], sk0_ref[s], sns_ref[s]

    def sub_rows(k):
        return pl.ds(pl.multiple_of(k * SUB, SUB), SUB)

    def out_copy(k, row0):
        dst = ys_ref.at[pl.ds(pl.multiple_of(row0 + k * SUB, SUB), SUB)]
        return pltpu.make_async_copy(xs_ref.at[sub_rows(k)], dst, sem.at[k])

    @pl.when((ns > 0) & (c == 0))
    def _gather():
        def group(k, g):
            slot = (lax.broadcasted_iota(jnp.int32, (SUB, 1), 0) + (k0 + k) * SUB).astype(F32)
            b0 = blo_ref[e * UNIT_STRIDE + k0 + k] + g * GATHER_BLOCKS
            t0 = pl.multiple_of(jnp.minimum(b0, N_TB - GATHER_BLOCKS) * TB, TB)
            lo = first_ref[e * FIRST_STRIDE + b0].astype(F32)
            cols = pl.ds(t0, GATHER_BLOCKS * TB)
            hit = rank_ref[pl.ds(e, 1), cols] == jnp.where(slot >= lo, slot, -2.0)
            rows = _dot(jnp.where(hit, 1.0, 0.0).astype(BF16), x_ref[cols, :])
            gate = jnp.sum(jnp.where(hit, gate_ref[pl.ds(e, 1), cols], 0.0), axis=-1, keepdims=True)
            return rows, gate

        def first(k):
            rows, gate = group(k, 0)
            xs_ref[sub_rows(k), :] = rows.astype(BF16)
            gs_ref[sub_rows(k), :] = gate
            yacc_ref[sub_rows(k), :] = jnp.zeros((SUB, D), F32)

        def more(k):
            def body(g, carry):
                rows, gate = group(k, g)
                xs_ref[sub_rows(k), :] = (xs_ref[sub_rows(k), :].astype(F32) + rows).astype(BF16)
                gs_ref[sub_rows(k), :] += gate
                return carry

            lax.fori_loop(1, ng_ref[e * UNIT_STRIDE + k0 + k], body, 0)

        def quad(p, carry):
            for j in range(4):
                first(4 * p + j)
            for j in range(4):
                more(4 * p + j)
            return carry

        def single(k, carry):
            first(k)
            more(k)
            return carry

        lax.fori_loop(0, ns >> 2, quad, 0)
        lax.fori_loop(ns & ~3, ns, single, 0)

    @pl.when(ns > 0)
    def _compute():
        row0 = sj0_ref[s] * SUB

        def swiglu(rows, wg, wu, wd):
            x = xs_ref[rows, :]
            g = _dot(x, wg)
            u = _dot(x, wu)
            yacc_ref[rows, :] += _dot((_silu(g) * u).astype(BF16), wd)

        def finish(first_sub, n):
            @pl.when(c == nc - 1)
            def _():
                for k in range(n):
                    rows = sub_rows(first_sub + k)
                    xs_ref[rows, :] = (yacc_ref[rows, :] * gs_ref[rows, :]).astype(BF16)
                    out_copy(first_sub + k, row0).start()

        def first_chain(rows):
            wg, wu, wd = wg_ref[0].astype(BF16), wu_ref[0].astype(BF16), wd_ref[0].astype(BF16)
            wgb_ref[...] = wg
            wub_ref[...] = wu
            wdb_ref[...] = wd
            swiglu(rows, wg, wu, wd)

        @pl.when(ns >= 4)
        def _():
            first_chain(pl.ds(0, 4 * SUB))
            finish(0, 4)

        @pl.when(ns < 4)
        def _():
            first_chain(pl.ds(0, SUB))
            finish(0, 1)

        done = jnp.where(ns >= 4, 4, 1)
        rest = ns - done

        def chain(first_sub, n):
            rows = pl.ds(pl.multiple_of(first_sub * SUB, SUB), n * SUB)
            swiglu(rows, wgb_ref[...], wub_ref[...], wdb_ref[...])
            finish(first_sub, n)

        def eight(k, carry):
            chain(done + 8 * k, 8)
            return carry

        lax.fori_loop(0, rest >> 3, eight, 0)
        done8 = done + (rest & ~7)
        for n in (4, 2, 1):
            @pl.when((rest & n) != 0)
            def _(n=n):
                chain(done8 + (rest & (7 & ~(2 * n - 1))), n)

    @pl.when((ns > 0) & (c == nc - 1))
    def _store_done():
        row0 = sj0_ref[s] * SUB

        def done(k, carry):
            out_copy(k, row0).wait()
            return carry

        lax.fori_loop(0, ns, done, 0)

    @pl.when((s == pl.num_programs(0) - 1) & (c == nc - 1))
    def _zero_tail():
        xs_ref[0:SUB, :] = jnp.zeros((SUB, D), BF16)
        nsub = nsub_ref[0]

        def fill(k, carry):
            cp = out_copy(0, (nsub + k) * SUB)
            cp.start()
            cp.wait()
            return carry

        lax.fori_loop(0, YS_ROWS // SUB - nsub, fill, 0)


def _moe_gmm(tab, h4, rank, gates, w_gu, w_down):
    nc = D_FFE // MOE_FC

    def expert(s, tab):
        return tab[_GMM_TAB["se"] + s]

    def chunk(s, c, tab):
        return jnp.where(tab[_GMM_TAB["sns"] + s] > 0, c, nc - 1)

    return pl.pallas_call(
        _moe_gmm_kernel,
        grid_spec=pltpu.PrefetchScalarGridSpec(
            num_scalar_prefetch=1,
            grid=(N_SUP_MAX, nc),
            in_specs=[
                pl.BlockSpec((T, D), lambda s, c, tab: (0, 0), pipeline_mode=pl.Buffered(1)),
                pl.BlockSpec((N_EXP, T), lambda s, c, tab: (0, 0)),
                pl.BlockSpec((N_EXP, T), lambda s, c, tab: (0, 0)),
                pl.BlockSpec((1, D, MOE_FC), lambda s, c, tab: (expert(s, tab), 0, chunk(s, c, tab))),
                pl.BlockSpec((1, D, MOE_FC), lambda s, c, tab: (expert(s, tab), 0, nc + chunk(s, c, tab))),
                pl.BlockSpec((1, MOE_FC, D), lambda s, c, tab: (expert(s, tab), chunk(s, c, tab), 0)),
            ],
            out_specs=pl.BlockSpec(memory_space=pl.ANY),
            scratch_shapes=[
                pltpu.VMEM((SM, D), BF16), pltpu.VMEM((SM, 1), F32), pltpu.VMEM((SM, D), F32),
                pltpu.VMEM((D, MOE_FC), BF16), pltpu.VMEM((D, MOE_FC), BF16), pltpu.VMEM((MOE_FC, D), BF16),
                pltpu.SemaphoreType.DMA((SUBS,)),
            ],
        ),
        out_shape=jax.ShapeDtypeStruct((YS_ROWS, D), BF16),
        compiler_params=_params("arbitrary", "arbitrary"),
        name="moe_gmm",
    )(tab, h4, rank, gates, w_gu, w_gu, w_down)


def _moe_combine_kernel(tab_ref, *refs):
    rel_ref, wide_ref = _Section(tab_ref, _COMB_TAB["rel"]), _Section(tab_ref, _COMB_TAB["wide"])
    ya, yb = refs[:N_EXP], refs[N_EXP:2 * N_EXP]
    rank_ref, x3_ref, mod_ref, yp_ref, ys_ref, ycat_ref, acc_ref = refs[2 * N_EXP:]
    b = pl.program_id(0)
    row = lax.broadcasted_iota(jnp.int32, (WIN_HALF, 1), 0)

    def onehot(e, first_row):
        slot = (row + (rel_ref[b * N_EXP + e] + first_row)).astype(F32)
        return jnp.where(rank_ref[e:e + 1, :] == slot, 1.0, 0.0).astype(BF16)

    def gather(y_refs, first_row, base):
        pieces = []
        for e in range(N_EXP):
            ycat_ref[base + e * WIN_HALF:base + (e + 1) * WIN_HALF, :] = y_refs[e][...]
            pieces.append(onehot(e, first_row))
        return pieces

    def combine(pieces, rows):
        return lax.dot_general(jnp.concatenate(pieces, axis=0), ycat_ref[0:rows, :], (((0,), (0,)), ((), ())),
                               preferred_element_type=F32)

    @pl.when(wide_ref[b] == 0)
    def _():
        acc_ref[...] = combine(gather(ya, 0, 0), N_EXP * WIN_HALF)

    @pl.when(wide_ref[b] != 0)
    def _():
        pieces = gather(ya, 0, 0) + gather(yb, WIN_HALF, N_EXP * WIN_HALF)
        acc_ref[...] = combine(pieces, 2 * N_EXP * WIN_HALF)

    out = x3_ref[...] + _mod_row(mod_ref, b * TB)[:, 5 * D:6 * D] * acc_ref[...]

    @pl.when(b < T_CTX // TB)
    def _():
        yp_ref[...] = out

    @pl.when(b >= T_CTX // TB)
    def _():
        ys_ref[...] = out


def _moe_combine(tab, ysorted, rank, x3, mods):
    ctx_blocks = T_CTX // TB

    def window(e, second):
        def index(b, tab):
            start = tab[_COMB_TAB["winb" if second else "wina"] + b * N_EXP + e]
            return pl.multiple_of(start, WIN_ALIGN), 0
        return pl.BlockSpec((pl.Element(WIN_HALF), pl.Element(D)), index)

    return pl.pallas_call(
        _moe_combine_kernel,
        grid_spec=pltpu.PrefetchScalarGridSpec(
            num_scalar_prefetch=1,
            grid=(N_TB,),
            in_specs=[window(e, False) for e in range(N_EXP)] + [window(e, True) for e in range(N_EXP)] + [
                pl.BlockSpec((N_EXP, TB), lambda b, *_: (0, b)),
                pl.BlockSpec((TB, D), lambda b, *_: (b, 0)),
                _mod_spec(1, 1),
            ],
            out_specs=[pl.BlockSpec((TB, D), lambda b, *_: (jnp.minimum(b, ctx_blocks - 1), 0)),
                       pl.BlockSpec((TB, D), lambda b, *_: (jnp.maximum(b - ctx_blocks, 0), 0))],
            scratch_shapes=[pltpu.VMEM((2 * N_EXP * WIN_HALF, D), BF16), pltpu.VMEM((TB, D), F32)],
        ),
        out_shape=[jax.ShapeDtypeStruct((T_CTX, D), F32), jax.ShapeDtypeStruct((T_LAT, D), F32)],
        compiler_params=_params("arbitrary"),
        name="moe_combine",
    )(tab, *([ysorted] * (2 * N_EXP)), rank, x3, mods)


def _pad_heads(w, width):
    lead = w.shape[:-1]
    w = w.reshape(*lead, HEADS, width)
    w = jnp.pad(w, [(0, 0)] * len(lead) + [(0, 0), (0, HEAD_PAD - width)])
    return w.reshape(*lead, HEADS * HEAD_PAD)


def kernel(x_prompt, x_sample, cache_ckv, cache_kpe, c, c_ctx, ada_w, ada_b, norm_mix, norm_ffn, w_in, q_a_norm,
           w_qb, kv_a_norm, w_kvb, q_norm, k_norm, w_sc, w_o, ffn_gu, ffn_down, conv_pw1, conv_pw1_b, conv_dw,
           conv_dw_b, conv_ln_g, conv_ln_b, conv_pw2, conv_pw2_b, router, moe_gu, moe_down):
    xp = x_prompt.reshape(T_CTX, D)
    xs = x_sample.reshape(T_LAT, D)

    mods = _adaln(c_ctx, c, ada_w, ada_b)

    wqb = _pad_heads(w_qb[0], QK_HEAD).astype(BF16)
    wkvb = w_kvb[0].astype(BF16)
    qn = jnp.pad(q_norm[0], (0, HEAD_PAD - QK_HEAD)).reshape(1, HEAD_PAD)
    kn = jnp.pad(k_norm[0], (0, HEAD_PAD - QK_HEAD)).reshape(1, HEAD_PAD)
    tabs = _rope_tables()

    w_in_t = jnp.swapaxes(w_in[0], 0, 1)
    q, ckv, kpe, sc, state_ckv, state_kpe = _even_proj(xp, xs, mods, norm_mix, w_in_t, q_a_norm, wqb, kv_a_norm,
                                                       qn, w_sc, tabs)

    lat_tile0 = T_CTX // TKV
    ident = LAT_LEN // TKV
    k, kv = _kv_proj(ckv, kpe, wkvb, kn, tabs,
                     lambda i: jnp.where(i < lat_tile0, ident, (i - lat_tile0) % ident), "kv_proj", TKV)
    cache_kpe_p = jnp.pad(cache_kpe[:, 0].reshape(N_LAT_SEQ * PAST, QK_ROPE), ((0, 0), (0, HEAD_PAD - QK_ROPE)))
    kc, kvc = _kv_proj(cache_ckv[:, 0].reshape(N_LAT_SEQ * PAST, KV_LORA), cache_kpe_p, wkvb, kn, tabs,
                       lambda i: LAT_LEN // (N_LAT_SEQ * PAST), "kv_proj_cache", N_LAT_SEQ * PAST)

    oc = _attn_ctx(q, k, kv)
    ol = _attn_lat(q, kc, kvc, k, kv)
    x2, h3 = _ffn(oc, ol, sc, xp, xs, w_o[0].astype(BF16), norm_ffn, ffn_gu[0], ffn_down[0], mods, norm_mix)

    x3, h4, logits_t = _conf(h3, x2, conv_pw1[0].astype(BF16), conv_pw1_b, conv_dw, conv_dw_b, conv_ln_g,
                             conv_ln_b, conv_pw2[0].astype(BF16), conv_pw2_b, mods, norm_ffn, router[0].T)
    gates, rank, first = _route(logits_t)
    gmm_tab, comb_tab = _moe_plan(first)
    ysorted = _moe_gmm(gmm_tab, h4, rank, gates, moe_gu[0], moe_down[0])
    yp, ys = _moe_combine(comb_tab, ysorted, rank, x3, mods)

    return (yp.reshape(N_CTX_SEQ, CTX_LEN, D), ys.reshape(N_LAT_SEQ, LAT_LEN, D),
            state_ckv.reshape(N_CTX_SEQ, 1, CTX_LEN, KV_LORA), state_kpe.reshape(N_CTX_SEQ, 1, CTX_LEN, QK_ROPE))
```

```python
import jax
import jax.numpy as jnp
import numpy as np
from jax import lax
from jax.experimental import pallas as pl
from jax.experimental.pallas import tpu as pltpu

F32 = jnp.float32
BF16 = jnp.bfloat16

D = 1024
N_CTX_SEQ, CTX_LEN = 16, 256
N_LAT_SEQ, LAT_LEN = 2, 1024
T_CTX = N_CTX_SEQ * CTX_LEN
T_LAT = N_LAT_SEQ * LAT_LEN
T = T_CTX + T_LAT
PAST = 256
GRID_W = 64
HEADS = 8
QK_NOPE, QK_ROPE, V_HEAD = 64, 32, 64
QK_HEAD = QK_NOPE + QK_ROPE
HEAD_PAD = 128
Q_LORA, KV_LORA = 256, 128
SC_W = 512
IN0_W = Q_LORA + KV_LORA + QK_ROPE + 3 * SC_W
CONF_K = 31
D_FF = 2816
N_EXP = 8
D_FFE = 3584
EPS = 1e-6
ROPE_THETA = 10000.0

TM = 1024
N_TILES = T // TM
CTX_TILES = T_CTX // TM
TKV = 1024
TQ = 512
FFN_FC = 256
MOE_FC = 512
VMEM_LIMIT = 56 * 1024 * 1024


def _dot(a, b):
    return jnp.dot(a, b, preferred_element_type=F32)


def _dot_nt(a, b):
    return lax.dot_general(a, b, (((1,), (1,)), ((), ())), preferred_element_type=F32)


def _rms(x, g):
    return x * lax.rsqrt(jnp.mean(x * x, axis=-1, keepdims=True) + EPS) * g


def _silu(x):
    return x * jax.nn.sigmoid(x)


def _params(*sem):
    return pltpu.CompilerParams(dimension_semantics=sem, vmem_limit_bytes=VMEM_LIMIT)


def _mod_row(mod_ref, row0):
    cond = jnp.maximum(row0 - (T_CTX - LAT_LEN), 0) >> 10
    return mod_ref[0, pl.ds(cond, 1), :]


def _mod_spec(layer):
    return pl.BlockSpec((1, 8, 6 * D), lambda *_: (layer, 0, 0))


def _adaln_kernel(cc_ref, c_ref, w_ref, b_ref, o_ref):
    l = pl.program_id(0)
    row = lax.broadcasted_iota(jnp.int32, (8, 1), 0)
    cond = jnp.where(row == 0, cc_ref[...], 0.0)
    for b in range(N_LAT_SEQ):
        cond = jnp.where(row == 1 + b, c_ref[b:b + 1, :], cond)
    o_ref[0] = _dot(_silu(cond).astype(BF16), w_ref[0].astype(BF16)) + b_ref[pl.ds(l, 1), :]


def _adaln(c_ctx, c, ada_w, ada_b):
    depth = ada_w.shape[0]
    tn = 3072
    return pl.pallas_call(
        _adaln_kernel,
        grid=(depth, 6 * D // tn),
        in_specs=[
            pl.BlockSpec((1, D), lambda l, j: (0, 0)),
            pl.BlockSpec((N_LAT_SEQ, D), lambda l, j: (0, 0)),
            pl.BlockSpec((1, D, tn), lambda l, j: (l, 0, j)),
            pl.BlockSpec((depth, tn), lambda l, j: (0, j)),
        ],
        out_specs=pl.BlockSpec((1, 8, tn), lambda l, j: (l, 0, j)),
        out_shape=jax.ShapeDtypeStruct((depth, 8, 6 * D), F32),
        compiler_params=_params("arbitrary", "arbitrary"),
        name="adaln",
    )(c_ctx.reshape(1, D), c, ada_w, ada_b)


def _rope_tables():
    half = QK_ROPE // 2
    nf = half // 2
    pos = np.arange(LAT_LEN)
    inv = ROPE_THETA ** (-np.arange(nf, dtype=np.float64) / nf)
    k = np.arange(QK_ROPE)
    part, idx = k // half, k % half
    p = np.where(part[None, :] == 0, (pos // GRID_W)[:, None], (pos % GRID_W)[:, None])
    ang = p * inv[idx % nf][None, :]
    cos, sin = np.cos(ang), np.sin(ang)
    first = (idx < nf)[None, :]
    s1 = np.where(first, -sin, 0.0)
    s2 = np.where(first, 0.0, sin)

    def place(t, fill):
        tab = np.full((2 * LAT_LEN, HEAD_PAD), fill, np.float32)
        tab[:LAT_LEN, QK_NOPE:QK_HEAD] = t
        return jnp.asarray(tab)

    return place(cos, 1.0), place(s1, 0.0), place(s2, 0.0)


def _rope(blk, cos, s1, s2):
    return blk * cos + pltpu.roll(blk, 8, 1) * s2 + pltpu.roll(blk, HEAD_PAD - 8, 1) * s1


def _head_norm(blk, g):
    ms = jnp.sum(blk * blk, axis=-1, keepdims=True) * (1.0 / QK_HEAD)
    return blk * lax.rsqrt(ms + EPS) * g


def _even_proj_kernel(xp_ref, xs_ref, mod_ref, nm_ref, win_ref, qan_ref, wqb_ref, kvan_ref,
                      qn_ref, wsc_ref, cos_ref, s1_ref, s2_ref,
                      q_ref, ckv_ref, kpe_ref, sc_ref, sckv_ref, skpe_ref, wt_ref):
    i = pl.program_id(0)
    n_a = Q_LORA + KV_LORA + QK_ROPE

    @pl.when(i == 0)
    def _():
        wt_ref[...] = win_ref[...].astype(BF16)

    x = jnp.where(i < CTX_TILES, xp_ref[...], xs_ref[...])
    m = _mod_row(mod_ref, i * TM)
    h = _rms(x, nm_ref[0:1, :]) * (1.0 + m[:, D:2 * D]) + m[:, 0:D]
    hb = h.astype(BF16)

    za = _dot_nt(hb, wt_ref[0:512, :])
    ckv = _rms(za[:, Q_LORA:Q_LORA + KV_LORA], kvan_ref[...])
    lane = lax.broadcasted_iota(jnp.int32, (1, HEAD_PAD), 1)
    kpe = jnp.where(lane < QK_ROPE, za[:, Q_LORA + KV_LORA:], 0.0)
    ckv_ref[...] = ckv
    kpe_ref[...] = kpe

    @pl.when(i < CTX_TILES)
    def _():
        sckv_ref[...] = ckv
        skpe_ref[...] = kpe[:, :QK_ROPE]

    qa = _rms(za[:, :Q_LORA], qan_ref[...]).astype(BF16)
    cos, s1, s2 = cos_ref[...], s1_ref[...], s2_ref[...]
    qn = qn_ref[...]
    scale = QK_HEAD ** -0.5
    for hp in range(HEADS // 2):
        qq = _dot(qa, wqb_ref[:, hp * 256:(hp + 1) * 256])
        for j in range(2):
            blk = _head_norm(qq[:, j * HEAD_PAD:(j + 1) * HEAD_PAD], qn)
            blk = _rope(blk, cos, s1, s2) * scale
            h0 = (2 * hp + j) * HEAD_PAD
            q_ref[:, h0:h0 + HEAD_PAD] = blk.astype(BF16)

    gb = _dot_nt(hb, wt_ref[n_a:n_a + SC_W, :])
    v = _dot_nt(hb, wt_ref[n_a + SC_W:n_a + 2 * SC_W, :]) * _dot_nt(hb, wt_ref[n_a + 2 * SC_W:n_a + 3 * SC_W, :])
    seq = jnp.where(i < CTX_TILES, CTX_LEN, LAT_LEN)
    r = lax.broadcasted_iota(jnp.int32, (TM, 1), 0) & (seq - 1)
    vp = jnp.where(r == 0, 0.0, pltpu.roll(v, 1, 0))
    vn = jnp.where(r == seq - 1, 0.0, pltpu.roll(v, TM - 1, 0))
    w = wsc_ref[0]
    y = w[0:1] * vp + w[1:2] * v + w[2:3] * vn
    sc_ref[...] = (gb * y).astype(BF16)


def _even_proj(xp, xs, mods, norm_mix, w_in, q_a_norm, wqb, kv_a_norm, qn, w_sc, tabs):
    full = lambda shape: pl.BlockSpec(shape, lambda i: (0,) * len(shape))
    tab = pl.BlockSpec((TM, HEAD_PAD), lambda i: (jnp.where(i < CTX_TILES, 1, 0), 0))
    row = lambda n: pl.BlockSpec((TM, n), lambda i: (i, 0))
    ctx_row = lambda n: pl.BlockSpec((TM, n), lambda i: (jnp.minimum(i, CTX_TILES - 1), 0))
    return pl.pallas_call(
        _even_proj_kernel,
        grid=(N_TILES,),
        in_specs=[
            ctx_row(D),
            pl.BlockSpec((TM, D), lambda i: (jnp.maximum(i - CTX_TILES, 0), 0)),
            _mod_spec(0),
            full((2, D)),
            pl.BlockSpec((IN0_W, D), lambda i: (0, 0), pipeline_mode=pl.Buffered(1)),
            full((1, Q_LORA)),
            full((Q_LORA, HEADS * HEAD_PAD)), full((1, KV_LORA)), full((1, HEAD_PAD)),
            full((1, 3, SC_W)), tab, tab, tab,
        ],
        out_specs=[row(HEADS * HEAD_PAD), row(KV_LORA), row(HEAD_PAD), row(SC_W),
                   ctx_row(KV_LORA), ctx_row(QK_ROPE)],
        out_shape=[
            jax.ShapeDtypeStruct((T, HEADS * HEAD_PAD), BF16),
            jax.ShapeDtypeStruct((T, KV_LORA), F32),
            jax.ShapeDtypeStruct((T, HEAD_PAD), F32),
            jax.ShapeDtypeStruct((T, SC_W), BF16),
            jax.ShapeDtypeStruct((T_CTX, KV_LORA), F32),
            jax.ShapeDtypeStruct((T_CTX, QK_ROPE), F32),
        ],
        scratch_shapes=[pltpu.VMEM((IN0_W, D), BF16)],
        compiler_params=_params("arbitrary"),
        name="even_proj",
    )(xp, xs, mods, norm_mix, w_in, q_a_norm, wqb, kv_a_norm, qn, w_sc, *tabs)


def _kv_proj_kernel(ckv_ref, kpe_ref, wkvb_ref, kn_ref, cos_ref, s1_ref, s2_ref, k_ref, kv_ref):
    kv = _dot(ckv_ref[...].astype(BF16), wkvb_ref[...])
    kv_ref[...] = kv.astype(BF16)
    kpe = pltpu.roll(kpe_ref[...], QK_NOPE, 1)
    lane = lax.broadcasted_iota(jnp.int32, (1, HEAD_PAD), 1)
    kn = kn_ref[...]
    pe_sq = jnp.sum(kpe * kpe, axis=-1, keepdims=True)
    pe = _rope(kpe * kn, cos_ref[...], s1_ref[...], s2_ref[...])
    for h in range(HEADS):
        blk = kv[:, h * HEAD_PAD:(h + 1) * HEAD_PAD]
        nope = jnp.where(lane < QK_NOPE, blk, 0.0)
        ms = (jnp.sum(nope * nope, axis=-1, keepdims=True) + pe_sq) * (1.0 / QK_HEAD)
        k = jnp.where(lane < QK_NOPE, blk * kn, pe) * lax.rsqrt(ms + EPS)
        k_ref[:, h * HEAD_PAD:(h + 1) * HEAD_PAD] = k.astype(BF16)


def _kv_proj(ckv, kpe, wkvb, kn, tabs, tab_index, name, TKV):
    n = ckv.shape[0]
    full = lambda shape: pl.BlockSpec(shape, lambda i: (0,) * len(shape))
    tab = pl.BlockSpec((TKV, HEAD_PAD), lambda i: (tab_index(i), 0))
    row = lambda w: pl.BlockSpec((TKV, w), lambda i: (i, 0))
    return pl.pallas_call(
        _kv_proj_kernel,
        grid=(n // TKV,),
        in_specs=[row(KV_LORA), row(HEAD_PAD), full((KV_LORA, HEADS * HEAD_PAD)), full((1, HEAD_PAD)),
                  tab, tab, tab],
        out_specs=[row(HEADS * HEAD_PAD), row(HEADS * HEAD_PAD)],
        out_shape=[jax.ShapeDtypeStruct((n, HEADS * HEAD_PAD), BF16)] * 2,
        compiler_params=_params("arbitrary"),
        name=name,
    )(ckv, kpe, wkvb, kn, *tabs)


def _pair_out(o0, o1):
    lane = lax.broadcasted_iota(jnp.int32, (1, HEAD_PAD), 1)
    return jnp.where(lane < V_HEAD, pltpu.roll(o0, V_HEAD, 1), o1).astype(BF16)


CTX_SEQS = 8


def _attn_ctx_kernel(q_ref, k_ref, kv_ref, o_ref):
    for b in range(CTX_SEQS):
        rows = slice(b * CTX_LEN, (b + 1) * CTX_LEN)
        for hp in range(HEADS // 2):
            outs = []
            for j in range(2):
                lanes = slice((2 * hp + j) * HEAD_PAD, (2 * hp + j + 1) * HEAD_PAD)
                s = _dot_nt(q_ref[rows, lanes], k_ref[rows, lanes])
                p = jnp.exp(s - jnp.max(s, axis=-1, keepdims=True))
                l = jnp.sum(p, axis=-1, keepdims=True)
                outs.append(_dot(p.astype(BF16), kv_ref[rows, lanes]) / l)
            o_ref[rows, hp * HEAD_PAD:(hp + 1) * HEAD_PAD] = _pair_out(*outs)


def _attn_ctx(q, k, kv):
    blk = pl.BlockSpec((CTX_SEQS * CTX_LEN, HEADS * HEAD_PAD), lambda b: (b, 0))
    return pl.pallas_call(
        _attn_ctx_kernel,
        grid=(N_CTX_SEQ // CTX_SEQS,),
        in_specs=[blk, blk, blk],
        out_specs=pl.BlockSpec((CTX_SEQS * CTX_LEN, HEADS * V_HEAD), lambda b: (b, 0)),
        out_shape=jax.ShapeDtypeStruct((T_CTX, HEADS * V_HEAD), BF16),
        compiler_params=_params("arbitrary"),
        name="attn_ctx",
    )(q, k, kv)


LAT_HEADS = 8


def _attn_lat_kernel(q_ref, kc_ref, kvc_ref, kl_ref, kvl_ref, o_ref):
    for hp in range(LAT_HEADS // 2):
        outs = []
        for j in range(2):
            h0 = (2 * hp + j) * HEAD_PAD
            lanes = slice(h0, h0 + HEAD_PAD)
            q = q_ref[:, lanes]
            sc = _dot_nt(q, kc_ref[:, lanes])
            sl = _dot_nt(q, kl_ref[:, lanes])
            m = jnp.maximum(jnp.max(sc, axis=-1, keepdims=True), jnp.max(sl, axis=-1, keepdims=True))
            pc, pl_ = jnp.exp(sc - m), jnp.exp(sl - m)
            l = jnp.sum(pc, axis=-1, keepdims=True) + jnp.sum(pl_, axis=-1, keepdims=True)
            o = _dot(pc.astype(BF16), kvc_ref[:, lanes]) + _dot(pl_.astype(BF16), kvl_ref[:, lanes])
            outs.append(o / l)
        o_ref[:, hp * HEAD_PAD:(hp + 1) * HEAD_PAD] = _pair_out(*outs)


def _attn_lat(q, kc, kvc, k, kv):
    nq = LAT_LEN // TQ
    q0 = T_CTX // TQ
    kl0 = T_CTX // LAT_LEN
    width = LAT_HEADS * HEAD_PAD
    lat = pl.BlockSpec((LAT_LEN, width), lambda b, hg, t: (kl0 + b, hg))
    ctx = pl.BlockSpec((PAST, width), lambda b, hg, t: (b, hg))
    return pl.pallas_call(
        _attn_lat_kernel,
        grid=(N_LAT_SEQ, HEADS // LAT_HEADS, nq),
        in_specs=[pl.BlockSpec((TQ, width), lambda b, hg, t: (q0 + b * nq + t, hg)), ctx, ctx, lat, lat],
        out_specs=pl.BlockSpec((TQ, LAT_HEADS * V_HEAD), lambda b, hg, t: (b * nq + t, hg)),
        out_shape=jax.ShapeDtypeStruct((T_LAT, HEADS * V_HEAD), BF16),
        compiler_params=_params("arbitrary", "arbitrary", "arbitrary"),
        name="attn_lat",
    )(q, kc, kvc, k, kv)


FFN_NC = D_FF // FFN_FC
FFN_TM = 512
FFN_CTX_TILES = T_CTX // FFN_TM


def _ffn_kernel(oc_ref, ol_ref, sc_ref, xp_ref, xs_ref, wo_ref, nf_ref, wg_ref, wu_ref, wd_ref,
                mod0_ref, mod1_ref, nm_ref, x2_ref, h3_ref, wg_all, wu_all, wd_all, x1_ref, hs_ref, act_ref):
    t = pl.program_id(0)

    i = jnp.maximum(t - (FFN_NC - 1), 0)
    m0, m1 = _mod_row(mod0_ref, i * FFN_TM), _mod_row(mod1_ref, i * FFN_TM)

    def mixer():
        ctx = i < FFN_CTX_TILES
        attn = jnp.where(ctx, oc_ref[...], ol_ref[...])
        x = jnp.where(ctx, xp_ref[...], xs_ref[...])
        x1 = x + m0[:, 2 * D:3 * D] * _dot(jnp.concatenate([attn, sc_ref[...]], axis=1), wo_ref[...])
        h = (_rms(x1, nf_ref[0:1, :]) * (1.0 + m0[:, 4 * D:5 * D]) + m0[:, 3 * D:4 * D]).astype(BF16)
        return x1, h

    def up(h, c):
        return (_silu(_dot(h, wg_all[c])) * _dot(h, wu_all[c])).astype(BF16)

    def down(x1, act):
        x2 = x1 + m0[:, 5 * D:6 * D] * _dot(act, wd_all[...])
        x2_ref[...] = x2
        h3_ref[...] = (_rms(x2, nm_ref[1:2, :]) * (1.0 + m1[:, D:2 * D]) + m1[:, 0:D]).astype(BF16)

    @pl.when(t == 0)
    def _():
        x1, h = mixer()
        x1_ref[...] = x1
        hs_ref[...] = h

    @pl.when(t < FFN_NC)
    def _stage():
        wg_all[t] = wg_ref[...].astype(BF16)
        wu_all[t] = wu_ref[...].astype(BF16)
        wd_all[pl.ds(pl.multiple_of(t * FFN_FC, FFN_FC), FFN_FC), :] = wd_ref[...].astype(BF16)
        act_ref[t] = up(hs_ref[...], t)

    @pl.when(t == FFN_NC - 1)
    def _():
        down(x1_ref[...], jnp.concatenate([act_ref[c] for c in range(FFN_NC)], axis=1))

    @pl.when(t > FFN_NC - 1)
    def _tile():
        x1, h = mixer()
        down(x1, jnp.concatenate([up(h, c) for c in range(FFN_NC)], axis=1))


def _ffn(oc, ol, sc, xp, xs, wo, norm_ffn, w_gu, w_down, mods, norm_mix):
    chunk = lambda t: jnp.minimum(t, FFN_NC - 1)
    tile = lambda t: jnp.maximum(t - (FFN_NC - 1), 0)
    full = lambda shape: pl.BlockSpec(shape, lambda t: (0,) * len(shape))
    row = lambda n: pl.BlockSpec((FFN_TM, n), lambda t: (tile(t), 0))
    first = lambda n: pl.BlockSpec((FFN_TM, n), lambda t: (jnp.minimum(tile(t), FFN_CTX_TILES - 1), 0))
    second = lambda n: pl.BlockSpec((FFN_TM, n), lambda t: (jnp.maximum(tile(t) - FFN_CTX_TILES, 0), 0))
    return pl.pallas_call(
        _ffn_kernel,
        grid=(FFN_NC - 1 + T // FFN_TM,),
        in_specs=[first(HEADS * V_HEAD), second(HEADS * V_HEAD), row(SC_W), first(D), second(D),
                  full((HEADS * V_HEAD + SC_W, D)), full((2, D)),
                  pl.BlockSpec((D, FFN_FC), lambda t: (0, chunk(t))),
                  pl.BlockSpec((D, FFN_FC), lambda t: (0, FFN_NC + chunk(t))),
                  pl.BlockSpec((FFN_FC, D), lambda t: (chunk(t), 0)),
                  _mod_spec(0), _mod_spec(1), full((2, D))],
        out_specs=[row(D), row(D)],
        out_shape=[jax.ShapeDtypeStruct((T, D), F32), jax.ShapeDtypeStruct((T, D), BF16)],
        scratch_shapes=[pltpu.VMEM((FFN_NC, D, FFN_FC), BF16), pltpu.VMEM((FFN_NC, D, FFN_FC), BF16),
                        pltpu.VMEM((D_FF, D), BF16), pltpu.VMEM((FFN_TM, D), F32), pltpu.VMEM((FFN_TM, D), BF16),
                        pltpu.VMEM((FFN_NC, FFN_TM, FFN_FC), BF16)],
        compiler_params=_params("arbitrary"),
        name="ffn_dense",
    )(oc, ol, sc, xp, xs, wo, norm_ffn, w_gu, w_gu, w_down, mods, mods, norm_mix)


CONF_CB = 256
CONF_SEG = 256
CONF_HALO = 16
CONF_SEGP = CONF_SEG + 2 * CONF_HALO
CONF_PIECE = 256


def _conf_kernel(h_ref, x2_ref, w1_ref, b1_ref, wdw_ref, bdw_ref, lng_ref, lnb_ref, w2_ref, b2_ref,
                 mod_ref, nf_ref, rt_ref, x3_ref, h4_ref, lg_ref, pad_ref, conv_ref):
    i = pl.program_id(0)
    nseg = TM // CONF_SEG
    h = h_ref[...]
    joined = jnp.where(i < CTX_TILES, 0.0, 1.0)
    zeros_halo = jnp.zeros((CONF_HALO, CONF_CB), F32)
    for cb in range(D // CONF_CB):
        c0 = cb * CONF_CB
        a = _dot(h, w1_ref[:, c0:c0 + CONF_CB]) + b1_ref[:, c0:c0 + CONF_CB]
        g = _dot(h, w1_ref[:, D + c0:D + c0 + CONF_CB]) + b1_ref[:, D + c0:D + c0 + CONF_CB]
        u = a * jax.nn.sigmoid(g)
        for s in range(nseg):
            base = s * CONF_SEGP
            top = u[s * CONF_SEG - CONF_HALO:s * CONF_SEG] * joined if s > 0 else zeros_halo
            bot = (u[(s + 1) * CONF_SEG:(s + 1) * CONF_SEG + CONF_HALO] * joined
                   if s < nseg - 1 else zeros_halo)
            pad_ref[0, base:base + CONF_HALO, :] = top
            pad_ref[0, base + CONF_HALO:base + CONF_HALO + CONF_SEG, :] = u[s * CONF_SEG:(s + 1) * CONF_SEG]
            pad_ref[0, base + CONF_HALO + CONF_SEG:base + CONF_SEGP, :] = bot

        p0 = pad_ref[0]
        rows = nseg * CONF_SEGP
        for b in range(1, 8):
            pad_ref[b] = pltpu.roll(p0, rows - b, 0)

        def piece(t, carry):
            s = t // (CONF_SEG // CONF_PIECE)
            q0 = (t % (CONF_SEG // CONF_PIECE)) * CONF_PIECE
            src = pl.multiple_of(s * CONF_SEGP + q0, 8)
            acc = jnp.zeros((CONF_PIECE, CONF_CB), F32)
            for j in range(CONF_K):
                hi, lo = (j + 1) // 8, (j + 1) % 8
                acc = acc + wdw_ref[0, j:j + 1, c0:c0 + CONF_CB] * pad_ref[lo, pl.ds(src + 8 * hi, CONF_PIECE), :]
            dst = pl.multiple_of(s * CONF_SEG + q0, 8)
            conv_ref[pl.ds(dst, CONF_PIECE), c0:c0 + CONF_CB] = acc + bdw_ref[:, c0:c0 + CONF_CB]
            return carry

        lax.fori_loop(0, TM // CONF_PIECE, piece, 0)

    m = _mod_row(mod_ref, i * TM)
    half = TM // 2
    for r0 in (0, half):
        rows = slice(r0, r0 + half)
        y = conv_ref[rows, :]
        mu = jnp.mean(y, axis=-1, keepdims=True)
        yc = y - mu
        var = jnp.mean(yc * yc, axis=-1, keepdims=True)
        y = _silu(yc * lax.rsqrt(var + EPS) * lng_ref[...] + lnb_ref[...])
        out = _dot(y.astype(BF16), w2_ref[...]) + b2_ref[...]
        x3 = x2_ref[rows, :] + m[:, 2 * D:3 * D] * out
        x3_ref[rows, :] = x3
        h4 = _rms(x3, nf_ref[1:2, :]) * (1.0 + m[:, 4 * D:5 * D]) + m[:, 3 * D:4 * D]
        h4_ref[rows, :] = h4.astype(BF16)
        lg_ref[:, rows] = lax.dot_general(rt_ref[...], h4, (((1,), (1,)), ((), ())),
                                          precision=lax.Precision.HIGHEST, preferred_element_type=F32)


def _conf(h3, x2, w1, b1, wdw, bdw, lng, lnb, w2, b2, mods, norm_ffn1, router_t):
    full = lambda shape: pl.BlockSpec(shape, lambda i: (0,) * len(shape))
    row = lambda n: pl.BlockSpec((TM, n), lambda i: (i, 0))
    return pl.pallas_call(
        _conf_kernel,
        grid=(N_TILES,),
        in_specs=[row(D), row(D), full((D, 2 * D)), full((1, 2 * D)), full((1, CONF_K, D)), full((1, D)),
                  full((1, D)), full((1, D)), full((D, D)), full((1, D)),
                  _mod_spec(1), full((2, D)), full((N_EXP, D))],
        out_specs=[row(D), row(D), pl.BlockSpec((N_EXP, TM), lambda i: (0, i))],
        out_shape=[jax.ShapeDtypeStruct((T, D), F32), jax.ShapeDtypeStruct((T, D), BF16),
                   jax.ShapeDtypeStruct((N_EXP, T), F32)],
        scratch_shapes=[pltpu.VMEM((8, (TM // CONF_SEG) * CONF_SEGP, CONF_CB), F32),
                        pltpu.VMEM((TM, D), F32)],
        compiler_params=_params("arbitrary"),
        name="conformer_conv",
    )(h3, x2, w1, b1, wdw, bdw, lng, lnb, w2, b2, mods, norm_ffn1, router_t)


TB = 256
N_TB = T // TB
SUB = 128
SUBS = 24
SM = SUBS * SUB
N_SUB_MAX = 2 * T // SUB + N_EXP
N_SUP_MAX = (N_SUB_MAX + N_EXP * (SUBS - 1)) // SUBS
YS_ROWS = (N_SUB_MAX + 4) * SUB
WIN_ALIGN = 16
WIN_HALF = TB // 2 + WIN_ALIGN
FIRST_STRIDE = 32
UNIT_STRIDE = 64
GATHER_BLOCKS = 4


def _route_kernel(lg_ref, g_ref, rank_ref, first_ref):
    lg = lg_ref[...]
    idx = lax.broadcasted_iota(jnp.int32, lg.shape, 0).astype(F32)
    none = float(N_EXP)
    m1 = jnp.max(lg, axis=0, keepdims=True)
    i1 = jnp.min(jnp.where(lg == m1, idx, none), axis=0, keepdims=True)
    rest = jnp.where(idx == i1, -jnp.inf, lg)
    m2 = jnp.max(rest, axis=0, keepdims=True)
    i2 = jnp.min(jnp.where(rest == m2, idx, none), axis=0, keepdims=True)
    e = jnp.exp(m2 - m1)
    w1 = 1.0 / (1.0 + e)
    w2 = e / (1.0 + e)
    g_ref[...] = jnp.where(idx == i1, w1, 0.0) + jnp.where(idx == i2, w2, 0.0)

    mask = jnp.where(idx == i1, 1.0, 0.0) + jnp.where(idx == i2, 1.0, 0.0)
    before = (lax.broadcasted_iota(jnp.int32, (TB, TB), 0) < lax.broadcasted_iota(jnp.int32, (TB, TB), 1))
    before = jnp.where(before, 1.0, 0.0).astype(BF16)
    lane = lax.broadcasted_iota(jnp.int32, (N_EXP, 128), 1)
    carry = jnp.zeros((N_EXP, 1), F32)
    first = jnp.zeros((N_EXP, 128), F32)
    for b in range(N_TB):
        mb = mask[:, b * TB:(b + 1) * TB]
        local = _dot(mb.astype(BF16), before)
        rank_ref[:, b * TB:(b + 1) * TB] = jnp.where(mb > 0.0, local + carry, -1.0)
        first = jnp.where(lane == b, carry, first)
        carry = carry + jnp.sum(mb, axis=1, keepdims=True)
    first_ref[...] = jnp.where(lane == N_TB, carry, first)


def _route(logits_t):
    return pl.pallas_call(
        _route_kernel,
        out_shape=[jax.ShapeDtypeStruct((N_EXP, T), F32), jax.ShapeDtypeStruct((N_EXP, T), F32),
                   jax.ShapeDtypeStruct((N_EXP, 128), F32)],
        compiler_params=pltpu.CompilerParams(vmem_limit_bytes=VMEM_LIMIT),
        name="route",
    )(logits_t)


def _moe_plan(first):
    first = first[:, :FIRST_STRIDE].astype(jnp.int32)
    cnt = first[:, N_TB]
    nt = (cnt + (SUB - 1)) // SUB
    off_end = jnp.cumsum(nt)
    off = off_end - nt
    nsub = off_end[-1]
    nsup = (nt + (SUBS - 1)) // SUBS
    sup_end = jnp.cumsum(nsup)
    sup_off = sup_end - nsup
    s = jnp.minimum(jnp.arange(N_SUP_MAX), sup_end[-1] - 1)
    valid = jnp.arange(N_SUP_MAX) < sup_end[-1]
    se = jnp.sum(s[:, None] >= sup_end[None, :], axis=1)
    mine = se[:, None] == jnp.arange(N_EXP)[None, :]
    pick = lambda v: jnp.sum(jnp.where(mine, v[None, :], 0), axis=1)
    sk0 = (s - pick(sup_off)) * SUBS
    sns = jnp.where(valid, jnp.clip(pick(nt) - sk0, 0, SUBS), 0)
    sj0 = pick(off) + sk0
    base = (jnp.arange(UNIT_STRIDE) * SUB)[None, :, None]
    blo = jnp.minimum(jnp.sum(first[:, None, 1:N_TB + 1] <= base, axis=2), N_TB - 1)
    end = jnp.minimum(base + SUB, cnt[:, None, None])
    bhi = jnp.maximum(jnp.sum(first[:, None, :N_TB] < end, axis=2) - 1, blo)
    ng = (bhi - blo) // GATHER_BLOCKS + 1
    start = SUB * off[:, None] + first[:, :N_TB]
    lead = start & (WIN_ALIGN - 1)
    wina = start - lead
    rel = first[:, :N_TB] - lead
    need = lead + (first[:, 1:N_TB + 1] - first[:, :N_TB]) > WIN_HALF
    winb = lax.cummax(jnp.where(need, wina + WIN_HALF, 0), axis=1)
    wide = jnp.any(need, axis=0)
    flat = lambda parts: jnp.concatenate([p.astype(jnp.int32).reshape(-1) for p in parts])
    gmm = dict(se=se, sk0=sk0, sns=sns, sj0=sj0, nsub=nsub, first=first, blo=blo, ng=ng)
    comb = dict(wina=wina.T, winb=winb.T, rel=rel.T, wide=wide)
    return flat([gmm[k] for k in _GMM_TAB]), flat([comb[k] for k in _COMB_TAB])


def _offsets(sizes):
    out, pos = {}, 0
    for name, n in sizes.items():
        out[name], pos = pos, pos + n
    return out


_GMM_TAB = _offsets(dict(se=N_SUP_MAX, sk0=N_SUP_MAX, sns=N_SUP_MAX, sj0=N_SUP_MAX, nsub=1,
                         first=N_EXP * FIRST_STRIDE, blo=N_EXP * UNIT_STRIDE, ng=N_EXP * UNIT_STRIDE))
_COMB_TAB = _offsets(dict(wina=N_TB * N_EXP, winb=N_TB * N_EXP, rel=N_TB * N_EXP, wide=N_TB))


class _Section:
    def __init__(self, ref, offset):
        self.ref, self.offset = ref, offset

    def __getitem__(self, i):
        return self.ref[self.offset + i]


def _moe_gmm_kernel(tab_ref, x_ref, rank_ref, gate_ref, wg_ref, wu_ref, wd_ref, ys_ref,
                    xs_ref, gs_ref, yacc_ref, wgb_ref, wub_ref, wdb_ref, sem):
    se_ref, sk0_ref, sns_ref, sj0_ref, nsub_ref, first_ref, blo_ref, ng_ref = (
        _Section(tab_ref, _GMM_TAB[k]) for k in ("se", "sk0", "sns", "sj0", "nsub", "first", "blo", "ng"))
    s, c = pl.program_id(0), pl.program_id(1)
    nc = pl.num_programs(1)
    e, k0, ns = se_ref[s], sk0_ref[s], sns_ref[s]

    def sub_rows(k):
        return pl.ds(pl.multiple_of(k * SUB, SUB), SUB)

    def out_copy(k, row0):
        dst = ys_ref.at[pl.ds(pl.multiple_of(row0 + k * SUB, SUB), SUB)]
        return pltpu.make_async_copy(xs_ref.at[sub_rows(k)], dst, sem.at[k])

    @pl.when((ns > 0) & (c == 0))
    def _gather():
        def group(k, g):
            slot = (lax.broadcasted_iota(jnp.int32, (SUB, 1), 0) + (k0 + k) * SUB).astype(F32)
            b0 = blo_ref[e * UNIT_STRIDE + k0 + k] + g * GATHER_BLOCKS
            t0 = pl.multiple_of(jnp.minimum(b0, N_TB - GATHER_BLOCKS) * TB, TB)
            lo = first_ref[e * FIRST_STRIDE + b0].astype(F32)
            cols = pl.ds(t0, GATHER_BLOCKS * TB)
            hit = rank_ref[pl.ds(e, 1), cols] == jnp.where(slot >= lo, slot, -2.0)
            rows = _dot(jnp.where(hit, 1.0, 0.0).astype(BF16), x_ref[cols, :])
            gate = jnp.sum(jnp.where(hit, gate_ref[pl.ds(e, 1), cols], 0.0), axis=-1, keepdims=True)
            return rows, gate

        def first(k):
            rows, gate = group(k, 0)
            xs_ref[sub_rows(k), :] = rows.astype(BF16)
            gs_ref[sub_rows(k), :] = gate
            yacc_ref[sub_rows(k), :] = jnp.zeros((SUB, D), F32)

        def more(k):
            def body(g, carry):
                rows, gate = group(k, g)
                xs_ref[sub_rows(k), :] = (xs_ref[sub_rows(k), :].astype(F32) + rows).astype(BF16)
                gs_ref[sub_rows(k), :] += gate
                return carry

            lax.fori_loop(1, ng_ref[e * UNIT_STRIDE + k0 + k], body, 0)

        def quad(p, carry):
            for j in range(4):
                first(4 * p + j)
            for j in range(4):
                more(4 * p + j)
            return carry

        def single(k, carry):
            first(k)
            more(k)
            return carry

        lax.fori_loop(0, ns >> 2, quad, 0)
        lax.fori_loop(ns & ~3, ns, single, 0)

    @pl.when(ns > 0)
    def _compute():
        row0 = sj0_ref[s] * SUB

        def swiglu(rows, wg, wu, wd):
            x = xs_ref[rows, :]
            g = _dot(x, wg)
            u = _dot(x, wu)
            yacc_ref[rows, :] += _dot((_silu(g) * u).astype(BF16), wd)

        def finish(first_sub, n):
            @pl.when(c == nc - 1)
            def _():
                for k in range(n):
                    rows = sub_rows(first_sub + k)
                    xs_ref[rows, :] = (yacc_ref[rows, :] * gs_ref[rows, :]).astype(BF16)
                    out_copy(first_sub + k, row0).start()

        def first_chain(rows):
            wg, wu, wd = wg_ref[0].astype(BF16), wu_ref[0].astype(BF16), wd_ref[0].astype(BF16)
            wgb_ref[...] = wg
            wub_ref[...] = wu
            wdb_ref[...] = wd
            swiglu(rows, wg, wu, wd)

        @pl.when(ns >= 4)
        def _():
            first_chain(pl.ds(0, 4 * SUB))
            finish(0, 4)

        @pl.when(ns < 4)
        def _():
            first_chain(pl.ds(0, SUB))
            finish(0, 1)

        done = jnp.where(ns >= 4, 4, 1)
        rest = ns - done

        def chain(first_sub, n):
            rows = pl.ds(pl.multiple_of(first_sub * SUB, SUB), n * SUB)
            swiglu(rows, wgb_ref[...], wub_ref[...], wdb_ref[...])
            finish(first_sub, n)

        def eight(k, carry):
            chain(done + 8 * k, 8)
            return carry

        lax.fori_loop(0, rest >> 3, eight, 0)
        done8 = done + (rest & ~7)
        for n in (4, 2, 1):
            @pl.when((rest & n) != 0)
            def _(n=n):
                chain(done8 + (rest & (7 & ~(2 * n - 1))), n)

    @pl.when((ns > 0) & (c == nc - 1))
    def _store_done():
        row0 = sj0_ref[s] * SUB

        def done(k, carry):
            out_copy(k, row0).wait()
            return carry

        lax.fori_loop(0, ns, done, 0)

    @pl.when((s == pl.num_programs(0) - 1) & (c == nc - 1))
    def _zero_tail():
        xs_ref[0:SUB, :] = jnp.zeros((SUB, D), BF16)
        nsub = nsub_ref[0]

        def fill(k, carry):
            cp = out_copy(0, (nsub + k) * SUB)
            cp.start()
            cp.wait()
            return carry

        lax.fori_loop(0, YS_ROWS // SUB - nsub, fill, 0)


def _moe_gmm(tab, h4, rank, gates, w_gu, w_down):
    nc = D_FFE // MOE_FC

    def expert(s, tab):
        return tab[_GMM_TAB["se"] + s]

    def chunk(s, c, tab):
        return jnp.where(tab[_GMM_TAB["sns"] + s] > 0, c, nc - 1)

    return pl.pallas_call(
        _moe_gmm_kernel,
        grid_spec=pltpu.PrefetchScalarGridSpec(
            num_scalar_prefetch=1,
            grid=(N_SUP_MAX, nc),
            in_specs=[
                pl.BlockSpec((T, D), lambda s, c, tab: (0, 0), pipeline_mode=pl.Buffered(1)),
                pl.BlockSpec((N_EXP, T), lambda s, c, tab: (0, 0)),
                pl.BlockSpec((N_EXP, T), lambda s, c, tab: (0, 0)),
                pl.BlockSpec((1, D, MOE_FC), lambda s, c, tab: (expert(s, tab), 0, chunk(s, c, tab))),
                pl.BlockSpec((1, D, MOE_FC), lambda s, c, tab: (expert(s, tab), 0, nc + chunk(s, c, tab))),
                pl.BlockSpec((1, MOE_FC, D), lambda s, c, tab: (expert(s, tab), chunk(s, c, tab), 0)),
            ],
            out_specs=pl.BlockSpec(memory_space=pl.ANY),
            scratch_shapes=[
                pltpu.VMEM((SM, D), BF16), pltpu.VMEM((SM, 1), F32), pltpu.VMEM((SM, D), F32),
                pltpu.VMEM((D, MOE_FC), BF16), pltpu.VMEM((D, MOE_FC), BF16), pltpu.VMEM((MOE_FC, D), BF16),
                pltpu.SemaphoreType.DMA((SUBS,)),
            ],
        ),
        out_shape=jax.ShapeDtypeStruct((YS_ROWS, D), BF16),
        compiler_params=_params("arbitrary", "arbitrary"),
        name="moe_gmm",
    )(tab, h4, rank, gates, w_gu, w_gu, w_down)


def _moe_combine_kernel(tab_ref, *refs):
    rel_ref, wide_ref = _Section(tab_ref, _COMB_TAB["rel"]), _Section(tab_ref, _COMB_TAB["wide"])
    ya, yb = refs[:N_EXP], refs[N_EXP:2 * N_EXP]
    rank_ref, x3_ref, mod_ref, yp_ref, ys_ref, ycat_ref, acc_ref = refs[2 * N_EXP:]
    b = pl.program_id(0)
    row = lax.broadcasted_iota(jnp.int32, (WIN_HALF, 1), 0)

    def onehot(e, first_row):
        slot = (row + (rel_ref[b * N_EXP + e] + first_row)).astype(F32)
        return jnp.where(rank_ref[e:e + 1, :] == slot, 1.0, 0.0).astype(BF16)

    def gather(y_refs, first_row, base):
        pieces = []
        for e in range(N_EXP):
            ycat_ref[base + e * WIN_HALF:base + (e + 1) * WIN_HALF, :] = y_refs[e][...]
            pieces.append(onehot(e, first_row))
        return pieces

    def combine(pieces, rows):
        return lax.dot_general(jnp.concatenate(pieces, axis=0), ycat_ref[0:rows, :], (((0,), (0,)), ((), ())),
                               preferred_element_type=F32)

    @pl.when(wide_ref[b] == 0)
    def _():
        acc_ref[...] = combine(gather(ya, 0, 0), N_EXP * WIN_HALF)

    @pl.when(wide_ref[b] != 0)
    def _():
        pieces = gather(ya, 0, 0) + gather(yb, WIN_HALF, N_EXP * WIN_HALF)
        acc_ref[...] = combine(pieces, 2 * N_EXP * WIN_HALF)

    out = x3_ref[...] + _mod_row(mod_ref, b * TB)[:, 5 * D:6 * D] * acc_ref[...]

    @pl.when(b < T_CTX // TB)
    def _():
        yp_ref[...] = out

    @pl.when(b >= T_CTX // TB)
    def _():
        ys_ref[...] = out


def _moe_combine(tab, ysorted, rank, x3, mods):
    ctx_blocks = T_CTX // TB

    def window(e, second):
        def index(b, tab):
            start = tab[_COMB_TAB["winb" if second else "wina"] + b * N_EXP + e]
            return pl.multiple_of(start, WIN_ALIGN), 0
        return pl.BlockSpec((pl.Element(WIN_HALF), pl.Element(D)), index)

    return pl.pallas_call(
        _moe_combine_kernel,
        grid_spec=pltpu.PrefetchScalarGridSpec(
            num_scalar_prefetch=1,
            grid=(N_TB,),
            in_specs=[window(e, False) for e in range(N_EXP)] + [window(e, True) for e in range(N_EXP)] + [
                pl.BlockSpec((N_EXP, TB), lambda b, *_: (0, b)),
                pl.BlockSpec((TB, D), lambda b, *_: (b, 0)),
                _mod_spec(1),
            ],
            out_specs=[pl.BlockSpec((TB, D), lambda b, *_: (jnp.minimum(b, ctx_blocks - 1), 0)),
                       pl.BlockSpec((TB, D), lambda b, *_: (jnp.maximum(b - ctx_blocks, 0), 0))],
            scratch_shapes=[pltpu.VMEM((2 * N_EXP * WIN_HALF, D), BF16), pltpu.VMEM((TB, D), F32)],
        ),
        out_shape=[jax.ShapeDtypeStruct((T_CTX, D), F32), jax.ShapeDtypeStruct((T_LAT, D), F32)],
        compiler_params=_params("arbitrary"),
        name="moe_combine",
    )(tab, *([ysorted] * (2 * N_EXP)), rank, x3, mods)


def _pad_heads(w, width):
    lead = w.shape[:-1]
    w = w.reshape(*lead, HEADS, width)
    w = jnp.pad(w, [(0, 0)] * len(lead) + [(0, 0), (0, HEAD_PAD - width)])
    return w.reshape(*lead, HEADS * HEAD_PAD)


def kernel(x_prompt, x_sample, cache_ckv, cache_kpe, c, c_ctx, ada_w, ada_b, norm_mix, norm_ffn, w_in, q_a_norm,
           w_qb, kv_a_norm, w_kvb, q_norm, k_norm, w_sc, w_o, ffn_gu, ffn_down, conv_pw1, conv_pw1_b, conv_dw,
           conv_dw_b, conv_ln_g, conv_ln_b, conv_pw2, conv_pw2_b, router, moe_gu, moe_down):
    xp = x_prompt.reshape(T_CTX, D)
    xs = x_sample.reshape(T_LAT, D)

    mods = _adaln(c_ctx, c, ada_w, ada_b)

    wqb = _pad_heads(w_qb[0], QK_HEAD).astype(BF16)
    wkvb = w_kvb[0].astype(BF16)
    qn = jnp.pad(q_norm[0], (0, HEAD_PAD - QK_HEAD)).reshape(1, HEAD_PAD)
    kn = jnp.pad(k_norm[0], (0, HEAD_PAD - QK_HEAD)).reshape(1, HEAD_PAD)
    tabs = _rope_tables()

    w_in_t = jnp.swapaxes(w_in[0], 0, 1)
    q, ckv, kpe, sc, state_ckv, state_kpe = _even_proj(xp, xs, mods, norm_mix, w_in_t, q_a_norm, wqb, kv_a_norm,
                                                       qn, w_sc, tabs)

    lat_tile0 = T_CTX // TKV
    ident = LAT_LEN // TKV
    k, kv = _kv_proj(ckv, kpe, wkvb, kn, tabs,
                     lambda i: jnp.where(i < lat_tile0, ident, (i - lat_tile0) % ident), "kv_proj", TKV)
    cache_kpe_p = jnp.pad(cache_kpe[:, 0].reshape(N_LAT_SEQ * PAST, QK_ROPE), ((0, 0), (0, HEAD_PAD - QK_ROPE)))
    kc, kvc = _kv_proj(cache_ckv[:, 0].reshape(N_LAT_SEQ * PAST, KV_LORA), cache_kpe_p, wkvb, kn, tabs,
                       lambda i: LAT_LEN // (N_LAT_SEQ * PAST), "kv_proj_cache", N_LAT_SEQ * PAST)

    oc = _attn_ctx(q, k, kv)
    ol = _attn_lat(q, kc, kvc, k, kv)
    x2, h3 = _ffn(oc, ol, sc, xp, xs, w_o[0].astype(BF16), norm_ffn, ffn_gu[0], ffn_down[0], mods, norm_mix)

    x3, h4, logits_t = _conf(h3, x2, conv_pw1[0].astype(BF16), conv_pw1_b, conv_dw, conv_dw_b, conv_ln_g,
                             conv_ln_b, conv_pw2[0].astype(BF16), conv_pw2_b, mods, norm_ffn, router[0].T)
    gates, rank, first = _route(logits_t)
    gmm_tab, comb_tab = _moe_plan(first)
    ysorted = _moe_gmm(gmm_tab, h4, rank, gates, moe_gu[0], moe_down[0])
    yp, ys = _moe_combine(comb_tab, ysorted, rank, x3, mods)

    return (yp.reshape(N_CTX_SEQ, CTX_LEN, D), ys.reshape(N_LAT_SEQ, LAT_LEN, D),
            state_ckv.reshape(N_CTX_SEQ, 1, CTX_LEN, KV_LORA), state_kpe.reshape(N_CTX_SEQ, 1, CTX_LEN, QK_ROPE))
```

```python
import jax
import jax.numpy as jnp
import numpy as np
from jax import lax
from jax.experimental import pallas as pl
from jax.experimental.pallas import tpu as pltpu

F32 = jnp.float32
BF16 = jnp.bfloat16

D = 1024
N_CTX_SEQ, CTX_LEN = 16, 256
N_LAT_SEQ, LAT_LEN = 2, 1024
T_CTX = N_CTX_SEQ * CTX_LEN
T_LAT = N_LAT_SEQ * LAT_LEN
T = T_CTX + T_LAT
PAST = 256
GRID_W = 64
HEADS = 8
QK_NOPE, QK_ROPE, V_HEAD = 64, 32, 64
QK_HEAD = QK_NOPE + QK_ROPE
HEAD_PAD = 128
Q_LORA, KV_LORA = 256, 128
SC_W = 512
IN0_W = Q_LORA + KV_LORA + QK_ROPE + 3 * SC_W
CONF_K = 31
D_FF = 2816
N_EXP = 8
D_FFE = 3584
EPS = 1e-6
ROPE_THETA = 10000.0

TM = 1024
N_TILES = T // TM
CTX_TILES = T_CTX // TM
TKV = 1024
TQ = 512
FFN_FC = 256
MOE_FC = 512
VMEM_LIMIT = 56 * 1024 * 1024


def _dot(a, b):
    return jnp.dot(a, b, preferred_element_type=F32)


def _dot_nt(a, b):
    return lax.dot_general(a, b, (((1,), (1,)), ((), ())), preferred_element_type=F32)


def _rms(x, g):
    return x * lax.rsqrt(jnp.mean(x * x, axis=-1, keepdims=True) + EPS) * g


def _silu(x):
    return x * jax.nn.sigmoid(x)


def _params(*sem):
    return pltpu.CompilerParams(dimension_semantics=sem, vmem_limit_bytes=VMEM_LIMIT)


def _mod_row(mod_ref, row0):
    cond = jnp.maximum(row0 - (T_CTX - LAT_LEN), 0) >> 10
    return mod_ref[0, pl.ds(cond, 1), :]


def _mod_spec(layer):
    return pl.BlockSpec((1, 8, 6 * D), lambda *_: (layer, 0, 0))


def _adaln_kernel(cc_ref, c_ref, w_ref, b_ref, o_ref):
    l = pl.program_id(0)
    row = lax.broadcasted_iota(jnp.int32, (8, 1), 0)
    cond = jnp.where(row == 0, cc_ref[...], 0.0)
    for b in range(N_LAT_SEQ):
        cond = jnp.where(row == 1 + b, c_ref[b:b + 1, :], cond)
    o_ref[0] = _dot(_silu(cond).astype(BF16), w_ref[0].astype(BF16)) + b_ref[pl.ds(l, 1), :]


def _adaln(c_ctx, c, ada_w, ada_b):
    depth = ada_w.shape[0]
    tn = 2048
    return pl.pallas_call(
        _adaln_kernel,
        grid=(depth, 6 * D // tn),
        in_specs=[
            pl.BlockSpec((1, D), lambda l, j: (0, 0)),
            pl.BlockSpec((N_LAT_SEQ, D), lambda l, j: (0, 0)),
            pl.BlockSpec((1, D, tn), lambda l, j: (l, 0, j)),
            pl.BlockSpec((depth, tn), lambda l, j: (0, j)),
        ],
        out_specs=pl.BlockSpec((1, 8, tn), lambda l, j: (l, 0, j)),
        out_shape=jax.ShapeDtypeStruct((depth, 8, 6 * D), F32),
        compiler_params=_params("arbitrary", "arbitrary"),
        name="adaln",
    )(c_ctx.reshape(1, D), c, ada_w, ada_b)


def _rope_tables():
    half = QK_ROPE // 2
    nf = half // 2
    pos = np.arange(LAT_LEN)
    inv = ROPE_THETA ** (-np.arange(nf, dtype=np.float64) / nf)
    k = np.arange(QK_ROPE)
    part, idx = k // half, k % half
    p = np.where(part[None, :] == 0, (pos // GRID_W)[:, None], (pos % GRID_W)[:, None])
    ang = p * inv[idx % nf][None, :]
    cos, sin = np.cos(ang), np.sin(ang)
    first = (idx < nf)[None, :]
    s1 = np.where(first, -sin, 0.0)
    s2 = np.where(first, 0.0, sin)

    def place(t, fill):
        tab = np.full((2 * LAT_LEN, HEAD_PAD), fill, np.float32)
        tab[:LAT_LEN, QK_NOPE:QK_HEAD] = t
        return jnp.asarray(tab)

    return place(cos, 1.0), place(s1, 0.0), place(s2, 0.0)


def _rope(blk, cos, s1, s2):
    return blk * cos + pltpu.roll(blk, 8, 1) * s2 + pltpu.roll(blk, HEAD_PAD - 8, 1) * s1


def _head_norm(blk, g):
    ms = jnp.sum(blk * blk, axis=-1, keepdims=True) * (1.0 / QK_HEAD)
    return blk * lax.rsqrt(ms + EPS) * g


def _even_proj_kernel(xp_ref, xs_ref, mod_ref, nm_ref, win_ref, qan_ref, wqb_ref, kvan_ref,
                      qn_ref, wsc_ref, cos_ref, s1_ref, s2_ref,
                      q_ref, ckv_ref, kpe_ref, sc_ref, sckv_ref, skpe_ref, wt_ref):
    i = pl.program_id(0)
    n_a = Q_LORA + KV_LORA + QK_ROPE

    @pl.when(i == 0)
    def _():
        wt_ref[...] = win_ref[...].astype(BF16)

    x = jnp.where(i < CTX_TILES, xp_ref[...], xs_ref[...])
    m = _mod_row(mod_ref, i * TM)
    h = _rms(x, nm_ref[0:1, :]) * (1.0 + m[:, D:2 * D]) + m[:, 0:D]
    hb = h.astype(BF16)

    za = _dot_nt(hb, wt_ref[0:512, :])
    ckv = _rms(za[:, Q_LORA:Q_LORA + KV_LORA], kvan_ref[...])
    lane = lax.broadcasted_iota(jnp.int32, (1, HEAD_PAD), 1)
    kpe = jnp.where(lane < QK_ROPE, za[:, Q_LORA + KV_LORA:], 0.0)
    ckv_ref[...] = ckv
    kpe_ref[...] = kpe

    @pl.when(i < CTX_TILES)
    def _():
        sckv_ref[...] = ckv
        skpe_ref[...] = kpe[:, :QK_ROPE]

    qa = _rms(za[:, :Q_LORA], qan_ref[...]).astype(BF16)
    cos, s1, s2 = cos_ref[...], s1_ref[...], s2_ref[...]
    qn = qn_ref[...]
    scale = QK_HEAD ** -0.5
    for hp in range(HEADS // 2):
        qq = _dot(qa, wqb_ref[:, hp * 256:(hp + 1) * 256])
        for j in range(2):
            blk = _head_norm(qq[:, j * HEAD_PAD:(j + 1) * HEAD_PAD], qn)
            blk = _rope(blk, cos, s1, s2) * scale
            h0 = (2 * hp + j) * HEAD_PAD
            q_ref[:, h0:h0 + HEAD_PAD] = blk.astype(BF16)

    gb = _dot_nt(hb, wt_ref[n_a:n_a + SC_W, :])
    v = _dot_nt(hb, wt_ref[n_a + SC_W:n_a + 2 * SC_W, :]) * _dot_nt(hb, wt_ref[n_a + 2 * SC_W:n_a + 3 * SC_W, :])
    seq = jnp.where(i < CTX_TILES, CTX_LEN, LAT_LEN)
    r = lax.broadcasted_iota(jnp.int32, (TM, 1), 0) & (seq - 1)
    vp = jnp.where(r == 0, 0.0, pltpu.roll(v, 1, 0))
    vn = jnp.where(r == seq - 1, 0.0, pltpu.roll(v, TM - 1, 0))
    w = wsc_ref[0]
    y = w[0:1] * vp + w[1:2] * v + w[2:3] * vn
    sc_ref[...] = (gb * y).astype(BF16)


def _even_proj(xp, xs, mods, norm_mix, w_in, q_a_norm, wqb, kv_a_norm, qn, w_sc, tabs):
    full = lambda shape: pl.BlockSpec(shape, lambda i: (0,) * len(shape))
    tab = pl.BlockSpec((TM, HEAD_PAD), lambda i: (jnp.where(i < CTX_TILES, 1, 0), 0))
    row = lambda n: pl.BlockSpec((TM, n), lambda i: (i, 0))
    ctx_row = lambda n: pl.BlockSpec((TM, n), lambda i: (jnp.minimum(i, CTX_TILES - 1), 0))
    return pl.pallas_call(
        _even_proj_kernel,
        grid=(N_TILES,),
        in_specs=[
            ctx_row(D),
            pl.BlockSpec((TM, D), lambda i: (jnp.maximum(i - CTX_TILES, 0), 0)),
            _mod_spec(0),
            full((2, D)),
            pl.BlockSpec((IN0_W, D), lambda i: (0, 0), pipeline_mode=pl.Buffered(1)),
            full((1, Q_LORA)),
            full((Q_LORA, HEADS * HEAD_PAD)), full((1, KV_LORA)), full((1, HEAD_PAD)),
            full((1, 3, SC_W)), tab, tab, tab,
        ],
        out_specs=[row(HEADS * HEAD_PAD), row(KV_LORA), row(HEAD_PAD), row(SC_W),
                   ctx_row(KV_LORA), ctx_row(QK_ROPE)],
        out_shape=[
            jax.ShapeDtypeStruct((T, HEADS * HEAD_PAD), BF16),
            jax.ShapeDtypeStruct((T, KV_LORA), F32),
            jax.ShapeDtypeStruct((T, HEAD_PAD), F32),
            jax.ShapeDtypeStruct((T, SC_W), BF16),
            jax.ShapeDtypeStruct((T_CTX, KV_LORA), F32),
            jax.ShapeDtypeStruct((T_CTX, QK_ROPE), F32),
        ],
        scratch_shapes=[pltpu.VMEM((IN0_W, D), BF16)],
        compiler_params=_params("arbitrary"),
        name="even_proj",
    )(xp, xs, mods, norm_mix, w_in, q_a_norm, wqb, kv_a_norm, qn, w_sc, *tabs)


def _kv_proj_kernel(ckv_ref, kpe_ref, wkvb_ref, kn_ref, cos_ref, s1_ref, s2_ref, k_ref, kv_ref):
    kv = _dot(ckv_ref[...].astype(BF16), wkvb_ref[...])
    kv_ref[...] = kv.astype(BF16)
    kpe = pltpu.roll(kpe_ref[...], QK_NOPE, 1)
    lane = lax.broadcasted_iota(jnp.int32, (1, HEAD_PAD), 1)
    kn = kn_ref[...]
    pe_sq = jnp.sum(kpe * kpe, axis=-1, keepdims=True)
    pe = _rope(kpe * kn, cos_ref[...], s1_ref[...], s2_ref[...])
    for h in range(HEADS):
        blk = kv[:, h * HEAD_PAD:(h + 1) * HEAD_PAD]
        nope = jnp.where(lane < QK_NOPE, blk, 0.0)
        ms = (jnp.sum(nope * nope, axis=-1, keepdims=True) + pe_sq) * (1.0 / QK_HEAD)
        k = jnp.where(lane < QK_NOPE, blk * kn, pe) * lax.rsqrt(ms + EPS)
        k_ref[:, h * HEAD_PAD:(h + 1) * HEAD_PAD] = k.astype(BF16)


def _kv_proj(ckv, kpe, wkvb, kn, tabs, tab_index, name, TKV):
    n = ckv.shape[0]
    full = lambda shape: pl.BlockSpec(shape, lambda i: (0,) * len(shape))
    tab = pl.BlockSpec((TKV, HEAD_PAD), lambda i: (tab_index(i), 0))
    row = lambda w: pl.BlockSpec((TKV, w), lambda i: (i, 0))
    return pl.pallas_call(
        _kv_proj_kernel,
        grid=(n // TKV,),
        in_specs=[row(KV_LORA), row(HEAD_PAD), full((KV_LORA, HEADS * HEAD_PAD)), full((1, HEAD_PAD)),
                  tab, tab, tab],
        out_specs=[row(HEADS * HEAD_PAD), row(HEADS * HEAD_PAD)],
        out_shape=[jax.ShapeDtypeStruct((n, HEADS * HEAD_PAD), BF16)] * 2,
        compiler_params=_params("arbitrary"),
        name=name,
    )(ckv, kpe, wkvb, kn, *tabs)


def _pair_out(o0, o1):
    lane = lax.broadcasted_iota(jnp.int32, (1, HEAD_PAD), 1)
    return jnp.where(lane < V_HEAD, pltpu.roll(o0, V_HEAD, 1), o1).astype(BF16)


CTX_SEQS = 4


def _attn_ctx_kernel(q_ref, k_ref, kv_ref, o_ref):
    for b in range(CTX_SEQS):
        rows = slice(b * CTX_LEN, (b + 1) * CTX_LEN)
        for hp in range(HEADS // 2):
            outs = []
            for j in range(2):
                lanes = slice((2 * hp + j) * HEAD_PAD, (2 * hp + j + 1) * HEAD_PAD)
                s = _dot_nt(q_ref[rows, lanes], k_ref[rows, lanes])
                p = jnp.exp(s - jnp.max(s, axis=-1, keepdims=True))
                l = jnp.sum(p, axis=-1, keepdims=True)
                outs.append(_dot(p.astype(BF16), kv_ref[rows, lanes]) / l)
            o_ref[rows, hp * HEAD_PAD:(hp + 1) * HEAD_PAD] = _pair_out(*outs)


def _attn_ctx(q, k, kv):
    blk = pl.BlockSpec((CTX_SEQS * CTX_LEN, HEADS * HEAD_PAD), lambda b: (b, 0))
    return pl.pallas_call(
        _attn_ctx_kernel,
        grid=(N_CTX_SEQ // CTX_SEQS,),
        in_specs=[blk, blk, blk],
        out_specs=pl.BlockSpec((CTX_SEQS * CTX_LEN, HEADS * V_HEAD), lambda b: (b, 0)),
        out_shape=jax.ShapeDtypeStruct((T_CTX, HEADS * V_HEAD), BF16),
        compiler_params=_params("arbitrary"),
        name="attn_ctx",
    )(q, k, kv)


LAT_HEADS = 8


def _attn_lat_kernel(q_ref, kc_ref, kvc_ref, kl_ref, kvl_ref, o_ref):
    for hp in range(LAT_HEADS // 2):
        outs = []
        for j in range(2):
            h0 = (2 * hp + j) * HEAD_PAD
            lanes = slice(h0, h0 + HEAD_PAD)
            q = q_ref[:, lanes]
            sc = _dot_nt(q, kc_ref[:, lanes])
            sl = _dot_nt(q, kl_ref[:, lanes])
            m = jnp.maximum(jnp.max(sc, axis=-1, keepdims=True), jnp.max(sl, axis=-1, keepdims=True))
            pc, pl_ = jnp.exp(sc - m), jnp.exp(sl - m)
            l = jnp.sum(pc, axis=-1, keepdims=True) + jnp.sum(pl_, axis=-1, keepdims=True)
            o = _dot(pc.astype(BF16), kvc_ref[:, lanes]) + _dot(pl_.astype(BF16), kvl_ref[:, lanes])
            outs.append(o / l)
        o_ref[:, hp * HEAD_PAD:(hp + 1) * HEAD_PAD] = _pair_out(*outs)


def _attn_lat(q, kc, kvc, k, kv):
    nq = LAT_LEN // TQ
    q0 = T_CTX // TQ
    kl0 = T_CTX // LAT_LEN
    width = LAT_HEADS * HEAD_PAD
    lat = pl.BlockSpec((LAT_LEN, width), lambda b, hg, t: (kl0 + b, hg))
    ctx = pl.BlockSpec((PAST, width), lambda b, hg, t: (b, hg))
    return pl.pallas_call(
        _attn_lat_kernel,
        grid=(N_LAT_SEQ, HEADS // LAT_HEADS, nq),
        in_specs=[pl.BlockSpec((TQ, width), lambda b, hg, t: (q0 + b * nq + t, hg)), ctx, ctx, lat, lat],
        out_specs=pl.BlockSpec((TQ, LAT_HEADS * V_HEAD), lambda b, hg, t: (b * nq + t, hg)),
        out_shape=jax.ShapeDtypeStruct((T_LAT, HEADS * V_HEAD), BF16),
        compiler_params=_params("arbitrary", "arbitrary", "arbitrary"),
        name="attn_lat",
    )(q, kc, kvc, k, kv)


FFN_NC = D_FF // FFN_FC
FFN_TM = 512
FFN_CTX_TILES = T_CTX // FFN_TM


def _ffn_kernel(oc_ref, ol_ref, sc_ref, xp_ref, xs_ref, wo_ref, nf_ref, wg_ref, wu_ref, wd_ref,
                mod0_ref, mod1_ref, nm_ref, x2_ref, h3_ref, wg_all, wu_all, wd_all, x1_ref, hs_ref, act_ref):
    t = pl.program_id(0)

    i = jnp.maximum(t - (FFN_NC - 1), 0)
    m0, m1 = _mod_row(mod0_ref, i * FFN_TM), _mod_row(mod1_ref, i * FFN_TM)

    def mixer():
        ctx = i < FFN_CTX_TILES
        attn = jnp.where(ctx, oc_ref[...], ol_ref[...])
        x = jnp.where(ctx, xp_ref[...], xs_ref[...])
        x1 = x + m0[:, 2 * D:3 * D] * _dot(jnp.concatenate([attn, sc_ref[...]], axis=1), wo_ref[...])
        h = (_rms(x1, nf_ref[0:1, :]) * (1.0 + m0[:, 4 * D:5 * D]) + m0[:, 3 * D:4 * D]).astype(BF16)
        return x1, h

    def up(h, c):
        return (_silu(_dot(h, wg_all[c])) * _dot(h, wu_all[c])).astype(BF16)

    def down(x1, act):
        x2 = x1 + m0[:, 5 * D:6 * D] * _dot(act, wd_all[...])
        x2_ref[...] = x2
        h3_ref[...] = (_rms(x2, nm_ref[1:2, :]) * (1.0 + m1[:, D:2 * D]) + m1[:, 0:D]).astype(BF16)

    @pl.when(t == 0)
    def _():
        x1, h = mixer()
        x1_ref[...] = x1
        hs_ref[...] = h

    @pl.when(t < FFN_NC)
    def _stage():
        wg_all[t] = wg_ref[...].astype(BF16)
        wu_all[t] = wu_ref[...].astype(BF16)
        wd_all[pl.ds(pl.multiple_of(t * FFN_FC, FFN_FC), FFN_FC), :] = wd_ref[...].astype(BF16)
        act_ref[t] = up(hs_ref[...], t)

    @pl.when(t == FFN_NC - 1)
    def _():
        down(x1_ref[...], jnp.concatenate([act_ref[c] for c in range(FFN_NC)], axis=1))

    @pl.when(t > FFN_NC - 1)
    def _tile():
        x1, h = mixer()
        down(x1, jnp.concatenate([up(h, c) for c in range(FFN_NC)], axis=1))


def _ffn(oc, ol, sc, xp, xs, wo, norm_ffn, w_gu, w_down, mods, norm_mix):
    chunk = lambda t: jnp.minimum(t, FFN_NC - 1)
    tile = lambda t: jnp.maximum(t - (FFN_NC - 1), 0)
    full = lambda shape: pl.BlockSpec(shape, lambda t: (0,) * len(shape))
    row = lambda n: pl.BlockSpec((FFN_TM, n), lambda t: (tile(t), 0))
    first = lambda n: pl.BlockSpec((FFN_TM, n), lambda t: (jnp.minimum(tile(t), FFN_CTX_TILES - 1), 0))
    second = lambda n: pl.BlockSpec((FFN_TM, n), lambda t: (jnp.maximum(tile(t) - FFN_CTX_TILES, 0), 0))
    return pl.pallas_call(
        _ffn_kernel,
        grid=(FFN_NC - 1 + T // FFN_TM,),
        in_specs=[first(HEADS * V_HEAD), second(HEADS * V_HEAD), row(SC_W), first(D), second(D),
                  full((HEADS * V_HEAD + SC_W, D)), full((2, D)),
                  pl.BlockSpec((D, FFN_FC), lambda t: (0, chunk(t))),
                  pl.BlockSpec((D, FFN_FC), lambda t: (0, FFN_NC + chunk(t))),
                  pl.BlockSpec((FFN_FC, D), lambda t: (chunk(t), 0)),
                  _mod_spec(0), _mod_spec(1), full((2, D))],
        out_specs=[row(D), row(D)],
        out_shape=[jax.ShapeDtypeStruct((T, D), F32), jax.ShapeDtypeStruct((T, D), BF16)],
        scratch_shapes=[pltpu.VMEM((FFN_NC, D, FFN_FC), BF16), pltpu.VMEM((FFN_NC, D, FFN_FC), BF16),
                        pltpu.VMEM((D_FF, D), BF16), pltpu.VMEM((FFN_TM, D), F32), pltpu.VMEM((FFN_TM, D), BF16),
                        pltpu.VMEM((FFN_NC, FFN_TM, FFN_FC), BF16)],
        compiler_params=_params("arbitrary"),
        name="ffn_dense",
    )(oc, ol, sc, xp, xs, wo, norm_ffn, w_gu, w_gu, w_down, mods, mods, norm_mix)


CONF_CB = 256
CONF_SEG = 256
CONF_HALO = 16
CONF_SEGP = CONF_SEG + 2 * CONF_HALO
CONF_PIECE = 256


def _conf_kernel(h_ref, x2_ref, w1_ref, b1_ref, wdw_ref, bdw_ref, lng_ref, lnb_ref, w2_ref, b2_ref,
                 mod_ref, nf_ref, rt_ref, x3_ref, h4_ref, lg_ref, pad_ref, conv_ref):
    i = pl.program_id(0)
    nseg = TM // CONF_SEG
    h = h_ref[...]
    joined = jnp.where(i < CTX_TILES, 0.0, 1.0)
    zeros_halo = jnp.zeros((CONF_HALO, CONF_CB), F32)
    for cb in range(D // CONF_CB):
        c0 = cb * CONF_CB
        a = _dot(h, w1_ref[:, c0:c0 + CONF_CB]) + b1_ref[:, c0:c0 + CONF_CB]
        g = _dot(h, w1_ref[:, D + c0:D + c0 + CONF_CB]) + b1_ref[:, D + c0:D + c0 + CONF_CB]
        u = a * jax.nn.sigmoid(g)
        for s in range(nseg):
            base = s * CONF_SEGP
            top = u[s * CONF_SEG - CONF_HALO:s * CONF_SEG] * joined if s > 0 else zeros_halo
            bot = (u[(s + 1) * CONF_SEG:(s + 1) * CONF_SEG + CONF_HALO] * joined
                   if s < nseg - 1 else zeros_halo)
            pad_ref[0, base:base + CONF_HALO, :] = top
            pad_ref[0, base + CONF_HALO:base + CONF_HALO + CONF_SEG, :] = u[s * CONF_SEG:(s + 1) * CONF_SEG]
            pad_ref[0, base + CONF_HALO + CONF_SEG:base + CONF_SEGP, :] = bot

        p0 = pad_ref[0]
        rows = nseg * CONF_SEGP
        for b in range(1, 8):
            pad_ref[b] = pltpu.roll(p0, rows - b, 0)

        def piece(t, carry):
            s = t // (CONF_SEG // CONF_PIECE)
            q0 = (t % (CONF_SEG // CONF_PIECE)) * CONF_PIECE
            src = pl.multiple_of(s * CONF_SEGP + q0, 8)
            acc = jnp.zeros((CONF_PIECE, CONF_CB), F32)
            for j in range(CONF_K):
                hi, lo = (j + 1) // 8, (j + 1) % 8
                acc = acc + wdw_ref[0, j:j + 1, c0:c0 + CONF_CB] * pad_ref[lo, pl.ds(src + 8 * hi, CONF_PIECE), :]
            dst = pl.multiple_of(s * CONF_SEG + q0, 8)
            conv_ref[pl.ds(dst, CONF_PIECE), c0:c0 + CONF_CB] = acc + bdw_ref[:, c0:c0 + CONF_CB]
            return carry

        lax.fori_loop(0, TM // CONF_PIECE, piece, 0)

    m = _mod_row(mod_ref, i * TM)
    half = TM // 2
    for r0 in (0, half):
        rows = slice(r0, r0 + half)
        y = conv_ref[rows, :]
        mu = jnp.mean(y, axis=-1, keepdims=True)
        yc = y - mu
        var = jnp.mean(yc * yc, axis=-1, keepdims=True)
        y = _silu(yc * lax.rsqrt(var + EPS) * lng_ref[...] + lnb_ref[...])
        out = _dot(y.astype(BF16), w2_ref[...]) + b2_ref[...]
        x3 = x2_ref[rows, :] + m[:, 2 * D:3 * D] * out
        x3_ref[rows, :] = x3
        h4 = _rms(x3, nf_ref[1:2, :]) * (1.0 + m[:, 4 * D:5 * D]) + m[:, 3 * D:4 * D]
        h4_ref[rows, :] = h4.astype(BF16)
        lg_ref[:, rows] = lax.dot_general(rt_ref[...], h4, (((1,), (1,)), ((), ())),
                                          precision=lax.Precision.HIGHEST, preferred_element_type=F32)


def _conf(h3, x2, w1, b1, wdw, bdw, lng, lnb, w2, b2, mods, norm_ffn1, router_t):
    full = lambda shape: pl.BlockSpec(shape, lambda i: (0,) * len(shape))
    row = lambda n: pl.BlockSpec((TM, n), lambda i: (i, 0))
    return pl.pallas_call(
        _conf_kernel,
        grid=(N_TILES,),
        in_specs=[row(D), row(D), full((D, 2 * D)), full((1, 2 * D)), full((1, CONF_K, D)), full((1, D)),
                  full((1, D)), full((1, D)), full((D, D)), full((1, D)),
                  _mod_spec(1), full((2, D)), full((N_EXP, D))],
        out_specs=[row(D), row(D), pl.BlockSpec((N_EXP, TM), lambda i: (0, i))],
        out_shape=[jax.ShapeDtypeStruct((T, D), F32), jax.ShapeDtypeStruct((T, D), BF16),
                   jax.ShapeDtypeStruct((N_EXP, T), F32)],
        scratch_shapes=[pltpu.VMEM((8, (TM // CONF_SEG) * CONF_SEGP, CONF_CB), F32),
                        pltpu.VMEM((TM, D), F32)],
        compiler_params=_params("arbitrary"),
        name="conformer_conv",
    )(h3, x2, w1, b1, wdw, bdw, lng, lnb, w2, b2, mods, norm_ffn1, router_t)


TB = 256
N_TB = T // TB
SUB = 128
SUBS = 24
SM = SUBS * SUB
N_SUB_MAX = 2 * T // SUB + N_EXP
N_SUP_MAX = (N_SUB_MAX + N_EXP * (SUBS - 1)) // SUBS
YS_ROWS = (N_SUB_MAX + 4) * SUB
WIN_ALIGN = 16
WIN_HALF = TB // 2 + WIN_ALIGN
FIRST_STRIDE = 32
UNIT_STRIDE = 64
GATHER_BLOCKS = 4


def _route_kernel(lg_ref, g_ref, rank_ref, first_ref):
    lg = lg_ref[...]
    idx = lax.broadcasted_iota(jnp.int32, lg.shape, 0).astype(F32)
    none = float(N_EXP)
    m1 = jnp.max(lg, axis=0, keepdims=True)
    i1 = jnp.min(jnp.where(lg == m1, idx, none), axis=0, keepdims=True)
    rest = jnp.where(idx == i1, -jnp.inf, lg)
    m2 = jnp.max(rest, axis=0, keepdims=True)
    i2 = jnp.min(jnp.where(rest == m2, idx, none), axis=0, keepdims=True)
    e = jnp.exp(m2 - m1)
    w1 = 1.0 / (1.0 + e)
    w2 = e / (1.0 + e)
    g_ref[...] = jnp.where(idx == i1, w1, 0.0) + jnp.where(idx == i2, w2, 0.0)

    mask = jnp.where(idx == i1, 1.0, 0.0) + jnp.where(idx == i2, 1.0, 0.0)
    before = (lax.broadcasted_iota(jnp.int32, (TB, TB), 0) < lax.broadcasted_iota(jnp.int32, (TB, TB), 1))
    before = jnp.where(before, 1.0, 0.0).astype(BF16)
    lane = lax.broadcasted_iota(jnp.int32, (N_EXP, 128), 1)
    carry = jnp.zeros((N_EXP, 1), F32)
    first = jnp.zeros((N_EXP, 128), F32)
    for b in range(N_TB):
        mb = mask[:, b * TB:(b + 1) * TB]
        local = _dot(mb.astype(BF16), before)
        rank_ref[:, b * TB:(b + 1) * TB] = jnp.where(mb > 0.0, local + carry, -1.0)
        first = jnp.where(lane == b, carry, first)
        carry = carry + jnp.sum(mb, axis=1, keepdims=True)
    first_ref[...] = jnp.where(lane == N_TB, carry, first)


def _route(logits_t):
    return pl.pallas_call(
        _route_kernel,
        out_shape=[jax.ShapeDtypeStruct((N_EXP, T), F32), jax.ShapeDtypeStruct((N_EXP, T), F32),
                   jax.ShapeDtypeStruct((N_EXP, 128), F32)],
        compiler_params=pltpu.CompilerParams(vmem_limit_bytes=VMEM_LIMIT),
        name="route",
    )(logits_t)


def _moe_plan(first):
    first = first[:, :FIRST_STRIDE].astype(jnp.int32)
    cnt = first[:, N_TB]
    nt = (cnt + (SUB - 1)) // SUB
    off_end = jnp.cumsum(nt)
    off = off_end - nt
    nsub = off_end[-1]
    nsup = (nt + (SUBS - 1)) // SUBS
    sup_end = jnp.cumsum(nsup)
    sup_off = sup_end - nsup
    s = jnp.minimum(jnp.arange(N_SUP_MAX), sup_end[-1] - 1)
    valid = jnp.arange(N_SUP_MAX) < sup_end[-1]
    se = jnp.sum(s[:, None] >= sup_end[None, :], axis=1)
    mine = se[:, None] == jnp.arange(N_EXP)[None, :]
    pick = lambda v: jnp.sum(jnp.where(mine, v[None, :], 0), axis=1)
    sk0 = (s - pick(sup_off)) * SUBS
    sns = jnp.where(valid, jnp.clip(pick(nt) - sk0, 0, SUBS), 0)
    sj0 = pick(off) + sk0
    base = (jnp.arange(UNIT_STRIDE) * SUB)[None, :, None]
    blo = jnp.minimum(jnp.sum(first[:, None, 1:N_TB + 1] <= base, axis=2), N_TB - 1)
    end = jnp.minimum(base + SUB, cnt[:, None, None])
    bhi = jnp.maximum(jnp.sum(first[:, None, :N_TB] < end, axis=2) - 1, blo)
    ng = (bhi - blo) // GATHER_BLOCKS + 1
    start = SUB * off[:, None] + first[:, :N_TB]
    lead = start & (WIN_ALIGN - 1)
    wina = start - lead
    rel = first[:, :N_TB] - lead
    need = lead + (first[:, 1:N_TB + 1] - first[:, :N_TB]) > WIN_HALF
    winb = lax.cummax(jnp.where(need, wina + WIN_HALF, 0), axis=1)
    wide = jnp.any(need, axis=0)
    flat = lambda parts: jnp.concatenate([p.astype(jnp.int32).reshape(-1) for p in parts])
    gmm = dict(se=se, sk0=sk0, sns=sns, sj0=sj0, nsub=nsub, first=first, blo=blo, ng=ng)
    comb = dict(wina=wina.T, winb=winb.T, rel=rel.T, wide=wide)
    return flat([gmm[k] for k in _GMM_TAB]), flat([comb[k] for k in _COMB_TAB])


def _offsets(sizes):
    out, pos = {}, 0
    for name, n in sizes.items():
        out[name], pos = pos, pos + n
    return out


_GMM_TAB = _offsets(dict(se=N_SUP_MAX, sk0=N_SUP_MAX, sns=N_SUP_MAX, sj0=N_SUP_MAX, nsub=1,
                         first=N_EXP * FIRST_STRIDE, blo=N_EXP * UNIT_STRIDE, ng=N_EXP * UNIT_STRIDE))
_COMB_TAB = _offsets(dict(wina=N_TB * N_EXP, winb=N_TB * N_EXP, rel=N_TB * N_EXP, wide=N_TB))


class _Section:
    def __init__(self, ref, offset):
        self.ref, self.offset = ref, offset

    def __getitem__(self, i):
        return self.ref[self.offset + i]


def _moe_gmm_kernel(tab_ref, x_ref, rank_ref, gate_ref, wg_ref, wu_ref, wd_ref, ys_ref,
                    xs_ref, gs_ref, yacc_ref, wgb_ref, wub_ref, wdb_ref, sem):
    se_ref, sk0_ref, sns_ref, sj0_ref, nsub_ref, first_ref, blo_ref, ng_ref = (
        _Section(tab_ref, _GMM_TAB[k]) for k in ("se", "sk0", "sns", "sj0", "nsub", "first", "blo", "ng"))
    s, c = pl.program_id(0), pl.program_id(1)
    nc = pl.num_programs(1)
    e, k0, ns = se_ref[s], sk0_ref[s], sns_ref[s]

    def sub_rows(k):
        return pl.ds(pl.multiple_of(k * SUB, SUB), SUB)

    def out_copy(k, row0):
        dst = ys_ref.at[pl.ds(pl.multiple_of(row0 + k * SUB, SUB), SUB)]
        return pltpu.make_async_copy(xs_ref.at[sub_rows(k)], dst, sem.at[k])

    @pl.when((ns > 0) & (c == 0))
    def _gather():
        def group(k, g):
            slot = (lax.broadcasted_iota(jnp.int32, (SUB, 1), 0) + (k0 + k) * SUB).astype(F32)
            b0 = blo_ref[e * UNIT_STRIDE + k0 + k] + g * GATHER_BLOCKS
            t0 = pl.multiple_of(jnp.minimum(b0, N_TB - GATHER_BLOCKS) * TB, TB)
            lo = first_ref[e * FIRST_STRIDE + b0].astype(F32)
            cols = pl.ds(t0, GATHER_BLOCKS * TB)
            hit = rank_ref[pl.ds(e, 1), cols] == jnp.where(slot >= lo, slot, -2.0)
            rows = _dot(jnp.where(hit, 1.0, 0.0).astype(BF16), x_ref[cols, :])
            gate = jnp.sum(jnp.where(hit, gate_ref[pl.ds(e, 1), cols], 0.0), axis=-1, keepdims=True)
            return rows, gate

        def first(k):
            rows, gate = group(k, 0)
            xs_ref[sub_rows(k), :] = rows.astype(BF16)
            gs_ref[sub_rows(k), :] = gate
            yacc_ref[sub_rows(k), :] = jnp.zeros((SUB, D), F32)

        def more(k):
            def body(g, carry):
                rows, gate = group(k, g)
                xs_ref[sub_rows(k), :] = (xs_ref[sub_rows(k), :].astype(F32) + rows).astype(BF16)
                gs_ref[sub_rows(k), :] += gate
                return carry

            lax.fori_loop(1, ng_ref[e * UNIT_STRIDE + k0 + k], body, 0)

        def quad(p, carry):
            for j in range(4):
                first(4 * p + j)
            for j in range(4):
                more(4 * p + j)
            return carry

        def single(k, carry):
            first(k)
            more(k)
            return carry

        lax.fori_loop(0, ns >> 2, quad, 0)
        lax.fori_loop(ns & ~3, ns, single, 0)

    @pl.when(ns > 0)
    def _compute():
        row0 = sj0_ref[s] * SUB

        def swiglu(rows, wg, wu, wd):
            x = xs_ref[rows, :]
            g = _dot(x, wg)
            u = _dot(x, wu)
            yacc_ref[rows, :] += _dot((_silu(g) * u).astype(BF16), wd)

        def finish(first_sub, n):
            @pl.when(c == nc - 1)
            def _():
                for k in range(n):
                    rows = sub_rows(first_sub + k)
                    xs_ref[rows, :] = (yacc_ref[rows, :] * gs_ref[rows, :]).astype(BF16)
                    out_copy(first_sub + k, row0).start()

        def first_chain(rows):
            wg, wu, wd = wg_ref[0].astype(BF16), wu_ref[0].astype(BF16), wd_ref[0].astype(BF16)
            wgb_ref[...] = wg
            wub_ref[...] = wu
            wdb_ref[...] = wd
            swiglu(rows, wg, wu, wd)

        @pl.when(ns >= 4)
        def _():
            first_chain(pl.ds(0, 4 * SUB))
            finish(0, 4)

        @pl.when(ns < 4)
        def _():
            first_chain(pl.ds(0, SUB))
            finish(0, 1)

        done = jnp.where(ns >= 4, 4, 1)
        rest = ns - done

        def chain(first_sub, n):
            rows = pl.ds(pl.multiple_of(first_sub * SUB, SUB), n * SUB)
            swiglu(rows, wgb_ref[...], wub_ref[...], wdb_ref[...])
            finish(first_sub, n)

        def eight(k, carry):
            chain(done + 8 * k, 8)
            return carry

        lax.fori_loop(0, rest >> 3, eight, 0)
        done8 = done + (rest & ~7)
        for n in (4, 2, 1):
            @pl.when((rest & n) != 0)
            def _(n=n):
                chain(done8 + (rest & (7 & ~(2 * n - 1))), n)

    @pl.when((ns > 0) & (c == nc - 1))
    def _store_done():
        row0 = sj0_ref[s] * SUB

        def done(k, carry):
            out_copy(k, row0).wait()
            return carry

        lax.fori_loop(0, ns, done, 0)

    @pl.when((s == pl.num_programs(0) - 1) & (c == nc - 1))
    def _zero_tail():
        xs_ref[0:SUB, :] = jnp.zeros((SUB, D), BF16)
        nsub = nsub_ref[0]

        def fill(k, carry):
            cp = out_copy(0, (nsub + k) * SUB)
            cp.start()
            cp.wait()
            return carry

        lax.fori_loop(0, YS_ROWS // SUB - nsub, fill, 0)


def _moe_gmm(tab, h4, rank, gates, w_gu, w_down):
    nc = D_FFE // MOE_FC

    def expert(s, tab):
        return tab[_GMM_TAB["se"] + s]

    def chunk(s, c, tab):
        return jnp.where(tab[_GMM_TAB["sns"] + s] > 0, c, nc - 1)

    return pl.pallas_call(
        _moe_gmm_kernel,
        grid_spec=pltpu.PrefetchScalarGridSpec(
            num_scalar_prefetch=1,
            grid=(N_SUP_MAX, nc),
            in_specs=[
                pl.BlockSpec((T, D), lambda s, c, tab: (0, 0), pipeline_mode=pl.Buffered(1)),
                pl.BlockSpec((N_EXP, T), lambda s, c, tab: (0, 0)),
                pl.BlockSpec((N_EXP, T), lambda s, c, tab: (0, 0)),
                pl.BlockSpec((1, D, MOE_FC), lambda s, c, tab: (expert(s, tab), 0, chunk(s, c, tab))),
                pl.BlockSpec((1, D, MOE_FC), lambda s, c, tab: (expert(s, tab), 0, nc + chunk(s, c, tab))),
                pl.BlockSpec((1, MOE_FC, D), lambda s, c, tab: (expert(s, tab), chunk(s, c, tab), 0)),
            ],
            out_specs=pl.BlockSpec(memory_space=pl.ANY),
            scratch_shapes=[
                pltpu.VMEM((SM, D), BF16), pltpu.VMEM((SM, 1), F32), pltpu.VMEM((SM, D), F32),
                pltpu.VMEM((D, MOE_FC), BF16), pltpu.VMEM((D, MOE_FC), BF16), pltpu.VMEM((MOE_FC, D), BF16),
                pltpu.SemaphoreType.DMA((SUBS,)),
            ],
        ),
        out_shape=jax.ShapeDtypeStruct((YS_ROWS, D), BF16),
        compiler_params=_params("arbitrary", "arbitrary"),
        name="moe_gmm",
    )(tab, h4, rank, gates, w_gu, w_gu, w_down)


def _moe_combine_kernel(tab_ref, *refs):
    rel_ref, wide_ref = _Section(tab_ref, _COMB_TAB["rel"]), _Section(tab_ref, _COMB_TAB["wide"])
    ya, yb = refs[:N_EXP], refs[N_EXP:2 * N_EXP]
    rank_ref, x3_ref, mod_ref, yp_ref, ys_ref, ycat_ref, acc_ref = refs[2 * N_EXP:]
    b = pl.program_id(0)
    row = lax.broadcasted_iota(jnp.int32, (WIN_HALF, 1), 0)

    def onehot(e, first_row):
        slot = (row + (rel_ref[b * N_EXP + e] + first_row)).astype(F32)
        return jnp.where(rank_ref[e:e + 1, :] == slot, 1.0, 0.0).astype(BF16)

    def gather(y_refs, first_row, base):
        pieces = []
        for e in range(N_EXP):
            ycat_ref[base + e * WIN_HALF:base + (e + 1) * WIN_HALF, :] = y_refs[e][...]
            pieces.append(onehot(e, first_row))
        return pieces

    def combine(pieces, rows):
        return lax.dot_general(jnp.concatenate(pieces, axis=0), ycat_ref[0:rows, :], (((0,), (0,)), ((), ())),
                               preferred_element_type=F32)

    @pl.when(wide_ref[b] == 0)
    def _():
        acc_ref[...] = combine(gather(ya, 0, 0), N_EXP * WIN_HALF)

    @pl.when(wide_ref[b] != 0)
    def _():
        pieces = gather(ya, 0, 0) + gather(yb, WIN_HALF, N_EXP * WIN_HALF)
        acc_ref[...] = combine(pieces, 2 * N_EXP * WIN_HALF)

    out = x3_ref[...] + _mod_row(mod_ref, b * TB)[:, 5 * D:6 * D] * acc_ref[...]

    @pl.when(b < T_CTX // TB)
    def _():
        yp_ref[...] = out

    @pl.when(b >= T_CTX // TB)
    def _():
        ys_ref[...] = out


def _moe_combine(tab, ysorted, rank, x3, mods):
    ctx_blocks = T_CTX // TB

    def window(e, second):
        def index(b, tab):
            start = tab[_COMB_TAB["winb" if second else "wina"] + b * N_EXP + e]
            return pl.multiple_of(start, WIN_ALIGN), 0
        return pl.BlockSpec((pl.Element(WIN_HALF), pl.Element(D)), index)

    return pl.pallas_call(
        _moe_combine_kernel,
        grid_spec=pltpu.PrefetchScalarGridSpec(
            num_scalar_prefetch=1,
            grid=(N_TB,),
            in_specs=[window(e, False) for e in range(N_EXP)] + [window(e, True) for e in range(N_EXP)] + [
                pl.BlockSpec((N_EXP, TB), lambda b, *_: (0, b)),
                pl.BlockSpec((TB, D), lambda b, *_: (b, 0)),
                _mod_spec(1),
            ],
            out_specs=[pl.BlockSpec((TB, D), lambda b, *_: (jnp.minimum(b, ctx_blocks - 1), 0)),
                       pl.BlockSpec((TB, D), lambda b, *_: (jnp.maximum(b - ctx_blocks, 0), 0))],
            scratch_shapes=[pltpu.VMEM((2 * N_EXP * WIN_HALF, D), BF16), pltpu.VMEM((TB, D), F32)],
        ),
        out_shape=[jax.ShapeDtypeStruct((T_CTX, D), F32), jax.ShapeDtypeStruct((T_LAT, D), F32)],
        compiler_params=_params("arbitrary"),
        name="moe_combine",
    )(tab, *([ysorted] * (2 * N_EXP)), rank, x3, mods)


def _pad_heads(w, width):
    lead = w.shape[:-1]
    w = w.reshape(*lead, HEADS, width)
    w = jnp.pad(w, [(0, 0)] * len(lead) + [(0, 0), (0, HEAD_PAD - width)])
    return w.reshape(*lead, HEADS * HEAD_PAD)


def kernel(x_prompt, x_sample, cache_ckv, cache_kpe, c, c_ctx, ada_w, ada_b, norm_mix, norm_ffn, w_in, q_a_norm,
           w_qb, kv_a_norm, w_kvb, q_norm, k_norm, w_sc, w_o, ffn_gu, ffn_down, conv_pw1, conv_pw1_b, conv_dw,
           conv_dw_b, conv_ln_g, conv_ln_b, conv_pw2, conv_pw2_b, router, moe_gu, moe_down):
    xp = x_prompt.reshape(T_CTX, D)
    xs = x_sample.reshape(T_LAT, D)

    mods = _adaln(c_ctx, c, ada_w, ada_b)

    wqb = _pad_heads(w_qb[0], QK_HEAD).astype(BF16)
    wkvb = w_kvb[0].astype(BF16)
    qn = jnp.pad(q_norm[0], (0, HEAD_PAD - QK_HEAD)).reshape(1, HEAD_PAD)
    kn = jnp.pad(k_norm[0], (0, HEAD_PAD - QK_HEAD)).reshape(1, HEAD_PAD)
    tabs = _rope_tables()

    w_in_t = jnp.swapaxes(w_in[0], 0, 1)
    q, ckv, kpe, sc, state_ckv, state_kpe = _even_proj(xp, xs, mods, norm_mix, w_in_t, q_a_norm, wqb, kv_a_norm,
                                                       qn, w_sc, tabs)

    lat_tile0 = T_CTX // TKV
    ident = LAT_LEN // TKV
    k, kv = _kv_proj(ckv, kpe, wkvb, kn, tabs,
                     lambda i: jnp.where(i < lat_tile0, ident, (i - lat_tile0) % ident), "kv_proj", TKV)
    cache_kpe_p = jnp.pad(cache_kpe[:, 0].reshape(N_LAT_SEQ * PAST, QK_ROPE), ((0, 0), (0, HEAD_PAD - QK_ROPE)))
    kc, kvc = _kv_proj(cache_ckv[:, 0].reshape(N_LAT_SEQ * PAST, KV_LORA), cache_kpe_p, wkvb, kn, tabs,
                       lambda i: LAT_LEN // (N_LAT_SEQ * PAST), "kv_proj_cache", N_LAT_SEQ * PAST)

    oc = _attn_ctx(q, k, kv)
    ol = _attn_lat(q, kc, kvc, k, kv)
    x2, h3 = _ffn(oc, ol, sc, xp, xs, w_o[0].astype(BF16), norm_ffn, ffn_gu[0], ffn_down[0], mods, norm_mix)

    x3, h4, logits_t = _conf(h3, x2, conv_pw1[0].astype(BF16), conv_pw1_b, conv_dw, conv_dw_b, conv_ln_g,
                             conv_ln_b, conv_pw2[0].astype(BF16), conv_pw2_b, mods, norm_ffn, router[0].T)
    gates, rank, first = _route(logits_t)
    gmm_tab, comb_tab = _moe_plan(first)
    ysorted = _moe_gmm(gmm_tab, h4, rank, gates, moe_gu[0], moe_down[0])
    yp, ys = _moe_combine(comb_tab, ysorted, rank, x3, mods)

    return (yp.reshape(N_CTX_SEQ, CTX_LEN, D), ys.reshape(N_LAT_SEQ, LAT_LEN, D),
            state_ckv.reshape(N_CTX_SEQ, 1, CTX_LEN, KV_LORA), state_kpe.reshape(N_CTX_SEQ, 1, CTX_LEN, QK_ROPE))
```

```python
import jax
import jax.numpy as jnp
import numpy as np
from jax import lax
from jax.experimental import pallas as pl
from jax.experimental.pallas import tpu as pltpu

F32 = jnp.float32
BF16 = jnp.bfloat16

D = 1024
N_CTX_SEQ, CTX_LEN = 16, 256
N_LAT_SEQ, LAT_LEN = 2, 1024
T_CTX = N_CTX_SEQ * CTX_LEN
T_LAT = N_LAT_SEQ * LAT_LEN
T = T_CTX + T_LAT
PAST = 256
GRID_W = 64
HEADS = 8
QK_NOPE, QK_ROPE, V_HEAD = 64, 32, 64
QK_HEAD = QK_NOPE + QK_ROPE
HEAD_PAD = 128
Q_LORA, KV_LORA = 256, 128
SC_W = 512
IN0_W = Q_LORA + KV_LORA + QK_ROPE + 3 * SC_W
CONF_K = 31
D_FF = 2816
N_EXP = 8
D_FFE = 3584
EPS = 1e-6
ROPE_THETA = 10000.0

TM = 1024
N_TILES = T // TM
CTX_TILES = T_CTX // TM
TKV = 1024
TQ = 512
FFN_FC = 256
MOE_FC = 512
VMEM_LIMIT = 56 * 1024 * 1024


def _dot(a, b):
    return jnp.dot(a, b, preferred_element_type=F32)


def _dot_nt(a, b):
    return lax.dot_general(a, b, (((1,), (1,)), ((), ())), preferred_element_type=F32)


def _rms(x, g):
    return x * lax.rsqrt(jnp.mean(x * x, axis=-1, keepdims=True) + EPS) * g


def _silu(x):
    return x * jax.nn.sigmoid(x)


def _params(*sem):
    return pltpu.CompilerParams(dimension_semantics=sem, vmem_limit_bytes=VMEM_LIMIT)


def _mod_row(mod_ref, row0):
    cond = jnp.maximum(row0 - (T_CTX - LAT_LEN), 0) >> 10
    return mod_ref[0, pl.ds(cond, 1), :]


def _mod_spec(layer):
    return pl.BlockSpec((1, 8, 6 * D), lambda *_: (layer, 0, 0))


def _adaln_kernel(cc_ref, c_ref, w_ref, b_ref, o_ref):
    l = pl.program_id(0)
    row = lax.broadcasted_iota(jnp.int32, (8, 1), 0)
    cond = jnp.where(row == 0, cc_ref[...], 0.0)
    for b in range(N_LAT_SEQ):
        cond = jnp.where(row == 1 + b, c_ref[b:b + 1, :], cond)
    o_ref[0] = _dot(_silu(cond).astype(BF16), w_ref[0].astype(BF16)) + b_ref[pl.ds(l, 1), :]


def _adaln(c_ctx, c, ada_w, ada_b):
    depth = ada_w.shape[0]
    tn = 2048
    return pl.pallas_call(
        _adaln_kernel,
        grid=(depth, 6 * D // tn),
        in_specs=[
            pl.BlockSpec((1, D), lambda l, j: (0, 0)),
            pl.BlockSpec((N_LAT_SEQ, D), lambda l, j: (0, 0)),
            pl.BlockSpec((1, D, tn), lambda l, j: (l, 0, j)),
            pl.BlockSpec((depth, tn), lambda l, j: (0, j)),
        ],
        out_specs=pl.BlockSpec((1, 8, tn), lambda l, j: (l, 0, j)),
        out_shape=jax.ShapeDtypeStruct((depth, 8, 6 * D), F32),
        compiler_params=_params("arbitrary", "arbitrary"),
        name="adaln",
    )(c_ctx.reshape(1, D), c, ada_w, ada_b)


def _rope_tables():
    half = QK_ROPE // 2
    nf = half // 2
    pos = np.arange(LAT_LEN)
    inv = ROPE_THETA ** (-np.arange(nf, dtype=np.float64) / nf)
    k = np.arange(QK_ROPE)
    part, idx = k // half, k % half
    p = np.where(part[None, :] == 0, (pos // GRID_W)[:, None], (pos % GRID_W)[:, None])
    ang = p * inv[idx % nf][None, :]
    cos, sin = np.cos(ang), np.sin(ang)
    first = (idx < nf)[None, :]
    s1 = np.where(first, -sin, 0.0)
    s2 = np.where(first, 0.0, sin)

    def place(t, fill):
        tab = np.full((2 * LAT_LEN, HEAD_PAD), fill, np.float32)
        tab[:LAT_LEN, QK_NOPE:QK_HEAD] = t
        return jnp.asarray(tab)

    return place(cos, 1.0), place(s1, 0.0), place(s2, 0.0)


def _rope(blk, cos, s1, s2):
    return blk * cos + pltpu.roll(blk, 8, 1) * s2 + pltpu.roll(blk, HEAD_PAD - 8, 1) * s1


def _head_norm(blk, g):
    ms = jnp.sum(blk * blk, axis=-1, keepdims=True) * (1.0 / QK_HEAD)
    return blk * lax.rsqrt(ms + EPS) * g


def _even_proj_kernel(xp_ref, xs_ref, mod_ref, nm_ref, win_ref, qan_ref, wqb_ref, kvan_ref,
                      qn_ref, wsc_ref, cos_ref, s1_ref, s2_ref,
                      q_ref, ckv_ref, kpe_ref, sc_ref, sckv_ref, skpe_ref, wt_ref):
    i = pl.program_id(0)
    n_a = Q_LORA + KV_LORA + QK_ROPE

    @pl.when(i == 0)
    def _():
        wt_ref[...] = win_ref[...].astype(BF16)

    x = jnp.where(i < CTX_TILES, xp_ref[...], xs_ref[...])
    m = _mod_row(mod_ref, i * TM)
    h = _rms(x, nm_ref[0:1, :]) * (1.0 + m[:, D:2 * D]) + m[:, 0:D]
    hb = h.astype(BF16)

    za = _dot_nt(hb, wt_ref[0:512, :])
    ckv = _rms(za[:, Q_LORA:Q_LORA + KV_LORA], kvan_ref[...])
    lane = lax.broadcasted_iota(jnp.int32, (1, HEAD_PAD), 1)
    kpe = jnp.where(lane < QK_ROPE, za[:, Q_LORA + KV_LORA:], 0.0)
    ckv_ref[...] = ckv
    kpe_ref[...] = kpe

    @pl.when(i < CTX_TILES)
    def _():
        sckv_ref[...] = ckv
        skpe_ref[...] = kpe[:, :QK_ROPE]

    qa = _rms(za[:, :Q_LORA], qan_ref[...]).astype(BF16)
    cos, s1, s2 = cos_ref[...], s1_ref[...], s2_ref[...]
    qn = qn_ref[...]
    scale = QK_HEAD ** -0.5
    for hp in range(HEADS // 2):
        qq = _dot(qa, wqb_ref[:, hp * 256:(hp + 1) * 256])
        for j in range(2):
            blk = _head_norm(qq[:, j * HEAD_PAD:(j + 1) * HEAD_PAD], qn)
            blk = _rope(blk, cos, s1, s2) * scale
            h0 = (2 * hp + j) * HEAD_PAD
            q_ref[:, h0:h0 + HEAD_PAD] = blk.astype(BF16)

    gb = _dot_nt(hb, wt_ref[n_a:n_a + SC_W, :])
    v = _dot_nt(hb, wt_ref[n_a + SC_W:n_a + 2 * SC_W, :]) * _dot_nt(hb, wt_ref[n_a + 2 * SC_W:n_a + 3 * SC_W, :])
    seq = jnp.where(i < CTX_TILES, CTX_LEN, LAT_LEN)
    r = lax.broadcasted_iota(jnp.int32, (TM, 1), 0) & (seq - 1)
    vp = jnp.where(r == 0, 0.0, pltpu.roll(v, 1, 0))
    vn = jnp.where(r == seq - 1, 0.0, pltpu.roll(v, TM - 1, 0))
    w = wsc_ref[0]
    y = w[0:1] * vp + w[1:2] * v + w[2:3] * vn
    sc_ref[...] = (gb * y).astype(BF16)


def _even_proj(xp, xs, mods, norm_mix, w_in, q_a_norm, wqb, kv_a_norm, qn, w_sc, tabs):
    full = lambda shape: pl.BlockSpec(shape, lambda i: (0,) * len(shape))
    tab = pl.BlockSpec((TM, HEAD_PAD), lambda i: (jnp.where(i < CTX_TILES, 1, 0), 0))
    row = lambda n: pl.BlockSpec((TM, n), lambda i: (i, 0))
    ctx_row = lambda n: pl.BlockSpec((TM, n), lambda i: (jnp.minimum(i, CTX_TILES - 1), 0))
    return pl.pallas_call(
        _even_proj_kernel,
        grid=(N_TILES,),
        in_specs=[
            ctx_row(D),
            pl.BlockSpec((TM, D), lambda i: (jnp.maximum(i - CTX_TILES, 0), 0)),
            _mod_spec(0),
            full((2, D)),
            pl.BlockSpec((IN0_W, D), lambda i: (0, 0), pipeline_mode=pl.Buffered(1)),
            full((1, Q_LORA)),
            full((Q_LORA, HEADS * HEAD_PAD)), full((1, KV_LORA)), full((1, HEAD_PAD)),
            full((1, 3, SC_W)), tab, tab, tab,
        ],
        out_specs=[row(HEADS * HEAD_PAD), row(KV_LORA), row(HEAD_PAD), row(SC_W),
                   ctx_row(KV_LORA), ctx_row(QK_ROPE)],
        out_shape=[
            jax.ShapeDtypeStruct((T, HEADS * HEAD_PAD), BF16),
            jax.ShapeDtypeStruct((T, KV_LORA), F32),
            jax.ShapeDtypeStruct((T, HEAD_PAD), F32),
            jax.ShapeDtypeStruct((T, SC_W), BF16),
            jax.ShapeDtypeStruct((T_CTX, KV_LORA), F32),
            jax.ShapeDtypeStruct((T_CTX, QK_ROPE), F32),
        ],
        scratch_shapes=[pltpu.VMEM((IN0_W, D), BF16)],
        compiler_params=_params("arbitrary"),
        name="even_proj",
    )(xp, xs, mods, norm_mix, w_in, q_a_norm, wqb, kv_a_norm, qn, w_sc, *tabs)


def _kv_proj_kernel(ckv_ref, kpe_ref, wkvb_ref, kn_ref, cos_ref, s1_ref, s2_ref, k_ref, kv_ref):
    kv = _dot(ckv_ref[...].astype(BF16), wkvb_ref[...])
    kv_ref[...] = kv.astype(BF16)
    kpe = pltpu.roll(kpe_ref[...], QK_NOPE, 1)
    lane = lax.broadcasted_iota(jnp.int32, (1, HEAD_PAD), 1)
    kn = kn_ref[...]
    pe_sq = jnp.sum(kpe * kpe, axis=-1, keepdims=True)
    pe = _rope(kpe * kn, cos_ref[...], s1_ref[...], s2_ref[...])
    for h in range(HEADS):
        blk = kv[:, h * HEAD_PAD:(h + 1) * HEAD_PAD]
        nope = jnp.where(lane < QK_NOPE, blk, 0.0)
        ms = (jnp.sum(nope * nope, axis=-1, keepdims=True) + pe_sq) * (1.0 / QK_HEAD)
        k = jnp.where(lane < QK_NOPE, blk * kn, pe) * lax.rsqrt(ms + EPS)
        k_ref[:, h * HEAD_PAD:(h + 1) * HEAD_PAD] = k.astype(BF16)


def _kv_proj(ckv, kpe, wkvb, kn, tabs, tab_index, name, TKV):
    n = ckv.shape[0]
    full = lambda shape: pl.BlockSpec(shape, lambda i: (0,) * len(shape))
    tab = pl.BlockSpec((TKV, HEAD_PAD), lambda i: (tab_index(i), 0))
    row = lambda w: pl.BlockSpec((TKV, w), lambda i: (i, 0))
    return pl.pallas_call(
        _kv_proj_kernel,
        grid=(n // TKV,),
        in_specs=[row(KV_LORA), row(HEAD_PAD), full((KV_LORA, HEADS * HEAD_PAD)), full((1, HEAD_PAD)),
                  tab, tab, tab],
        out_specs=[row(HEADS * HEAD_PAD), row(HEADS * HEAD_PAD)],
        out_shape=[jax.ShapeDtypeStruct((n, HEADS * HEAD_PAD), BF16)] * 2,
        compiler_params=_params("arbitrary"),
        name=name,
    )(ckv, kpe, wkvb, kn, *tabs)


def _pair_out(o0, o1):
    lane = lax.broadcasted_iota(jnp.int32, (1, HEAD_PAD), 1)
    return jnp.where(lane < V_HEAD, pltpu.roll(o0, V_HEAD, 1), o1).astype(BF16)


CTX_SEQS = 4


def _attn_ctx_kernel(q_ref, k_ref, kv_ref, o_ref):
    for b in range(CTX_SEQS):
        rows = slice(b * CTX_LEN, (b + 1) * CTX_LEN)
        for hp in range(HEADS // 2):
            outs = []
            for j in range(2):
                lanes = slice((2 * hp + j) * HEAD_PAD, (2 * hp + j + 1) * HEAD_PAD)
                s = _dot_nt(q_ref[rows, lanes], k_ref[rows, lanes])
                p = jnp.exp(s - jnp.max(s, axis=-1, keepdims=True))
                l = jnp.sum(p, axis=-1, keepdims=True)
                outs.append(_dot(p.astype(BF16), kv_ref[rows, lanes]) / l)
            o_ref[rows, hp * HEAD_PAD:(hp + 1) * HEAD_PAD] = _pair_out(*outs)


def _attn_ctx(q, k, kv):
    blk = pl.BlockSpec((CTX_SEQS * CTX_LEN, HEADS * HEAD_PAD), lambda b: (b, 0))
    return pl.pallas_call(
        _attn_ctx_kernel,
        grid=(N_CTX_SEQ // CTX_SEQS,),
        in_specs=[blk, blk, blk],
        out_specs=pl.BlockSpec((CTX_SEQS * CTX_LEN, HEADS * V_HEAD), lambda b: (b, 0)),
        out_shape=jax.ShapeDtypeStruct((T_CTX, HEADS * V_HEAD), BF16),
        compiler_params=_params("arbitrary"),
        name="attn_ctx",
    )(q, k, kv)


LAT_HEADS = 8


def _attn_lat_kernel(q_ref, kc_ref, kvc_ref, kl_ref, kvl_ref, o_ref):
    for hp in range(LAT_HEADS // 2):
        outs = []
        for j in range(2):
            h0 = (2 * hp + j) * HEAD_PAD
            lanes = slice(h0, h0 + HEAD_PAD)
            q = q_ref[:, lanes]
            sc = _dot_nt(q, kc_ref[:, lanes])
            sl = _dot_nt(q, kl_ref[:, lanes])
            m = jnp.maximum(jnp.max(sc, axis=-1, keepdims=True), jnp.max(sl, axis=-1, keepdims=True))
            pc, pl_ = jnp.exp(sc - m), jnp.exp(sl - m)
            l = jnp.sum(pc, axis=-1, keepdims=True) + jnp.sum(pl_, axis=-1, keepdims=True)
            o = _dot(pc.astype(BF16), kvc_ref[:, lanes]) + _dot(pl_.astype(BF16), kvl_ref[:, lanes])
            outs.append(o / l)
        o_ref[:, hp * HEAD_PAD:(hp + 1) * HEAD_PAD] = _pair_out(*outs)


def _attn_lat(q, kc, kvc, k, kv):
    nq = LAT_LEN // TQ
    q0 = T_CTX // TQ
    kl0 = T_CTX // LAT_LEN
    width = LAT_HEADS * HEAD_PAD
    lat = pl.BlockSpec((LAT_LEN, width), lambda b, hg, t: (kl0 + b, hg))
    ctx = pl.BlockSpec((PAST, width), lambda b, hg, t: (b, hg))
    return pl.pallas_call(
        _attn_lat_kernel,
        grid=(N_LAT_SEQ, HEADS // LAT_HEADS, nq),
        in_specs=[pl.BlockSpec((TQ, width), lambda b, hg, t: (q0 + b * nq + t, hg)), ctx, ctx, lat, lat],
        out_specs=pl.BlockSpec((TQ, LAT_HEADS * V_HEAD), lambda b, hg, t: (b * nq + t, hg)),
        out_shape=jax.ShapeDtypeStruct((T_LAT, HEADS * V_HEAD), BF16),
        compiler_params=_params("arbitrary", "arbitrary", "arbitrary"),
        name="attn_lat",
    )(q, kc, kvc, k, kv)


FFN_NC = D_FF // FFN_FC
FFN_TM = 512
FFN_CTX_TILES = T_CTX // FFN_TM


def _ffn_kernel(oc_ref, ol_ref, sc_ref, xp_ref, xs_ref, wo_ref, nf_ref, wg_ref, wu_ref, wd_ref,
                mod0_ref, mod1_ref, nm_ref, x2_ref, h3_ref, wg_all, wu_all, wd_all, x1_ref, hs_ref, act_ref):
    t = pl.program_id(0)

    i = jnp.maximum(t - (FFN_NC - 1), 0)
    m0, m1 = _mod_row(mod0_ref, i * FFN_TM), _mod_row(mod1_ref, i * FFN_TM)

    def mixer():
        ctx = i < FFN_CTX_TILES
        attn = jnp.where(ctx, oc_ref[...], ol_ref[...])
        x = jnp.where(ctx, xp_ref[...], xs_ref[...])
        x1 = x + m0[:, 2 * D:3 * D] * _dot(jnp.concatenate([attn, sc_ref[...]], axis=1), wo_ref[...])
        h = (_rms(x1, nf_ref[0:1, :]) * (1.0 + m0[:, 4 * D:5 * D]) + m0[:, 3 * D:4 * D]).astype(BF16)
        return x1, h

    def up(h, c):
        return (_silu(_dot(h, wg_all[c])) * _dot(h, wu_all[c])).astype(BF16)

    def down(x1, act):
        x2 = x1 + m0[:, 5 * D:6 * D] * _dot(act, wd_all[...])
        x2_ref[...] = x2
        h3_ref[...] = (_rms(x2, nm_ref[1:2, :]) * (1.0 + m1[:, D:2 * D]) + m1[:, 0:D]).astype(BF16)

    @pl.when(t == 0)
    def _():
        x1, h = mixer()
        x1_ref[...] = x1
        hs_ref[...] = h

    @pl.when(t < FFN_NC)
    def _stage():
        wg_all[t] = wg_ref[...].astype(BF16)
        wu_all[t] = wu_ref[...].astype(BF16)
        wd_all[pl.ds(pl.multiple_of(t * FFN_FC, FFN_FC), FFN_FC), :] = wd_ref[...].astype(BF16)
        act_ref[t] = up(hs_ref[...], t)

    @pl.when(t == FFN_NC - 1)
    def _():
        down(x1_ref[...], jnp.concatenate([act_ref[c] for c in range(FFN_NC)], axis=1))

    @pl.when(t > FFN_NC - 1)
    def _tile():
        x1, h = mixer()
        down(x1, jnp.concatenate([up(h, c) for c in range(FFN_NC)], axis=1))


def _ffn(oc, ol, sc, xp, xs, wo, norm_ffn, w_gu, w_down, mods, norm_mix):
    chunk = lambda t: jnp.minimum(t, FFN_NC - 1)
    tile = lambda t: jnp.maximum(t - (FFN_NC - 1), 0)
    full = lambda shape: pl.BlockSpec(shape, lambda t: (0,) * len(shape))
    row = lambda n: pl.BlockSpec((FFN_TM, n), lambda t: (tile(t), 0))
    first = lambda n: pl.BlockSpec((FFN_TM, n), lambda t: (jnp.minimum(tile(t), FFN_CTX_TILES - 1), 0))
    second = lambda n: pl.BlockSpec((FFN_TM, n), lambda t: (jnp.maximum(tile(t) - FFN_CTX_TILES, 0), 0))
    return pl.pallas_call(
        _ffn_kernel,
        grid=(FFN_NC - 1 + T // FFN_TM,),
        in_specs=[first(HEADS * V_HEAD), second(HEADS * V_HEAD), row(SC_W), first(D), second(D),
                  full((HEADS * V_HEAD + SC_W, D)), full((2, D)),
                  pl.BlockSpec((D, FFN_FC), lambda t: (0, chunk(t))),
                  pl.BlockSpec((D, FFN_FC), lambda t: (0, FFN_NC + chunk(t))),
                  pl.BlockSpec((FFN_FC, D), lambda t: (chunk(t), 0)),
                  _mod_spec(0), _mod_spec(1), full((2, D))],
        out_specs=[row(D), row(D)],
        out_shape=[jax.ShapeDtypeStruct((T, D), F32), jax.ShapeDtypeStruct((T, D), BF16)],
        scratch_shapes=[pltpu.VMEM((FFN_NC, D, FFN_FC), BF16), pltpu.VMEM((FFN_NC, D, FFN_FC), BF16),
                        pltpu.VMEM((D_FF, D), BF16), pltpu.VMEM((FFN_TM, D), F32), pltpu.VMEM((FFN_TM, D), BF16),
                        pltpu.VMEM((FFN_NC, FFN_TM, FFN_FC), BF16)],
        compiler_params=_params("arbitrary"),
        name="ffn_dense",
    )(oc, ol, sc, xp, xs, wo, norm_ffn, w_gu, w_gu, w_down, mods, mods, norm_mix)


CONF_CB = 256
CONF_SEG = 256
CONF_HALO = 16
CONF_SEGP = CONF_SEG + 2 * CONF_HALO
CONF_PIECE = 256


def _conf_kernel(h_ref, x2_ref, w1_ref, b1_ref, wdw_ref, bdw_ref, lng_ref, lnb_ref, w2_ref, b2_ref,
                 mod_ref, nf_ref, rt_ref, x3_ref, h4_ref, lg_ref, pad_ref, conv_ref):
    i = pl.program_id(0)
    nseg = TM // CONF_SEG
    h = h_ref[...]
    joined = jnp.where(i < CTX_TILES, 0.0, 1.0)
    zeros_halo = jnp.zeros((CONF_HALO, CONF_CB), F32)
    for cb in range(D // CONF_CB):
        c0 = cb * CONF_CB
        a = _dot(h, w1_ref[:, c0:c0 + CONF_CB]) + b1_ref[:, c0:c0 + CONF_CB]
        g = _dot(h, w1_ref[:, D + c0:D + c0 + CONF_CB]) + b1_ref[:, D + c0:D + c0 + CONF_CB]
        u = a * jax.nn.sigmoid(g)
        for s in range(nseg):
            base = s * CONF_SEGP
            top = u[s * CONF_SEG - CONF_HALO:s * CONF_SEG] * joined if s > 0 else zeros_halo
            bot = (u[(s + 1) * CONF_SEG:(s + 1) * CONF_SEG + CONF_HALO] * joined
                   if s < nseg - 1 else zeros_halo)
            pad_ref[0, base:base + CONF_HALO, :] = top
            pad_ref[0, base + CONF_HALO:base + CONF_HALO + CONF_SEG, :] = u[s * CONF_SEG:(s + 1) * CONF_SEG]
            pad_ref[0, base + CONF_HALO + CONF_SEG:base + CONF_SEGP, :] = bot

        p0 = pad_ref[0]
        rows = nseg * CONF_SEGP
        for b in range(1, 8):
            pad_ref[b] = pltpu.roll(p0, rows - b, 0)

        def piece(t, carry):
            s = t // (CONF_SEG // CONF_PIECE)
            q0 = (t % (CONF_SEG // CONF_PIECE)) * CONF_PIECE
            src = pl.multiple_of(s * CONF_SEGP + q0, 8)
            acc = jnp.zeros((CONF_PIECE, CONF_CB), F32)
            for j in range(CONF_K):
                hi, lo = (j + 1) // 8, (j + 1) % 8
                acc = acc + wdw_ref[0, j:j + 1, c0:c0 + CONF_CB] * pad_ref[lo, pl.ds(src + 8 * hi, CONF_PIECE), :]
            dst = pl.multiple_of(s * CONF_SEG + q0, 8)
            conv_ref[pl.ds(dst, CONF_PIECE), c0:c0 + CONF_CB] = acc + bdw_ref[:, c0:c0 + CONF_CB]
            return carry

        lax.fori_loop(0, TM // CONF_PIECE, piece, 0)

    m = _mod_row(mod_ref, i * TM)
    half = TM // 2
    for r0 in (0, half):
        rows = slice(r0, r0 + half)
        y = conv_ref[rows, :]
        mu = jnp.mean(y, axis=-1, keepdims=True)
        yc = y - mu
        var = jnp.mean(yc * yc, axis=-1, keepdims=True)
        y = _silu(yc * lax.rsqrt(var + EPS) * lng_ref[...] + lnb_ref[...])
        out = _dot(y.astype(BF16), w2_ref[...]) + b2_ref[...]
        x3 = x2_ref[rows, :] + m[:, 2 * D:3 * D] * out
        x3_ref[rows, :] = x3
        h4 = _rms(x3, nf_ref[1:2, :]) * (1.0 + m[:, 4 * D:5 * D]) + m[:, 3 * D:4 * D]
        h4_ref[rows, :] = h4.astype(BF16)
        lg_ref[:, rows] = lax.dot_general(rt_ref[...], h4, (((1,), (1,)), ((), ())),
                                          precision=lax.Precision.HIGHEST, preferred_element_type=F32)


def _conf(h3, x2, w1, b1, wdw, bdw, lng, lnb, w2, b2, mods, norm_ffn1, router_t):
    full = lambda shape: pl.BlockSpec(shape, lambda i: (0,) * len(shape))
    row = lambda n: pl.BlockSpec((TM, n), lambda i: (i, 0))
    return pl.pallas_call(
        _conf_kernel,
        grid=(N_TILES,),
        in_specs=[row(D), row(D), full((D, 2 * D)), full((1, 2 * D)), full((1, CONF_K, D)), full((1, D)),
                  full((1, D)), full((1, D)), full((D, D)), full((1, D)),
                  _mod_spec(1), full((2, D)), full((N_EXP, D))],
        out_specs=[row(D), row(D), pl.BlockSpec((N_EXP, TM), lambda i: (0, i))],
        out_shape=[jax.ShapeDtypeStruct((T, D), F32), jax.ShapeDtypeStruct((T, D), BF16),
                   jax.ShapeDtypeStruct((N_EXP, T), F32)],
        scratch_shapes=[pltpu.VMEM((8, (TM // CONF_SEG) * CONF_SEGP, CONF_CB), F32),
                        pltpu.VMEM((TM, D), F32)],
        compiler_params=_params("arbitrary"),
        name="conformer_conv",
    )(h3, x2, w1, b1, wdw, bdw, lng, lnb, w2, b2, mods, norm_ffn1, router_t)


TB = 256
N_TB = T // TB
SUB = 128
SUBS = 24
SM = SUBS * SUB
N_SUB_MAX = 2 * T // SUB + N_EXP
N_SUP_MAX = (N_SUB_MAX + N_EXP * (SUBS - 1)) // SUBS
YS_ROWS = (N_SUB_MAX + 4) * SUB
WIN_ALIGN = 16
WIN_A = TB // 2
WIN_B = TB // 2 + WIN_ALIGN
FIRST_STRIDE = 32
UNIT_STRIDE = 64
GATHER_BLOCKS = 4


def _route_kernel(lg_ref, g_ref, rank_ref, first_ref):
    lg = lg_ref[...]
    idx = lax.broadcasted_iota(jnp.int32, lg.shape, 0).astype(F32)
    none = float(N_EXP)
    m1 = jnp.max(lg, axis=0, keepdims=True)
    i1 = jnp.min(jnp.where(lg == m1, idx, none), axis=0, keepdims=True)
    rest = jnp.where(idx == i1, -jnp.inf, lg)
    m2 = jnp.max(rest, axis=0, keepdims=True)
    i2 = jnp.min(jnp.where(rest == m2, idx, none), axis=0, keepdims=True)
    e = jnp.exp(m2 - m1)
    w1 = 1.0 / (1.0 + e)
    w2 = e / (1.0 + e)
    g_ref[...] = jnp.where(idx == i1, w1, 0.0) + jnp.where(idx == i2, w2, 0.0)

    mask = jnp.where(idx == i1, 1.0, 0.0) + jnp.where(idx == i2, 1.0, 0.0)
    before = (lax.broadcasted_iota(jnp.int32, (TB, TB), 0) < lax.broadcasted_iota(jnp.int32, (TB, TB), 1))
    before = jnp.where(before, 1.0, 0.0).astype(BF16)
    lane = lax.broadcasted_iota(jnp.int32, (N_EXP, 128), 1)
    carry = jnp.zeros((N_EXP, 1), F32)
    first = jnp.zeros((N_EXP, 128), F32)
    for b in range(N_TB):
        mb = mask[:, b * TB:(b + 1) * TB]
        local = _dot(mb.astype(BF16), before)
        rank_ref[:, b * TB:(b + 1) * TB] = jnp.where(mb > 0.0, local + carry, -1.0)
        first = jnp.where(lane == b, carry, first)
        carry = carry + jnp.sum(mb, axis=1, keepdims=True)
    first_ref[...] = jnp.where(lane == N_TB, carry, first)


def _route(logits_t):
    return pl.pallas_call(
        _route_kernel,
        out_shape=[jax.ShapeDtypeStruct((N_EXP, T), F32), jax.ShapeDtypeStruct((N_EXP, T), F32),
                   jax.ShapeDtypeStruct((N_EXP, 128), F32)],
        compiler_params=pltpu.CompilerParams(vmem_limit_bytes=VMEM_LIMIT),
        name="route",
    )(logits_t)


def _moe_plan(first):
    first = first[:, :FIRST_STRIDE].astype(jnp.int32)
    cnt = first[:, N_TB]
    nt = (cnt + (SUB - 1)) // SUB
    off_end = jnp.cumsum(nt)
    off = off_end - nt
    nsub = off_end[-1]
    nsup = (nt + (SUBS - 1)) // SUBS
    sup_end = jnp.cumsum(nsup)
    sup_off = sup_end - nsup
    s = jnp.minimum(jnp.arange(N_SUP_MAX), sup_end[-1] - 1)
    valid = jnp.arange(N_SUP_MAX) < sup_end[-1]
    se = jnp.sum(s[:, None] >= sup_end[None, :], axis=1)
    mine = se[:, None] == jnp.arange(N_EXP)[None, :]
    pick = lambda v: jnp.sum(jnp.where(mine, v[None, :], 0), axis=1)
    sk0 = (s - pick(sup_off)) * SUBS
    sns = jnp.where(valid, jnp.clip(pick(nt) - sk0, 0, SUBS), 0)
    sj0 = pick(off) + sk0
    base = (jnp.arange(UNIT_STRIDE) * SUB)[None, :, None]
    blo = jnp.minimum(jnp.sum(first[:, None, 1:N_TB + 1] <= base, axis=2), N_TB - 1)
    end = jnp.minimum(base + SUB, cnt[:, None, None])
    bhi = jnp.maximum(jnp.sum(first[:, None, :N_TB] < end, axis=2) - 1, blo)
    ng = (bhi - blo) // GATHER_BLOCKS + 1
    start = SUB * off[:, None] + first[:, :N_TB]
    lead = start & (WIN_ALIGN - 1)
    wina = start - lead
    rel = first[:, :N_TB] - lead
    need = lead + (first[:, 1:N_TB + 1] - first[:, :N_TB]) > WIN_A
    winb = lax.cummax(jnp.where(need, wina + WIN_A, 0), axis=1)
    wide = jnp.any(need, axis=0)
    flat = lambda parts: jnp.concatenate([p.astype(jnp.int32).reshape(-1) for p in parts])
    gmm = dict(se=se, sk0=sk0, sns=sns, sj0=sj0, nsub=nsub, first=first, blo=blo, ng=ng)
    comb = dict(wina=wina.T, winb=winb.T, rel=rel.T, wide=wide)
    return flat([gmm[k] for k in _GMM_TAB]), flat([comb[k] for k in _COMB_TAB])


def _offsets(sizes):
    out, pos = {}, 0
    for name, n in sizes.items():
        out[name], pos = pos, pos + n
    return out


_GMM_TAB = _offsets(dict(se=N_SUP_MAX, sk0=N_SUP_MAX, sns=N_SUP_MAX, sj0=N_SUP_MAX, nsub=1,
                         first=N_EXP * FIRST_STRIDE, blo=N_EXP * UNIT_STRIDE, ng=N_EXP * UNIT_STRIDE))
_COMB_TAB = _offsets(dict(wina=N_TB * N_EXP, winb=N_TB * N_EXP, rel=N_TB * N_EXP, wide=N_TB))


class _Section:
    def __init__(self, ref, offset):
        self.ref, self.offset = ref, offset

    def __getitem__(self, i):
        return self.ref[self.offset + i]


def _moe_gmm_kernel(tab_ref, x_ref, rank_ref, gate_ref, wg_ref, wu_ref, wd_ref, ys_ref,
                    xs_ref, gs_ref, yacc_ref, wgb_ref, wub_ref, wdb_ref, sem):
    se_ref, sk0_ref, sns_ref, sj0_ref, nsub_ref, first_ref, blo_ref, ng_ref = (
        _Section(tab_ref, _GMM_TAB[k]) for k in ("se", "sk0", "sns", "sj0", "nsub", "first", "blo", "ng"))
    s, c = pl.program_id(0), pl.program_id(1)
    nc = pl.num_programs(1)
    e, k0, ns = se_ref[s], sk0_ref[s], sns_ref[s]

    def sub_rows(k):
        return pl.ds(pl.multiple_of(k * SUB, SUB), SUB)

    def out_copy(k, row0):
        dst = ys_ref.at[pl.ds(pl.multiple_of(row0 + k * SUB, SUB), SUB)]
        return pltpu.make_async_copy(xs_ref.at[sub_rows(k)], dst, sem.at[k])

    @pl.when((ns > 0) & (c == 0))
    def _gather():
        def group(k, g):
            slot = (lax.broadcasted_iota(jnp.int32, (SUB, 1), 0) + (k0 + k) * SUB).astype(F32)
            b0 = blo_ref[e * UNIT_STRIDE + k0 + k] + g * GATHER_BLOCKS
            t0 = pl.multiple_of(jnp.minimum(b0, N_TB - GATHER_BLOCKS) * TB, TB)
            lo = first_ref[e * FIRST_STRIDE + b0].astype(F32)
            cols = pl.ds(t0, GATHER_BLOCKS * TB)
            hit = rank_ref[pl.ds(e, 1), cols] == jnp.where(slot >= lo, slot, -2.0)
            rows = _dot(jnp.where(hit, 1.0, 0.0).astype(BF16), x_ref[cols, :])
            gate = jnp.sum(jnp.where(hit, gate_ref[pl.ds(e, 1), cols], 0.0), axis=-1, keepdims=True)
            return rows, gate

        def first(k):
            rows, gate = group(k, 0)
            xs_ref[sub_rows(k), :] = rows.astype(BF16)
            gs_ref[sub_rows(k), :] = gate
            yacc_ref[sub_rows(k), :] = jnp.zeros((SUB, D), F32)

        def more(k):
            def body(g, carry):
                rows, gate = group(k, g)
                xs_ref[sub_rows(k), :] = (xs_ref[sub_rows(k), :].astype(F32) + rows).astype(BF16)
                gs_ref[sub_rows(k), :] += gate
                return carry

            lax.fori_loop(1, ng_ref[e * UNIT_STRIDE + k0 + k], body, 0)

        def quad(p, carry):
            for j in range(4):
                first(4 * p + j)
            for j in range(4):
                more(4 * p + j)
            return carry

        def single(k, carry):
            first(k)
            more(k)
            return carry

        lax.fori_loop(0, ns >> 2, quad, 0)
        lax.fori_loop(ns & ~3, ns, single, 0)

    @pl.when(ns > 0)
    def _compute():
        row0 = sj0_ref[s] * SUB

        def swiglu(rows, wg, wu, wd):
            x = xs_ref[rows, :]
            g = _dot(x, wg)
            u = _dot(x, wu)
            yacc_ref[rows, :] += _dot((_silu(g) * u).astype(BF16), wd)

        def finish(first_sub, n):
            @pl.when(c == nc - 1)
            def _():
                for k in range(n):
                    rows = sub_rows(first_sub + k)
                    xs_ref[rows, :] = (yacc_ref[rows, :] * gs_ref[rows, :]).astype(BF16)
                    out_copy(first_sub + k, row0).start()

        def first_chain(rows):
            wg, wu, wd = wg_ref[0].astype(BF16), wu_ref[0].astype(BF16), wd_ref[0].astype(BF16)
            wgb_ref[...] = wg
            wub_ref[...] = wu
            wdb_ref[...] = wd
            swiglu(rows, wg, wu, wd)

        @pl.when(ns >= 4)
        def _():
            first_chain(pl.ds(0, 4 * SUB))
            finish(0, 4)

        @pl.when(ns < 4)
        def _():
            first_chain(pl.ds(0, SUB))
            finish(0, 1)

        done = jnp.where(ns >= 4, 4, 1)
        rest = ns - done

        def chain(first_sub, n):
            rows = pl.ds(pl.multiple_of(first_sub * SUB, SUB), n * SUB)
            swiglu(rows, wgb_ref[...], wub_ref[...], wdb_ref[...])
            finish(first_sub, n)

        def eight(k, carry):
            chain(done + 8 * k, 8)
            return carry

        lax.fori_loop(0, rest >> 3, eight, 0)
        done8 = done + (rest & ~7)
        for n in (4, 2, 1):
            @pl.when((rest & n) != 0)
            def _(n=n):
                chain(done8 + (rest & (7 & ~(2 * n - 1))), n)

    @pl.when((ns > 0) & (c == nc - 1))
    def _store_done():
        row0 = sj0_ref[s] * SUB

        def done(k, carry):
            out_copy(k, row0).wait()
            return carry

        lax.fori_loop(0, ns, done, 0)

    @pl.when((s == pl.num_programs(0) - 1) & (c == nc - 1))
    def _zero_tail():
        xs_ref[0:SUB, :] = jnp.zeros((SUB, D), BF16)
        nsub = nsub_ref[0]

        def fill(k, carry):
            cp = out_copy(0, (nsub + k) * SUB)
            cp.start()
            cp.wait()
            return carry

        lax.fori_loop(0, YS_ROWS // SUB - nsub, fill, 0)


def _moe_gmm(tab, h4, rank, gates, w_gu, w_down):
    nc = D_FFE // MOE_FC

    def expert(s, tab):
        return tab[_GMM_TAB["se"] + s]

    def chunk(s, c, tab):
        return jnp.where(tab[_GMM_TAB["sns"] + s] > 0, c, nc - 1)

    return pl.pallas_call(
        _moe_gmm_kernel,
        grid_spec=pltpu.PrefetchScalarGridSpec(
            num_scalar_prefetch=1,
            grid=(N_SUP_MAX, nc),
            in_specs=[
                pl.BlockSpec((T, D), lambda s, c, tab: (0, 0), pipeline_mode=pl.Buffered(1)),
                pl.BlockSpec((N_EXP, T), lambda s, c, tab: (0, 0)),
                pl.BlockSpec((N_EXP, T), lambda s, c, tab: (0, 0)),
                pl.BlockSpec((1, D, MOE_FC), lambda s, c, tab: (expert(s, tab), 0, chunk(s, c, tab))),
                pl.BlockSpec((1, D, MOE_FC), lambda s, c, tab: (expert(s, tab), 0, nc + chunk(s, c, tab))),
                pl.BlockSpec((1, MOE_FC, D), lambda s, c, tab: (expert(s, tab), chunk(s, c, tab), 0)),
            ],
            out_specs=pl.BlockSpec(memory_space=pl.ANY),
            scratch_shapes=[
                pltpu.VMEM((SM, D), BF16), pltpu.VMEM((SM, 1), F32), pltpu.VMEM((SM, D), F32),
                pltpu.VMEM((D, MOE_FC), BF16), pltpu.VMEM((D, MOE_FC), BF16), pltpu.VMEM((MOE_FC, D), BF16),
                pltpu.SemaphoreType.DMA((SUBS,)),
            ],
        ),
        out_shape=jax.ShapeDtypeStruct((YS_ROWS, D), BF16),
        compiler_params=_params("arbitrary", "arbitrary"),
        name="moe_gmm",
    )(tab, h4, rank, gates, w_gu, w_gu, w_down)


def _moe_combine_kernel(tab_ref, *refs):
    rel_ref, wide_ref = _Section(tab_ref, _COMB_TAB["rel"]), _Section(tab_ref, _COMB_TAB["wide"])
    ya, yb = refs[:N_EXP], refs[N_EXP:2 * N_EXP]
    rank_ref, x3_ref, mod_ref, yp_ref, ys_ref, ycat_ref, acc_ref = refs[2 * N_EXP:]
    b = pl.program_id(0)

    def onehot(e, first_row, n):
        row = lax.broadcasted_iota(jnp.int32, (n, 1), 0)
        slot = (row + (rel_ref[b * N_EXP + e] + first_row)).astype(F32)
        return jnp.where(rank_ref[e:e + 1, :] == slot, 1.0, 0.0).astype(BF16)

    def gather(y_refs, first_row, base, n):
        pieces = []
        for e in range(N_EXP):
            ycat_ref[base + e * n:base + (e + 1) * n, :] = y_refs[e][...]
            pieces.append(onehot(e, first_row, n))
        return pieces

    def combine(pieces, rows):
        return lax.dot_general(jnp.concatenate(pieces, axis=0), ycat_ref[0:rows, :], (((0,), (0,)), ((), ())),
                               preferred_element_type=F32)

    @pl.when(wide_ref[b] == 0)
    def _():
        acc_ref[...] = combine(gather(ya, 0, 0, WIN_A), N_EXP * WIN_A)

    @pl.when(wide_ref[b] != 0)
    def _():
        pieces = gather(ya, 0, 0, WIN_A) + gather(yb, WIN_A, N_EXP * WIN_A, WIN_B)
        acc_ref[...] = combine(pieces, N_EXP * (WIN_A + WIN_B))

    out = x3_ref[...] + _mod_row(mod_ref, b * TB)[:, 5 * D:6 * D] * acc_ref[...]

    @pl.when(b < T_CTX // TB)
    def _():
        yp_ref[...] = out

    @pl.when(b >= T_CTX // TB)
    def _():
        ys_ref[...] = out


def _moe_combine(tab, ysorted, rank, x3, mods):
    ctx_blocks = T_CTX // TB

    def window(e, second):
        def index(b, tab):
            start = tab[_COMB_TAB["winb" if second else "wina"] + b * N_EXP + e]
            return pl.multiple_of(start, WIN_ALIGN), 0
        return pl.BlockSpec((pl.Element(WIN_B if second else WIN_A), pl.Element(D)), index)

    return pl.pallas_call(
        _moe_combine_kernel,
        grid_spec=pltpu.PrefetchScalarGridSpec(
            num_scalar_prefetch=1,
            grid=(N_TB,),
            in_specs=[window(e, False) for e in range(N_EXP)] + [window(e, True) for e in range(N_EXP)] + [
                pl.BlockSpec((N_EXP, TB), lambda b, *_: (0, b)),
                pl.BlockSpec((TB, D), lambda b, *_: (b, 0)),
                _mod_spec(1),
            ],
            out_specs=[pl.BlockSpec((TB, D), lambda b, *_: (jnp.minimum(b, ctx_blocks - 1), 0)),
                       pl.BlockSpec((TB, D), lambda b, *_: (jnp.maximum(b - ctx_blocks, 0), 0))],
            scratch_shapes=[pltpu.VMEM((N_EXP * (WIN_A + WIN_B), D), BF16), pltpu.VMEM((TB, D), F32)],
        ),
        out_shape=[jax.ShapeDtypeStruct((T_CTX, D), F32), jax.ShapeDtypeStruct((T_LAT, D), F32)],
        compiler_params=_params("arbitrary"),
        name="moe_combine",
    )(tab, *([ysorted] * (2 * N_EXP)), rank, x3, mods)


def _pad_heads(w, width):
    lead = w.shape[:-1]
    w = w.reshape(*lead, HEADS, width)
    w = jnp.pad(w, [(0, 0)] * len(lead) + [(0, 0), (0, HEAD_PAD - width)])
    return w.reshape(*lead, HEADS * HEAD_PAD)


def kernel(x_prompt, x_sample, cache_ckv, cache_kpe, c, c_ctx, ada_w, ada_b, norm_mix, norm_ffn, w_in, q_a_norm,
           w_qb, kv_a_norm, w_kvb, q_norm, k_norm, w_sc, w_o, ffn_gu, ffn_down, conv_pw1, conv_pw1_b, conv_dw,
           conv_dw_b, conv_ln_g, conv_ln_b, conv_pw2, conv_pw2_b, router, moe_gu, moe_down):
    xp = x_prompt.reshape(T_CTX, D)
    xs = x_sample.reshape(T_LAT, D)

    mods = _adaln(c_ctx, c, ada_w, ada_b)

    wqb = _pad_heads(w_qb[0], QK_HEAD).astype(BF16)
    wkvb = w_kvb[0].astype(BF16)
    qn = jnp.pad(q_norm[0], (0, HEAD_PAD - QK_HEAD)).reshape(1, HEAD_PAD)
    kn = jnp.pad(k_norm[0], (0, HEAD_PAD - QK_HEAD)).reshape(1, HEAD_PAD)
    tabs = _rope_tables()

    w_in_t = jnp.swapaxes(w_in[0], 0, 1)
    q, ckv, kpe, sc, state_ckv, state_kpe = _even_proj(xp, xs, mods, norm_mix, w_in_t, q_a_norm, wqb, kv_a_norm,
                                                       qn, w_sc, tabs)

    lat_tile0 = T_CTX // TKV
    ident = LAT_LEN // TKV
    k, kv = _kv_proj(ckv, kpe, wkvb, kn, tabs,
                     lambda i: jnp.where(i < lat_tile0, ident, (i - lat_tile0) % ident), "kv_proj", TKV)
    cache_kpe_p = jnp.pad(cache_kpe[:, 0].reshape(N_LAT_SEQ * PAST, QK_ROPE), ((0, 0), (0, HEAD_PAD - QK_ROPE)))
    kc, kvc = _kv_proj(cache_ckv[:, 0].reshape(N_LAT_SEQ * PAST, KV_LORA), cache_kpe_p, wkvb, kn, tabs,
                       lambda i: LAT_LEN // (N_LAT_SEQ * PAST), "kv_proj_cache", N_LAT_SEQ * PAST)

    oc = _attn_ctx(q, k, kv)
    ol = _attn_lat(q, kc, kvc, k, kv)
    x2, h3 = _ffn(oc, ol, sc, xp, xs, w_o[0].astype(BF16), norm_ffn, ffn_gu[0], ffn_down[0], mods, norm_mix)

    x3, h4, logits_t = _conf(h3, x2, conv_pw1[0].astype(BF16), conv_pw1_b, conv_dw, conv_dw_b, conv_ln_g,
                             conv_ln_b, conv_pw2[0].astype(BF16), conv_pw2_b, mods, norm_ffn, router[0].T)
    gates, rank, first = _route(logits_t)
    gmm_tab, comb_tab = _moe_plan(first)
    ysorted = _moe_gmm(gmm_tab, h4, rank, gates, moe_gu[0], moe_down[0])
    yp, ys = _moe_combine(comb_tab, ysorted, rank, x3, mods)

    return (yp.reshape(N_CTX_SEQ, CTX_LEN, D), ys.reshape(N_LAT_SEQ, LAT_LEN, D),
            state_ckv.reshape(N_CTX_SEQ, 1, CTX_LEN, KV_LORA), state_kpe.reshape(N_CTX_SEQ, 1, CTX_LEN, QK_ROPE))
```

```python
import jax
import jax.numpy as jnp
import numpy as np
from jax import lax
from jax.experimental import pallas as pl
from jax.experimental.pallas import tpu as pltpu

F32 = jnp.float32
BF16 = jnp.bfloat16

D = 1024
N_CTX_SEQ, CTX_LEN = 16, 256
N_LAT_SEQ, LAT_LEN = 2, 1024
T_CTX = N_CTX_SEQ * CTX_LEN
T_LAT = N_LAT_SEQ * LAT_LEN
T = T_CTX + T_LAT
PAST = 256
GRID_W = 64
HEADS = 8
QK_NOPE, QK_ROPE, V_HEAD = 64, 32, 64
QK_HEAD = QK_NOPE + QK_ROPE
HEAD_PAD = 128
Q_LORA, KV_LORA = 256, 128
SC_W = 512
IN0_W = Q_LORA + KV_LORA + QK_ROPE + 3 * SC_W
CONF_K = 31
D_FF = 2816
N_EXP = 8
D_FFE = 3584
EPS = 1e-6
ROPE_THETA = 10000.0

TM = 1024
N_TILES = T // TM
CTX_TILES = T_CTX // TM
TKV = 1024
TQ = 512
FFN_FC = 256
MOE_FC = 512
VMEM_LIMIT = 56 * 1024 * 1024


def _dot(a, b):
    return jnp.dot(a, b, preferred_element_type=F32)


def _dot_nt(a, b):
    return lax.dot_general(a, b, (((1,), (1,)), ((), ())), preferred_element_type=F32)


def _rms(x, g):
    return x * lax.rsqrt(jnp.mean(x * x, axis=-1, keepdims=True) + EPS) * g


def _silu(x):
    return x * jax.nn.sigmoid(x)


def _params(*sem):
    return pltpu.CompilerParams(dimension_semantics=sem, vmem_limit_bytes=VMEM_LIMIT)


def _mod_row(mod_ref, row0):
    cond = jnp.maximum(row0 - (T_CTX - LAT_LEN), 0) >> 10
    return mod_ref[0, pl.ds(cond, 1), :]


def _mod_spec(layer):
    return pl.BlockSpec((1, 8, 6 * D), lambda *_: (layer, 0, 0))


def _adaln_kernel(cc_ref, c_ref, w_ref, b_ref, o_ref):
    l = pl.program_id(0)
    row = lax.broadcasted_iota(jnp.int32, (8, 1), 0)
    cond = jnp.where(row == 0, cc_ref[...], 0.0)
    for b in range(N_LAT_SEQ):
        cond = jnp.where(row == 1 + b, c_ref[b:b + 1, :], cond)
    o_ref[0] = _dot(_silu(cond).astype(BF16), w_ref[0].astype(BF16)) + b_ref[pl.ds(l, 1), :]


def _adaln(c_ctx, c, ada_w, ada_b):
    depth = ada_w.shape[0]
    tn = 2048
    return pl.pallas_call(
        _adaln_kernel,
        grid=(depth, 6 * D // tn),
        in_specs=[
            pl.BlockSpec((1, D), lambda l, j: (0, 0)),
            pl.BlockSpec((N_LAT_SEQ, D), lambda l, j: (0, 0)),
            pl.BlockSpec((1, D, tn), lambda l, j: (l, 0, j)),
            pl.BlockSpec((depth, tn), lambda l, j: (0, j)),
        ],
        out_specs=pl.BlockSpec((1, 8, tn), lambda l, j: (l, 0, j)),
        out_shape=jax.ShapeDtypeStruct((depth, 8, 6 * D), F32),
        compiler_params=_params("arbitrary", "arbitrary"),
        name="adaln",
    )(c_ctx.reshape(1, D), c, ada_w, ada_b)


def _rope_tables():
    half = QK_ROPE // 2
    nf = half // 2
    pos = np.arange(LAT_LEN)
    inv = ROPE_THETA ** (-np.arange(nf, dtype=np.float64) / nf)
    k = np.arange(QK_ROPE)
    part, idx = k // half, k % half
    p = np.where(part[None, :] == 0, (pos // GRID_W)[:, None], (pos % GRID_W)[:, None])
    ang = p * inv[idx % nf][None, :]
    cos, sin = np.cos(ang), np.sin(ang)
    first = (idx < nf)[None, :]
    s1 = np.where(first, -sin, 0.0)
    s2 = np.where(first, 0.0, sin)

    def place(t, fill):
        tab = np.full((2 * LAT_LEN, HEAD_PAD), fill, np.float32)
        tab[:LAT_LEN, QK_NOPE:QK_HEAD] = t
        return jnp.asarray(tab)

    return place(cos, 1.0), place(s1, 0.0), place(s2, 0.0)


def _rope(blk, cos, s1, s2):
    return blk * cos + pltpu.roll(blk, 8, 1) * s2 + pltpu.roll(blk, HEAD_PAD - 8, 1) * s1


def _head_norm(blk, g):
    ms = jnp.sum(blk * blk, axis=-1, keepdims=True) * (1.0 / QK_HEAD)
    return blk * lax.rsqrt(ms + EPS) * g


def _even_proj_kernel(xp_ref, xs_ref, mod_ref, nm_ref, win_ref, qan_ref, wqb_ref, kvan_ref,
                      qn_ref, wsc_ref, cos_ref, s1_ref, s2_ref,
                      q_ref, ckv_ref, kpe_ref, sc_ref, sckv_ref, skpe_ref, wt_ref):
    i = pl.program_id(0)
    n_a = Q_LORA + KV_LORA + QK_ROPE

    @pl.when(i == 0)
    def _():
        wt_ref[...] = win_ref[...].astype(BF16)

    x = jnp.where(i < CTX_TILES, xp_ref[...], xs_ref[...])
    m = _mod_row(mod_ref, i * TM)
    h = _rms(x, nm_ref[0:1, :]) * (1.0 + m[:, D:2 * D]) + m[:, 0:D]
    hb = h.astype(BF16)

    za = _dot_nt(hb, wt_ref[0:512, :])
    ckv = _rms(za[:, Q_LORA:Q_LORA + KV_LORA], kvan_ref[...])
    lane = lax.broadcasted_iota(jnp.int32, (1, HEAD_PAD), 1)
    kpe = jnp.where(lane < QK_ROPE, za[:, Q_LORA + KV_LORA:], 0.0)
    ckv_ref[...] = ckv
    kpe_ref[...] = kpe

    @pl.when(i < CTX_TILES)
    def _():
        sckv_ref[...] = ckv
        skpe_ref[...] = kpe[:, :QK_ROPE]

    qa = _rms(za[:, :Q_LORA], qan_ref[...]).astype(BF16)
    cos, s1, s2 = cos_ref[...], s1_ref[...], s2_ref[...]
    qn = qn_ref[...]
    scale = QK_HEAD ** -0.5
    for hp in range(HEADS // 2):
        qq = _dot(qa, wqb_ref[:, hp * 256:(hp + 1) * 256])
        for j in range(2):
            blk = _head_norm(qq[:, j * HEAD_PAD:(j + 1) * HEAD_PAD], qn)
            blk = _rope(blk, cos, s1, s2) * scale
            h0 = (2 * hp + j) * HEAD_PAD
            q_ref[:, h0:h0 + HEAD_PAD] = blk.astype(BF16)

    gb = _dot_nt(hb, wt_ref[n_a:n_a + SC_W, :])
    v = _dot_nt(hb, wt_ref[n_a + SC_W:n_a + 2 * SC_W, :]) * _dot_nt(hb, wt_ref[n_a + 2 * SC_W:n_a + 3 * SC_W, :])
    seq = jnp.where(i < CTX_TILES, CTX_LEN, LAT_LEN)
    r = lax.broadcasted_iota(jnp.int32, (TM, 1), 0) & (seq - 1)
    vp = jnp.where(r == 0, 0.0, pltpu.roll(v, 1, 0))
    vn = jnp.where(r == seq - 1, 0.0, pltpu.roll(v, TM - 1, 0))
    w = wsc_ref[0]
    y = w[0:1] * vp + w[1:2] * v + w[2:3] * vn
    sc_ref[...] = (gb * y).astype(BF16)


def _even_proj(xp, xs, mods, norm_mix, w_in, q_a_norm, wqb, kv_a_norm, qn, w_sc, tabs):
    full = lambda shape: pl.BlockSpec(shape, lambda i: (0,) * len(shape))
    tab = pl.BlockSpec((TM, HEAD_PAD), lambda i: (jnp.where(i < CTX_TILES, 1, 0), 0))
    row = lambda n: pl.BlockSpec((TM, n), lambda i: (i, 0))
    ctx_row = lambda n: pl.BlockSpec((TM, n), lambda i: (jnp.minimum(i, CTX_TILES - 1), 0))
    return pl.pallas_call(
        _even_proj_kernel,
        grid=(N_TILES,),
        in_specs=[
            ctx_row(D),
            pl.BlockSpec((TM, D), lambda i: (jnp.maximum(i - CTX_TILES, 0), 0)),
            _mod_spec(0),
            full((2, D)),
            pl.BlockSpec((IN0_W, D), lambda i: (0, 0), pipeline_mode=pl.Buffered(1)),
            full((1, Q_LORA)),
            full((Q_LORA, HEADS * HEAD_PAD)), full((1, KV_LORA)), full((1, HEAD_PAD)),
            full((1, 3, SC_W)), tab, tab, tab,
        ],
        out_specs=[row(HEADS * HEAD_PAD), row(KV_LORA), row(HEAD_PAD), row(SC_W),
                   ctx_row(KV_LORA), ctx_row(QK_ROPE)],
        out_shape=[
            jax.ShapeDtypeStruct((T, HEADS * HEAD_PAD), BF16),
            jax.ShapeDtypeStruct((T, KV_LORA), F32),
            jax.ShapeDtypeStruct((T, HEAD_PAD), F32),
            jax.ShapeDtypeStruct((T, SC_W), BF16),
            jax.ShapeDtypeStruct((T_CTX, KV_LORA), F32),
            jax.ShapeDtypeStruct((T_CTX, QK_ROPE), F32),
        ],
        scratch_shapes=[pltpu.VMEM((IN0_W, D), BF16)],
        compiler_params=_params("arbitrary"),
        name="even_proj",
    )(xp, xs, mods, norm_mix, w_in, q_a_norm, wqb, kv_a_norm, qn, w_sc, *tabs)


def _kv_proj_kernel(ckv_ref, kpe_ref, wkvb_ref, kn_ref, cos_ref, s1_ref, s2_ref, k_ref, kv_ref):
    kv = _dot(ckv_ref[...].astype(BF16), wkvb_ref[...])
    kv_ref[...] = kv.astype(BF16)
    kpe = pltpu.roll(kpe_ref[...], QK_NOPE, 1)
    lane = lax.broadcasted_iota(jnp.int32, (1, HEAD_PAD), 1)
    kn = kn_ref[...]
    pe_sq = jnp.sum(kpe * kpe, axis=-1, keepdims=True)
    pe = _rope(kpe * kn, cos_ref[...], s1_ref[...], s2_ref[...])
    for h in range(HEADS):
        blk = kv[:, h * HEAD_PAD:(h + 1) * HEAD_PAD]
        nope = jnp.where(lane < QK_NOPE, blk, 0.0)
        ms = (jnp.sum(nope * nope, axis=-1, keepdims=True) + pe_sq) * (1.0 / QK_HEAD)
        k = jnp.where(lane < QK_NOPE, blk * kn, pe) * lax.rsqrt(ms + EPS)
        k_ref[:, h * HEAD_PAD:(h + 1) * HEAD_PAD] = k.astype(BF16)


def _kv_proj(ckv, kpe, wkvb, kn, tabs, tab_index, name, TKV):
    n = ckv.shape[0]
    full = lambda shape: pl.BlockSpec(shape, lambda i: (0,) * len(shape))
    tab = pl.BlockSpec((TKV, HEAD_PAD), lambda i: (tab_index(i), 0))
    row = lambda w: pl.BlockSpec((TKV, w), lambda i: (i, 0))
    return pl.pallas_call(
        _kv_proj_kernel,
        grid=(n // TKV,),
        in_specs=[row(KV_LORA), row(HEAD_PAD), full((KV_LORA, HEADS * HEAD_PAD)), full((1, HEAD_PAD)),
                  tab, tab, tab],
        out_specs=[row(HEADS * HEAD_PAD), row(HEADS * HEAD_PAD)],
        out_shape=[jax.ShapeDtypeStruct((n, HEADS * HEAD_PAD), BF16)] * 2,
        compiler_params=_params("arbitrary"),
        name=name,
    )(ckv, kpe, wkvb, kn, *tabs)


def _pair_out(o0, o1):
    lane = lax.broadcasted_iota(jnp.int32, (1, HEAD_PAD), 1)
    return jnp.where(lane < V_HEAD, pltpu.roll(o0, V_HEAD, 1), o1).astype(BF16)


CTX_SEQS = 4


def _attn_ctx_kernel(q_ref, k_ref, kv_ref, o_ref):
    for b in range(CTX_SEQS):
        rows = slice(b * CTX_LEN, (b + 1) * CTX_LEN)
        for hp in range(HEADS // 2):
            outs = []
            for j in range(2):
                lanes = slice((2 * hp + j) * HEAD_PAD, (2 * hp + j + 1) * HEAD_PAD)
                s = _dot_nt(q_ref[rows, lanes], k_ref[rows, lanes])
                p = jnp.exp(s - jnp.max(s, axis=-1, keepdims=True))
                l = jnp.sum(p, axis=-1, keepdims=True)
                outs.append(_dot(p.astype(BF16), kv_ref[rows, lanes]) / l)
            o_ref[rows, hp * HEAD_PAD:(hp + 1) * HEAD_PAD] = _pair_out(*outs)


def _attn_ctx(q, k, kv):
    blk = pl.BlockSpec((CTX_SEQS * CTX_LEN, HEADS * HEAD_PAD), lambda b: (b, 0))
    return pl.pallas_call(
        _attn_ctx_kernel,
        grid=(N_CTX_SEQ // CTX_SEQS,),
        in_specs=[blk, blk, blk],
        out_specs=pl.BlockSpec((CTX_SEQS * CTX_LEN, HEADS * V_HEAD), lambda b: (b, 0)),
        out_shape=jax.ShapeDtypeStruct((T_CTX, HEADS * V_HEAD), BF16),
        compiler_params=_params("arbitrary"),
        name="attn_ctx",
    )(q, k, kv)


LAT_HEADS = 8


def _attn_lat_kernel(q_ref, kc_ref, kvc_ref, kl_ref, kvl_ref, o_ref):
    for hp in range(LAT_HEADS // 2):
        outs = []
        for j in range(2):
            h0 = (2 * hp + j) * HEAD_PAD
            lanes = slice(h0, h0 + HEAD_PAD)
            q = q_ref[:, lanes]
            k = jnp.concatenate([kc_ref[:, lanes], kl_ref[:, lanes]], axis=0)
            kv = jnp.concatenate([kvc_ref[:, lanes], kvl_ref[:, lanes]], axis=0)
            s = _dot_nt(q, k)
            p = jnp.exp(s - jnp.max(s, axis=-1, keepdims=True))
            l = jnp.sum(p, axis=-1, keepdims=True)
            outs.append(_dot(p.astype(BF16), kv) / l)
        o_ref[:, hp * HEAD_PAD:(hp + 1) * HEAD_PAD] = _pair_out(*outs)


def _attn_lat(q, kc, kvc, k, kv):
    nq = LAT_LEN // TQ
    q0 = T_CTX // TQ
    kl0 = T_CTX // LAT_LEN
    width = LAT_HEADS * HEAD_PAD
    lat = pl.BlockSpec((LAT_LEN, width), lambda b, hg, t: (kl0 + b, hg))
    ctx = pl.BlockSpec((PAST, width), lambda b, hg, t: (b, hg))
    return pl.pallas_call(
        _attn_lat_kernel,
        grid=(N_LAT_SEQ, HEADS // LAT_HEADS, nq),
        in_specs=[pl.BlockSpec((TQ, width), lambda b, hg, t: (q0 + b * nq + t, hg)), ctx, ctx, lat, lat],
        out_specs=pl.BlockSpec((TQ, LAT_HEADS * V_HEAD), lambda b, hg, t: (b * nq + t, hg)),
        out_shape=jax.ShapeDtypeStruct((T_LAT, HEADS * V_HEAD), BF16),
        compiler_params=_params("arbitrary", "arbitrary", "arbitrary"),
        name="attn_lat",
    )(q, kc, kvc, k, kv)


FFN_NC = D_FF // FFN_FC
FFN_TM = 512
FFN_CTX_TILES = T_CTX // FFN_TM


def _ffn_kernel(oc_ref, ol_ref, sc_ref, xp_ref, xs_ref, wo_ref, nf_ref, wg_ref, wu_ref, wd_ref,
                mod0_ref, mod1_ref, nm_ref, x2_ref, h3_ref, wg_all, wu_all, wd_all, x1_ref, hs_ref, act_ref):
    t = pl.program_id(0)

    i = jnp.maximum(t - (FFN_NC - 1), 0)
    m0, m1 = _mod_row(mod0_ref, i * FFN_TM), _mod_row(mod1_ref, i * FFN_TM)

    def mixer():
        ctx = i < FFN_CTX_TILES
        attn = jnp.where(ctx, oc_ref[...], ol_ref[...])
        x = jnp.where(ctx, xp_ref[...], xs_ref[...])
        x1 = x + m0[:, 2 * D:3 * D] * _dot(jnp.concatenate([attn, sc_ref[...]], axis=1), wo_ref[...])
        h = (_rms(x1, nf_ref[0:1, :]) * (1.0 + m0[:, 4 * D:5 * D]) + m0[:, 3 * D:4 * D]).astype(BF16)
        return x1, h

    def up(h, c):
        return (_silu(_dot(h, wg_all[c])) * _dot(h, wu_all[c])).astype(BF16)

    def down(x1, act):
        x2 = x1 + m0[:, 5 * D:6 * D] * _dot(act, wd_all[...])
        x2_ref[...] = x2
        h3_ref[...] = (_rms(x2, nm_ref[1:2, :]) * (1.0 + m1[:, D:2 * D]) + m1[:, 0:D]).astype(BF16)

    @pl.when(t == 0)
    def _():
        x1, h = mixer()
        x1_ref[...] = x1
        hs_ref[...] = h

    @pl.when(t < FFN_NC)
    def _stage():
        wg_all[t] = wg_ref[...].astype(BF16)
        wu_all[t] = wu_ref[...].astype(BF16)
        wd_all[pl.ds(pl.multiple_of(t * FFN_FC, FFN_FC), FFN_FC), :] = wd_ref[...].astype(BF16)
        act_ref[t] = up(hs_ref[...], t)

    @pl.when(t == FFN_NC - 1)
    def _():
        down(x1_ref[...], jnp.concatenate([act_ref[c] for c in range(FFN_NC)], axis=1))

    @pl.when(t > FFN_NC - 1)
    def _tile():
        x1, h = mixer()
        down(x1, jnp.concatenate([up(h, c) for c in range(FFN_NC)], axis=1))


def _ffn(oc, ol, sc, xp, xs, wo, norm_ffn, w_gu, w_down, mods, norm_mix):
    chunk = lambda t: jnp.minimum(t, FFN_NC - 1)
    tile = lambda t: jnp.maximum(t - (FFN_NC - 1), 0)
    full = lambda shape: pl.BlockSpec(shape, lambda t: (0,) * len(shape))
    row = lambda n: pl.BlockSpec((FFN_TM, n), lambda t: (tile(t), 0))
    first = lambda n: pl.BlockSpec((FFN_TM, n), lambda t: (jnp.minimum(tile(t), FFN_CTX_TILES - 1), 0))
    second = lambda n: pl.BlockSpec((FFN_TM, n), lambda t: (jnp.maximum(tile(t) - FFN_CTX_TILES, 0), 0))
    return pl.pallas_call(
        _ffn_kernel,
        grid=(FFN_NC - 1 + T // FFN_TM,),
        in_specs=[first(HEADS * V_HEAD), second(HEADS * V_HEAD), row(SC_W), first(D), second(D),
                  full((HEADS * V_HEAD + SC_W, D)), full((2, D)),
                  pl.BlockSpec((D, FFN_FC), lambda t: (0, chunk(t))),
                  pl.BlockSpec((D, FFN_FC), lambda t: (0, FFN_NC + chunk(t))),
                  pl.BlockSpec((FFN_FC, D), lambda t: (chunk(t), 0)),
                  _mod_spec(0), _mod_spec(1), full((2, D))],
        out_specs=[row(D), row(D)],
        out_shape=[jax.ShapeDtypeStruct((T, D), F32), jax.ShapeDtypeStruct((T, D), BF16)],
        scratch_shapes=[pltpu.VMEM((FFN_NC, D, FFN_FC), BF16), pltpu.VMEM((FFN_NC, D, FFN_FC), BF16),
                        pltpu.VMEM((D_FF, D), BF16), pltpu.VMEM((FFN_TM, D), F32), pltpu.VMEM((FFN_TM, D), BF16),
                        pltpu.VMEM((FFN_NC, FFN_TM, FFN_FC), BF16)],
        compiler_params=_params("arbitrary"),
        name="ffn_dense",
    )(oc, ol, sc, xp, xs, wo, norm_ffn, w_gu, w_gu, w_down, mods, mods, norm_mix)


CONF_CB = 256
CONF_SEG = 256
CONF_HALO = 16
CONF_SEGP = CONF_SEG + 2 * CONF_HALO
CONF_PIECE = 256


def _conf_kernel(h_ref, x2_ref, w1_ref, b1_ref, wdw_ref, bdw_ref, lng_ref, lnb_ref, w2_ref, b2_ref,
                 mod_ref, nf_ref, rt_ref, x3_ref, h4_ref, lg_ref, pad_ref, conv_ref):
    i = pl.program_id(0)
    nseg = TM // CONF_SEG
    h = h_ref[...]
    joined = jnp.where(i < CTX_TILES, 0.0, 1.0)
    zeros_halo = jnp.zeros((CONF_HALO, CONF_CB), F32)
    for cb in range(D // CONF_CB):
        c0 = cb * CONF_CB
        a = _dot(h, w1_ref[:, c0:c0 + CONF_CB]) + b1_ref[:, c0:c0 + CONF_CB]
        g = _dot(h, w1_ref[:, D + c0:D + c0 + CONF_CB]) + b1_ref[:, D + c0:D + c0 + CONF_CB]
        u = a * jax.nn.sigmoid(g)
        for s in range(nseg):
            base = s * CONF_SEGP
            top = u[s * CONF_SEG - CONF_HALO:s * CONF_SEG] * joined if s > 0 else zeros_halo
            bot = (u[(s + 1) * CONF_SEG:(s + 1) * CONF_SEG + CONF_HALO] * joined
                   if s < nseg - 1 else zeros_halo)
            pad_ref[0, base:base + CONF_HALO, :] = top
            pad_ref[0, base + CONF_HALO:base + CONF_HALO + CONF_SEG, :] = u[s * CONF_SEG:(s + 1) * CONF_SEG]
            pad_ref[0, base + CONF_HALO + CONF_SEG:base + CONF_SEGP, :] = bot

        p0 = pad_ref[0]
        rows = nseg * CONF_SEGP
        for b in range(1, 8):
            pad_ref[b] = pltpu.roll(p0, rows - b, 0)

        def piece(t, carry):
            s = t // (CONF_SEG // CONF_PIECE)
            q0 = (t % (CONF_SEG // CONF_PIECE)) * CONF_PIECE
            src = pl.multiple_of(s * CONF_SEGP + q0, 8)
            acc = jnp.zeros((CONF_PIECE, CONF_CB), F32)
            for j in range(CONF_K):
                hi, lo = (j + 1) // 8, (j + 1) % 8
                acc = acc + wdw_ref[0, j:j + 1, c0:c0 + CONF_CB] * pad_ref[lo, pl.ds(src + 8 * hi, CONF_PIECE), :]
            dst = pl.multiple_of(s * CONF_SEG + q0, 8)
            conv_ref[pl.ds(dst, CONF_PIECE), c0:c0 + CONF_CB] = acc + bdw_ref[:, c0:c0 + CONF_CB]
            return carry

        lax.fori_loop(0, TM // CONF_PIECE, piece, 0)

    m = _mod_row(mod_ref, i * TM)
    half = TM // 2
    for r0 in (0, half):
        rows = slice(r0, r0 + half)
        y = conv_ref[rows, :]
        mu = jnp.mean(y, axis=-1, keepdims=True)
        yc = y - mu
        var = jnp.mean(yc * yc, axis=-1, keepdims=True)
        y = _silu(yc * lax.rsqrt(var + EPS) * lng_ref[...] + lnb_ref[...])
        out = _dot(y.astype(BF16), w2_ref[...]) + b2_ref[...]
        x3 = x2_ref[rows, :] + m[:, 2 * D:3 * D] * out
        x3_ref[rows, :] = x3
        h4 = _rms(x3, nf_ref[1:2, :]) * (1.0 + m[:, 4 * D:5 * D]) + m[:, 3 * D:4 * D]
        h4_ref[rows, :] = h4.astype(BF16)
        lg_ref[:, rows] = lax.dot_general(rt_ref[...], h4, (((1,), (1,)), ((), ())),
                                          precision=lax.Precision.HIGHEST, preferred_element_type=F32)


def _conf(h3, x2, w1, b1, wdw, bdw, lng, lnb, w2, b2, mods, norm_ffn1, router_t):
    full = lambda shape: pl.BlockSpec(shape, lambda i: (0,) * len(shape))
    row = lambda n: pl.BlockSpec((TM, n), lambda i: (i, 0))
    return pl.pallas_call(
        _conf_kernel,
        grid=(N_TILES,),
        in_specs=[row(D), row(D), full((D, 2 * D)), full((1, 2 * D)), full((1, CONF_K, D)), full((1, D)),
                  full((1, D)), full((1, D)), full((D, D)), full((1, D)),
                  _mod_spec(1), full((2, D)), full((N_EXP, D))],
        out_specs=[row(D), row(D), pl.BlockSpec((N_EXP, TM), lambda i: (0, i))],
        out_shape=[jax.ShapeDtypeStruct((T, D), F32), jax.ShapeDtypeStruct((T, D), BF16),
                   jax.ShapeDtypeStruct((N_EXP, T), F32)],
        scratch_shapes=[pltpu.VMEM((8, (TM // CONF_SEG) * CONF_SEGP, CONF_CB), F32),
                        pltpu.VMEM((TM, D), F32)],
        compiler_params=_params("arbitrary"),
        name="conformer_conv",
    )(h3, x2, w1, b1, wdw, bdw, lng, lnb, w2, b2, mods, norm_ffn1, router_t)


TB = 256
N_TB = T // TB
SUB = 128
SUBS = 24
SM = SUBS * SUB
N_SUB_MAX = 2 * T // SUB + N_EXP
N_SUP_MAX = (N_SUB_MAX + N_EXP * (SUBS - 1)) // SUBS
YS_ROWS = (N_SUB_MAX + 4) * SUB
WIN_ALIGN = 16
WIN_A = TB // 2
WIN_B = TB // 2 + WIN_ALIGN
FIRST_STRIDE = 32
UNIT_STRIDE = 64
GATHER_BLOCKS = 4


def _route_kernel(lg_ref, g_ref, rank_ref, first_ref):
    lg = lg_ref[...]
    idx = lax.broadcasted_iota(jnp.int32, lg.shape, 0).astype(F32)
    none = float(N_EXP)
    m1 = jnp.max(lg, axis=0, keepdims=True)
    i1 = jnp.min(jnp.where(lg == m1, idx, none), axis=0, keepdims=True)
    rest = jnp.where(idx == i1, -jnp.inf, lg)
    m2 = jnp.max(rest, axis=0, keepdims=True)
    i2 = jnp.min(jnp.where(rest == m2, idx, none), axis=0, keepdims=True)
    e = jnp.exp(m2 - m1)
    w1 = 1.0 / (1.0 + e)
    w2 = e / (1.0 + e)
    g_ref[...] = jnp.where(idx == i1, w1, 0.0) + jnp.where(idx == i2, w2, 0.0)

    mask = jnp.where(idx == i1, 1.0, 0.0) + jnp.where(idx == i2, 1.0, 0.0)
    before = (lax.broadcasted_iota(jnp.int32, (TB, TB), 0) < lax.broadcasted_iota(jnp.int32, (TB, TB), 1))
    before = jnp.where(before, 1.0, 0.0).astype(BF16)
    lane = lax.broadcasted_iota(jnp.int32, (N_EXP, 128), 1)
    carry = jnp.zeros((N_EXP, 1), F32)
    first = jnp.zeros((N_EXP, 128), F32)
    for b in range(N_TB):
        mb = mask[:, b * TB:(b + 1) * TB]
        local = _dot(mb.astype(BF16), before)
        rank_ref[:, b * TB:(b + 1) * TB] = jnp.where(mb > 0.0, local + carry, -1.0)
        first = jnp.where(lane == b, carry, first)
        carry = carry + jnp.sum(mb, axis=1, keepdims=True)
    first_ref[...] = jnp.where(lane == N_TB, carry, first)


def _route(logits_t):
    return pl.pallas_call(
        _route_kernel,
        out_shape=[jax.ShapeDtypeStruct((N_EXP, T), F32), jax.ShapeDtypeStruct((N_EXP, T), F32),
                   jax.ShapeDtypeStruct((N_EXP, 128), F32)],
        compiler_params=pltpu.CompilerParams(vmem_limit_bytes=VMEM_LIMIT),
        name="route",
    )(logits_t)


def _moe_plan(first):
    first = first[:, :FIRST_STRIDE].astype(jnp.int32)
    cnt = first[:, N_TB]
    nt = (cnt + (SUB - 1)) // SUB
    off_end = jnp.cumsum(nt)
    off = off_end - nt
    nsub = off_end[-1]
    nsup = (nt + (SUBS - 1)) // SUBS
    sup_end = jnp.cumsum(nsup)
    sup_off = sup_end - nsup
    s = jnp.minimum(jnp.arange(N_SUP_MAX), sup_end[-1] - 1)
    valid = jnp.arange(N_SUP_MAX) < sup_end[-1]
    se = jnp.sum(s[:, None] >= sup_end[None, :], axis=1)
    mine = se[:, None] == jnp.arange(N_EXP)[None, :]
    pick = lambda v: jnp.sum(jnp.where(mine, v[None, :], 0), axis=1)
    sk0 = (s - pick(sup_off)) * SUBS
    sns = jnp.where(valid, jnp.clip(pick(nt) - sk0, 0, SUBS), 0)
    sj0 = pick(off) + sk0
    base = (jnp.arange(UNIT_STRIDE) * SUB)[None, :, None]
    blo = jnp.minimum(jnp.sum(first[:, None, 1:N_TB + 1] <= base, axis=2), N_TB - 1)
    end = jnp.minimum(base + SUB, cnt[:, None, None])
    bhi = jnp.maximum(jnp.sum(first[:, None, :N_TB] < end, axis=2) - 1, blo)
    ng = (bhi - blo) // GATHER_BLOCKS + 1
    start = SUB * off[:, None] + first[:, :N_TB]
    lead = start & (WIN_ALIGN - 1)
    wina = start - lead
    rel = first[:, :N_TB] - lead
    need = lead + (first[:, 1:N_TB + 1] - first[:, :N_TB]) > WIN_A
    winb = lax.cummax(jnp.where(need, wina + WIN_A, 0), axis=1)
    wide = jnp.any(need, axis=0)
    flat = lambda parts: jnp.concatenate([p.astype(jnp.int32).reshape(-1) for p in parts])
    gmm = dict(se=se, sk0=sk0, sns=sns, sj0=sj0, nsub=nsub, first=first, blo=blo, ng=ng)
    comb = dict(wina=wina.T, winb=winb.T, rel=rel.T, wide=wide)
    return flat([gmm[k] for k in _GMM_TAB]), flat([comb[k] for k in _COMB_TAB])


def _offsets(sizes):
    out, pos = {}, 0
    for name, n in sizes.items():
        out[name], pos = pos, pos + n
    return out


_GMM_TAB = _offsets(dict(se=N_SUP_MAX, sk0=N_SUP_MAX, sns=N_SUP_MAX, sj0=N_SUP_MAX, nsub=1,
                         first=N_EXP * FIRST_STRIDE, blo=N_EXP * UNIT_STRIDE, ng=N_EXP * UNIT_STRIDE))
_COMB_TAB = _offsets(dict(wina=N_TB * N_EXP, winb=N_TB * N_EXP, rel=N_TB * N_EXP, wide=N_TB))


class _Section:
    def __init__(self, ref, offset):
        self.ref, self.offset = ref, offset

    def __getitem__(self, i):
        return self.ref[self.offset + i]


def _moe_gmm_kernel(tab_ref, x_ref, rank_ref, gate_ref, wg_ref, wu_ref, wd_ref, ys_ref,
                    xs_ref, gs_ref, yacc_ref, wgb_ref, wub_ref, wdb_ref, sem):
    se_ref, sk0_ref, sns_ref, sj0_ref, nsub_ref, first_ref, blo_ref, ng_ref = (
        _Section(tab_ref, _GMM_TAB[k]) for k in ("se", "sk0", "sns", "sj0", "nsub", "first", "blo", "ng"))
    s, c = pl.program_id(0), pl.program_id(1)
    nc = pl.num_programs(1)
    e, k0, ns = se_ref[s], sk0_ref[s], sns_ref[s]

    def sub_rows(k):
        return pl.ds(pl.multiple_of(k * SUB, SUB), SUB)

    def out_copy(k, row0):
        dst = ys_ref.at[pl.ds(pl.multiple_of(row0 + k * SUB, SUB), SUB)]
        return pltpu.make_async_copy(xs_ref.at[sub_rows(k)], dst, sem.at[k])

    @pl.when((ns > 0) & (c == 0))
    def _gather():
        def group(k, g):
            slot = (lax.broadcasted_iota(jnp.int32, (SUB, 1), 0) + (k0 + k) * SUB).astype(F32)
            b0 = blo_ref[e * UNIT_STRIDE + k0 + k] + g * GATHER_BLOCKS
            t0 = pl.multiple_of(jnp.minimum(b0, N_TB - GATHER_BLOCKS) * TB, TB)
            lo = first_ref[e * FIRST_STRIDE + b0].astype(F32)
            cols = pl.ds(t0, GATHER_BLOCKS * TB)
            hit = rank_ref[pl.ds(e, 1), cols] == jnp.where(slot >= lo, slot, -2.0)
            rows = _dot(jnp.where(hit, 1.0, 0.0).astype(BF16), x_ref[cols, :])
            gate = jnp.sum(jnp.where(hit, gate_ref[pl.ds(e, 1), cols], 0.0), axis=-1, keepdims=True)
            return rows, gate

        def first(k):
            rows, gate = group(k, 0)
            xs_ref[sub_rows(k), :] = rows.astype(BF16)
            gs_ref[sub_rows(k), :] = gate
            yacc_ref[sub_rows(k), :] = jnp.zeros((SUB, D), F32)

        def more(k):
            def body(g, carry):
                rows, gate = group(k, g)
                xs_ref[sub_rows(k), :] = (xs_ref[sub_rows(k), :].astype(F32) + rows).astype(BF16)
                gs_ref[sub_rows(k), :] += gate
                return carry

            lax.fori_loop(1, ng_ref[e * UNIT_STRIDE + k0 + k], body, 0)

        def quad(p, carry):
            for j in range(4):
                first(4 * p + j)
            for j in range(4):
                more(4 * p + j)
            return carry

        def single(k, carry):
            first(k)
            more(k)
            return carry

        lax.fori_loop(0, ns >> 2, quad, 0)
        lax.fori_loop(ns & ~3, ns, single, 0)

    @pl.when(ns > 0)
    def _compute():
        row0 = sj0_ref[s] * SUB

        def swiglu(rows, wg, wu, wd):
            x = xs_ref[rows, :]
            g = _dot(x, wg)
            u = _dot(x, wu)
            yacc_ref[rows, :] += _dot((_silu(g) * u).astype(BF16), wd)

        def finish(first_sub, n):
            @pl.when(c == nc - 1)
            def _():
                for k in range(n):
                    rows = sub_rows(first_sub + k)
                    xs_ref[rows, :] = (yacc_ref[rows, :] * gs_ref[rows, :]).astype(BF16)
                    out_copy(first_sub + k, row0).start()

        def first_chain(rows):
            wg, wu, wd = wg_ref[0].astype(BF16), wu_ref[0].astype(BF16), wd_ref[0].astype(BF16)
            wgb_ref[...] = wg
            wub_ref[...] = wu
            wdb_ref[...] = wd
            swiglu(rows, wg, wu, wd)

        @pl.when(ns >= 4)
        def _():
            first_chain(pl.ds(0, 4 * SUB))
            finish(0, 4)

        @pl.when(ns < 4)
        def _():
            first_chain(pl.ds(0, SUB))
            finish(0, 1)

        done = jnp.where(ns >= 4, 4, 1)
        rest = ns - done

        def chain(first_sub, n):
            rows = pl.ds(pl.multiple_of(first_sub * SUB, SUB), n * SUB)
            swiglu(rows, wgb_ref[...], wub_ref[...], wdb_ref[...])
            finish(first_sub, n)

        def eight(k, carry):
            chain(done + 8 * k, 8)
            return carry

        lax.fori_loop(0, rest >> 3, eight, 0)
        done8 = done + (rest & ~7)
        for n in (4, 2, 1):
            @pl.when((rest & n) != 0)
            def _(n=n):
                chain(done8 + (rest & (7 & ~(2 * n - 1))), n)

    @pl.when((ns > 0) & (c == nc - 1))
    def _store_done():
        row0 = sj0_ref[s] * SUB

        def done(k, carry):
            out_copy(k, row0).wait()
            return carry

        lax.fori_loop(0, ns, done, 0)

    @pl.when((s == pl.num_programs(0) - 1) & (c == nc - 1))
    def _zero_tail():
        xs_ref[0:SUB, :] = jnp.zeros((SUB, D), BF16)
        nsub = nsub_ref[0]

        def fill(k, carry):
            cp = out_copy(0, (nsub + k) * SUB)
            cp.start()
            cp.wait()
            return carry

        lax.fori_loop(0, YS_ROWS // SUB - nsub, fill, 0)


def _moe_gmm(tab, h4, rank, gates, w_gu, w_down):
    nc = D_FFE // MOE_FC

    def expert(s, tab):
        return tab[_GMM_TAB["se"] + s]

    def chunk(s, c, tab):
        return jnp.where(tab[_GMM_TAB["sns"] + s] > 0, c, nc - 1)

    return pl.pallas_call(
        _moe_gmm_kernel,
        grid_spec=pltpu.PrefetchScalarGridSpec(
            num_scalar_prefetch=1,
            grid=(N_SUP_MAX, nc),
            in_specs=[
                pl.BlockSpec((T, D), lambda s, c, tab: (0, 0), pipeline_mode=pl.Buffered(1)),
                pl.BlockSpec((N_EXP, T), lambda s, c, tab: (0, 0)),
                pl.BlockSpec((N_EXP, T), lambda s, c, tab: (0, 0)),
                pl.BlockSpec((1, D, MOE_FC), lambda s, c, tab: (expert(s, tab), 0, chunk(s, c, tab))),
                pl.BlockSpec((1, D, MOE_FC), lambda s, c, tab: (expert(s, tab), 0, nc + chunk(s, c, tab))),
                pl.BlockSpec((1, MOE_FC, D), lambda s, c, tab: (expert(s, tab), chunk(s, c, tab), 0)),
            ],
            out_specs=pl.BlockSpec(memory_space=pl.ANY),
            scratch_shapes=[
                pltpu.VMEM((SM, D), BF16), pltpu.VMEM((SM, 1), F32), pltpu.VMEM((SM, D), F32),
                pltpu.VMEM((D, MOE_FC), BF16), pltpu.VMEM((D, MOE_FC), BF16), pltpu.VMEM((MOE_FC, D), BF16),
                pltpu.SemaphoreType.DMA((SUBS,)),
            ],
        ),
        out_shape=jax.ShapeDtypeStruct((YS_ROWS, D), BF16),
        compiler_params=_params("arbitrary", "arbitrary"),
        name="moe_gmm",
    )(tab, h4, rank, gates, w_gu, w_gu, w_down)


def _moe_combine_kernel(tab_ref, *refs):
    rel_ref, wide_ref = _Section(tab_ref, _COMB_TAB["rel"]), _Section(tab_ref, _COMB_TAB["wide"])
    ya, yb = refs[:N_EXP], refs[N_EXP:2 * N_EXP]
    rank_ref, x3_ref, mod_ref, yp_ref, ys_ref, ycat_ref, acc_ref = refs[2 * N_EXP:]
    b = pl.program_id(0)

    def onehot(e, first_row, n):
        row = lax.broadcasted_iota(jnp.int32, (n, 1), 0)
        slot = (row + (rel_ref[b * N_EXP + e] + first_row)).astype(F32)
        return jnp.where(rank_ref[e:e + 1, :] == slot, 1.0, 0.0).astype(BF16)

    def gather(y_refs, first_row, base, n):
        pieces = []
        for e in range(N_EXP):
            ycat_ref[base + e * n:base + (e + 1) * n, :] = y_refs[e][...]
            pieces.append(onehot(e, first_row, n))
        return pieces

    def combine(pieces, rows):
        return lax.dot_general(jnp.concatenate(pieces, axis=0), ycat_ref[0:rows, :], (((0,), (0,)), ((), ())),
                               preferred_element_type=F32)

    @pl.when(wide_ref[b] == 0)
    def _():
        acc_ref[...] = combine(gather(ya, 0, 0, WIN_A), N_EXP * WIN_A)

    @pl.when(wide_ref[b] != 0)
    def _():
        pieces = gather(ya, 0, 0, WIN_A) + gather(yb, WIN_A, N_EXP * WIN_A, WIN_B)
        acc_ref[...] = combine(pieces, N_EXP * (WIN_A + WIN_B))

    out = x3_ref[...] + _mod_row(mod_ref, b * TB)[:, 5 * D:6 * D] * acc_ref[...]

    @pl.when(b < T_CTX // TB)
    def _():
        yp_ref[...] = out

    @pl.when(b >= T_CTX // TB)
    def _():
        ys_ref[...] = out


def _moe_combine(tab, ysorted, rank, x3, mods):
    ctx_blocks = T_CTX // TB

    def window(e, second):
        def index(b, tab):
            start = tab[_COMB_TAB["winb" if second else "wina"] + b * N_EXP + e]
            return pl.multiple_of(start, WIN_ALIGN), 0
        return pl.BlockSpec((pl.Element(WIN_B if second else WIN_A), pl.Element(D)), index)

    return pl.pallas_call(
        _moe_combine_kernel,
        grid_spec=pltpu.PrefetchScalarGridSpec(
            num_scalar_prefetch=1,
            grid=(N_TB,),
            in_specs=[window(e, False) for e in range(N_EXP)] + [window(e, True) for e in range(N_EXP)] + [
                pl.BlockSpec((N_EXP, TB), lambda b, *_: (0, b)),
                pl.BlockSpec((TB, D), lambda b, *_: (b, 0)),
                _mod_spec(1),
            ],
            out_specs=[pl.BlockSpec((TB, D), lambda b, *_: (jnp.minimum(b, ctx_blocks - 1), 0)),
                       pl.BlockSpec((TB, D), lambda b, *_: (jnp.maximum(b - ctx_blocks, 0), 0))],
            scratch_shapes=[pltpu.VMEM((N_EXP * (WIN_A + WIN_B), D), BF16), pltpu.VMEM((TB, D), F32)],
        ),
        out_shape=[jax.ShapeDtypeStruct((T_CTX, D), F32), jax.ShapeDtypeStruct((T_LAT, D), F32)],
        compiler_params=_params("arbitrary"),
        name="moe_combine",
    )(tab, *([ysorted] * (2 * N_EXP)), rank, x3, mods)


def _pad_heads(w, width):
    lead = w.shape[:-1]
    w = w.reshape(*lead, HEADS, width)
    w = jnp.pad(w, [(0, 0)] * len(lead) + [(0, 0), (0, HEAD_PAD - width)])
    return w.reshape(*lead, HEADS * HEAD_PAD)


def kernel(x_prompt, x_sample, cache_ckv, cache_kpe, c, c_ctx, ada_w, ada_b, norm_mix, norm_ffn, w_in, q_a_norm,
           w_qb, kv_a_norm, w_kvb, q_norm, k_norm, w_sc, w_o, ffn_gu, ffn_down, conv_pw1, conv_pw1_b, conv_dw,
           conv_dw_b, conv_ln_g, conv_ln_b, conv_pw2, conv_pw2_b, router, moe_gu, moe_down):
    xp = x_prompt.reshape(T_CTX, D)
    xs = x_sample.reshape(T_LAT, D)

    mods = _adaln(c_ctx, c, ada_w, ada_b)

    wqb = _pad_heads(w_qb[0], QK_HEAD).astype(BF16)
    wkvb = w_kvb[0].astype(BF16)
    qn = jnp.pad(q_norm[0], (0, HEAD_PAD - QK_HEAD)).reshape(1, HEAD_PAD)
    kn = jnp.pad(k_norm[0], (0, HEAD_PAD - QK_HEAD)).reshape(1, HEAD_PAD)
    tabs = _rope_tables()

    w_in_t = jnp.swapaxes(w_in[0], 0, 1)
    q, ckv, kpe, sc, state_ckv, state_kpe = _even_proj(xp, xs, mods, norm_mix, w_in_t, q_a_norm, wqb, kv_a_norm,
                                                       qn, w_sc, tabs)

    lat_tile0 = T_CTX // TKV
    ident = LAT_LEN // TKV
    k, kv = _kv_proj(ckv, kpe, wkvb, kn, tabs,
                     lambda i: jnp.where(i < lat_tile0, ident, (i - lat_tile0) % ident), "kv_proj", TKV)
    cache_kpe_p = jnp.pad(cache_kpe[:, 0].reshape(N_LAT_SEQ * PAST, QK_ROPE), ((0, 0), (0, HEAD_PAD - QK_ROPE)))
    kc, kvc = _kv_proj(cache_ckv[:, 0].reshape(N_LAT_SEQ * PAST, KV_LORA), cache_kpe_p, wkvb, kn, tabs,
                       lambda i: LAT_LEN // (N_LAT_SEQ * PAST), "kv_proj_cache", N_LAT_SEQ * PAST)

    oc = _attn_ctx(q, k, kv)
    ol = _attn_lat(q, kc, kvc, k, kv)
    x2, h3 = _ffn(oc, ol, sc, xp, xs, w_o[0].astype(BF16), norm_ffn, ffn_gu[0], ffn_down[0], mods, norm_mix)

    x3, h4, logits_t = _conf(h3, x2, conv_pw1[0].astype(BF16), conv_pw1_b, conv_dw, conv_dw_b, conv_ln_g,
                             conv_ln_b, conv_pw2[0].astype(BF16), conv_pw2_b, mods, norm_ffn, router[0].T)
    gates, rank, first = _route(logits_t)
    gmm_tab, comb_tab = _moe_plan(first)
    ysorted = _moe_gmm(gmm_tab, h4, rank, gates, moe_gu[0], moe_down[0])
    yp, ys = _moe_combine(comb_tab, ysorted, rank, x3, mods)

    return (yp.reshape(N_CTX_SEQ, CTX_LEN, D), ys.reshape(N_LAT_SEQ, LAT_LEN, D),
            state_ckv.reshape(N_CTX_SEQ, 1, CTX_LEN, KV_LORA), state_kpe.reshape(N_CTX_SEQ, 1, CTX_LEN, QK_ROPE))
```

```python
import jax
import jax.numpy as jnp
import numpy as np
from jax import lax
from jax.experimental import pallas as pl
from jax.experimental.pallas import tpu as pltpu

F32 = jnp.float32
BF16 = jnp.bfloat16

D = 1024
N_CTX_SEQ, CTX_LEN = 16, 256
N_LAT_SEQ, LAT_LEN = 2, 1024
T_CTX = N_CTX_SEQ * CTX_LEN
T_LAT = N_LAT_SEQ * LAT_LEN
T = T_CTX + T_LAT
PAST = 256
GRID_W = 64
HEADS = 8
QK_NOPE, QK_ROPE, V_HEAD = 64, 32, 64
QK_HEAD = QK_NOPE + QK_ROPE
HEAD_PAD = 128
Q_LORA, KV_LORA = 256, 128
SC_W = 512
IN0_W = Q_LORA + KV_LORA + QK_ROPE + 3 * SC_W
CONF_K = 31
D_FF = 2816
N_EXP = 8
D_FFE = 3584
EPS = 1e-6
ROPE_THETA = 10000.0

TM = 1024
N_TILES = T // TM
CTX_TILES = T_CTX // TM
TKV = 1024
TQ = 512
FFN_FC = 256
MOE_FC = 512
VMEM_LIMIT = 56 * 1024 * 1024


def _dot(a, b):
    return jnp.dot(a, b, preferred_element_type=F32)


def _dot_nt(a, b):
    return lax.dot_general(a, b, (((1,), (1,)), ((), ())), preferred_element_type=F32)


def _rms(x, g):
    return x * lax.rsqrt(jnp.mean(x * x, axis=-1, keepdims=True) + EPS) * g


def _silu(x):
    return x * jax.nn.sigmoid(x)


def _params(*sem):
    return pltpu.CompilerParams(dimension_semantics=sem, vmem_limit_bytes=VMEM_LIMIT)


def _mod_row(mod_ref, row0):
    cond = jnp.maximum(row0 - (T_CTX - LAT_LEN), 0) >> 10
    return mod_ref[0, pl.ds(cond, 1), :]


def _mod_spec(layer):
    return pl.BlockSpec((1, 8, 6 * D), lambda *_: (layer, 0, 0))


def _adaln_kernel(cc_ref, c_ref, w_ref, b_ref, o_ref):
    l = pl.program_id(0)
    row = lax.broadcasted_iota(jnp.int32, (8, 1), 0)
    cond = jnp.where(row == 0, cc_ref[...], 0.0)
    for b in range(N_LAT_SEQ):
        cond = jnp.where(row == 1 + b, c_ref[b:b + 1, :], cond)
    o_ref[0] = _dot(_silu(cond).astype(BF16), w_ref[0].astype(BF16)) + b_ref[pl.ds(l, 1), :]


def _adaln(c_ctx, c, ada_w, ada_b):
    depth = ada_w.shape[0]
    tn = 2048
    return pl.pallas_call(
        _adaln_kernel,
        grid=(depth, 6 * D // tn),
        in_specs=[
            pl.BlockSpec((1, D), lambda l, j: (0, 0)),
            pl.BlockSpec((N_LAT_SEQ, D), lambda l, j: (0, 0)),
            pl.BlockSpec((1, D, tn), lambda l, j: (l, 0, j)),
            pl.BlockSpec((depth, tn), lambda l, j: (0, j)),
        ],
        out_specs=pl.BlockSpec((1, 8, tn), lambda l, j: (l, 0, j)),
        out_shape=jax.ShapeDtypeStruct((depth, 8, 6 * D), F32),
        compiler_params=_params("arbitrary", "arbitrary"),
        name="adaln",
    )(c_ctx.reshape(1, D), c, ada_w, ada_b)


def _rope_tables():
    half = QK_ROPE // 2
    nf = half // 2
    pos = np.arange(LAT_LEN)
    inv = ROPE_THETA ** (-np.arange(nf, dtype=np.float64) / nf)
    k = np.arange(QK_ROPE)
    part, idx = k // half, k % half
    p = np.where(part[None, :] == 0, (pos // GRID_W)[:, None], (pos % GRID_W)[:, None])
    ang = p * inv[idx % nf][None, :]
    cos, sin = np.cos(ang), np.sin(ang)
    first = (idx < nf)[None, :]
    s1 = np.where(first, -sin, 0.0)
    s2 = np.where(first, 0.0, sin)

    def place(t, fill):
        tab = np.full((2 * LAT_LEN, HEAD_PAD), fill, np.float32)
        tab[:LAT_LEN, QK_NOPE:QK_HEAD] = t
        return jnp.asarray(tab)

    return place(cos, 1.0), place(s1, 0.0), place(s2, 0.0)


def _rope(blk, cos, s1, s2):
    return blk * cos + pltpu.roll(blk, 8, 1) * s2 + pltpu.roll(blk, HEAD_PAD - 8, 1) * s1


def _head_norm(blk, g):
    ms = jnp.sum(blk * blk, axis=-1, keepdims=True) * (1.0 / QK_HEAD)
    return blk * lax.rsqrt(ms + EPS) * g


def _even_proj_kernel(xp_ref, xs_ref, mod_ref, nm_ref, win_ref, qan_ref, wqb_ref, kvan_ref,
                      qn_ref, wsc_ref, cos_ref, s1_ref, s2_ref,
                      q_ref, ckv_ref, kpe_ref, sc_ref, sckv_ref, skpe_ref, wt_ref):
    i = pl.program_id(0)
    n_a = Q_LORA + KV_LORA + QK_ROPE

    @pl.when(i == 0)
    def _():
        wt_ref[...] = win_ref[...].astype(BF16)

    x = jnp.where(i < CTX_TILES, xp_ref[...], xs_ref[...])
    m = _mod_row(mod_ref, i * TM)
    h = _rms(x, nm_ref[0:1, :]) * (1.0 + m[:, D:2 * D]) + m[:, 0:D]
    hb = h.astype(BF16)

    za = _dot_nt(hb, wt_ref[0:512, :])
    ckv = _rms(za[:, Q_LORA:Q_LORA + KV_LORA], kvan_ref[...])
    lane = lax.broadcasted_iota(jnp.int32, (1, HEAD_PAD), 1)
    kpe = jnp.where(lane < QK_ROPE, za[:, Q_LORA + KV_LORA:], 0.0)
    ckv_ref[...] = ckv
    kpe_ref[...] = kpe

    @pl.when(i < CTX_TILES)
    def _():
        sckv_ref[...] = ckv
        skpe_ref[...] = kpe[:, :QK_ROPE]

    qa = _rms(za[:, :Q_LORA], qan_ref[...]).astype(BF16)
    cos, s1, s2 = cos_ref[...], s1_ref[...], s2_ref[...]
    qn = qn_ref[...]
    scale = QK_HEAD ** -0.5
    for hp in range(HEADS // 2):
        qq = _dot(qa, wqb_ref[:, hp * 256:(hp + 1) * 256])
        for j in range(2):
            blk = _head_norm(qq[:, j * HEAD_PAD:(j + 1) * HEAD_PAD], qn)
            blk = _rope(blk, cos, s1, s2) * scale
            h0 = (2 * hp + j) * HEAD_PAD
            q_ref[:, h0:h0 + HEAD_PAD] = blk.astype(BF16)

    gb = _dot_nt(hb, wt_ref[n_a:n_a + SC_W, :])
    v = _dot_nt(hb, wt_ref[n_a + SC_W:n_a + 2 * SC_W, :]) * _dot_nt(hb, wt_ref[n_a + 2 * SC_W:n_a + 3 * SC_W, :])
    seq = jnp.where(i < CTX_TILES, CTX_LEN, LAT_LEN)
    r = lax.broadcasted_iota(jnp.int32, (TM, 1), 0) & (seq - 1)
    vp = jnp.where(r == 0, 0.0, pltpu.roll(v, 1, 0))
    vn = jnp.where(r == seq - 1, 0.0, pltpu.roll(v, TM - 1, 0))
    w = wsc_ref[0]
    y = w[0:1] * vp + w[1:2] * v + w[2:3] * vn
    sc_ref[...] = (gb * y).astype(BF16)


def _even_proj(xp, xs, mods, norm_mix, w_in, q_a_norm, wqb, kv_a_norm, qn, w_sc, tabs):
    full = lambda shape: pl.BlockSpec(shape, lambda i: (0,) * len(shape))
    tab = pl.BlockSpec((TM, HEAD_PAD), lambda i: (jnp.where(i < CTX_TILES, 1, 0), 0))
    row = lambda n: pl.BlockSpec((TM, n), lambda i: (i, 0))
    ctx_row = lambda n: pl.BlockSpec((TM, n), lambda i: (jnp.minimum(i, CTX_TILES - 1), 0))
    return pl.pallas_call(
        _even_proj_kernel,
        grid=(N_TILES,),
        in_specs=[
            ctx_row(D),
            pl.BlockSpec((TM, D), lambda i: (jnp.maximum(i - CTX_TILES, 0), 0)),
            _mod_spec(0),
            full((2, D)),
            pl.BlockSpec((IN0_W, D), lambda i: (0, 0), pipeline_mode=pl.Buffered(1)),
            full((1, Q_LORA)),
            full((Q_LORA, HEADS * HEAD_PAD)), full((1, KV_LORA)), full((1, HEAD_PAD)),
            full((1, 3, SC_W)), tab, tab, tab,
        ],
        out_specs=[row(HEADS * HEAD_PAD), row(KV_LORA), row(HEAD_PAD), row(SC_W),
                   ctx_row(KV_LORA), ctx_row(QK_ROPE)],
        out_shape=[
            jax.ShapeDtypeStruct((T, HEADS * HEAD_PAD), BF16),
            jax.ShapeDtypeStruct((T, KV_LORA), F32),
            jax.ShapeDtypeStruct((T, HEAD_PAD), F32),
            jax.ShapeDtypeStruct((T, SC_W), BF16),
            jax.ShapeDtypeStruct((T_CTX, KV_LORA), F32),
            jax.ShapeDtypeStruct((T_CTX, QK_ROPE), F32),
        ],
        scratch_shapes=[pltpu.VMEM((IN0_W, D), BF16)],
        compiler_params=_params("arbitrary"),
        name="even_proj",
    )(xp, xs, mods, norm_mix, w_in, q_a_norm, wqb, kv_a_norm, qn, w_sc, *tabs)


def _kv_proj_kernel(ckv_ref, kpe_ref, wkvb_ref, kn_ref, cos_ref, s1_ref, s2_ref, k_ref, kv_ref):
    kv = _dot(ckv_ref[...].astype(BF16), wkvb_ref[...])
    kv_ref[...] = kv.astype(BF16)
    kpe = pltpu.roll(kpe_ref[...], QK_NOPE, 1)
    lane = lax.broadcasted_iota(jnp.int32, (1, HEAD_PAD), 1)
    kn = kn_ref[...]
    pe_sq = jnp.sum(kpe * kpe, axis=-1, keepdims=True)
    pe = _rope(kpe * kn, cos_ref[...], s1_ref[...], s2_ref[...])
    for h in range(HEADS):
        blk = kv[:, h * HEAD_PAD:(h + 1) * HEAD_PAD]
        nope = jnp.where(lane < QK_NOPE, blk, 0.0)
        ms = (jnp.sum(nope * nope, axis=-1, keepdims=True) + pe_sq) * (1.0 / QK_HEAD)
        k = jnp.where(lane < QK_NOPE, blk * kn, pe) * lax.rsqrt(ms + EPS)
        k_ref[:, h * HEAD_PAD:(h + 1) * HEAD_PAD] = k.astype(BF16)


def _kv_proj(ckv, kpe, wkvb, kn, tabs, tab_index, name, TKV):
    n = ckv.shape[0]
    full = lambda shape: pl.BlockSpec(shape, lambda i: (0,) * len(shape))
    tab = pl.BlockSpec((TKV, HEAD_PAD), lambda i: (tab_index(i), 0))
    row = lambda w: pl.BlockSpec((TKV, w), lambda i: (i, 0))
    return pl.pallas_call(
        _kv_proj_kernel,
        grid=(n // TKV,),
        in_specs=[row(KV_LORA), row(HEAD_PAD), full((KV_LORA, HEADS * HEAD_PAD)), full((1, HEAD_PAD)),
                  tab, tab, tab],
        out_specs=[row(HEADS * HEAD_PAD), row(HEADS * HEAD_PAD)],
        out_shape=[jax.ShapeDtypeStruct((n, HEADS * HEAD_PAD), BF16)] * 2,
        compiler_params=_params("arbitrary"),
        name=name,
    )(ckv, kpe, wkvb, kn, *tabs)


def _pair_out(o0, o1):
    lane = lax.broadcasted_iota(jnp.int32, (1, HEAD_PAD), 1)
    return jnp.where(lane < V_HEAD, pltpu.roll(o0, V_HEAD, 1), o1).astype(BF16)


CTX_SEQS = 4


def _attn_ctx_kernel(q_ref, k_ref, kv_ref, o_ref):
    for b in range(CTX_SEQS):
        rows = slice(b * CTX_LEN, (b + 1) * CTX_LEN)
        for hp in range(HEADS // 2):
            outs = []
            for j in range(2):
                lanes = slice((2 * hp + j) * HEAD_PAD, (2 * hp + j + 1) * HEAD_PAD)
                s = _dot_nt(q_ref[rows, lanes], k_ref[rows, lanes])
                p = jnp.exp(s - jnp.max(s, axis=-1, keepdims=True))
                l = jnp.sum(p, axis=-1, keepdims=True)
                outs.append(_dot(p.astype(BF16), kv_ref[rows, lanes]) / l)
            o_ref[rows, hp * HEAD_PAD:(hp + 1) * HEAD_PAD] = _pair_out(*outs)


def _attn_ctx(q, k, kv):
    blk = pl.BlockSpec((CTX_SEQS * CTX_LEN, HEADS * HEAD_PAD), lambda b: (b, 0))
    return pl.pallas_call(
        _attn_ctx_kernel,
        grid=(N_CTX_SEQ // CTX_SEQS,),
        in_specs=[blk, blk, blk],
        out_specs=pl.BlockSpec((CTX_SEQS * CTX_LEN, HEADS * V_HEAD), lambda b: (b, 0)),
        out_shape=jax.ShapeDtypeStruct((T_CTX, HEADS * V_HEAD), BF16),
        compiler_params=_params("arbitrary"),
        name="attn_ctx",
    )(q, k, kv)


LAT_HEADS = 8


def _attn_lat_kernel(q_ref, kc_ref, kvc_ref, kl_ref, kvl_ref, o_ref):
    for hp in range(LAT_HEADS // 2):
        outs = []
        for j in range(2):
            h0 = (2 * hp + j) * HEAD_PAD
            lanes = slice(h0, h0 + HEAD_PAD)
            q = q_ref[:, lanes]
            sc = _dot_nt(q, kc_ref[:, lanes])
            sl = _dot_nt(q, kl_ref[:, lanes])
            m = jnp.maximum(jnp.max(sc, axis=-1, keepdims=True), jnp.max(sl, axis=-1, keepdims=True))
            pc, pl_ = jnp.exp(sc - m), jnp.exp(sl - m)
            l = jnp.sum(pc, axis=-1, keepdims=True) + jnp.sum(pl_, axis=-1, keepdims=True)
            o = _dot(pc.astype(BF16), kvc_ref[:, lanes]) + _dot(pl_.astype(BF16), kvl_ref[:, lanes])
            outs.append(o / l)
        o_ref[:, hp * HEAD_PAD:(hp + 1) * HEAD_PAD] = _pair_out(*outs)


def _attn_lat(q, kc, kvc, k, kv):
    nq = LAT_LEN // TQ
    q0 = T_CTX // TQ
    kl0 = T_CTX // LAT_LEN
    width = LAT_HEADS * HEAD_PAD
    lat = pl.BlockSpec((LAT_LEN, width), lambda b, hg, t: (kl0 + b, hg))
    ctx = pl.BlockSpec((PAST, width), lambda b, hg, t: (b, hg))
    return pl.pallas_call(
        _attn_lat_kernel,
        grid=(N_LAT_SEQ, HEADS // LAT_HEADS, nq),
        in_specs=[pl.BlockSpec((TQ, width), lambda b, hg, t: (q0 + b * nq + t, hg)), ctx, ctx, lat, lat],
        out_specs=pl.BlockSpec((TQ, LAT_HEADS * V_HEAD), lambda b, hg, t: (b * nq + t, hg)),
        out_shape=jax.ShapeDtypeStruct((T_LAT, HEADS * V_HEAD), BF16),
        compiler_params=_params("arbitrary", "arbitrary", "arbitrary"),
        name="attn_lat",
    )(q, kc, kvc, k, kv)


FFN_NC = D_FF // FFN_FC
FFN_TM = 512
FFN_CTX_TILES = T_CTX // FFN_TM


def _ffn_kernel(oc_ref, ol_ref, sc_ref, xp_ref, xs_ref, wo_ref, nf_ref, wg_ref, wu_ref, wd_ref,
                mod0_ref, mod1_ref, nm_ref, x2_ref, h3_ref, wg_all, wu_all, wd_all, x1_ref, hs_ref, act_ref):
    t = pl.program_id(0)

    i = jnp.maximum(t - (FFN_NC - 1), 0)
    m0, m1 = _mod_row(mod0_ref, i * FFN_TM), _mod_row(mod1_ref, i * FFN_TM)

    def mixer():
        ctx = i < FFN_CTX_TILES
        attn = jnp.where(ctx, oc_ref[...], ol_ref[...])
        x = jnp.where(ctx, xp_ref[...], xs_ref[...])
        x1 = x + m0[:, 2 * D:3 * D] * _dot(jnp.concatenate([attn, sc_ref[...]], axis=1), wo_ref[...])
        h = (_rms(x1, nf_ref[0:1, :]) * (1.0 + m0[:, 4 * D:5 * D]) + m0[:, 3 * D:4 * D]).astype(BF16)
        return x1, h

    def up(h, c):
        return (_silu(_dot(h, wg_all[c])) * _dot(h, wu_all[c])).astype(BF16)

    def down(x1, act):
        x2 = x1 + m0[:, 5 * D:6 * D] * _dot(act, wd_all[...])
        x2_ref[...] = x2
        h3_ref[...] = (_rms(x2, nm_ref[1:2, :]) * (1.0 + m1[:, D:2 * D]) + m1[:, 0:D]).astype(BF16)

    @pl.when(t == 0)
    def _():
        x1, h = mixer()
        x1_ref[...] = x1
        hs_ref[...] = h

    @pl.when(t < FFN_NC)
    def _stage():
        wg_all[t] = wg_ref[...].astype(BF16)
        wu_all[t] = wu_ref[...].astype(BF16)
        wd_all[pl.ds(pl.multiple_of(t * FFN_FC, FFN_FC), FFN_FC), :] = wd_ref[...].astype(BF16)
        act_ref[t] = up(hs_ref[...], t)

    @pl.when(t == FFN_NC - 1)
    def _():
        down(x1_ref[...], jnp.concatenate([act_ref[c] for c in range(FFN_NC)], axis=1))

    @pl.when(t > FFN_NC - 1)
    def _tile():
        x1, h = mixer()
        down(x1, jnp.concatenate([up(h, c) for c in range(FFN_NC)], axis=1))


def _ffn(oc, ol, sc, xp, xs, wo, norm_ffn, w_gu, w_down, mods, norm_mix):
    chunk = lambda t: jnp.minimum(t, FFN_NC - 1)
    tile = lambda t: jnp.maximum(t - (FFN_NC - 1), 0)
    full = lambda shape: pl.BlockSpec(shape, lambda t: (0,) * len(shape))
    row = lambda n: pl.BlockSpec((FFN_TM, n), lambda t: (tile(t), 0))
    first = lambda n: pl.BlockSpec((FFN_TM, n), lambda t: (jnp.minimum(tile(t), FFN_CTX_TILES - 1), 0))
    second = lambda n: pl.BlockSpec((FFN_TM, n), lambda t: (jnp.maximum(tile(t) - FFN_CTX_TILES, 0), 0))
    return pl.pallas_call(
        _ffn_kernel,
        grid=(FFN_NC - 1 + T // FFN_TM,),
        in_specs=[first(HEADS * V_HEAD), second(HEADS * V_HEAD), row(SC_W), first(D), second(D),
                  full((HEADS * V_HEAD + SC_W, D)), full((2, D)),
                  pl.BlockSpec((D, FFN_FC), lambda t: (0, chunk(t))),
                  pl.BlockSpec((D, FFN_FC), lambda t: (0, FFN_NC + chunk(t))),
                  pl.BlockSpec((FFN_FC, D), lambda t: (chunk(t), 0)),
                  _mod_spec(0), _mod_spec(1), full((2, D))],
        out_specs=[row(D), row(D)],
        out_shape=[jax.ShapeDtypeStruct((T, D), F32), jax.ShapeDtypeStruct((T, D), BF16)],
        scratch_shapes=[pltpu.VMEM((FFN_NC, D, FFN_FC), BF16), pltpu.VMEM((FFN_NC, D, FFN_FC), BF16),
                        pltpu.VMEM((D_FF, D), BF16), pltpu.VMEM((FFN_TM, D), F32), pltpu.VMEM((FFN_TM, D), BF16),
                        pltpu.VMEM((FFN_NC, FFN_TM, FFN_FC), BF16)],
        compiler_params=_params("arbitrary"),
        name="ffn_dense",
    )(oc, ol, sc, xp, xs, wo, norm_ffn, w_gu, w_gu, w_down, mods, mods, norm_mix)


CONF_CB = 256
CONF_SEG = 256
CONF_HALO = 16
CONF_SEGP = CONF_SEG + 2 * CONF_HALO
CONF_PIECE = 256


def _conf_kernel(h_ref, x2_ref, w1_ref, b1_ref, wdw_ref, bdw_ref, lng_ref, lnb_ref, w2_ref, b2_ref,
                 mod_ref, nf_ref, rt_ref, x3_ref, h4_ref, lg_ref, pad_ref, conv_ref):
    i = pl.program_id(0)
    nseg = TM // CONF_SEG
    h = h_ref[...]
    joined = jnp.where(i < CTX_TILES, 0.0, 1.0)
    zeros_halo = jnp.zeros((CONF_HALO, CONF_CB), F32)
    for cb in range(D // CONF_CB):
        c0 = cb * CONF_CB
        a = _dot(h, w1_ref[:, c0:c0 + CONF_CB]) + b1_ref[:, c0:c0 + CONF_CB]
        g = _dot(h, w1_ref[:, D + c0:D + c0 + CONF_CB]) + b1_ref[:, D + c0:D + c0 + CONF_CB]
        u = a * jax.nn.sigmoid(g)
        for s in range(nseg):
            base = s * CONF_SEGP
            top = u[s * CONF_SEG - CONF_HALO:s * CONF_SEG] * joined if s > 0 else zeros_halo
            bot = (u[(s + 1) * CONF_SEG:(s + 1) * CONF_SEG + CONF_HALO] * joined
                   if s < nseg - 1 else zeros_halo)
            pad_ref[0, base:base + CONF_HALO, :] = top
            pad_ref[0, base + CONF_HALO:base + CONF_HALO + CONF_SEG, :] = u[s * CONF_SEG:(s + 1) * CONF_SEG]
            pad_ref[0, base + CONF_HALO + CONF_SEG:base + CONF_SEGP, :] = bot

        p0 = pad_ref[0]
        rows = nseg * CONF_SEGP
        for b in range(1, 8):
            pad_ref[b] = pltpu.roll(p0, rows - b, 0)

        def piece(t, carry):
            s = t // (CONF_SEG // CONF_PIECE)
            q0 = (t % (CONF_SEG // CONF_PIECE)) * CONF_PIECE
            src = pl.multiple_of(s * CONF_SEGP + q0, 8)
            acc = jnp.zeros((CONF_PIECE, CONF_CB), F32)
            for j in range(CONF_K):
                hi, lo = (j + 1) // 8, (j + 1) % 8
                acc = acc + wdw_ref[0, j:j + 1, c0:c0 + CONF_CB] * pad_ref[lo, pl.ds(src + 8 * hi, CONF_PIECE), :]
            dst = pl.multiple_of(s * CONF_SEG + q0, 8)
            conv_ref[pl.ds(dst, CONF_PIECE), c0:c0 + CONF_CB] = acc + bdw_ref[:, c0:c0 + CONF_CB]
            return carry

        lax.fori_loop(0, TM // CONF_PIECE, piece, 0)

    m = _mod_row(mod_ref, i * TM)
    half = TM // 2
    for r0 in (0, half):
        rows = slice(r0, r0 + half)
        y = conv_ref[rows, :]
        mu = jnp.mean(y, axis=-1, keepdims=True)
        yc = y - mu
        var = jnp.mean(yc * yc, axis=-1, keepdims=True)
        y = _silu(yc * lax.rsqrt(var + EPS) * lng_ref[...] + lnb_ref[...])
        out = _dot(y.astype(BF16), w2_ref[...]) + b2_ref[...]
        x3 = x2_ref[rows, :] + m[:, 2 * D:3 * D] * out
        x3_ref[rows, :] = x3
        h4 = _rms(x3, nf_ref[1:2, :]) * (1.0 + m[:, 4 * D:5 * D]) + m[:, 3 * D:4 * D]
        h_hi = h4.astype(BF16)
        h4_ref[rows, :] = h_hi
        h_lo = (h4 - h_hi.astype(F32)).astype(BF16)
        r = rt_ref[...]
        r_hi = r.astype(BF16)
        r_lo = (r - r_hi.astype(F32)).astype(BF16)
        lg_ref[:, rows] = _dot_nt(r_hi, h_hi) + _dot_nt(r_lo, h_hi) + _dot_nt(r_hi, h_lo)


def _conf(h3, x2, w1, b1, wdw, bdw, lng, lnb, w2, b2, mods, norm_ffn1, router_t):
    full = lambda shape: pl.BlockSpec(shape, lambda i: (0,) * len(shape))
    row = lambda n: pl.BlockSpec((TM, n), lambda i: (i, 0))
    return pl.pallas_call(
        _conf_kernel,
        grid=(N_TILES,),
        in_specs=[row(D), row(D), full((D, 2 * D)), full((1, 2 * D)), full((1, CONF_K, D)), full((1, D)),
                  full((1, D)), full((1, D)), full((D, D)), full((1, D)),
                  _mod_spec(1), full((2, D)), full((N_EXP, D))],
        out_specs=[row(D), row(D), pl.BlockSpec((N_EXP, TM), lambda i: (0, i))],
        out_shape=[jax.ShapeDtypeStruct((T, D), F32), jax.ShapeDtypeStruct((T, D), BF16),
                   jax.ShapeDtypeStruct((N_EXP, T), F32)],
        scratch_shapes=[pltpu.VMEM((8, (TM // CONF_SEG) * CONF_SEGP, CONF_CB), F32),
                        pltpu.VMEM((TM, D), F32)],
        compiler_params=_params("arbitrary"),
        name="conformer_conv",
    )(h3, x2, w1, b1, wdw, bdw, lng, lnb, w2, b2, mods, norm_ffn1, router_t)


TB = 256
N_TB = T // TB
SUB = 128
SUBS = 24
SM = SUBS * SUB
N_SUB_MAX = 2 * T // SUB + N_EXP
N_SUP_MAX = (N_SUB_MAX + N_EXP * (SUBS - 1)) // SUBS
YS_ROWS = (N_SUB_MAX + 4) * SUB
WIN_ALIGN = 16
WIN_A = TB // 2
WIN_B = TB // 2 + WIN_ALIGN
FIRST_STRIDE = 32
UNIT_STRIDE = 64
GATHER_BLOCKS = 4


def _route_kernel(lg_ref, g_ref, rank_ref, first_ref):
    lg = lg_ref[...]
    idx = lax.broadcasted_iota(jnp.int32, lg.shape, 0).astype(F32)
    none = float(N_EXP)
    m1 = jnp.max(lg, axis=0, keepdims=True)
    i1 = jnp.min(jnp.where(lg == m1, idx, none), axis=0, keepdims=True)
    rest = jnp.where(idx == i1, -jnp.inf, lg)
    m2 = jnp.max(rest, axis=0, keepdims=True)
    i2 = jnp.min(jnp.where(rest == m2, idx, none), axis=0, keepdims=True)
    e = jnp.exp(m2 - m1)
    w1 = 1.0 / (1.0 + e)
    w2 = e / (1.0 + e)
    g_ref[...] = jnp.where(idx == i1, w1, 0.0) + jnp.where(idx == i2, w2, 0.0)

    mask = jnp.where(idx == i1, 1.0, 0.0) + jnp.where(idx == i2, 1.0, 0.0)
    before = (lax.broadcasted_iota(jnp.int32, (TB, TB), 0) < lax.broadcasted_iota(jnp.int32, (TB, TB), 1))
    before = jnp.where(before, 1.0, 0.0).astype(BF16)
    lane = lax.broadcasted_iota(jnp.int32, (N_EXP, 128), 1)
    carry = jnp.zeros((N_EXP, 1), F32)
    first = jnp.zeros((N_EXP, 128), F32)
    for b in range(N_TB):
        mb = mask[:, b * TB:(b + 1) * TB]
        local = _dot(mb.astype(BF16), before)
        rank_ref[:, b * TB:(b + 1) * TB] = jnp.where(mb > 0.0, local + carry, -1.0)
        first = jnp.where(lane == b, carry, first)
        carry = carry + jnp.sum(mb, axis=1, keepdims=True)
    first_ref[...] = jnp.where(lane == N_TB, carry, first)


def _route(logits_t):
    return pl.pallas_call(
        _route_kernel,
        out_shape=[jax.ShapeDtypeStruct((N_EXP, T), F32), jax.ShapeDtypeStruct((N_EXP, T), F32),
                   jax.ShapeDtypeStruct((N_EXP, 128), F32)],
        compiler_params=pltpu.CompilerParams(vmem_limit_bytes=VMEM_LIMIT),
        name="route",
    )(logits_t)


def _moe_plan(first):
    first = first[:, :FIRST_STRIDE].astype(jnp.int32)
    cnt = first[:, N_TB]
    nt = (cnt + (SUB - 1)) // SUB
    off_end = jnp.cumsum(nt)
    off = off_end - nt
    nsub = off_end[-1]
    nsup = (nt + (SUBS - 1)) // SUBS
    sup_end = jnp.cumsum(nsup)
    sup_off = sup_end - nsup
    s = jnp.minimum(jnp.arange(N_SUP_MAX), sup_end[-1] - 1)
    valid = jnp.arange(N_SUP_MAX) < sup_end[-1]
    se = jnp.sum(s[:, None] >= sup_end[None, :], axis=1)
    mine = se[:, None] == jnp.arange(N_EXP)[None, :]
    pick = lambda v: jnp.sum(jnp.where(mine, v[None, :], 0), axis=1)
    sk0 = (s - pick(sup_off)) * SUBS
    sns = jnp.where(valid, jnp.clip(pick(nt) - sk0, 0, SUBS), 0)
    sj0 = pick(off) + sk0
    base = (jnp.arange(UNIT_STRIDE) * SUB)[None, :, None]
    blo = jnp.minimum(jnp.sum(first[:, None, 1:N_TB + 1] <= base, axis=2), N_TB - 1)
    end = jnp.minimum(base + SUB, cnt[:, None, None])
    bhi = jnp.maximum(jnp.sum(first[:, None, :N_TB] < end, axis=2) - 1, blo)
    ng = (bhi - blo) // GATHER_BLOCKS + 1
    start = SUB * off[:, None] + first[:, :N_TB]
    lead = start & (WIN_ALIGN - 1)
    wina = start - lead
    rel = first[:, :N_TB] - lead
    need = lead + (first[:, 1:N_TB + 1] - first[:, :N_TB]) > WIN_A
    winb = lax.cummax(jnp.where(need, wina + WIN_A, 0), axis=1)
    wide = jnp.any(need, axis=0)
    flat = lambda parts: jnp.concatenate([p.astype(jnp.int32).reshape(-1) for p in parts])
    gmm = dict(se=se, sk0=sk0, sns=sns, sj0=sj0, nsub=nsub, first=first, blo=blo, ng=ng)
    comb = dict(wina=wina.T, winb=winb.T, rel=rel.T, wide=wide)
    return flat([gmm[k] for k in _GMM_TAB]), flat([comb[k] for k in _COMB_TAB])


def _offsets(sizes):
    out, pos = {}, 0
    for name, n in sizes.items():
        out[name], pos = pos, pos + n
    return out


_GMM_TAB = _offsets(dict(se=N_SUP_MAX, sk0=N_SUP_MAX, sns=N_SUP_MAX, sj0=N_SUP_MAX, nsub=1,
                         first=N_EXP * FIRST_STRIDE, blo=N_EXP * UNIT_STRIDE, ng=N_EXP * UNIT_STRIDE))
_COMB_TAB = _offsets(dict(wina=N_TB * N_EXP, winb=N_TB * N_EXP, rel=N_TB * N_EXP, wide=N_TB))


class _Section:
    def __init__(self, ref, offset):
        self.ref, self.offset = ref, offset

    def __getitem__(self, i):
        return self.ref[self.offset + i]


def _moe_gmm_kernel(tab_ref, x_ref, rank_ref, gate_ref, wg_ref, wu_ref, wd_ref, ys_ref,
                    xs_ref, gs_ref, yacc_ref, wgb_ref, wub_ref, wdb_ref, sem):
    se_ref, sk0_ref, sns_ref, sj0_ref, nsub_ref, first_ref, blo_ref, ng_ref = (
        _Section(tab_ref, _GMM_TAB[k]) for k in ("se", "sk0", "sns", "sj0", "nsub", "first", "blo", "ng"))
    s, c = pl.program_id(0), pl.program_id(1)
    nc = pl.num_programs(1)
    e, k0, ns = se_ref[s], sk0_ref[s], sns_ref[s]

    def sub_rows(k):
        return pl.ds(pl.multiple_of(k * SUB, SUB), SUB)

    def out_copy(k, row0):
        dst = ys_ref.at[pl.ds(pl.multiple_of(row0 + k * SUB, SUB), SUB)]
        return pltpu.make_async_copy(xs_ref.at[sub_rows(k)], dst, sem.at[k])

    @pl.when((ns > 0) & (c == 0))
    def _gather():
        def group(k, g):
            slot = (lax.broadcasted_iota(jnp.int32, (SUB, 1), 0) + (k0 + k) * SUB).astype(F32)
            b0 = blo_ref[e * UNIT_STRIDE + k0 + k] + g * GATHER_BLOCKS
            t0 = pl.multiple_of(jnp.minimum(b0, N_TB - GATHER_BLOCKS) * TB, TB)
            lo = first_ref[e * FIRST_STRIDE + b0].astype(F32)
            cols = pl.ds(t0, GATHER_BLOCKS * TB)
            hit = rank_ref[pl.ds(e, 1), cols] == jnp.where(slot >= lo, slot, -2.0)
            rows = _dot(jnp.where(hit, 1.0, 0.0).astype(BF16), x_ref[cols, :])
            gate = jnp.sum(jnp.where(hit, gate_ref[pl.ds(e, 1), cols], 0.0), axis=-1, keepdims=True)
            return rows, gate

        def first(k):
            rows, gate = group(k, 0)
            xs_ref[sub_rows(k), :] = rows.astype(BF16)
            gs_ref[sub_rows(k), :] = gate
            yacc_ref[sub_rows(k), :] = jnp.zeros((SUB, D), F32)

        def more(k):
            def body(g, carry):
                rows, gate = group(k, g)
                xs_ref[sub_rows(k), :] = (xs_ref[sub_rows(k), :].astype(F32) + rows).astype(BF16)
                gs_ref[sub_rows(k), :] += gate
                return carry

            lax.fori_loop(1, ng_ref[e * UNIT_STRIDE + k0 + k], body, 0)

        def quad(p, carry):
            for j in range(4):
                first(4 * p + j)
            for j in range(4):
                more(4 * p + j)
            return carry

        def single(k, carry):
            first(k)
            more(k)
            return carry

        lax.fori_loop(0, ns >> 2, quad, 0)
        lax.fori_loop(ns & ~3, ns, single, 0)

    @pl.when(ns > 0)
    def _compute():
        row0 = sj0_ref[s] * SUB

        def swiglu(rows, wg, wu, wd):
            x = xs_ref[rows, :]
            g = _dot(x, wg)
            u = _dot(x, wu)
            yacc_ref[rows, :] += _dot((_silu(g) * u).astype(BF16), wd)

        def finish(first_sub, n):
            @pl.when(c == nc - 1)
            def _():
                for k in range(n):
                    rows = sub_rows(first_sub + k)
                    xs_ref[rows, :] = (yacc_ref[rows, :] * gs_ref[rows, :]).astype(BF16)
                    out_copy(first_sub + k, row0).start()

        def first_chain(rows):
            wg, wu, wd = wg_ref[0].astype(BF16), wu_ref[0].astype(BF16), wd_ref[0].astype(BF16)
            wgb_ref[...] = wg
            wub_ref[...] = wu
            wdb_ref[...] = wd
            swiglu(rows, wg, wu, wd)

        @pl.when(ns >= 4)
        def _():
            first_chain(pl.ds(0, 4 * SUB))
            finish(0, 4)

        @pl.when(ns < 4)
        def _():
            first_chain(pl.ds(0, SUB))
            finish(0, 1)

        done = jnp.where(ns >= 4, 4, 1)
        rest = ns - done

        def chain(first_sub, n):
            rows = pl.ds(pl.multiple_of(first_sub * SUB, SUB), n * SUB)
            swiglu(rows, wgb_ref[...], wub_ref[...], wdb_ref[...])
            finish(first_sub, n)

        def eight(k, carry):
            chain(done + 8 * k, 8)
            return carry

        lax.fori_loop(0, rest >> 3, eight, 0)
        done8 = done + (rest & ~7)
        for n in (4, 2, 1):
            @pl.when((rest & n) != 0)
            def _(n=n):
                chain(done8 + (rest & (7 & ~(2 * n - 1))), n)

    @pl.when((ns > 0) & (c == nc - 1))
    def _store_done():
        row0 = sj0_ref[s] * SUB

        def done(k, carry):
            out_copy(k, row0).wait()
            return carry

        lax.fori_loop(0, ns, done, 0)

    @pl.when((s == pl.num_programs(0) - 1) & (c == nc - 1))
    def _zero_tail():
        xs_ref[0:SUB, :] = jnp.zeros((SUB, D), BF16)
        nsub = nsub_ref[0]

        def fill(k, carry):
            cp = out_copy(0, (nsub + k) * SUB)
            cp.start()
            cp.wait()
            return carry

        lax.fori_loop(0, YS_ROWS // SUB - nsub, fill, 0)


def _moe_gmm(tab, h4, rank, gates, w_gu, w_down):
    nc = D_FFE // MOE_FC

    def expert(s, tab):
        return tab[_GMM_TAB["se"] + s]

    def chunk(s, c, tab):
        return jnp.where(tab[_GMM_TAB["sns"] + s] > 0, c, nc - 1)

    return pl.pallas_call(
        _moe_gmm_kernel,
        grid_spec=pltpu.PrefetchScalarGridSpec(
            num_scalar_prefetch=1,
            grid=(N_SUP_MAX, nc),
            in_specs=[
                pl.BlockSpec((T, D), lambda s, c, tab: (0, 0), pipeline_mode=pl.Buffered(1)),
                pl.BlockSpec((N_EXP, T), lambda s, c, tab: (0, 0)),
                pl.BlockSpec((N_EXP, T), lambda s, c, tab: (0, 0)),
                pl.BlockSpec((1, D, MOE_FC), lambda s, c, tab: (expert(s, tab), 0, chunk(s, c, tab))),
                pl.BlockSpec((1, D, MOE_FC), lambda s, c, tab: (expert(s, tab), 0, nc + chunk(s, c, tab))),
                pl.BlockSpec((1, MOE_FC, D), lambda s, c, tab: (expert(s, tab), chunk(s, c, tab), 0)),
            ],
            out_specs=pl.BlockSpec(memory_space=pl.ANY),
            scratch_shapes=[
                pltpu.VMEM((SM, D), BF16), pltpu.VMEM((SM, 1), F32), pltpu.VMEM((SM, D), F32),
                pltpu.VMEM((D, MOE_FC), BF16), pltpu.VMEM((D, MOE_FC), BF16), pltpu.VMEM((MOE_FC, D), BF16),
                pltpu.SemaphoreType.DMA((SUBS,)),
            ],
        ),
        out_shape=jax.ShapeDtypeStruct((YS_ROWS, D), BF16),
        compiler_params=_params("arbitrary", "arbitrary"),
        name="moe_gmm",
    )(tab, h4, rank, gates, w_gu, w_gu, w_down)


def _moe_combine_kernel(tab_ref, *refs):
    rel_ref, wide_ref = _Section(tab_ref, _COMB_TAB["rel"]), _Section(tab_ref, _COMB_TAB["wide"])
    ya, yb = refs[:N_EXP], refs[N_EXP:2 * N_EXP]
    rank_ref, x3_ref, mod_ref, yp_ref, ys_ref, ycat_ref, acc_ref = refs[2 * N_EXP:]
    b = pl.program_id(0)

    def onehot(e, first_row, n):
        row = lax.broadcasted_iota(jnp.int32, (n, 1), 0)
        slot = (row + (rel_ref[b * N_EXP + e] + first_row)).astype(F32)
        return jnp.where(rank_ref[e:e + 1, :] == slot, 1.0, 0.0).astype(BF16)

    def gather(y_refs, first_row, base, n):
        pieces = []
        for e in range(N_EXP):
            ycat_ref[base + e * n:base + (e + 1) * n, :] = y_refs[e][...]
            pieces.append(onehot(e, first_row, n))
        return pieces

    def combine(pieces, rows):
        return lax.dot_general(jnp.concatenate(pieces, axis=0), ycat_ref[0:rows, :], (((0,), (0,)), ((), ())),
                               preferred_element_type=F32)

    @pl.when(wide_ref[b] == 0)
    def _():
        acc_ref[...] = combine(gather(ya, 0, 0, WIN_A), N_EXP * WIN_A)

    @pl.when(wide_ref[b] != 0)
    def _():
        pieces = gather(ya, 0, 0, WIN_A) + gather(yb, WIN_A, N_EXP * WIN_A, WIN_B)
        acc_ref[...] = combine(pieces, N_EXP * (WIN_A + WIN_B))

    out = x3_ref[...] + _mod_row(mod_ref, b * TB)[:, 5 * D:6 * D] * acc_ref[...]

    @pl.when(b < T_CTX // TB)
    def _():
        yp_ref[...] = out

    @pl.when(b >= T_CTX // TB)
    def _():
        ys_ref[...] = out


def _moe_combine(tab, ysorted, rank, x3, mods):
    ctx_blocks = T_CTX // TB

    def window(e, second):
        def index(b, tab):
            start = tab[_COMB_TAB["winb" if second else "wina"] + b * N_EXP + e]
            return pl.multiple_of(start, WIN_ALIGN), 0
        return pl.BlockSpec((pl.Element(WIN_B if second else WIN_A), pl.Element(D)), index)

    return pl.pallas_call(
        _moe_combine_kernel,
        grid_spec=pltpu.PrefetchScalarGridSpec(
            num_scalar_prefetch=1,
            grid=(N_TB,),
            in_specs=[window(e, False) for e in range(N_EXP)] + [window(e, True) for e in range(N_EXP)] + [
                pl.BlockSpec((N_EXP, TB), lambda b, *_: (0, b)),
                pl.BlockSpec((TB, D), lambda b, *_: (b, 0)),
                _mod_spec(1),
            ],
            out_specs=[pl.BlockSpec((TB, D), lambda b, *_: (jnp.minimum(b, ctx_blocks - 1), 0)),
                       pl.BlockSpec((TB, D), lambda b, *_: (jnp.maximum(b - ctx_blocks, 0), 0))],
            scratch_shapes=[pltpu.VMEM((N_EXP * (WIN_A + WIN_B), D), BF16), pltpu.VMEM((TB, D), F32)],
        ),
        out_shape=[jax.ShapeDtypeStruct((T_CTX, D), F32), jax.ShapeDtypeStruct((T_LAT, D), F32)],
        compiler_params=_params("arbitrary"),
        name="moe_combine",
    )(tab, *([ysorted] * (2 * N_EXP)), rank, x3, mods)


def _pad_heads(w, width):
    lead = w.shape[:-1]
    w = w.reshape(*lead, HEADS, width)
    w = jnp.pad(w, [(0, 0)] * len(lead) + [(0, 0), (0, HEAD_PAD - width)])
    return w.reshape(*lead, HEADS * HEAD_PAD)


def kernel(x_prompt, x_sample, cache_ckv, cache_kpe, c, c_ctx, ada_w, ada_b, norm_mix, norm_ffn, w_in, q_a_norm,
           w_qb, kv_a_norm, w_kvb, q_norm, k_norm, w_sc, w_o, ffn_gu, ffn_down, conv_pw1, conv_pw1_b, conv_dw,
           conv_dw_b, conv_ln_g, conv_ln_b, conv_pw2, conv_pw2_b, router, moe_gu, moe_down):
    xp = x_prompt.reshape(T_CTX, D)
    xs = x_sample.reshape(T_LAT, D)

    mods = _adaln(c_ctx, c, ada_w, ada_b)

    wqb = _pad_heads(w_qb[0], QK_HEAD).astype(BF16)
    wkvb = w_kvb[0].astype(BF16)
    qn = jnp.pad(q_norm[0], (0, HEAD_PAD - QK_HEAD)).reshape(1, HEAD_PAD)
    kn = jnp.pad(k_norm[0], (0, HEAD_PAD - QK_HEAD)).reshape(1, HEAD_PAD)
    tabs = _rope_tables()

    w_in_t = jnp.swapaxes(w_in[0], 0, 1)
    q, ckv, kpe, sc, state_ckv, state_kpe = _even_proj(xp, xs, mods, norm_mix, w_in_t, q_a_norm, wqb, kv_a_norm,
                                                       qn, w_sc, tabs)

    lat_tile0 = T_CTX // TKV
    ident = LAT_LEN // TKV
    k, kv = _kv_proj(ckv, kpe, wkvb, kn, tabs,
                     lambda i: jnp.where(i < lat_tile0, ident, (i - lat_tile0) % ident), "kv_proj", TKV)
    cache_kpe_p = jnp.pad(cache_kpe[:, 0].reshape(N_LAT_SEQ * PAST, QK_ROPE), ((0, 0), (0, HEAD_PAD - QK_ROPE)))
    kc, kvc = _kv_proj(cache_ckv[:, 0].reshape(N_LAT_SEQ * PAST, KV_LORA), cache_kpe_p, wkvb, kn, tabs,
                       lambda i: LAT_LEN // (N_LAT_SEQ * PAST), "kv_proj_cache", N_LAT_SEQ * PAST)

    oc = _attn_ctx(q, k, kv)
    ol = _attn_lat(q, kc, kvc, k, kv)
    x2, h3 = _ffn(oc, ol, sc, xp, xs, w_o[0].astype(BF16), norm_ffn, ffn_gu[0], ffn_down[0], mods, norm_mix)

    x3, h4, logits_t = _conf(h3, x2, conv_pw1[0].astype(BF16), conv_pw1_b, conv_dw, conv_dw_b, conv_ln_g,
                             conv_ln_b, conv_pw2[0].astype(BF16), conv_pw2_b, mods, norm_ffn, router[0].T)
    gates, rank, first = _route(logits_t)
    gmm_tab, comb_tab = _moe_plan(first)
    ysorted = _moe_gmm(gmm_tab, h4, rank, gates, moe_gu[0], moe_down[0])
    yp, ys = _moe_combine(comb_tab, ysorted, rank, x3, mods)

    return (yp.reshape(N_CTX_SEQ, CTX_LEN, D), ys.reshape(N_LAT_SEQ, LAT_LEN, D),
            state_ckv.reshape(N_CTX_SEQ, 1, CTX_LEN, KV_LORA), state_kpe.reshape(N_CTX_SEQ, 1, CTX_LEN, QK_ROPE))
```
